```python
import functools
import jax, jax.numpy as jnp
from jax import lax
import numpy as np

D_MODEL = 1024
BATCH = 8
SEQ = 2048
DEPTH = 1
DEC_BATCH = 128
DEC_SEQ = 1
PAST_LEN = 16384
PAGE_SIZE = 128

MIX_WIDTH = D_MODEL
HEAD_DIM = 64
RWKV_WIDTH = MIX_WIDTH // 2
RWKV_HEADS = RWKV_WIDTH // HEAD_DIM
RWKV_DECAY_LORA = 64
RWKV_A_LORA = 64
RWKV_GATE_LORA = 128
RWKV_GN_EPS = 64e-5
SWA_WIDTH = MIX_WIDTH - RWKV_WIDTH
SWA_HEADS = SWA_WIDTH // HEAD_DIM
SWA_KV_HEADS = 2
SWA_GROUP = SWA_HEADS // SWA_KV_HEADS
WINDOW = 128
ROPE_THETA = 10000.0
ATTN_SCALE = HEAD_DIM ** -0.5
D_FF = 2816
CONV_WIDTH = 3
NORM_EPS = 1e-6
MASK_VALUE = -1e30

RWKV_SIZES = (RWKV_WIDTH, RWKV_WIDTH, RWKV_WIDTH, RWKV_DECAY_LORA, RWKV_A_LORA, RWKV_GATE_LORA)
RWKV_COLS = sum(RWKV_SIZES)
SWA_SIZES = (SWA_WIDTH, SWA_KV_HEADS * HEAD_DIM, SWA_KV_HEADS * HEAD_DIM)
PROJ_COLS = RWKV_COLS + sum(SWA_SIZES)

kernel_name = 'hymba_rwkv7_swa_sink_convffn_step'


def _offsets(sizes):
    return [int(s) for s in np.cumsum(sizes)[:-1]]


def _rmsnorm(x, g):
    xf = x.astype(jnp.float32)
    y = xf * lax.rsqrt(jnp.mean(xf * xf, axis=-1, keepdims=True) + NORM_EPS)
    return (y * g.astype(jnp.float32)).astype(x.dtype)


def _rope(x, pos):
    half = HEAD_DIM // 2
    inv = ROPE_THETA ** (-jnp.arange(half, dtype=jnp.float32) / half)
    ang = pos.astype(jnp.float32)[:, None] * inv[None, :]
    cos = jnp.cos(ang)[None, :, None, :]
    sin = jnp.sin(ang)[None, :, None, :]
    x1 = x[..., :half].astype(jnp.float32)
    x2 = x[..., half:].astype(jnp.float32)
    return jnp.concatenate([x1 * cos - x2 * sin, x2 * cos + x1 * sin], axis=-1).astype(x.dtype)


def _wkv_scan(r, w, k, v, kk, a, S0):
    def step(S, inp):
        r_t, w_t, k_t, v_t, kk_t, a_t = inp
        sa = jnp.einsum('bhvk,bhk->bhv', S, -kk_t)
        S = (S * w_t[:, :, None, :] + sa[..., None] * (kk_t * a_t)[:, :, None, :]
             + v_t[..., None] * k_t[:, :, None, :])
        return S, jnp.einsum('bhvk,bhk->bhv', S, r_t)
    xs = tuple(jnp.swapaxes(t, 0, 1) for t in (r, w, k, v, kk, a))
    S_T, o = lax.scan(step, S0, xs)
    return S_T, jnp.swapaxes(o, 0, 1)


def _rwkv7(p, prev_row, S0, P):
    B, T, _ = p.shape
    f32 = jnp.float32
    prev = jnp.concatenate([prev_row[:, None, :].astype(p.dtype), p[:, :-1]], axis=1)
    xm = p + (prev - p) * P['rwkv_mu']
    r, k, v, wd, ad, gd = jnp.split(xm, _offsets(RWKV_SIZES), axis=-1)
    w_log = -jax.nn.softplus(-(P['rwkv_w0'] + jnp.tanh(wd) @ P['rwkv_w_up']).astype(f32)) - 0.5
    decay = jnp.exp(-jnp.exp(w_log))
    a = jax.nn.sigmoid((P['rwkv_a0'] + ad @ P['rwkv_a_up']).astype(f32))
    g = (jax.nn.sigmoid(gd) @ P['rwkv_g_up']).astype(f32)
    heads = lambda t: t.reshape(B, T, RWKV_HEADS, HEAD_DIM)
    kf = k.astype(f32)
    kk = heads(kf * P['rwkv_k_k'].astype(f32))
    kk = kk / jnp.maximum(jnp.sqrt(jnp.sum(kk * kk, axis=-1, keepdims=True)), 1e-12)
    kf = heads(kf * (1.0 + (a - 1.0) * P['rwkv_k_a'].astype(f32)))
    rf, vf, a, decay = heads(r.astype(f32)), heads(v.astype(f32)), heads(a), heads(decay)
    S_T, o = _wkv_scan(rf, decay, kf, vf, kk, a, S0.astype(f32))
    mu = jnp.mean(o, axis=-1, keepdims=True)
    var = jnp.mean(jnp.square(o - mu), axis=-1, keepdims=True)
    o = ((o - mu) * lax.rsqrt(var + RWKV_GN_EPS)).reshape(B, T, RWKV_WIDTH)
    o = o * P['rwkv_ln_w'].astype(f32) + P['rwkv_ln_b'].astype(f32)
    bonus = (jnp.sum(rf * kf * P['rwkv_r_k'].astype(f32), axis=-1, keepdims=True) * vf).reshape(B, T, RWKV_WIDTH)
    out = ((o + bonus) * g).astype(p.dtype)
    return out, S_T.astype(S0.dtype), p[:, -1]


def _sink_attention(q, k, v, qpos, kpos, sinks):
    s = jnp.einsum('...qkgd,...skd->...kgqs', q, k, preferred_element_type=jnp.float32) * ATTN_SCALE
    diff = qpos[..., :, None] - kpos[..., None, :]
    valid = (diff >= 0) & (diff < WINDOW) & (kpos[..., None, :] >= 0)
    s = jnp.where(valid[..., None, None, :, :], s, MASK_VALUE)
    sink = sinks.astype(jnp.float32).reshape(SWA_KV_HEADS, SWA_GROUP)[:, :, None, None]
    m = jnp.maximum(jnp.max(s, axis=-1, keepdims=True), sink)
    e = jnp.exp(s - m)
    prob = e / (jnp.sum(e, axis=-1, keepdims=True) + jnp.exp(sink - m))
    return jnp.einsum('...kgqs,...skd->...qkgd', prob.astype(v.dtype), v)


def _swa_prompt(q, k, v, sinks):
    B, T, _, _ = q.shape
    nb = T // WINDOW
    qb = q.reshape(B, nb, WINDOW, SWA_KV_HEADS, SWA_GROUP, HEAD_DIM)
    kb = k.reshape(B, nb, WINDOW, SWA_KV_HEADS, HEAD_DIM)
    vb = v.reshape(B, nb, WINDOW, SWA_KV_HEADS, HEAD_DIM)
    pad = ((0, 0), (1, 0), (0, 0), (0, 0), (0, 0))
    kc = jnp.concatenate([jnp.pad(kb[:, :-1], pad), kb], axis=2)
    vc = jnp.concatenate([jnp.pad(vb[:, :-1], pad), vb], axis=2)
    qpos = jnp.arange(T, dtype=jnp.int32).reshape(nb, WINDOW)
    kpos = qpos[:, :1] - WINDOW + jnp.arange(2 * WINDOW, dtype=jnp.int32)[None, :]
    o = _sink_attention(qb, kc, vc, qpos, kpos, sinks)
    keep = min(WINDOW, T)
    return o.reshape(B, T, SWA_WIDTH), k[:, -keep:], v[:, -keep:]


def _swa_sample(q, k, v, sinks, cache_k, cache_v):
    B, T, _, _ = q.shape
    L = cache_k.shape[1]
    kc = jnp.concatenate([cache_k.astype(k.dtype), k], axis=1)
    vc = jnp.concatenate([cache_v.astype(v.dtype), v], axis=1)
    kpos = PAST_LEN - L + jnp.arange(L + T, dtype=jnp.int32)
    qpos = PAST_LEN + jnp.arange(T, dtype=jnp.int32)
    o = _sink_attention(q.reshape(B, T, SWA_KV_HEADS, SWA_GROUP, HEAD_DIM), kc, vc, qpos, kpos, sinks)
    return o.reshape(B, T, SWA_WIDTH), kc[:, -L:], vc[:, -L:]


def _conv_ffn(h, conv_prev, P):
    T = h.shape[1]
    u = h @ P['ffn_w_up']
    ext = jnp.concatenate([conv_prev.astype(u.dtype), u], axis=1)
    z = P['ffn_conv_b']
    for j in range(CONV_WIDTH):
        z = z + ext[:, j:j + T] * P['ffn_conv_w'][j]
    za, zb = jnp.split(z, 2, axis=-1)
    return (jax.nn.silu(za) * zb) @ P['ffn_w_down'], ext[:, T:]


def _layer(x, c, pos, wkv0, shift0, conv0, attend, P):
    B, T, _ = x.shape
    mod = jax.nn.silu(c) @ P['w_ada'] + P['b_ada']
    sh1, sc1, g1, sh2, sc2, g2 = jnp.split(mod[:, None, :], 6, axis=-1)
    h = _rmsnorm(x, P['norm_pre_mix']) * (1.0 + sc1) + sh1
    proj = h @ P['w_in']
    p_rwkv, q, k, v = jnp.split(proj, _offsets((RWKV_COLS,) + SWA_SIZES), axis=-1)
    y_rwkv, wkv_T, shift_T = _rwkv7(p_rwkv, shift0, wkv0, P)
    q = _rope(q.reshape(B, T, SWA_HEADS, HEAD_DIM), pos)
    k = _rope(k.reshape(B, T, SWA_KV_HEADS, HEAD_DIM), pos)
    v = v.reshape(B, T, SWA_KV_HEADS, HEAD_DIM)
    y_swa, kbuf, vbuf = attend(q, k, v, P['swa_sinks'])
    mix = jnp.concatenate([y_rwkv, y_swa], axis=-1) @ P['w_out']
    x = x + g1 * _rmsnorm(mix, P['norm_post_mix'])
    h2 = _rmsnorm(x, P['norm_pre_ffn']) * (1.0 + sc2) + sh2
    ff, conv_T = _conv_ffn(h2, conv0, P)
    x = x + g2 * _rmsnorm(ff, P['norm_post_ffn'])
    return x, (wkv_T, shift_T, kbuf, vbuf, conv_T)


def setup_inputs(seed: int = 0) -> dict:
    key = jax.random.key(seed)
    ks = iter(jax.random.split(key, 48))
    L = DEPTH
    win = min(WINDOW, PAST_LEN)

    def nrm(shape, scale=1.0):
        return jax.random.normal(next(ks), shape, jnp.float32) * scale

    def unif(shape, lo, hi):
        return jax.random.uniform(next(ks), shape, jnp.float32, lo, hi)

    return {
        'x_prompt': nrm((BATCH, SEQ, D_MODEL)),
        'x_sample': nrm((DEC_BATCH, DEC_SEQ, D_MODEL)),
        'state_rwkv_wkv': nrm((L, DEC_BATCH, RWKV_HEADS, HEAD_DIM, HEAD_DIM), 0.3),
        'state_rwkv_shift': nrm((L, DEC_BATCH, RWKV_COLS)),
        'cache_swa_k': nrm((L, DEC_BATCH, win, SWA_KV_HEADS, HEAD_DIM)),
        'cache_swa_v': nrm((L, DEC_BATCH, win, SWA_KV_HEADS, HEAD_DIM)),
        'state_ffn_conv': nrm((L, DEC_BATCH, CONV_WIDTH - 1, 2 * D_FF)),
        'c_prompt': nrm((BATCH, D_MODEL)),
        'c_sample': nrm((DEC_BATCH, D_MODEL)),
        'w_ada': nrm((L, D_MODEL, 6 * D_MODEL), 0.5 * D_MODEL ** -0.5),
        'b_ada': nrm((L, 6 * D_MODEL), 0.01),
        'norm_pre_mix': 1.0 + nrm((L, D_MODEL), 0.05),
        'norm_post_mix': 1.0 + nrm((L, D_MODEL), 0.05),
        'norm_pre_ffn': 1.0 + nrm((L, D_MODEL), 0.05),
        'norm_post_ffn': 1.0 + nrm((L, D_MODEL), 0.05),
        'w_in': nrm((L, D_MODEL, PROJ_COLS), D_MODEL ** -0.5),
        'rwkv_mu': unif((L, RWKV_COLS), 0.0, 1.0),
        'rwkv_w0': unif((L, RWKV_WIDTH), -5.0, -1.0),
        'rwkv_w_up': nrm((L, RWKV_DECAY_LORA, RWKV_WIDTH), RWKV_DECAY_LORA ** -0.5),
        'rwkv_a0': nrm((L, RWKV_WIDTH), 0.1),
        'rwkv_a_up': nrm((L, RWKV_A_LORA, RWKV_WIDTH), RWKV_A_LORA ** -0.5),
        'rwkv_g_up': nrm((L, RWKV_GATE_LORA, RWKV_WIDTH), RWKV_GATE_LORA ** -0.5),
        'rwkv_k_k': 0.85 + nrm((L, RWKV_WIDTH), 0.05),
        'rwkv_k_a': 1.0 + nrm((L, RWKV_WIDTH), 0.05),
        'rwkv_r_k': nrm((L, RWKV_HEADS, HEAD_DIM), 0.1),
        'rwkv_ln_w': 1.0 + nrm((L, RWKV_WIDTH), 0.05),
        'rwkv_ln_b': nrm((L, RWKV_WIDTH), 0.01),
        'swa_sinks': nrm((L, SWA_HEADS), 0.5),
        'w_out': nrm((L, MIX_WIDTH, D_MODEL), MIX_WIDTH ** -0.5),
        'ffn_w_up': nrm((L, D_MODEL, 2 * D_FF), D_MODEL ** -0.5),
        'ffn_conv_w': nrm((L, CONV_WIDTH, 2 * D_FF), CONV_WIDTH ** -0.5),
        'ffn_conv_b': nrm((L, 2 * D_FF), 0.01),
        'ffn_w_down': nrm((L, D_FF, D_MODEL), D_FF ** -0.5),
    }


def reference(x_prompt, x_sample, state_rwkv_wkv, state_rwkv_shift, cache_swa_k, cache_swa_v,
              state_ffn_conv, c_prompt, c_sample, w_ada, b_ada, norm_pre_mix, norm_post_mix,
              norm_pre_ffn, norm_post_ffn, w_in, rwkv_mu, rwkv_w0, rwkv_w_up, rwkv_a0, rwkv_a_up,
              rwkv_g_up, rwkv_k_k, rwkv_k_a, rwkv_r_k, rwkv_ln_w, rwkv_ln_b, swa_sinks, w_out,
              ffn_w_up, ffn_conv_w, ffn_conv_b, ffn_w_down):
    B, T = x_prompt.shape[0], x_prompt.shape[1]
    Td = x_sample.shape[1]
    dt = x_prompt.dtype
    pos_p = jnp.arange(T, dtype=jnp.int32)
    pos_s = PAST_LEN + jnp.arange(Td, dtype=jnp.int32)
    yp, ys = x_prompt, x_sample
    new_p, new_s = [], []
    for l in range(DEPTH):
        P = dict(w_ada=w_ada[l], b_ada=b_ada[l], norm_pre_mix=norm_pre_mix[l],
                 norm_post_mix=norm_post_mix[l], norm_pre_ffn=norm_pre_ffn[l],
                 norm_post_ffn=norm_post_ffn[l], w_in=w_in[l], rwkv_mu=rwkv_mu[l],
                 rwkv_w0=rwkv_w0[l], rwkv_w_up=rwkv_w_up[l], rwkv_a0=rwkv_a0[l],
                 rwkv_a_up=rwkv_a_up[l], rwkv_g_up=rwkv_g_up[l], rwkv_k_k=rwkv_k_k[l],
                 rwkv_k_a=rwkv_k_a[l], rwkv_r_k=rwkv_r_k[l], rwkv_ln_w=rwkv_ln_w[l],
                 rwkv_ln_b=rwkv_ln_b[l], swa_sinks=swa_sinks[l], w_out=w_out[l],
                 ffn_w_up=ffn_w_up[l], ffn_conv_w=ffn_conv_w[l], ffn_conv_b=ffn_conv_b[l],
                 ffn_w_down=ffn_w_down[l])
        yp, st_p = _layer(yp, c_prompt, pos_p,
                          jnp.zeros((B, RWKV_HEADS, HEAD_DIM, HEAD_DIM), dt),
                          jnp.zeros((B, RWKV_COLS), dt),
                          jnp.zeros((B, CONV_WIDTH - 1, 2 * D_FF), dt),
                          _swa_prompt, P)
        ys, st_s = _layer(ys, c_sample, pos_s, state_rwkv_wkv[l], state_rwkv_shift[l],
                          state_ffn_conv[l],
                          functools.partial(_swa_sample, cache_k=cache_swa_k[l], cache_v=cache_swa_v[l]),
                          P)
        new_p.append(st_p)
        new_s.append(st_s)
    wkv_p, shift_p, k_p, v_p, conv_p = [jnp.stack(a) for a in zip(*new_p)]
    wkv_s, shift_s, k_s, v_s, conv_s = [jnp.stack(a) for a in zip(*new_s)]
    return (yp, ys, wkv_p, shift_p, k_p, v_p, conv_p, wkv_s, shift_s, k_s, v_s, conv_s)
```

```python
import functools
import math

import jax
import jax.numpy as jnp
from jax import lax
from jax.experimental import pallas as pl
from jax.experimental.pallas import tpu as pltpu

D_MODEL = 1024
HEAD_DIM = 64
RWKV_WIDTH = 512
RWKV_HEADS = 8
RWKV_COLS = 1792
RWKV_GN_EPS = 64e-5
SWA_WIDTH = 512
SWA_HEADS = 8
SWA_KV_HEADS = 2
SWA_GROUP = 4
KV_WIDTH = SWA_KV_HEADS * HEAD_DIM
WINDOW = 128
PAST_LEN = 16384
ROPE_THETA = 10000.0
ATTN_SCALE = HEAD_DIM ** -0.5
D_FF = 2816
NORM_EPS = 1e-6
MASK_VALUE = -1e30
PROJ_COLS = RWKV_COLS + SWA_WIDTH + 2 * KV_WIDTH

LANES = 128
SUBLANES = 8
CHUNK = 64
GROUP_LANES = 256
FF_CHUNK = 256
N_FF_CHUNKS = D_FF // FF_CHUNK
VMEM_LIMIT = 56 * 1024 * 1024

F32 = jnp.float32
BF16 = jnp.bfloat16


def _sigmoid(x):
    return 1.0 / (1.0 + jnp.exp(-x))


def _silu(x):
    return x * _sigmoid(x)


def _rmsnorm(x, g):
    return x * lax.rsqrt(jnp.mean(x * x, axis=-1, keepdims=True) + NORM_EPS) * g


def _dot(a, b):
    return jnp.dot(a.astype(BF16), b.astype(BF16), preferred_element_type=F32)


def _dot_nt(a, b):
    return lax.dot_general(a.astype(BF16), b.astype(BF16), (((1,), (1,)), ((), ())),
                           preferred_element_type=F32)


def _dot_tn(a, b):
    return lax.dot_general(a.astype(BF16), b.astype(BF16), (((0,), (0,)), ((), ())),
                           preferred_element_type=F32)


def _swap_halves(x):
    w = x.shape[-1]
    lane = lax.broadcasted_iota(jnp.int32, x.shape, x.ndim - 1)
    lo = (lane & (HEAD_DIM // 2)) == 0
    return jnp.where(lo, pltpu.roll(x, w - HEAD_DIM // 2, x.ndim - 1),
                     pltpu.roll(x, HEAD_DIM // 2, x.ndim - 1))


def _rope(x, cos, sin):
    reps = x.shape[-1] // LANES
    cos_w = jnp.concatenate([cos] * reps, axis=-1) if reps > 1 else cos
    sin_w = jnp.concatenate([sin] * reps, axis=-1) if reps > 1 else sin
    return x * cos_w + _swap_halves(x) * sin_w


def _proj_features(x, shift, scale, g_pre, w_in, prev_fn, mu, w0, wa_up, a0, g_up,
                   k_k, k_a, r_k, seg512, cos, sin):
    h = _rmsnorm(x, g_pre) * (1.0 + scale) + shift
    p = _dot(h, w_in)
    p_rwkv = p[:, :RWKV_COLS]
    prev = prev_fn(p_rwkv)
    xm = p_rwkv + (prev - p_rwkv) * mu
    r = xm[:, 0:512]
    k = xm[:, 512:1024]
    v = xm[:, 1024:1536]
    wa = xm[:, 1536:1664]
    gd = xm[:, 1664:1792]
    lane = lax.broadcasted_iota(jnp.int32, wa.shape, 1)
    wa_act = jnp.where(lane < 64, jnp.tanh(wa), wa)
    lora = _dot(wa_act, wa_up)
    lw = -math.exp(-0.5) * _sigmoid(w0 + lora[:, :512])
    a = _sigmoid(a0 + lora[:, 512:])
    g = _dot(_sigmoid(gd), g_up)
    kk = k * k_k
    ss = _dot(kk * kk, seg512)
    kk = kk / jnp.maximum(jnp.sqrt(ss), 1e-12)
    kf = k * (1.0 + (a - 1.0) * k_a)
    bonus = _dot(r * kf * r_k, seg512) * v
    feats = (r, lw, kf, v, -kk, kk * a, g, bonus)
    q = _rope(p[:, RWKV_COLS:RWKV_COLS + SWA_WIDTH], cos, sin)
    ks = _rope(p[:, RWKV_COLS + SWA_WIDTH:RWKV_COLS + SWA_WIDTH + KV_WIDTH], cos, sin)
    vs = p[:, RWKV_COLS + SWA_WIDTH + KV_WIDTH:]
    return feats, q, ks, vs, p_rwkv


def _gn_epilogue(o, bonus, g, ln_w, ln_b, seg):
    mu = _dot(o, seg) * (1.0 / HEAD_DIM)
    d = o - mu
    var = _dot(d * d, seg) * (1.0 / HEAD_DIM)
    gn = d * lax.rsqrt(var + RWKV_GN_EPS) * ln_w + ln_b
    return (gn + bonus) * g


def _seg_ones(n):
    r = lax.broadcasted_iota(jnp.int32, (n, n), 0) // HEAD_DIM
    c = lax.broadcasted_iota(jnp.int32, (n, n), 1) // HEAD_DIM
    return (r == c).astype(BF16)


def _ada_kernel(c_ref, w_ref, b_ref, o_ref):
    o_ref[0] = _dot(_silu(c_ref[...]), w_ref[...]) + b_ref[...]


def _ada_call(c_all, w_ada, b_ada):
    rows = c_all.shape[0]
    return pl.pallas_call(
        _ada_kernel,
        grid=(6,),
        in_specs=[
            pl.BlockSpec((rows, D_MODEL), lambda j: (0, 0)),
            pl.BlockSpec((D_MODEL, D_MODEL), lambda j: (0, j)),
            pl.BlockSpec((1, D_MODEL), lambda j: (0, j)),
        ],
        out_specs=pl.BlockSpec((1, rows, D_MODEL), lambda j: (j, 0, 0)),
        out_shape=jax.ShapeDtypeStruct((6, rows, D_MODEL), F32),
        compiler_params=pltpu.CompilerParams(
            dimension_semantics=("arbitrary",), vmem_limit_bytes=VMEM_LIMIT),
        name="ada",
    )(c_all, w_ada, b_ada)


_N_PROJ_CONSTS = 13


def _proj_prompt_kernel(x_ref, mod_ref, cos_ref, sin_ref, g_pre, w_in, mu, w0, wa_up, a0,
                        g_up, k_k, k_a, r_k, seg512,
                        feat_ref, q_ref, k_ref, v_ref, plast_ref, carry_ref):
    t = pl.program_id(1)

    @pl.when(t == 0)
    def _():
        carry_ref[...] = jnp.zeros_like(carry_ref)

    tm = x_ref.shape[1]
    carry_row = carry_ref[SUBLANES - 1:SUBLANES, :]

    def prev_fn(p_rwkv):
        row = lax.broadcasted_iota(jnp.int32, p_rwkv.shape, 0)
        return jnp.where(row == 0, carry_row, pltpu.roll(p_rwkv, 1, 0))

    feats, q, ks, vs, p_rwkv = _proj_features(
        x_ref[0], mod_ref[0, 0], mod_ref[1, 0], g_pre[...], w_in[...], prev_fn, mu[...],
        w0[...], wa_up[...], a0[...], g_up[...], k_k[...], k_a[...], r_k[...], seg512[...],
        cos_ref[...], sin_ref[...])
    for i, f in enumerate(feats):
        feat_ref[0, i] = f
    q_ref[0] = (q * ATTN_SCALE).astype(BF16)
    k_ref[0] = ks
    v_ref[0] = vs
    last = p_rwkv[tm - SUBLANES:, :]
    carry_ref[...] = last
    plast_ref[0] = last


def _proj_sample_kernel(x_ref, mod_ref, cos_ref, sin_ref, prev_ref, g_pre, w_in, mu, w0,
                        wa_up, a0, g_up, k_k, k_a, r_k, seg512,
                        feat_ref, q_ref, k_ref, v_ref, p_ref):
    feats, q, ks, vs, p_rwkv = _proj_features(
        x_ref[...], mod_ref[0], mod_ref[1], g_pre[...], w_in[...], lambda p: prev_ref[...],
        mu[...], w0[...], wa_up[...], a0[...], g_up[...], k_k[...], k_a[...], r_k[...],
        seg512[...], cos_ref[...], sin_ref[...])
    for i, f in enumerate(feats):
        feat_ref[i] = f
    q_ref[...] = (q * ATTN_SCALE).astype(BF16)
    k_ref[...] = ks
    v_ref[...] = vs
    p_ref[...] = p_rwkv


def _const_spec(arr, grid_rank):
    zeros = (0,) * arr.ndim
    if grid_rank == 1:
        return pl.BlockSpec(arr.shape, lambda i: zeros)
    return pl.BlockSpec(arr.shape, lambda i, j: zeros)


def _proj_prompt_call(x, mod_p, cos, sin, consts, tm):
    b, t, _ = x.shape
    nt = t // tm
    in_specs = [
        pl.BlockSpec((1, tm, D_MODEL), lambda i, j: (i, j, 0)),
        pl.BlockSpec((6, 1, 1, D_MODEL), lambda i, j: (0, i, 0, 0)),
        pl.BlockSpec((tm, LANES), lambda i, j: (j, 0)),
        pl.BlockSpec((tm, LANES), lambda i, j: (j, 0)),
    ] + [_const_spec(c, 2) for c in consts]
    out_shape = (
        jax.ShapeDtypeStruct((b, 8, t, RWKV_WIDTH), F32),
        jax.ShapeDtypeStruct((b, t, SWA_WIDTH), BF16),
        jax.ShapeDtypeStruct((b, t, KV_WIDTH), F32),
        jax.ShapeDtypeStruct((b, t, KV_WIDTH), F32),
        jax.ShapeDtypeStruct((b, SUBLANES, RWKV_COLS), F32),
    )
    out_specs = (
        pl.BlockSpec((1, 8, tm, RWKV_WIDTH), lambda i, j: (i, 0, j, 0)),
        pl.BlockSpec((1, tm, SWA_WIDTH), lambda i, j: (i, j, 0)),
        pl.BlockSpec((1, tm, KV_WIDTH), lambda i, j: (i, j, 0)),
        pl.BlockSpec((1, tm, KV_WIDTH), lambda i, j: (i, j, 0)),
        pl.BlockSpec((1, SUBLANES, RWKV_COLS), lambda i, j: (i, 0, 0)),
    )
    return pl.pallas_call(
        _proj_prompt_kernel,
        grid=(b, nt),
        in_specs=in_specs,
        out_specs=out_specs,
        out_shape=out_shape,
        scratch_shapes=[pltpu.VMEM((SUBLANES, RWKV_COLS), F32)],
        compiler_params=pltpu.CompilerParams(
            dimension_semantics=("arbitrary", "arbitrary"), vmem_limit_bytes=VMEM_LIMIT),
        name="proj_prompt",
    )(x, mod_p, cos, sin, *consts)


def _proj_sample_call(x, mod_s, cos, sin, prev, consts):
    n = x.shape[0]
    args = (x, mod_s, cos, sin, prev) + tuple(consts)
    out_shape = (
        jax.ShapeDtypeStruct((8, n, RWKV_WIDTH), F32),
        jax.ShapeDtypeStruct((n, SWA_WIDTH), BF16),
        jax.ShapeDtypeStruct((n, KV_WIDTH), F32),
        jax.ShapeDtypeStruct((n, KV_WIDTH), F32),
        jax.ShapeDtypeStruct((n, RWKV_COLS), F32),
    )
    return pl.pallas_call(
        _proj_sample_kernel,
        grid=(1,),
        in_specs=[_const_spec(a, 1) for a in args],
        out_specs=tuple(pl.BlockSpec(s.shape, lambda i, nd=len(s.shape): (0,) * nd)
                        for s in out_shape),
        out_shape=out_shape,
        compiler_params=pltpu.CompilerParams(
            dimension_semantics=("arbitrary",), vmem_limit_bytes=VMEM_LIMIT),
        name="proj_sample",
    )(*args)


def _wkv_prompt_kernel(feat_ref, ln_w, ln_b, y_ref, s_out_ref, s_ref):
    t = pl.program_id(1)
    nt = pl.num_programs(1)
    tt = feat_ref.shape[2]
    n_chunks = tt // CHUNK
    n_groups = RWKV_WIDTH // GROUP_LANES
    gl = GROUP_LANES
    heads_per_group = gl // HEAD_DIM

    @pl.when(t == 0)
    def _():
        s_ref[...] = jnp.zeros_like(s_ref)

    row_c = lax.broadcasted_iota(jnp.int32, (CHUNK, gl), 0)
    col_c = lax.broadcasted_iota(jnp.int32, (CHUNK, gl), 1) % CHUNK
    strict = row_c > col_c
    incl = row_c >= col_c
    eye_cat = (row_c == col_c).astype(F32)
    rb = lax.broadcasted_iota(jnp.int32, (gl, gl), 0) // HEAD_DIM
    cb = lax.broadcasted_iota(jnp.int32, (gl, gl), 1) // HEAD_DIM
    bd_mask = rb == cb
    tri_r = lax.broadcasted_iota(jnp.int32, (CHUNK, CHUNK), 0)
    tri_c = lax.broadcasted_iota(jnp.int32, (CHUNK, CHUNK), 1)
    tril_ones = (tri_r >= tri_c).astype(BF16)
    seg = bd_mask.astype(BF16)

    def bd(x):
        xb = x.astype(BF16)
        return jnp.where(bd_mask, jnp.concatenate([xb] * heads_per_group, axis=0),
                         jnp.zeros((), BF16))

    def chunk_body(c, carry):
        r0 = pl.multiple_of(c * CHUNK, CHUNK)
        for gi in range(n_groups):
            ls = slice(gi * gl, (gi + 1) * gl)
            r = feat_ref[0, 0, pl.ds(r0, CHUNK), ls]
            lw = feat_ref[0, 1, pl.ds(r0, CHUNK), ls]
            kf = feat_ref[0, 2, pl.ds(r0, CHUNK), ls]
            v = feat_ref[0, 3, pl.ds(r0, CHUNK), ls]
            na = feat_ref[0, 4, pl.ds(r0, CHUNK), ls]
            bb = feat_ref[0, 5, pl.ds(r0, CHUNK), ls]
            g = feat_ref[0, 6, pl.ds(r0, CHUNK), ls]
            bonus = feat_ref[0, 7, pl.ds(r0, CHUNK), ls]

            lw_hi = lw.astype(BF16)
            lw_lo = (lw - lw_hi.astype(F32)).astype(BF16)
            cum = (jnp.dot(tril_ones, lw_hi, preferred_element_type=F32)
                   + jnp.dot(tril_ones, lw_lo, preferred_element_type=F32))
            cum_last = cum[CHUNK - 1:CHUNK, :]
            e_in = jnp.exp(cum)
            e_out = jnp.exp(-cum)
            e_end = jnp.exp(cum_last - cum)
            a_t = na * jnp.exp(cum - lw)
            r_t = r * e_in
            b_t = bb * e_out
            k_t = kf * e_out
            b_end = bb * e_end
            k_end = kf * e_end

            ar = jnp.concatenate([a_t, r_t], axis=0)
            pb = _dot_nt(ar, bd(b_t))
            pk = _dot_nt(ar, bd(k_t))
            l_ab = jnp.where(strict, pb[:CHUNK], 0.0)
            l_ak = jnp.where(strict, pk[:CHUNK], 0.0)
            m_rb = jnp.where(incl, pb[CHUNK:], 0.0)
            m_rk = jnp.where(incl, pk[CHUNK:], 0.0)

            x_acc = eye_cat + l_ab
            pw = l_ab
            n_sq = int(math.log2(CHUNK)) - 1
            for _ in range(n_sq):
                pw = _dot(pw, bd(pw))
                x_acc = x_acc + _dot(x_acc, bd(pw))
            t_inv = x_acc

            bd_v = bd(v)
            y_loc = _dot(l_ak, bd_v)
            w_t = _dot(t_inv, bd(a_t))
            u_loc = _dot(t_inv, bd(y_loc))

            s_bd = s_ref[gi]
            u = _dot_nt(w_t, s_bd) + u_loc
            o = _dot_nt(r_t, s_bd) + _dot(m_rb, bd(u)) + _dot(m_rk, bd_v)
            uv = jnp.concatenate([u, v], axis=0)
            bk = jnp.concatenate([b_end, k_end], axis=0)
            upd = _dot_tn(uv, bk)
            s_ref[gi] = s_bd * jnp.exp(cum_last) + jnp.where(bd_mask, upd, 0.0)

            y = _gn_epilogue(o, bonus, g, ln_w[:, ls], ln_b[:, ls], seg)
            y_ref[0, pl.ds(r0, CHUNK), ls] = y.astype(y_ref.dtype)
        return carry

    lax.fori_loop(0, n_chunks, chunk_body, 0)

    @pl.when(t == nt - 1)
    def _():
        for gi in range(n_groups):
            s_bd = s_ref[gi]
            acc = s_bd[0:HEAD_DIM]
            for hh in range(1, heads_per_group):
                acc = acc + s_bd[hh * HEAD_DIM:(hh + 1) * HEAD_DIM]
            s_out_ref[0, :, gi * gl:(gi + 1) * gl] = acc


def _wkv_prompt_call(feat, ln_w, ln_b, tt):
    b, _, t, _ = feat.shape
    return pl.pallas_call(
        _wkv_prompt_kernel,
        grid=(b, t // tt),
        in_specs=[
            pl.BlockSpec((1, 8, tt, RWKV_WIDTH), lambda i, j: (i, 0, j, 0)),
            pl.BlockSpec((1, RWKV_WIDTH), lambda i, j: (0, 0)),
            pl.BlockSpec((1, RWKV_WIDTH), lambda i, j: (0, 0)),
        ],
        out_specs=(
            pl.BlockSpec((1, tt, RWKV_WIDTH), lambda i, j: (i, j, 0)),
            pl.BlockSpec((1, HEAD_DIM, RWKV_WIDTH), lambda i, j: (i, 0, 0)),
        ),
        out_shape=(
            jax.ShapeDtypeStruct((b, t, RWKV_WIDTH), BF16),
            jax.ShapeDtypeStruct((b, HEAD_DIM, RWKV_WIDTH), F32),
        ),
        scratch_shapes=[pltpu.VMEM((RWKV_WIDTH // GROUP_LANES, GROUP_LANES, GROUP_LANES), F32)],
        compiler_params=pltpu.CompilerParams(
            dimension_semantics=("arbitrary", "arbitrary"), vmem_limit_bytes=VMEM_LIMIT),
        name="wkv_prompt",
    )(feat, ln_w, ln_b)


def _wkv_sample_kernel(feat_ref, vt_ref, s_ref, s_out_ref, ot_ref):
    nb = s_ref.shape[0]
    for b in range(nb):
        for h in range(RWKV_HEADS):
            ls = slice(h * HEAD_DIM, (h + 1) * HEAD_DIM)
            r = feat_ref[0, b:b + 1, ls]
            w = jnp.exp(feat_ref[1, b:b + 1, ls])
            kf = feat_ref[2, b:b + 1, ls]
            na = feat_ref[4, b:b + 1, ls]
            bb = feat_ref[5, b:b + 1, ls]
            v_col = vt_ref[0, ls, b:b + 1]
            s = s_ref[b, h]
            sa = jnp.sum(s * na, axis=1, keepdims=True)
            s_new = s * w + sa * bb + v_col * kf
            s_out_ref[b, h] = s_new
            ot_ref[0, ls, b:b + 1] = jnp.sum(s_new * r, axis=1, keepdims=True)


def _wkv_sample_call(feat_s, v_t, state, nb):
    n = state.shape[0]
    return pl.pallas_call(
        _wkv_sample_kernel,
        grid=(n // nb,),
        in_specs=[
            pl.BlockSpec((8, nb, RWKV_WIDTH), lambda i: (0, i, 0)),
            pl.BlockSpec((1, RWKV_WIDTH, nb), lambda i: (i, 0, 0)),
            pl.BlockSpec((nb, RWKV_HEADS, HEAD_DIM, HEAD_DIM), lambda i: (i, 0, 0, 0)),
        ],
        out_specs=(
            pl.BlockSpec((nb, RWKV_HEADS, HEAD_DIM, HEAD_DIM), lambda i: (i, 0, 0, 0)),
            pl.BlockSpec((1, RWKV_WIDTH, nb), lambda i: (i, 0, 0)),
        ),
        out_shape=(
            jax.ShapeDtypeStruct(state.shape, F32),
            jax.ShapeDtypeStruct((n // nb, RWKV_WIDTH, nb), F32),
        ),
        compiler_params=pltpu.CompilerParams(
            dimension_semantics=("arbitrary",), vmem_limit_bytes=VMEM_LIMIT),
        name="wkv_sample",
    )(feat_s, v_t, state)


def _swa_prompt_kernel(sink_ref, q_ref, kp_ref, kc_ref, vp_ref, vc_ref, o_ref):
    j = pl.program_id(1)
    w = WINDOW
    q = q_ref[0]
    k2 = jnp.concatenate([kp_ref[0], kc_ref[0]], axis=0)
    v2 = jnp.concatenate([vp_ref[0], vc_ref[0]], axis=0)
    rows = SWA_GROUP * w
    qi = lax.broadcasted_iota(jnp.int32, (rows, 2 * w), 0) % w
    ki = lax.broadcasted_iota(jnp.int32, (rows, 2 * w), 1)
    diff = qi - (ki - w)
    kpos_ok = (j * w + ki - w) >= 0
    valid = (diff >= 0) & (diff < WINDOW) & kpos_ok
    outs = []
    for g in range(SWA_KV_HEADS):
        kg = k2[:, g * HEAD_DIM:(g + 1) * HEAD_DIM]
        vg = v2[:, g * HEAD_DIM:(g + 1) * HEAD_DIM]
        heads = range(g * SWA_GROUP, (g + 1) * SWA_GROUP)
        qg = jnp.concatenate([q[:, h * HEAD_DIM:(h + 1) * HEAD_DIM] for h in heads], axis=0)
        s = _dot_nt(qg, kg)
        s = jnp.where(valid, s, MASK_VALUE)
        sink = jnp.concatenate(
            [jnp.full((w, 1), sink_ref[h], F32) for h in heads], axis=0)
        m = jnp.maximum(jnp.max(s, axis=-1, keepdims=True), sink)
        e = jnp.exp(s - m)
        prob = e / (jnp.sum(e, axis=-1, keepdims=True) + jnp.exp(sink - m))
        og = _dot(prob, vg)
        outs.extend(og[i * w:(i + 1) * w] for i in range(SWA_GROUP))
    o_ref[0] = jnp.concatenate(outs, axis=-1).astype(o_ref.dtype)


def _swa_prompt_call(sinks, q, k, v):
    b, t, _ = q.shape
    w = WINDOW
    prev = lambda i, j: (i, jnp.maximum(j - 1, 0), 0)
    cur = lambda i, j: (i, j, 0)
    return pl.pallas_call(
        _swa_prompt_kernel,
        grid=(b, t // w),
        in_specs=[
            pl.BlockSpec(memory_space=pltpu.SMEM),
            pl.BlockSpec((1, w, SWA_WIDTH), cur),
            pl.BlockSpec((1, w, KV_WIDTH), prev),
            pl.BlockSpec((1, w, KV_WIDTH), cur),
            pl.BlockSpec((1, w, KV_WIDTH), prev),
            pl.BlockSpec((1, w, KV_WIDTH), cur),
        ],
        out_specs=pl.BlockSpec((1, w, SWA_WIDTH), cur),
        out_shape=jax.ShapeDtypeStruct((b, t, SWA_WIDTH), BF16),
        compiler_params=pltpu.CompilerParams(
            dimension_semantics=("arbitrary", "arbitrary"), vmem_limit_bytes=VMEM_LIMIT),
        name="swa_prompt",
    )(sinks, q, k, k, v, v)


def _swa_sample_kernel(sink_ref, qbd_ref, kn_ref, vn_ref, ck_ref, cv_ref,
                       o_ref, ko_ref, vo_ref):
    l = ck_ref.shape[1]
    qbd = qbd_ref[...]
    kn = kn_ref[...]
    vn = vn_ref[...]
    ck = ck_ref[...]
    cv = cv_ref[...]
    ki = lax.broadcasted_iota(jnp.int32, (1, 1, l), 2)
    kpos = PAST_LEN - l + ki
    diff = PAST_LEN - kpos
    valid = (diff >= 0) & (diff < WINDOW) & (kpos >= 0)
    s_c = lax.dot_general(qbd, ck.astype(BF16), (((2,), (2,)), ((0,), (0,))),
                          preferred_element_type=F32)
    s_c = jnp.where(valid, s_c, MASK_VALUE)
    s_n = jnp.sum(qbd.astype(F32) * kn.astype(BF16).astype(F32)[:, None, :],
                  axis=-1, keepdims=True)
    sink = sink_ref[...][None, :, 0:1]
    m = jnp.maximum(jnp.maximum(jnp.max(s_c, axis=-1, keepdims=True), s_n), sink)
    e_c = jnp.exp(s_c - m)
    e_n = jnp.exp(s_n - m)
    denom = jnp.sum(e_c, axis=-1, keepdims=True) + e_n + jnp.exp(sink - m)
    p_c = (e_c / denom).astype(BF16)
    p_n = (e_n / denom).astype(BF16).astype(F32)
    o = lax.dot_general(p_c, cv.astype(BF16), (((2,), (1,)), ((0,), (0,))),
                        preferred_element_type=F32)
    o = o + p_n * vn.astype(BF16).astype(F32)[:, None, :]
    o_ref[...] = o
    ko_ref[:, 0:l - 1, :] = ck_ref[:, 1:l, :]
    ko_ref[:, l - 1:l, :] = kn[:, None, :]
    vo_ref[:, 0:l - 1, :] = cv_ref[:, 1:l, :]
    vo_ref[:, l - 1:l, :] = vn[:, None, :]


def _swa_sample_call(sink_b, qbd, kn, vn, ck, cv, nb):
    n, l, _ = ck.shape
    return pl.pallas_call(
        _swa_sample_kernel,
        grid=(n // nb,),
        in_specs=[
            pl.BlockSpec((SWA_HEADS, LANES), lambda i: (0, 0)),
            pl.BlockSpec((nb, SWA_HEADS, KV_WIDTH), lambda i: (i, 0, 0)),
            pl.BlockSpec((nb, KV_WIDTH), lambda i: (i, 0)),
            pl.BlockSpec((nb, KV_WIDTH), lambda i: (i, 0)),
            pl.BlockSpec((nb, l, KV_WIDTH), lambda i: (i, 0, 0)),
            pl.BlockSpec((nb, l, KV_WIDTH), lambda i: (i, 0, 0)),
        ],
        out_specs=(
            pl.BlockSpec((nb, SWA_HEADS, KV_WIDTH), lambda i: (i, 0, 0)),
            pl.BlockSpec((nb, l, KV_WIDTH), lambda i: (i, 0, 0)),
            pl.BlockSpec((nb, l, KV_WIDTH), lambda i: (i, 0, 0)),
        ),
        out_shape=(
            jax.ShapeDtypeStruct((n, SWA_HEADS, KV_WIDTH), F32),
            jax.ShapeDtypeStruct((n, l, KV_WIDTH), F32),
            jax.ShapeDtypeStruct((n, l, KV_WIDTH), F32),
        ),
        compiler_params=pltpu.CompilerParams(
            dimension_semantics=("arbitrary",), vmem_limit_bytes=VMEM_LIMIT),
        name="swa_sample",
    )(sink_b, qbd, kn, vn, ck, cv)


def _tail_mix(x, y_rwkv, y_swa, g1, sh2, sc2, w_out, n_post_mix, n_pre_ffn):
    mix = _dot(y_rwkv, w_out[:RWKV_WIDTH]) + _dot(y_swa, w_out[RWKV_WIDTH:])
    x1 = x + g1 * _rmsnorm(mix, n_post_mix)
    h2 = (_rmsnorm(x1, n_pre_ffn) * (1.0 + sc2) + sh2).astype(BF16)
    return x1, h2


def _tail_prompt_kernel(x_ref, yr_ref, ys_ref, mod_ref, w_out, n_post_mix, n_pre_ffn,
                        n_post_ffn, wa_ref, wb_ref, wd_ref, cw_ref, cb_ref,
                        y_ref, cp_ref, acc_ref, carry_ref):
    t = pl.program_id(1)
    tm = x_ref.shape[1]

    @pl.when(t == 0)
    def _():
        carry_ref[...] = jnp.zeros_like(carry_ref)

    g1, sh2, sc2, g2 = mod_ref[2, 0], mod_ref[3, 0], mod_ref[4, 0], mod_ref[5, 0]
    x1, h2 = _tail_mix(x_ref[0], yr_ref[0], ys_ref[0], g1, sh2, sc2, w_out[...],
                       n_post_mix[...], n_pre_ffn[...])
    acc_ref[...] = jnp.zeros_like(acc_ref)

    def conv(u, carry, cw, cb):
        ext = jnp.concatenate([carry, u], axis=0)
        p1 = pltpu.roll(ext, 1, 0)[SUBLANES:]
        p2 = pltpu.roll(ext, 2, 0)[SUBLANES:]
        return cb + p2 * cw[0:1] + p1 * cw[1:2] + u * cw[2:3]

    def body(c, carry):
        ua = jnp.dot(h2, wa_ref[c], preferred_element_type=F32)
        ub = jnp.dot(h2, wb_ref[c], preferred_element_type=F32)
        za = conv(ua, carry_ref[0, c], cw_ref[0, c], cb_ref[0, c])
        zb = conv(ub, carry_ref[1, c], cw_ref[1, c], cb_ref[1, c])
        la = ua[tm - SUBLANES:]
        lb = ub[tm - SUBLANES:]
        carry_ref[0, c] = la
        carry_ref[1, c] = lb
        cp_ref[0, 0, c] = la
        cp_ref[0, 1, c] = lb
        acc_ref[...] += _dot(_silu(za) * zb, wd_ref[c])
        return carry

    lax.fori_loop(0, N_FF_CHUNKS, body, 0)
    y_ref[0] = x1 + g2 * _rmsnorm(acc_ref[...], n_post_ffn[...])


def _tail_sample_kernel(x_ref, o_ref, feat_ref, ys_ref, mod_ref, p0_ref, p1_ref, ln_w, ln_b,
                        w_out, n_post_mix, n_pre_ffn, n_post_ffn, wa_ref, wb_ref, wd_ref,
                        cw_ref, cb_ref, y_ref, u_ref, acc_ref):
    seg = _seg_ones(RWKV_WIDTH)
    y_rwkv = _gn_epilogue(o_ref[...], feat_ref[7], feat_ref[6], ln_w[...], ln_b[...], seg)
    g1, sh2, sc2, g2 = mod_ref[2], mod_ref[3], mod_ref[4], mod_ref[5]
    x1, h2 = _tail_mix(x_ref[...], y_rwkv, ys_ref[...], g1, sh2, sc2, w_out[...],
                       n_post_mix[...], n_pre_ffn[...])
    acc_ref[...] = jnp.zeros_like(acc_ref)

    def body(c, carry):
        ua = jnp.dot(h2, wa_ref[c], preferred_element_type=F32)
        ub = jnp.dot(h2, wb_ref[c], preferred_element_type=F32)
        cwa, cwb = cw_ref[0, c], cw_ref[1, c]
        za = cb_ref[0, c] + p0_ref[0, c] * cwa[0:1] + p1_ref[0, c] * cwa[1:2] + ua * cwa[2:3]
        zb = cb_ref[1, c] + p0_ref[1, c] * cwb[0:1] + p1_ref[1, c] * cwb[1:2] + ub * cwb[2:3]
        u_ref[0, c] = ua
        u_ref[1, c] = ub
        acc_ref[...] += _dot(_silu(za) * zb, wd_ref[c])
        return carry

    lax.fori_loop(0, N_FF_CHUNKS, body, 0)
    y_ref[...] = x1 + g2 * _rmsnorm(acc_ref[...], n_post_ffn[...])


def _tail_prompt_call(x, y_rwkv, y_swa, mod_p, consts, tm):
    b, t, _ = x.shape
    tok = lambda i, j: (i, j, 0)
    in_specs = [
        pl.BlockSpec((1, tm, D_MODEL), tok),
        pl.BlockSpec((1, tm, RWKV_WIDTH), tok),
        pl.BlockSpec((1, tm, SWA_WIDTH), tok),
        pl.BlockSpec((6, 1, 1, D_MODEL), lambda i, j: (0, i, 0, 0)),
    ] + [_const_spec(c, 2) for c in consts]
    return pl.pallas_call(
        _tail_prompt_kernel,
        grid=(b, t // tm),
        in_specs=in_specs,
        out_specs=(
            pl.BlockSpec((1, tm, D_MODEL), tok),
            pl.BlockSpec((1, 2, N_FF_CHUNKS, SUBLANES, FF_CHUNK), lambda i, j: (i, 0, 0, 0, 0)),
        ),
        out_shape=(
            jax.ShapeDtypeStruct((b, t, D_MODEL), F32),
            jax.ShapeDtypeStruct((b, 2, N_FF_CHUNKS, SUBLANES, FF_CHUNK), F32),
        ),
        scratch_shapes=[
            pltpu.VMEM((tm, D_MODEL), F32),
            pltpu.VMEM((2, N_FF_CHUNKS, SUBLANES, FF_CHUNK), F32),
        ],
        compiler_params=pltpu.CompilerParams(
            dimension_semantics=("arbitrary", "arbitrary"), vmem_limit_bytes=VMEM_LIMIT),
        name="tail_prompt",
    )(x, y_rwkv, y_swa, mod_p, *consts)


def _tail_sample_call(x, o_s, feat_s, y_swa, mod_s, p0, p1, ln_w, ln_b, consts):
    n = x.shape[0]
    args = (x, o_s, feat_s, y_swa, mod_s, p0, p1, ln_w, ln_b) + tuple(consts)
    out_shape = (
        jax.ShapeDtypeStruct((n, D_MODEL), F32),
        jax.ShapeDtypeStruct((2, N_FF_CHUNKS, n, FF_CHUNK), F32),
    )
    return pl.pallas_call(
        _tail_sample_kernel,
        grid=(1,),
        in_specs=[_const_spec(a, 1) for a in args],
        out_specs=tuple(pl.BlockSpec(s.shape, lambda i, nd=len(s.shape): (0,) * nd)
                        for s in out_shape),
        out_shape=out_shape,
        scratch_shapes=[pltpu.VMEM((n, D_MODEL), F32)],
        compiler_params=pltpu.CompilerParams(
            dimension_semantics=("arbitrary",), vmem_limit_bytes=VMEM_LIMIT),
        name="tail_sample",
    )(*args)


def _ff_chunks(w_cols):
    lead = w_cols.shape[:-1]
    x = w_cols.reshape(lead + (2, N_FF_CHUNKS, FF_CHUNK))
    return jnp.moveaxis(x, (-3, -2), (0, 1))


def _ff_unchunk(x):
    y = jnp.moveaxis(x, (0, 1), (-3, -2))
    return y.reshape(y.shape[:-3] + (2 * D_FF,))


def _rope_tables(pos):
    half = HEAD_DIM // 2
    inv = ROPE_THETA ** (-jnp.arange(half, dtype=F32) / half)
    ang = pos.astype(F32)[:, None] * inv[None, :]
    cos, sin = jnp.cos(ang), jnp.sin(ang)
    cos_h = jnp.concatenate([cos, cos], axis=-1)
    sin_h = jnp.concatenate([-sin, sin], axis=-1)
    return jnp.tile(cos_h, (1, LANES // HEAD_DIM)), jnp.tile(sin_h, (1, LANES // HEAD_DIM))


def kernel(x_prompt, x_sample, state_rwkv_wkv, state_rwkv_shift, cache_swa_k, cache_swa_v,
           state_ffn_conv, c_prompt, c_sample, w_ada, b_ada, norm_pre_mix, norm_post_mix,
           norm_pre_ffn, norm_post_ffn, w_in, rwkv_mu, rwkv_w0, rwkv_w_up, rwkv_a0, rwkv_a_up,
           rwkv_g_up, rwkv_k_k, rwkv_k_a, rwkv_r_k, rwkv_ln_w, rwkv_ln_b, swa_sinks, w_out,
           ffn_w_up, ffn_conv_w, ffn_conv_b, ffn_w_down):
    depth = w_ada.shape[0]
    assert depth == 1 and x_sample.shape[1] == 1
    b, t, _ = x_prompt.shape
    n = x_sample.shape[0]
    l = cache_swa_k.shape[2]
    tm = min(256, t)
    assert t % tm == 0 and t % WINDOW == 0 and tm % CHUNK == 0
    nb_wkv = 8 if n % 8 == 0 else n
    nb_swa = 8 if n % 8 == 0 else n
    li = 0

    row = lambda v: v.reshape(1, -1)
    w_in_b = w_in[li].astype(BF16)
    zeros_l = jnp.zeros((64, RWKV_WIDTH), F32)
    wa_up = jnp.concatenate([
        jnp.concatenate([rwkv_w_up[li], zeros_l], axis=1),
        jnp.concatenate([zeros_l, rwkv_a_up[li]], axis=1)], axis=0).astype(BF16)
    hid = jnp.arange(RWKV_WIDTH) // HEAD_DIM
    seg512 = (hid[:, None] == hid[None, :]).astype(BF16)
    proj_consts = (row(norm_pre_mix[li]), w_in_b, row(rwkv_mu[li]), row(rwkv_w0[li]), wa_up,
                   row(rwkv_a0[li]), rwkv_g_up[li].astype(BF16), row(rwkv_k_k[li]),
                   row(rwkv_k_a[li]), row(rwkv_r_k[li]), seg512)
    w_up_c = _ff_chunks(ffn_w_up[li].astype(BF16))
    w_down_c = ffn_w_down[li].astype(BF16).reshape(N_FF_CHUNKS, FF_CHUNK, D_MODEL)
    cw_c = _ff_chunks(ffn_conv_w[li])
    cb_c = _ff_chunks(ffn_conv_b[li].reshape(1, -1))
    tail_consts = (w_out[li].astype(BF16), row(norm_post_mix[li]), row(norm_pre_ffn[li]),
                   row(norm_post_ffn[li]), w_up_c[0], w_up_c[1], w_down_c, cw_c, cb_c)
    ln_w, ln_b = row(rwkv_ln_w[li]), row(rwkv_ln_b[li])

    mod = _ada_call(jnp.concatenate([c_prompt, c_sample], axis=0), w_ada[li], row(b_ada[li]))
    mod_p = mod[:, :b].reshape(6, b, 1, D_MODEL)
    mod_s = mod[:, b:]

    cos_p, sin_p = _rope_tables(jnp.arange(t, dtype=jnp.int32))
    feat_p, q_p, kr_p, vv_p, plast = _proj_prompt_call(x_prompt, mod_p, cos_p, sin_p,
                                                      proj_consts, tm)
    y_rwkv_p, s_cat = _wkv_prompt_call(feat_p, ln_w, ln_b, tm)
    y_swa_p = _swa_prompt_call(swa_sinks[li], q_p, kr_p, vv_p)
    y_p, cp = _tail_prompt_call(x_prompt, y_rwkv_p, y_swa_p, mod_p, tail_consts, tm)

    wkv_p = s_cat.reshape(b, HEAD_DIM, RWKV_HEADS, HEAD_DIM).transpose(0, 2, 1, 3)
    shift_p = plast[:, SUBLANES - 1]
    keep = min(WINDOW, t)
    k_p = kr_p[:, t - keep:].reshape(b, keep, SWA_KV_HEADS, HEAD_DIM)
    v_p = vv_p[:, t - keep:].reshape(b, keep, SWA_KV_HEADS, HEAD_DIM)
    conv_p = _ff_unchunk(cp.transpose(1, 2, 0, 3, 4))[:, SUBLANES - 2:]

    cos_s, sin_s = _rope_tables(jnp.full((1,), PAST_LEN, jnp.int32))
    feat_s, q_s, kn_s, vn_s, p_s = _proj_sample_call(
        x_sample[:, 0], mod_s, cos_s, sin_s, state_rwkv_shift[li], proj_consts)
    v_t = feat_s[3].reshape(n // nb_wkv, nb_wkv, RWKV_WIDTH).transpose(0, 2, 1)
    wkv_s, o_t = _wkv_sample_call(feat_s, v_t, state_rwkv_wkv[li], nb_wkv)
    o_s = o_t.transpose(0, 2, 1).reshape(n, RWKV_WIDTH)

    head_group = jnp.arange(SWA_HEADS) // SWA_GROUP
    sel = (head_group[:, None] == jnp.arange(SWA_KV_HEADS)[None, :])
    q4 = q_s.reshape(n, SWA_HEADS, 1, HEAD_DIM)
    qbd = jnp.where(sel[None, :, :, None], q4, jnp.zeros((), BF16)).reshape(n, SWA_HEADS, KV_WIDTH)
    sink_b = jnp.broadcast_to(swa_sinks[li][:, None], (SWA_HEADS, LANES))
    ck = cache_swa_k[li].reshape(n, l, KV_WIDTH)
    cv = cache_swa_v[li].reshape(n, l, KV_WIDTH)
    o_att, k_s, v_s = _swa_sample_call(sink_b, qbd, kn_s, vn_s, ck, cv, nb_swa)
    o4 = o_att.reshape(n, SWA_HEADS, SWA_KV_HEADS, HEAD_DIM)
    y_swa_s = jnp.sum(jnp.where(sel[None, :, :, None], o4, 0.0), axis=2).reshape(n, SWA_WIDTH)

    conv0 = state_ffn_conv[li]
    p0 = _ff_chunks(conv0[:, 0])
    p1 = _ff_chunks(conv0[:, 1])
    y_s, u_s = _tail_sample_call(x_sample[:, 0], o_s, feat_s, y_swa_s.astype(BF16), mod_s,
                                 p0, p1, ln_w, ln_b, tail_consts)
    conv_s = jnp.stack([conv0[:, 1], _ff_unchunk(u_s)], axis=1)

    expand = lambda a: a[None]
    return (y_p, y_s[:, None, :], expand(wkv_p), expand(shift_p), expand(k_p), expand(v_p),
            expand(conv_p), expand(wkv_s), expand(p_s),
            expand(k_s.reshape(n, l, SWA_KV_HEADS, HEAD_DIM)),
            expand(v_s.reshape(n, l, SWA_KV_HEADS, HEAD_DIM)), expand(conv_s))
```

```python
import math

import jax
import jax.numpy as jnp
from jax import lax
from jax.experimental import pallas as pl
from jax.experimental.pallas import tpu as pltpu

D_MODEL = 1024
HEAD_DIM = 64
RWKV_WIDTH = 512
RWKV_HEADS = 8
RWKV_COLS = 1792
RWKV_GN_EPS = 64e-5
SWA_WIDTH = 512
SWA_HEADS = 8
SWA_KV_HEADS = 2
SWA_GROUP = 4
KV_WIDTH = SWA_KV_HEADS * HEAD_DIM
WINDOW = 128
PAST_LEN = 16384
ROPE_THETA = 10000.0
ATTN_SCALE = HEAD_DIM ** -0.5
D_FF = 2816
NORM_EPS = 1e-6
MASK_VALUE = -1e30
PROJ_COLS = RWKV_COLS + SWA_WIDTH + 2 * KV_WIDTH

LANES = 128
SUBLANES = 8
CHUNK = 64
GROUP_LANES = 256
FF_CHUNK = 256
N_FF_CHUNKS = D_FF // FF_CHUNK
FF_ROWS = 32
VMEM_LIMIT = 56 * 1024 * 1024

F32 = jnp.float32
BF16 = jnp.bfloat16


def _sigmoid(x):
    return 1.0 / (1.0 + jnp.exp(-x))


def _silu(x):
    return x * _sigmoid(x)


def _rmsnorm(x, g):
    return x * lax.rsqrt(jnp.mean(x * x, axis=-1, keepdims=True) + NORM_EPS) * g


def _dot(a, b):
    return jnp.dot(a.astype(BF16), b.astype(BF16), preferred_element_type=F32)


def _dot_nt(a, b):
    return lax.dot_general(a.astype(BF16), b.astype(BF16), (((1,), (1,)), ((), ())),
                           preferred_element_type=F32)


def _dot_tn(a, b):
    return lax.dot_general(a.astype(BF16), b.astype(BF16), (((0,), (0,)), ((), ())),
                           preferred_element_type=F32)


def _every(fn, *lists):
    return [fn(*a) for a in zip(*lists)]


def _swap_halves(x):
    w = x.shape[-1]
    lane = lax.broadcasted_iota(jnp.int32, x.shape, x.ndim - 1)
    lo = (lane & (HEAD_DIM // 2)) == 0
    return jnp.where(lo, pltpu.roll(x, w - HEAD_DIM // 2, x.ndim - 1),
                     pltpu.roll(x, HEAD_DIM // 2, x.ndim - 1))


def _rope(x, cos, sin):
    reps = x.shape[-1] // LANES
    cos_w = jnp.concatenate([cos] * reps, axis=-1) if reps > 1 else cos
    sin_w = jnp.concatenate([sin] * reps, axis=-1) if reps > 1 else sin
    return x * cos_w + _swap_halves(x) * sin_w


def _proj_features(x, shift, scale, g_pre, w_in, prev_fn, mu, w0, wa_up, a0, g_up,
                   k_k, k_a, r_k, seg512, cos, sin):
    h = _rmsnorm(x, g_pre) * (1.0 + scale) + shift
    p = _dot(h, w_in)
    p_rwkv = p[:, :RWKV_COLS]
    prev = prev_fn(p_rwkv)
    xm = p_rwkv + (prev - p_rwkv) * mu
    r = xm[:, 0:512]
    k = xm[:, 512:1024]
    v = xm[:, 1024:1536]
    wa = xm[:, 1536:1664]
    gd = xm[:, 1664:1792]
    lane = lax.broadcasted_iota(jnp.int32, wa.shape, 1)
    wa_act = jnp.where(lane < 64, jnp.tanh(wa), wa)
    lora = _dot(wa_act, wa_up)
    lw = -math.exp(-0.5) * _sigmoid(w0 + lora[:, :512])
    a = _sigmoid(a0 + lora[:, 512:])
    g = _dot(_sigmoid(gd), g_up)
    kk = k * k_k
    ss = _dot(kk * kk, seg512)
    kk = kk / jnp.maximum(jnp.sqrt(ss), 1e-12)
    kf = k * (1.0 + (a - 1.0) * k_a)
    bonus = _dot(r * kf * r_k, seg512) * v
    feats = (r, lw, kf, v, -kk, kk * a, g, bonus)
    q = _rope(p[:, RWKV_COLS:RWKV_COLS + SWA_WIDTH], cos, sin)
    ks = _rope(p[:, RWKV_COLS + SWA_WIDTH:RWKV_COLS + SWA_WIDTH + KV_WIDTH], cos, sin)
    vs = p[:, RWKV_COLS + SWA_WIDTH + KV_WIDTH:]
    return feats, q, ks, vs, p_rwkv


def _gn_epilogue(o, bonus, g, ln_w, ln_b, seg):
    mu = _dot(o, seg) * (1.0 / HEAD_DIM)
    d = o - mu
    var = _dot(d * d, seg) * (1.0 / HEAD_DIM)
    gn = d * lax.rsqrt(var + RWKV_GN_EPS) * ln_w + ln_b
    return (gn + bonus) * g


def _seg_ones(n):
    r = lax.broadcasted_iota(jnp.int32, (n, n), 0) // HEAD_DIM
    c = lax.broadcasted_iota(jnp.int32, (n, n), 1) // HEAD_DIM
    return (r == c).astype(BF16)


def _ada_kernel(c_ref, w_ref, b_ref, o_ref):
    o_ref[0] = _dot(_silu(c_ref[...]), w_ref[...]) + b_ref[...]


def _ada_call(c_all, w_ada, b_ada):
    rows = c_all.shape[0]
    return pl.pallas_call(
        _ada_kernel,
        grid=(6,),
        in_specs=[
            pl.BlockSpec((rows, D_MODEL), lambda j: (0, 0)),
            pl.BlockSpec((D_MODEL, D_MODEL), lambda j: (0, j)),
            pl.BlockSpec((1, D_MODEL), lambda j: (0, j)),
        ],
        out_specs=pl.BlockSpec((1, rows, D_MODEL), lambda j: (j, 0, 0)),
        out_shape=jax.ShapeDtypeStruct((6, rows, D_MODEL), F32),
        compiler_params=pltpu.CompilerParams(
            dimension_semantics=("arbitrary",), vmem_limit_bytes=VMEM_LIMIT),
        name="ada",
    )(c_all, w_ada, b_ada)


def _expand_kv(x):
    lane = lax.broadcasted_iota(jnp.int32, x.shape, 1)
    rolled = pltpu.roll(x, HEAD_DIM, 1)
    g0 = jnp.where(lane < HEAD_DIM, x, rolled)
    g1 = jnp.where(lane < HEAD_DIM, rolled, x)
    return jnp.concatenate([g0, g0, g1, g1], axis=1)


def _proj_prompt_kernel(x_ref, mod_ref, cos_ref, sin_ref, g_pre, w_in, mu, w0, wa_up, a0,
                        g_up, k_k, k_a, r_k, seg512,
                        feat_ref, q_ref, k_ref, v_ref, klast_ref, vlast_ref, plast_ref,
                        carry_ref):
    t = pl.program_id(1)

    @pl.when(t == 0)
    def _():
        carry_ref[...] = jnp.zeros_like(carry_ref)

    tm = x_ref.shape[1]
    carry_row = carry_ref[SUBLANES - 1:SUBLANES, :]

    def prev_fn(p_rwkv):
        row = lax.broadcasted_iota(jnp.int32, p_rwkv.shape, 0)
        return jnp.where(row == 0, carry_row, pltpu.roll(p_rwkv, 1, 0))

    feats, q, ks, vs, p_rwkv = _proj_features(
        x_ref[0], mod_ref[0, 0], mod_ref[1, 0], g_pre[...], w_in[...], prev_fn, mu[...],
        w0[...], wa_up[...], a0[...], g_up[...], k_k[...], k_a[...], r_k[...], seg512[...],
        cos_ref[...], sin_ref[...])
    for i, f in enumerate(feats):
        feat_ref[0, i] = f
    q_ref[0] = (q * ATTN_SCALE).astype(BF16)
    k_ref[0] = _expand_kv(ks).astype(BF16)
    v_ref[0] = _expand_kv(vs).astype(BF16)
    last = p_rwkv[tm - SUBLANES:, :]
    carry_ref[...] = last
    plast_ref[0] = last

    @pl.when(t == pl.num_programs(1) - 1)
    def _():
        klast_ref[0] = ks[tm - WINDOW:, :]
        vlast_ref[0] = vs[tm - WINDOW:, :]


def _proj_sample_kernel(x_ref, mod_ref, cos_ref, sin_ref, prev_ref, g_pre, w_in, mu, w0,
                        wa_up, a0, g_up, k_k, k_a, r_k, seg512,
                        feat_ref, q_ref, k_ref, v_ref, p_ref):
    feats, q, ks, vs, p_rwkv = _proj_features(
        x_ref[...], mod_ref[0], mod_ref[1], g_pre[...], w_in[...], lambda p: prev_ref[...],
        mu[...], w0[...], wa_up[...], a0[...], g_up[...], k_k[...], k_a[...], r_k[...],
        seg512[...], cos_ref[...], sin_ref[...])
    for i, f in enumerate(feats):
        feat_ref[i] = f
    q_ref[...] = (q * ATTN_SCALE).astype(BF16)
    k_ref[...] = ks
    v_ref[...] = vs
    p_ref[...] = p_rwkv


def _const_spec(arr, grid_rank):
    zeros = (0,) * arr.ndim
    if grid_rank == 1:
        return pl.BlockSpec(arr.shape, lambda i: zeros)
    return pl.BlockSpec(arr.shape, lambda i, j: zeros)


def _proj_prompt_call(x, mod_p, cos, sin, consts, tm):
    b, t, _ = x.shape
    nt = t // tm
    in_specs = [
        pl.BlockSpec((1, tm, D_MODEL), lambda i, j: (i, j, 0)),
        pl.BlockSpec((6, 1, 1, D_MODEL), lambda i, j: (0, i, 0, 0)),
        pl.BlockSpec((tm, LANES), lambda i, j: (j, 0)),
        pl.BlockSpec((tm, LANES), lambda i, j: (j, 0)),
    ] + [_const_spec(c, 2) for c in consts]
    out_shape = (
        jax.ShapeDtypeStruct((b, 8, t, RWKV_WIDTH), F32),
        jax.ShapeDtypeStruct((b, t, SWA_WIDTH), BF16),
        jax.ShapeDtypeStruct((b, t, SWA_WIDTH), BF16),
        jax.ShapeDtypeStruct((b, t, SWA_WIDTH), BF16),
        jax.ShapeDtypeStruct((b, WINDOW, KV_WIDTH), F32),
        jax.ShapeDtypeStruct((b, WINDOW, KV_WIDTH), F32),
        jax.ShapeDtypeStruct((b, SUBLANES, RWKV_COLS), F32),
    )
    out_specs = (
        pl.BlockSpec((1, 8, tm, RWKV_WIDTH), lambda i, j: (i, 0, j, 0)),
        pl.BlockSpec((1, tm, SWA_WIDTH), lambda i, j: (i, j, 0)),
        pl.BlockSpec((1, tm, SWA_WIDTH), lambda i, j: (i, j, 0)),
        pl.BlockSpec((1, tm, SWA_WIDTH), lambda i, j: (i, j, 0)),
        pl.BlockSpec((1, WINDOW, KV_WIDTH), lambda i, j: (i, 0, 0)),
        pl.BlockSpec((1, WINDOW, KV_WIDTH), lambda i, j: (i, 0, 0)),
        pl.BlockSpec((1, SUBLANES, RWKV_COLS), lambda i, j: (i, 0, 0)),
    )
    return pl.pallas_call(
        _proj_prompt_kernel,
        grid=(b, nt),
        in_specs=in_specs,
        out_specs=out_specs,
        out_shape=out_shape,
        scratch_shapes=[pltpu.VMEM((SUBLANES, RWKV_COLS), F32)],
        compiler_params=pltpu.CompilerParams(
            dimension_semantics=("arbitrary", "arbitrary"), vmem_limit_bytes=VMEM_LIMIT),
        name="proj_prompt",
    )(x, mod_p, cos, sin, *consts)


def _proj_sample_call(x, mod_s, cos, sin, prev, consts):
    n = x.shape[0]
    args = (x, mod_s, cos, sin, prev) + tuple(consts)
    out_shape = (
        jax.ShapeDtypeStruct((8, n, RWKV_WIDTH), F32),
        jax.ShapeDtypeStruct((n, SWA_WIDTH), BF16),
        jax.ShapeDtypeStruct((n, KV_WIDTH), F32),
        jax.ShapeDtypeStruct((n, KV_WIDTH), F32),
        jax.ShapeDtypeStruct((n, RWKV_COLS), F32),
    )
    return pl.pallas_call(
        _proj_sample_kernel,
        grid=(1,),
        in_specs=[_const_spec(a, 1) for a in args],
        out_specs=tuple(pl.BlockSpec(s.shape, lambda i, nd=len(s.shape): (0,) * nd)
                        for s in out_shape),
        out_shape=out_shape,
        compiler_params=pltpu.CompilerParams(
            dimension_semantics=("arbitrary",), vmem_limit_bytes=VMEM_LIMIT),
        name="proj_sample",
    )(*args)


def _wkv_prompt_kernel(feat_ref, ln_w, ln_b, y_ref, s_out_ref, s_ref):
    t = pl.program_id(1)
    nt = pl.num_programs(1)
    tt = feat_ref.shape[2]
    n_chunks = tt // CHUNK
    gl = GROUP_LANES
    n_groups = RWKV_WIDTH // gl
    heads_per_group = gl // HEAD_DIM
    probs = [(c, g) for c in range(n_chunks) for g in range(n_groups)]

    @pl.when(t == 0)
    def _():
        s_ref[...] = jnp.zeros_like(s_ref)

    row_c = lax.broadcasted_iota(jnp.int32, (CHUNK, gl), 0)
    col_c = lax.broadcasted_iota(jnp.int32, (CHUNK, gl), 1) % CHUNK
    strict = row_c > col_c
    incl = row_c >= col_c
    eye_cat = (row_c == col_c).astype(F32)
    rb = lax.broadcasted_iota(jnp.int32, (gl, gl), 0) // HEAD_DIM
    cb = lax.broadcasted_iota(jnp.int32, (gl, gl), 1) // HEAD_DIM
    bd_mask = rb == cb
    tri_r = lax.broadcasted_iota(jnp.int32, (CHUNK, CHUNK), 0)
    tri_c = lax.broadcasted_iota(jnp.int32, (CHUNK, CHUNK), 1)
    tril_ones = (tri_r >= tri_c).astype(BF16)
    seg = bd_mask.astype(BF16)

    def bd(x):
        xb = x.astype(BF16)
        return jnp.where(bd_mask, jnp.concatenate([xb] * heads_per_group, axis=0),
                         jnp.zeros((), BF16))

    def fold(x):
        xm = jnp.where(bd_mask, x, 0.0)
        acc = xm[0:HEAD_DIM]
        for hh in range(1, heads_per_group):
            acc = acc + xm[hh * HEAD_DIM:(hh + 1) * HEAD_DIM]
        return acc

    def ld(i, p):
        c, g = p
        return feat_ref[0, i, c * CHUNK:(c + 1) * CHUNK, g * gl:(g + 1) * gl]

    lw = [ld(1, p) for p in probs]
    na = [ld(4, p) for p in probs]
    bb = [ld(5, p) for p in probs]
    kf = [ld(2, p) for p in probs]
    r = [ld(0, p) for p in probs]
    v = [ld(3, p) for p in probs]

    def cumsum(x):
        hi = x.astype(BF16)
        lo = (x - hi.astype(F32)).astype(BF16)
        both = jnp.dot(tril_ones, jnp.concatenate([hi, lo], axis=1), preferred_element_type=F32)
        return both[:, :gl] + both[:, gl:]

    cum = _every(cumsum, lw)
    cum_last = [x[CHUNK - 1:CHUNK, :] for x in cum]
    e_out = [jnp.exp(-x) for x in cum]
    e_end = _every(lambda cl, x: jnp.exp(cl - x), cum_last, cum)
    a_t = _every(lambda n_, x, l_: n_ * jnp.exp(x - l_), na, cum, lw)
    r_t = _every(lambda r_, x: r_ * jnp.exp(x), r, cum)
    b_t = _every(lambda b_, e: b_ * e, bb, e_out)
    k_t = _every(lambda k_, e: k_ * e, kf, e_out)
    b_end = _every(lambda b_, e: b_ * e, bb, e_end)
    k_end = _every(lambda k_, e: k_ * e, kf, e_end)
    gamma = [jnp.exp(x) for x in cum_last]

    ar = _every(lambda a_, r_: jnp.concatenate([a_, r_], axis=0), a_t, r_t)
    pb = _every(lambda x, y: _dot_nt(x, bd(y)), ar, b_t)
    pk = _every(lambda x, y: _dot_nt(x, bd(y)), ar, k_t)
    l_ab = [jnp.where(strict, x[:CHUNK], 0.0) for x in pb]
    l_ak = [jnp.where(strict, x[:CHUNK], 0.0) for x in pk]
    m_rb = [jnp.where(incl, x[CHUNK:], 0.0) for x in pb]
    m_rk = [jnp.where(incl, x[CHUNK:], 0.0) for x in pk]

    x_acc = [eye_cat + l for l in l_ab]
    pw = _every(lambda l: _dot(l, bd(l)), l_ab)
    n_sq = int(math.log2(CHUNK)) - 1
    for lvl in range(n_sq):
        rhs = [bd(p_) for p_ in pw]
        if lvl < n_sq - 1:
            both = _every(lambda x, p_, w_: _dot(jnp.concatenate([x, p_], axis=0), w_),
                          x_acc, pw, rhs)
            x_acc = _every(lambda x, b_: x + b_[:CHUNK], x_acc, both)
            pw = [b_[CHUNK:] for b_ in both]
        else:
            x_acc = _every(lambda x, w_: x + _dot(x, w_), x_acc, rhs)
    t_inv = x_acc

    bd_v = [bd(x) for x in v]
    y_loc = _every(_dot, l_ak, bd_v)
    wu = _every(lambda t_, a_, y_: _dot(t_, jnp.concatenate([bd(a_), bd(y_)], axis=1)),
                t_inv, a_t, y_loc)
    w_t = [x[:, :gl] for x in wu]
    u_loc = [x[:, gl:] for x in wu]

    q_c = _every(lambda r_, m_, w_: r_ + _dot(m_, bd(w_)), r_t, m_rb, w_t)
    o_loc = _every(lambda mb, u_, mk, bv: _dot(jnp.concatenate([mb, mk], axis=1),
                                               jnp.concatenate([bd(u_), bv], axis=0)),
                   m_rb, u_loc, m_rk, bd_v)
    m_low = _every(lambda w_, b_: jnp.where(bd_mask, _dot_tn(w_, b_), 0.0).astype(BF16),
                   w_t, b_end)
    n_loc = _every(lambda u_, v_, b_, k_: fold(_dot_tn(jnp.concatenate([u_, v_], axis=0),
                                                       jnp.concatenate([b_, k_], axis=0))),
                   u_loc, v, b_end, k_end)

    state = [s_ref[g] for g in range(n_groups)]
    starts = []
    for c in range(n_chunks):
        starts.append(state)
        idx = [c * n_groups + g for g in range(n_groups)]
        state = [state[g] * gamma[i] + _dot(state[g], m_low[i]) + n_loc[i]
                 for g, i in enumerate(idx)]
    for g in range(n_groups):
        s_ref[g] = state[g]

    s0 = [starts[c][g] for (c, g) in probs]
    o = _every(lambda q_, s_, ol: _dot_nt(q_, bd(s_)) + ol, q_c, s0, o_loc)

    mu = [_dot(x, seg) * (1.0 / HEAD_DIM) for x in o]
    d = _every(lambda x, m_: x - m_, o, mu)
    var = [_dot(x * x, seg) * (1.0 / HEAD_DIM) for x in d]
    for i, (c, g) in enumerate(probs):
        ls = slice(g * gl, (g + 1) * gl)
        gn = d[i] * lax.rsqrt(var[i] + RWKV_GN_EPS) * ln_w[:, ls] + ln_b[:, ls]
        y = (gn + ld(7, (c, g))) * ld(6, (c, g))
        y_ref[0, c * CHUNK:(c + 1) * CHUNK, ls] = y.astype(y_ref.dtype)

    @pl.when(t == nt - 1)
    def _():
        for g in range(n_groups):
            s_out_ref[0, :, g * gl:(g + 1) * gl] = s_ref[g]


def _wkv_prompt_call(feat, ln_w, ln_b, tt):
    b, _, t, _ = feat.shape
    return pl.pallas_call(
        _wkv_prompt_kernel,
        grid=(b, t // tt),
        in_specs=[
            pl.BlockSpec((1, 8, tt, RWKV_WIDTH), lambda i, j: (i, 0, j, 0)),
            pl.BlockSpec((1, RWKV_WIDTH), lambda i, j: (0, 0)),
            pl.BlockSpec((1, RWKV_WIDTH), lambda i, j: (0, 0)),
        ],
        out_specs=(
            pl.BlockSpec((1, tt, RWKV_WIDTH), lambda i, j: (i, j, 0)),
            pl.BlockSpec((1, HEAD_DIM, RWKV_WIDTH), lambda i, j: (i, 0, 0)),
        ),
        out_shape=(
            jax.ShapeDtypeStruct((b, t, RWKV_WIDTH), BF16),
            jax.ShapeDtypeStruct((b, HEAD_DIM, RWKV_WIDTH), F32),
        ),
        scratch_shapes=[pltpu.VMEM((RWKV_WIDTH // GROUP_LANES, HEAD_DIM, GROUP_LANES), F32)],
        compiler_params=pltpu.CompilerParams(
            dimension_semantics=("arbitrary", "arbitrary"), vmem_limit_bytes=VMEM_LIMIT),
        name="wkv_prompt",
    )(feat, ln_w, ln_b)


def _wkv_sample_kernel(feat_ref, s_ref, s_out_ref, o_ref):
    nb = s_ref.shape[0]
    hd = HEAD_DIM
    ones = jnp.ones((hd, hd), BF16)
    ri = lax.broadcasted_iota(jnp.int32, (hd, hd), 0)
    ci = lax.broadcasted_iota(jnp.int32, (hd, hd), 1)
    eye = (ri == ci).astype(F32)

    def rows(i, b):
        full = feat_ref[i, b:b + 1, :]
        return jnp.concatenate(
            [jnp.broadcast_to(full[:, h * hd:(h + 1) * hd], (hd, hd)) for h in range(RWKV_HEADS)],
            axis=0)

    def split_dot(x):
        hi = x.astype(BF16)
        lo = (x - hi.astype(F32)).astype(BF16)
        return (jnp.dot(hi, ones, preferred_element_type=F32)
                + jnp.dot(lo, ones, preferred_element_type=F32))

    bs = list(range(nb))
    eye8 = jnp.concatenate([eye] * RWKV_HEADS, axis=0)
    s = [s_ref[b].reshape(RWKV_HEADS * hd, hd) for b in bs]
    sa = _every(lambda s_, b: _dot(s_ * rows(4, b), ones), s, bs)
    vx = [split_dot(eye8 * rows(3, b)) for b in bs]
    s_new = _every(lambda s_, sa_, vx_, b: s_ * jnp.exp(rows(1, b)) + sa_ * rows(5, b)
                   + vx_ * rows(2, b), s, sa, vx, bs)
    ox = _every(lambda s_, b: _dot(s_ * rows(0, b), ones), s_new, bs)
    for b in bs:
        s_out_ref[b] = s_new[b].reshape(RWKV_HEADS, hd, hd)
        d = eye8 * ox[b]
        o_ref[b:b + 1, :] = jnp.concatenate(
            [jnp.sum(d[h * hd:(h + 1) * hd], axis=0, keepdims=True) for h in range(RWKV_HEADS)],
            axis=1)


def _wkv_sample_call(feat_s, state, nb):
    n = state.shape[0]
    return pl.pallas_call(
        _wkv_sample_kernel,
        grid=(n // nb,),
        in_specs=[
            pl.BlockSpec((8, nb, RWKV_WIDTH), lambda i: (0, i, 0)),
            pl.BlockSpec((nb, RWKV_HEADS, HEAD_DIM, HEAD_DIM), lambda i: (i, 0, 0, 0)),
        ],
        out_specs=(
            pl.BlockSpec((nb, RWKV_HEADS, HEAD_DIM, HEAD_DIM), lambda i: (i, 0, 0, 0)),
            pl.BlockSpec((nb, RWKV_WIDTH), lambda i: (i, 0)),
        ),
        out_shape=(
            jax.ShapeDtypeStruct(state.shape, F32),
            jax.ShapeDtypeStruct((n, RWKV_WIDTH), F32),
        ),
        compiler_params=pltpu.CompilerParams(
            dimension_semantics=("arbitrary",), vmem_limit_bytes=VMEM_LIMIT),
        name="wkv_sample",
    )(feat_s, state)


def _swa_prompt_kernel(sink_ref, q_ref, kp_ref, kc_ref, vp_ref, vc_ref, o_ref):
    j = pl.program_id(1)
    w = WINDOW
    tq = q_ref.shape[1]
    n_blk = tq // w
    gl = GROUP_LANES
    rows = SWA_GROUP * w
    probs = [(qi, g) for qi in range(n_blk) for g in range(SWA_KV_HEADS)]

    lane_head = lax.broadcasted_iota(jnp.int32, (w, gl), 1) // HEAD_DIM
    head_mask = [lane_head == h for h in range(SWA_GROUP)]
    qi_ = lax.broadcasted_iota(jnp.int32, (rows, 2 * w), 0) % w
    ki_ = lax.broadcasted_iota(jnp.int32, (rows, 2 * w), 1)
    diff = qi_ - (ki_ - w)
    in_window = (diff >= 0) & (diff < WINDOW)
    first_valid = in_window & ((j * tq + ki_ - w) >= 0)

    def keys(ref_prev, ref_cur, qi, g):
        ls = slice(g * gl, (g + 1) * gl)
        prev = ref_prev[0, :, ls] if qi == 0 else ref_cur[0, (qi - 1) * w:qi * w, ls]
        return jnp.concatenate([prev, ref_cur[0, qi * w:(qi + 1) * w, ls]], axis=0)

    def lhs(qi, g):
        qg = q_ref[0, qi * w:(qi + 1) * w, g * gl:(g + 1) * gl]
        return jnp.concatenate([jnp.where(m, qg, jnp.zeros((), BF16)) for m in head_mask], axis=0)

    sinks = []
    for g in range(SWA_KV_HEADS):
        sinks.append(jnp.concatenate(
            [jnp.full((w, 1), sink_ref[g * SWA_GROUP + h], F32) for h in range(SWA_GROUP)], axis=0))

    s = [_dot_nt(lhs(qi, g), keys(kp_ref, kc_ref, qi, g)) for qi, g in probs]
    s = [jnp.where(first_valid if qi == 0 else in_window, x, MASK_VALUE)
         for x, (qi, g) in zip(s, probs)]
    sink = [sinks[g] for qi, g in probs]
    m = _every(lambda x, sk: jnp.maximum(jnp.max(x, axis=-1, keepdims=True), sk), s, sink)
    e = _every(lambda x, m_: jnp.exp(x - m_), s, m)
    inv = _every(lambda e_, sk, m_: 1.0 / (jnp.sum(e_, axis=-1, keepdims=True) + jnp.exp(sk - m_)),
                 e, sink, m)
    prob = _every(lambda e_, i_: (e_ * i_).astype(BF16), e, inv)
    og = [_dot(p_, keys(vp_ref, vc_ref, qi, g)) for p_, (qi, g) in zip(prob, probs)]
    for x, (qi, g) in zip(og, probs):
        y = jnp.where(head_mask[0], x[0:w], 0.0)
        for h in range(1, SWA_GROUP):
            y = y + jnp.where(head_mask[h], x[h * w:(h + 1) * w], 0.0)
        o_ref[0, qi * w:(qi + 1) * w, g * gl:(g + 1) * gl] = y.astype(o_ref.dtype)


def _swa_prompt_call(sinks, q, k, v, tq):
    b, t, _ = q.shape
    w = WINDOW
    per = tq // w
    prev = lambda i, j: (i, jnp.maximum(j * per - 1, 0), 0)
    cur = lambda i, j: (i, j, 0)
    return pl.pallas_call(
        _swa_prompt_kernel,
        grid=(b, t // tq),
        in_specs=[
            pl.BlockSpec(memory_space=pltpu.SMEM),
            pl.BlockSpec((1, tq, SWA_WIDTH), cur),
            pl.BlockSpec((1, w, SWA_WIDTH), prev),
            pl.BlockSpec((1, tq, SWA_WIDTH), cur),
            pl.BlockSpec((1, w, SWA_WIDTH), prev),
            pl.BlockSpec((1, tq, SWA_WIDTH), cur),
        ],
        out_specs=pl.BlockSpec((1, tq, SWA_WIDTH), cur),
        out_shape=jax.ShapeDtypeStruct((b, t, SWA_WIDTH), BF16),
        compiler_params=pltpu.CompilerParams(
            dimension_semantics=("arbitrary", "arbitrary"), vmem_limit_bytes=VMEM_LIMIT),
        name="swa_prompt",
    )(sinks, q, k, k, v, v)


def _swa_sample_kernel(sink_ref, qbd_ref, kn_ref, vn_ref, ck_ref, cv_ref,
                       o_ref, ko_ref, vo_ref):
    l = ck_ref.shape[1]
    qbd = qbd_ref[...]
    kn = kn_ref[...]
    vn = vn_ref[...]
    ck = ck_ref[...]
    cv = cv_ref[...]
    ki = lax.broadcasted_iota(jnp.int32, (1, 1, l), 2)
    kpos = PAST_LEN - l + ki
    diff = PAST_LEN - kpos
    valid = (diff >= 0) & (diff < WINDOW) & (kpos >= 0)
    s_c = lax.dot_general(qbd, ck.astype(BF16), (((2,), (2,)), ((0,), (0,))),
                          preferred_element_type=F32)
    s_c = jnp.where(valid, s_c, MASK_VALUE)
    s_n = jnp.sum(qbd.astype(F32) * kn.astype(BF16).astype(F32)[:, None, :],
                  axis=-1, keepdims=True)
    sink = sink_ref[...][None, :, 0:1]
    m = jnp.maximum(jnp.maximum(jnp.max(s_c, axis=-1, keepdims=True), s_n), sink)
    e_c = jnp.exp(s_c - m)
    e_n = jnp.exp(s_n - m)
    denom = jnp.sum(e_c, axis=-1, keepdims=True) + e_n + jnp.exp(sink - m)
    p_c = (e_c / denom).astype(BF16)
    p_n = (e_n / denom).astype(BF16).astype(F32)
    o = lax.dot_general(p_c, cv.astype(BF16), (((2,), (1,)), ((0,), (0,))),
                        preferred_element_type=F32)
    o = o + p_n * vn.astype(BF16).astype(F32)[:, None, :]
    o_ref[...] = o
    ko_ref[:, 0:l - 1, :] = ck_ref[:, 1:l, :]
    ko_ref[:, l - 1:l, :] = kn[:, None, :]
    vo_ref[:, 0:l - 1, :] = cv_ref[:, 1:l, :]
    vo_ref[:, l - 1:l, :] = vn[:, None, :]


def _swa_sample_call(sink_b, qbd, kn, vn, ck, cv, nb):
    n, l, _ = ck.shape
    return pl.pallas_call(
        _swa_sample_kernel,
        grid=(n // nb,),
        in_specs=[
            pl.BlockSpec((SWA_HEADS, LANES), lambda i: (0, 0)),
            pl.BlockSpec((nb, SWA_HEADS, KV_WIDTH), lambda i: (i, 0, 0)),
            pl.BlockSpec((nb, KV_WIDTH), lambda i: (i, 0)),
            pl.BlockSpec((nb, KV_WIDTH), lambda i: (i, 0)),
            pl.BlockSpec((nb, l, KV_WIDTH), lambda i: (i, 0, 0)),
            pl.BlockSpec((nb, l, KV_WIDTH), lambda i: (i, 0, 0)),
        ],
        out_specs=(
            pl.BlockSpec((nb, SWA_HEADS, KV_WIDTH), lambda i: (i, 0, 0)),
            pl.BlockSpec((nb, l, KV_WIDTH), lambda i: (i, 0, 0)),
            pl.BlockSpec((nb, l, KV_WIDTH), lambda i: (i, 0, 0)),
        ),
        out_shape=(
            jax.ShapeDtypeStruct((n, SWA_HEADS, KV_WIDTH), F32),
            jax.ShapeDtypeStruct((n, l, KV_WIDTH), F32),
            jax.ShapeDtypeStruct((n, l, KV_WIDTH), F32),
        ),
        compiler_params=pltpu.CompilerParams(
            dimension_semantics=("arbitrary",), vmem_limit_bytes=VMEM_LIMIT),
        name="swa_sample",
    )(sink_b, qbd, kn, vn, ck, cv)


def _tail_mix(x, y_rwkv, y_swa, g1, sh2, sc2, w_out, n_post_mix, n_pre_ffn):
    mix = _dot(y_rwkv, w_out[:RWKV_WIDTH]) + _dot(y_swa, w_out[RWKV_WIDTH:])
    x1 = x + g1 * _rmsnorm(mix, n_post_mix)
    h2 = (_rmsnorm(x1, n_pre_ffn) * (1.0 + sc2) + sh2).astype(BF16)
    return x1, h2


def _ffn_gate(act_ref, cw, cb, taps_fn, n_rows):
    def row_block(rb, carry):
        r0 = pl.multiple_of(rb * FF_ROWS, FF_ROWS)
        for c in range(N_FF_CHUNKS):
            z = []
            for half in range(2):
                cols = slice(half * D_FF + c * FF_CHUNK, half * D_FF + (c + 1) * FF_CHUNK)
                p1, p2, u = taps_fn(r0, cols)
                z.append(cb[:, cols] + p2 * cw[0:1, cols] + p1 * cw[1:2, cols] + u * cw[2:3, cols])
            act_ref[pl.ds(r0, FF_ROWS), c * FF_CHUNK:(c + 1) * FF_CHUNK] = (
                _silu(z[0]) * z[1]).astype(BF16)
        return carry

    lax.fori_loop(0, n_rows // FF_ROWS, row_block, 0)


def _tail_prompt_kernel(x_ref, yr_ref, ys_ref, mod_ref, w_out, n_post_mix, n_pre_ffn,
                        n_post_ffn, w_up, w_down, cw_ref, cb_ref,
                        y_ref, cp_ref, u_ref, act_ref):
    t = pl.program_id(1)
    tm = x_ref.shape[1]

    @pl.when(t == 0)
    def _():
        u_ref[0:SUBLANES, :] = jnp.zeros((SUBLANES, 2 * D_FF), F32)

    g1, sh2, sc2, g2 = mod_ref[2, 0], mod_ref[3, 0], mod_ref[4, 0], mod_ref[5, 0]
    x1, h2 = _tail_mix(x_ref[0], yr_ref[0], ys_ref[0], g1, sh2, sc2, w_out[...],
                       n_post_mix[...], n_pre_ffn[...])
    u_ref[SUBLANES:, :] = jnp.dot(h2, w_up[...], preferred_element_type=F32)

    def taps_fn(r0, cols):
        ext = u_ref[pl.ds(r0, FF_ROWS + SUBLANES), cols]
        return (pltpu.roll(ext, 1, 0)[SUBLANES:], pltpu.roll(ext, 2, 0)[SUBLANES:],
                ext[SUBLANES:])

    _ffn_gate(act_ref, cw_ref, cb_ref, taps_fn, tm)
    last = u_ref[tm:tm + SUBLANES, :]
    cp_ref[0] = last
    u_ref[0:SUBLANES, :] = last
    ff = jnp.dot(act_ref[...], w_down[...], preferred_element_type=F32)
    y_ref[0] = x1 + g2 * _rmsnorm(ff, n_post_ffn[...])


def _tail_sample_kernel(x_ref, o_ref, feat_ref, ys_ref, mod_ref, p0_ref, p1_ref, ln_w, ln_b,
                        w_out, n_post_mix, n_pre_ffn, n_post_ffn, w_up, w_down,
                        cw_ref, cb_ref, y_ref, u_ref, act_ref):
    n = x_ref.shape[0]
    seg = _seg_ones(RWKV_WIDTH)
    y_rwkv = _gn_epilogue(o_ref[...], feat_ref[7], feat_ref[6], ln_w[...], ln_b[...], seg)
    g1, sh2, sc2, g2 = mod_ref[2], mod_ref[3], mod_ref[4], mod_ref[5]
    x1, h2 = _tail_mix(x_ref[...], y_rwkv, ys_ref[...], g1, sh2, sc2, w_out[...],
                       n_post_mix[...], n_pre_ffn[...])
    u_ref[...] = jnp.dot(h2, w_up[...], preferred_element_type=F32)

    def taps_fn(r0, cols):
        rows = pl.ds(r0, FF_ROWS)
        return p1_ref[rows, cols], p0_ref[rows, cols], u_ref[rows, cols]

    _ffn_gate(act_ref, cw_ref, cb_ref, taps_fn, n)
    ff = jnp.dot(act_ref[...], w_down[...], preferred_element_type=F32)
    y_ref[...] = x1 + g2 * _rmsnorm(ff, n_post_ffn[...])


def _tail_prompt_call(x, y_rwkv, y_swa, mod_p, consts, tm):
    b, t, _ = x.shape
    tok = lambda i, j: (i, j, 0)
    in_specs = [
        pl.BlockSpec((1, tm, D_MODEL), tok),
        pl.BlockSpec((1, tm, RWKV_WIDTH), tok),
        pl.BlockSpec((1, tm, SWA_WIDTH), tok),
        pl.BlockSpec((6, 1, 1, D_MODEL), lambda i, j: (0, i, 0, 0)),
    ] + [_const_spec(c, 2) for c in consts]
    return pl.pallas_call(
        _tail_prompt_kernel,
        grid=(b, t // tm),
        in_specs=in_specs,
        out_specs=(
            pl.BlockSpec((1, tm, D_MODEL), tok),
            pl.BlockSpec((1, SUBLANES, 2 * D_FF), lambda i, j: (i, 0, 0)),
        ),
        out_shape=(
            jax.ShapeDtypeStruct((b, t, D_MODEL), F32),
            jax.ShapeDtypeStruct((b, SUBLANES, 2 * D_FF), F32),
        ),
        scratch_shapes=[
            pltpu.VMEM((SUBLANES + tm, 2 * D_FF), F32),
            pltpu.VMEM((tm, D_FF), BF16),
        ],
        compiler_params=pltpu.CompilerParams(
            dimension_semantics=("arbitrary", "arbitrary"), vmem_limit_bytes=VMEM_LIMIT),
        name="tail_prompt",
    )(x, y_rwkv, y_swa, mod_p, *consts)


def _tail_sample_call(x, o_s, feat_s, y_swa, mod_s, p0, p1, ln_w, ln_b, consts):
    n = x.shape[0]
    args = (x, o_s, feat_s, y_swa, mod_s, p0, p1, ln_w, ln_b) + tuple(consts)
    out_shape = (
        jax.ShapeDtypeStruct((n, D_MODEL), F32),
        jax.ShapeDtypeStruct((n, 2 * D_FF), F32),
    )
    return pl.pallas_call(
        _tail_sample_kernel,
        grid=(1,),
        in_specs=[_const_spec(a, 1) for a in args],
        out_specs=tuple(pl.BlockSpec(s.shape, lambda i, nd=len(s.shape): (0,) * nd)
                        for s in out_shape),
        out_shape=out_shape,
        scratch_shapes=[pltpu.VMEM((n, D_FF), BF16)],
        compiler_params=pltpu.CompilerParams(
            dimension_semantics=("arbitrary",), vmem_limit_bytes=VMEM_LIMIT),
        name="tail_sample",
    )(*args)


def _rope_tables(pos):
    half = HEAD_DIM // 2
    inv = ROPE_THETA ** (-jnp.arange(half, dtype=F32) / half)
    ang = pos.astype(F32)[:, None] * inv[None, :]
    cos, sin = jnp.cos(ang), jnp.sin(ang)
    cos_h = jnp.concatenate([cos, cos], axis=-1)
    sin_h = jnp.concatenate([-sin, sin], axis=-1)
    return jnp.tile(cos_h, (1, LANES // HEAD_DIM)), jnp.tile(sin_h, (1, LANES // HEAD_DIM))


def kernel(x_prompt, x_sample, state_rwkv_wkv, state_rwkv_shift, cache_swa_k, cache_swa_v,
           state_ffn_conv, c_prompt, c_sample, w_ada, b_ada, norm_pre_mix, norm_post_mix,
           norm_pre_ffn, norm_post_ffn, w_in, rwkv_mu, rwkv_w0, rwkv_w_up, rwkv_a0, rwkv_a_up,
           rwkv_g_up, rwkv_k_k, rwkv_k_a, rwkv_r_k, rwkv_ln_w, rwkv_ln_b, swa_sinks, w_out,
           ffn_w_up, ffn_conv_w, ffn_conv_b, ffn_w_down):
    depth = w_ada.shape[0]
    assert depth == 1 and x_sample.shape[1] == 1
    b, t, _ = x_prompt.shape
    n = x_sample.shape[0]
    l = cache_swa_k.shape[2]
    tm = min(256, t)
    assert t % tm == 0 and tm % WINDOW == 0 and tm % CHUNK == 0
    assert tm % FF_ROWS == 0 and n % FF_ROWS == 0
    nb_wkv = 8 if n % 8 == 0 else n
    nb_swa = 8 if n % 8 == 0 else n
    li = 0

    row = lambda v: v.reshape(1, -1)
    w_in_b = w_in[li].astype(BF16)
    zeros_l = jnp.zeros((64, RWKV_WIDTH), F32)
    wa_up = jnp.concatenate([
        jnp.concatenate([rwkv_w_up[li], zeros_l], axis=1),
        jnp.concatenate([zeros_l, rwkv_a_up[li]], axis=1)], axis=0).astype(BF16)
    hid = jnp.arange(RWKV_WIDTH) // HEAD_DIM
    seg512 = (hid[:, None] == hid[None, :]).astype(BF16)
    proj_consts = (row(norm_pre_mix[li]), w_in_b, row(rwkv_mu[li]), row(rwkv_w0[li]), wa_up,
                   row(rwkv_a0[li]), rwkv_g_up[li].astype(BF16), row(rwkv_k_k[li]),
                   row(rwkv_k_a[li]), row(rwkv_r_k[li]), seg512)
    tail_consts = (w_out[li].astype(BF16), row(norm_post_mix[li]), row(norm_pre_ffn[li]),
                   row(norm_post_ffn[li]), ffn_w_up[li].astype(BF16), ffn_w_down[li].astype(BF16),
                   ffn_conv_w[li], row(ffn_conv_b[li]))
    ln_w, ln_b = row(rwkv_ln_w[li]), row(rwkv_ln_b[li])

    mod = _ada_call(jnp.concatenate([c_prompt, c_sample], axis=0), w_ada[li], row(b_ada[li]))
    mod_p = mod[:, :b].reshape(6, b, 1, D_MODEL)
    mod_s = mod[:, b:]

    cos_p, sin_p = _rope_tables(jnp.arange(t, dtype=jnp.int32))
    feat_p, q_p, kx_p, vx_p, klast, vlast, plast = _proj_prompt_call(
        x_prompt, mod_p, cos_p, sin_p, proj_consts, tm)
    y_rwkv_p, s_cat = _wkv_prompt_call(feat_p, ln_w, ln_b, tm)
    y_swa_p = _swa_prompt_call(swa_sinks[li], q_p, kx_p, vx_p, tm)
    y_p, cp = _tail_prompt_call(x_prompt, y_rwkv_p, y_swa_p, mod_p, tail_consts, tm)

    wkv_p = s_cat.reshape(b, HEAD_DIM, RWKV_HEADS, HEAD_DIM).transpose(0, 2, 1, 3)
    shift_p = plast[:, SUBLANES - 1]
    k_p = klast.reshape(b, WINDOW, SWA_KV_HEADS, HEAD_DIM)
    v_p = vlast.reshape(b, WINDOW, SWA_KV_HEADS, HEAD_DIM)
    conv_p = cp[:, SUBLANES - 2:]

    cos_s, sin_s = _rope_tables(jnp.full((1,), PAST_LEN, jnp.int32))
    feat_s, q_s, kn_s, vn_s, p_s = _proj_sample_call(
        x_sample[:, 0], mod_s, cos_s, sin_s, state_rwkv_shift[li], proj_consts)
    wkv_s, o_s = _wkv_sample_call(feat_s, state_rwkv_wkv[li], nb_wkv)

    head_group = jnp.arange(SWA_HEADS) // SWA_GROUP
    sel = (head_group[:, None] == jnp.arange(SWA_KV_HEADS)[None, :])
    q4 = q_s.reshape(n, SWA_HEADS, 1, HEAD_DIM)
    qbd = jnp.where(sel[None, :, :, None], q4, jnp.zeros((), BF16)).reshape(n, SWA_HEADS, KV_WIDTH)
    sink_b = jnp.broadcast_to(swa_sinks[li][:, None], (SWA_HEADS, LANES))
    ck = cache_swa_k[li].reshape(n, l, KV_WIDTH)
    cv = cache_swa_v[li].reshape(n, l, KV_WIDTH)
    o_att, k_s, v_s = _swa_sample_call(sink_b, qbd, kn_s, vn_s, ck, cv, nb_swa)
    o4 = o_att.reshape(n, SWA_HEADS, SWA_KV_HEADS, HEAD_DIM)
    y_swa_s = jnp.sum(jnp.where(sel[None, :, :, None], o4, 0.0), axis=2).reshape(n, SWA_WIDTH)

    conv0 = state_ffn_conv[li]
    y_s, u_s = _tail_sample_call(x_sample[:, 0], o_s, feat_s, y_swa_s.astype(BF16), mod_s,
                                 conv0[:, 0], conv0[:, 1], ln_w, ln_b, tail_consts)
    conv_s = jnp.stack([conv0[:, 1], u_s], axis=1)

    expand = lambda a: a[None]
    return (y_p, y_s[:, None, :], expand(wkv_p), expand(shift_p), expand(k_p), expand(v_p),
            expand(conv_p), expand(wkv_s), expand(p_s),
            expand(k_s.reshape(n, l, SWA_KV_HEADS, HEAD_DIM)),
            expand(v_s.reshape(n, l, SWA_KV_HEADS, HEAD_DIM)), expand(conv_s))
```

```python
import math

import jax
import jax.numpy as jnp
from jax import lax
from jax.experimental import pallas as pl
from jax.experimental.pallas import tpu as pltpu

D_MODEL = 1024
HEAD_DIM = 64
RWKV_WIDTH = 512
RWKV_HEADS = 8
RWKV_COLS = 1792
RWKV_GN_EPS = 64e-5
SWA_WIDTH = 512
SWA_HEADS = 8
SWA_KV_HEADS = 2
SWA_GROUP = 4
KV_WIDTH = SWA_KV_HEADS * HEAD_DIM
WINDOW = 128
PAST_LEN = 16384
ROPE_THETA = 10000.0
ATTN_SCALE = HEAD_DIM ** -0.5
D_FF = 2816
NORM_EPS = 1e-6
MASK_VALUE = -1e30
PROJ_COLS = RWKV_COLS + SWA_WIDTH + 2 * KV_WIDTH

LANES = 128
SUBLANES = 8
CHUNK = 64
GROUP_LANES = 256
FF_CHUNK = 256
N_FF_CHUNKS = D_FF // FF_CHUNK
FF_ROWS = 32
VMEM_LIMIT = 56 * 1024 * 1024

F32 = jnp.float32
BF16 = jnp.bfloat16


def _sigmoid(x):
    return 1.0 / (1.0 + jnp.exp(-x))


def _silu(x):
    return x * _sigmoid(x)


def _rmsnorm(x, g):
    return x * lax.rsqrt(jnp.mean(x * x, axis=-1, keepdims=True) + NORM_EPS) * g


def _dot(a, b):
    return jnp.dot(a.astype(BF16), b.astype(BF16), preferred_element_type=F32)


def _dot_nt(a, b):
    return lax.dot_general(a.astype(BF16), b.astype(BF16), (((1,), (1,)), ((), ())),
                           preferred_element_type=F32)


def _dot_tn(a, b):
    return lax.dot_general(a.astype(BF16), b.astype(BF16), (((0,), (0,)), ((), ())),
                           preferred_element_type=F32)


def _every(fn, *lists):
    return [fn(*a) for a in zip(*lists)]


def _swap_halves(x):
    w = x.shape[-1]
    lane = lax.broadcasted_iota(jnp.int32, x.shape, x.ndim - 1)
    lo = (lane & (HEAD_DIM // 2)) == 0
    return jnp.where(lo, pltpu.roll(x, w - HEAD_DIM // 2, x.ndim - 1),
                     pltpu.roll(x, HEAD_DIM // 2, x.ndim - 1))


def _rope(x, cos, sin):
    reps = x.shape[-1] // LANES
    cos_w = jnp.concatenate([cos] * reps, axis=-1) if reps > 1 else cos
    sin_w = jnp.concatenate([sin] * reps, axis=-1) if reps > 1 else sin
    return x * cos_w + _swap_halves(x) * sin_w


def _proj_features(x, shift, scale, g_pre, w_in, prev_fn, mu, w0, wa_up, a0, g_up,
                   k_k, k_a, r_k, seg512, cos, sin):
    h = _rmsnorm(x, g_pre) * (1.0 + scale) + shift
    p = _dot(h, w_in)
    p_rwkv = p[:, :RWKV_COLS]
    prev = prev_fn(p_rwkv)
    xm = p_rwkv + (prev - p_rwkv) * mu
    r = xm[:, 0:512]
    k = xm[:, 512:1024]
    v = xm[:, 1024:1536]
    wa = xm[:, 1536:1664]
    gd = xm[:, 1664:1792]
    lane = lax.broadcasted_iota(jnp.int32, wa.shape, 1)
    wa_act = jnp.where(lane < 64, jnp.tanh(wa), wa)
    lora = _dot(wa_act, wa_up)
    lw = -math.exp(-0.5) * _sigmoid(w0 + lora[:, :512])
    a = _sigmoid(a0 + lora[:, 512:])
    g = _dot(_sigmoid(gd), g_up)
    kk = k * k_k
    ss = _dot(kk * kk, seg512)
    kk = kk / jnp.maximum(jnp.sqrt(ss), 1e-12)
    kf = k * (1.0 + (a - 1.0) * k_a)
    bonus = _dot(r * kf * r_k, seg512) * v
    feats = (r, lw, kf, v, -kk, kk * a, g, bonus)
    q = _rope(p[:, RWKV_COLS:RWKV_COLS + SWA_WIDTH], cos, sin)
    ks = _rope(p[:, RWKV_COLS + SWA_WIDTH:RWKV_COLS + SWA_WIDTH + KV_WIDTH], cos, sin)
    vs = p[:, RWKV_COLS + SWA_WIDTH + KV_WIDTH:]
    return feats, q, ks, vs, p_rwkv


def _gn_epilogue(o, bonus, g, ln_w, ln_b, seg):
    mu = _dot(o, seg) * (1.0 / HEAD_DIM)
    d = o - mu
    var = _dot(d * d, seg) * (1.0 / HEAD_DIM)
    gn = d * lax.rsqrt(var + RWKV_GN_EPS) * ln_w + ln_b
    return (gn + bonus) * g


def _seg_ones(n):
    r = lax.broadcasted_iota(jnp.int32, (n, n), 0) // HEAD_DIM
    c = lax.broadcasted_iota(jnp.int32, (n, n), 1) // HEAD_DIM
    return (r == c).astype(BF16)


def _ada_kernel(c_ref, w_ref, b_ref, o_ref):
    o_ref[0] = _dot(_silu(c_ref[...]), w_ref[...]) + b_ref[...]


def _ada_call(c_all, w_ada, b_ada):
    rows = c_all.shape[0]
    return pl.pallas_call(
        _ada_kernel,
        grid=(6,),
        in_specs=[
            pl.BlockSpec((rows, D_MODEL), lambda j: (0, 0)),
            pl.BlockSpec((D_MODEL, D_MODEL), lambda j: (0, j)),
            pl.BlockSpec((1, D_MODEL), lambda j: (0, j)),
        ],
        out_specs=pl.BlockSpec((1, rows, D_MODEL), lambda j: (j, 0, 0)),
        out_shape=jax.ShapeDtypeStruct((6, rows, D_MODEL), F32),
        compiler_params=pltpu.CompilerParams(
            dimension_semantics=("arbitrary",), vmem_limit_bytes=VMEM_LIMIT),
        name="ada",
    )(c_all, w_ada, b_ada)


def _expand_kv(x):
    lane = lax.broadcasted_iota(jnp.int32, x.shape, 1)
    rolled = pltpu.roll(x, HEAD_DIM, 1)
    g0 = jnp.where(lane < HEAD_DIM, x, rolled)
    g1 = jnp.where(lane < HEAD_DIM, rolled, x)
    return jnp.concatenate([g0, g0, g1, g1], axis=1)


def _mix_prompt_kernel(x_ref, mod_ref, cos_ref, sin_ref, g_pre, w_in, mu, w0, wa_up, a0,
                       g_up, k_k, k_a, r_k, seg512, ln_w, ln_b,
                       y_ref, s_out_ref, q_ref, k_ref, v_ref, klast_ref, vlast_ref, plast_ref,
                       carry_ref, s_ref):
    t = pl.program_id(1)

    @pl.when(t == 0)
    def _():
        carry_ref[...] = jnp.zeros_like(carry_ref)
        s_ref[...] = jnp.zeros_like(s_ref)

    tm = x_ref.shape[1]
    carry_row = carry_ref[SUBLANES - 1:SUBLANES, :]

    def prev_fn(p_rwkv):
        row = lax.broadcasted_iota(jnp.int32, p_rwkv.shape, 0)
        return jnp.where(row == 0, carry_row, pltpu.roll(p_rwkv, 1, 0))

    feats, q, ks, vs, p_rwkv = _proj_features(
        x_ref[0], mod_ref[0, 0], mod_ref[1, 0], g_pre[...], w_in[...], prev_fn, mu[...],
        w0[...], wa_up[...], a0[...], g_up[...], k_k[...], k_a[...], r_k[...], seg512[...],
        cos_ref[...], sin_ref[...])
    q_ref[0] = (q * ATTN_SCALE).astype(BF16)
    k_ref[0] = _expand_kv(ks).astype(BF16)
    v_ref[0] = _expand_kv(vs).astype(BF16)
    last = p_rwkv[tm - SUBLANES:, :]
    carry_ref[...] = last
    plast_ref[0] = last
    _wkv_tile(feats, ln_w, ln_b, y_ref, s_ref)

    @pl.when(t == pl.num_programs(1) - 1)
    def _():
        klast_ref[0] = ks[tm - WINDOW:, :]
        vlast_ref[0] = vs[tm - WINDOW:, :]
        for g in range(RWKV_WIDTH // GROUP_LANES):
            s_out_ref[0, :, g * GROUP_LANES:(g + 1) * GROUP_LANES] = s_ref[g]


def _proj_sample_kernel(x_ref, mod_ref, cos_ref, sin_ref, prev_ref, g_pre, w_in, mu, w0,
                        wa_up, a0, g_up, k_k, k_a, r_k, seg512,
                        feat_ref, q_ref, k_ref, v_ref, p_ref):
    feats, q, ks, vs, p_rwkv = _proj_features(
        x_ref[...], mod_ref[0], mod_ref[1], g_pre[...], w_in[...], lambda p: prev_ref[...],
        mu[...], w0[...], wa_up[...], a0[...], g_up[...], k_k[...], k_a[...], r_k[...],
        seg512[...], cos_ref[...], sin_ref[...])
    for i, f in enumerate(feats):
        feat_ref[i] = f
    q_ref[...] = (q * ATTN_SCALE).astype(BF16)
    k_ref[...] = ks
    v_ref[...] = vs
    p_ref[...] = p_rwkv


def _const_spec(arr, grid_rank):
    zeros = (0,) * arr.ndim
    if grid_rank == 1:
        return pl.BlockSpec(arr.shape, lambda i: zeros)
    return pl.BlockSpec(arr.shape, lambda i, j: zeros)


def _mix_prompt_call(x, mod_p, cos, sin, consts, tm):
    b, t, _ = x.shape
    nt = t // tm
    in_specs = [
        pl.BlockSpec((1, tm, D_MODEL), lambda i, j: (i, j, 0)),
        pl.BlockSpec((6, 1, 1, D_MODEL), lambda i, j: (0, i, 0, 0)),
        pl.BlockSpec((tm, LANES), lambda i, j: (j, 0)),
        pl.BlockSpec((tm, LANES), lambda i, j: (j, 0)),
    ] + [_const_spec(c, 2) for c in consts]
    out_shape = (
        jax.ShapeDtypeStruct((b, t, RWKV_WIDTH), BF16),
        jax.ShapeDtypeStruct((b, HEAD_DIM, RWKV_WIDTH), F32),
        jax.ShapeDtypeStruct((b, t, SWA_WIDTH), BF16),
        jax.ShapeDtypeStruct((b, t, SWA_WIDTH), BF16),
        jax.ShapeDtypeStruct((b, t, SWA_WIDTH), BF16),
        jax.ShapeDtypeStruct((b, WINDOW, KV_WIDTH), F32),
        jax.ShapeDtypeStruct((b, WINDOW, KV_WIDTH), F32),
        jax.ShapeDtypeStruct((b, SUBLANES, RWKV_COLS), F32),
    )
    out_specs = (
        pl.BlockSpec((1, tm, RWKV_WIDTH), lambda i, j: (i, j, 0)),
        pl.BlockSpec((1, HEAD_DIM, RWKV_WIDTH), lambda i, j: (i, 0, 0)),
        pl.BlockSpec((1, tm, SWA_WIDTH), lambda i, j: (i, j, 0)),
        pl.BlockSpec((1, tm, SWA_WIDTH), lambda i, j: (i, j, 0)),
        pl.BlockSpec((1, tm, SWA_WIDTH), lambda i, j: (i, j, 0)),
        pl.BlockSpec((1, WINDOW, KV_WIDTH), lambda i, j: (i, 0, 0)),
        pl.BlockSpec((1, WINDOW, KV_WIDTH), lambda i, j: (i, 0, 0)),
        pl.BlockSpec((1, SUBLANES, RWKV_COLS), lambda i, j: (i, 0, 0)),
    )
    return pl.pallas_call(
        _mix_prompt_kernel,
        grid=(b, nt),
        in_specs=in_specs,
        out_specs=out_specs,
        out_shape=out_shape,
        scratch_shapes=[
            pltpu.VMEM((SUBLANES, RWKV_COLS), F32),
            pltpu.VMEM((RWKV_WIDTH // GROUP_LANES, HEAD_DIM, GROUP_LANES), F32),
        ],
        compiler_params=pltpu.CompilerParams(
            dimension_semantics=("arbitrary", "arbitrary"), vmem_limit_bytes=VMEM_LIMIT),
        name="mix_prompt",
    )(x, mod_p, cos, sin, *consts)


def _proj_sample_call(x, mod_s, cos, sin, prev, consts):
    n = x.shape[0]
    args = (x, mod_s, cos, sin, prev) + tuple(consts)
    out_shape = (
        jax.ShapeDtypeStruct((8, n, RWKV_WIDTH), F32),
        jax.ShapeDtypeStruct((n, SWA_WIDTH), BF16),
        jax.ShapeDtypeStruct((n, KV_WIDTH), F32),
        jax.ShapeDtypeStruct((n, KV_WIDTH), F32),
        jax.ShapeDtypeStruct((n, RWKV_COLS), F32),
    )
    return pl.pallas_call(
        _proj_sample_kernel,
        grid=(1,),
        in_specs=[_const_spec(a, 1) for a in args],
        out_specs=tuple(pl.BlockSpec(s.shape, lambda i, nd=len(s.shape): (0,) * nd)
                        for s in out_shape),
        out_shape=out_shape,
        compiler_params=pltpu.CompilerParams(
            dimension_semantics=("arbitrary",), vmem_limit_bytes=VMEM_LIMIT),
        name="proj_sample",
    )(*args)


def _wkv_tile(feats, ln_w, ln_b, y_ref, s_ref):
    tt = feats[0].shape[0]
    n_chunks = tt // CHUNK
    gl = GROUP_LANES
    n_groups = RWKV_WIDTH // gl
    heads_per_group = gl // HEAD_DIM
    probs = [(c, g) for c in range(n_chunks) for g in range(n_groups)]

    row_c = lax.broadcasted_iota(jnp.int32, (CHUNK, gl), 0)
    col_c = lax.broadcasted_iota(jnp.int32, (CHUNK, gl), 1) % CHUNK
    strict = row_c > col_c
    incl = row_c >= col_c
    eye_cat = (row_c == col_c).astype(F32)
    rb = lax.broadcasted_iota(jnp.int32, (gl, gl), 0) // HEAD_DIM
    cb = lax.broadcasted_iota(jnp.int32, (gl, gl), 1) // HEAD_DIM
    bd_mask = rb == cb
    tri_r = lax.broadcasted_iota(jnp.int32, (CHUNK, CHUNK), 0)
    tri_c = lax.broadcasted_iota(jnp.int32, (CHUNK, CHUNK), 1)
    tril_ones = (tri_r >= tri_c).astype(BF16)
    seg = bd_mask.astype(BF16)

    def bd(x):
        xb = x.astype(BF16)
        return jnp.where(bd_mask, jnp.concatenate([xb] * heads_per_group, axis=0),
                         jnp.zeros((), BF16))

    def fold(x):
        xm = jnp.where(bd_mask, x, 0.0)
        acc = xm[0:HEAD_DIM]
        for hh in range(1, heads_per_group):
            acc = acc + xm[hh * HEAD_DIM:(hh + 1) * HEAD_DIM]
        return acc

    def ld(i, p):
        c, g = p
        return feats[i][c * CHUNK:(c + 1) * CHUNK, g * gl:(g + 1) * gl]

    lw = [ld(1, p) for p in probs]
    na = [ld(4, p) for p in probs]
    bb = [ld(5, p) for p in probs]
    kf = [ld(2, p) for p in probs]
    r = [ld(0, p) for p in probs]
    v = [ld(3, p) for p in probs]

    def cumsum(x):
        hi = x.astype(BF16)
        lo = (x - hi.astype(F32)).astype(BF16)
        both = jnp.dot(tril_ones, jnp.concatenate([hi, lo], axis=1), preferred_element_type=F32)
        return both[:, :gl] + both[:, gl:]

    cum = _every(cumsum, lw)
    cum_last = [x[CHUNK - 1:CHUNK, :] for x in cum]
    e_out = [jnp.exp(-x) for x in cum]
    e_end = _every(lambda cl, x: jnp.exp(cl - x), cum_last, cum)
    a_t = _every(lambda n_, x, l_: n_ * jnp.exp(x - l_), na, cum, lw)
    r_t = _every(lambda r_, x: r_ * jnp.exp(x), r, cum)
    b_t = _every(lambda b_, e: b_ * e, bb, e_out)
    k_t = _every(lambda k_, e: k_ * e, kf, e_out)
    b_end = _every(lambda b_, e: b_ * e, bb, e_end)
    k_end = _every(lambda k_, e: k_ * e, kf, e_end)
    gamma = [jnp.exp(x) for x in cum_last]

    ar = _every(lambda a_, r_: jnp.concatenate([a_, r_], axis=0), a_t, r_t)
    pb = _every(lambda x, y: _dot_nt(x, bd(y)), ar, b_t)
    pk = _every(lambda x, y: _dot_nt(x, bd(y)), ar, k_t)
    l_ab = [jnp.where(strict, x[:CHUNK], 0.0) for x in pb]
    l_ak = [jnp.where(strict, x[:CHUNK], 0.0) for x in pk]
    m_rb = [jnp.where(incl, x[CHUNK:], 0.0) for x in pb]
    m_rk = [jnp.where(incl, x[CHUNK:], 0.0) for x in pk]

    x_acc = [eye_cat + l for l in l_ab]
    pw = _every(lambda l: _dot(l, bd(l)), l_ab)
    n_sq = int(math.log2(CHUNK)) - 1
    for lvl in range(n_sq):
        rhs = [bd(p_) for p_ in pw]
        if lvl < n_sq - 1:
            both = _every(lambda x, p_, w_: _dot(jnp.concatenate([x, p_], axis=0), w_),
                          x_acc, pw, rhs)
            x_acc = _every(lambda x, b_: x + b_[:CHUNK], x_acc, both)
            pw = [b_[CHUNK:] for b_ in both]
        else:
            x_acc = _every(lambda x, w_: x + _dot(x, w_), x_acc, rhs)
    t_inv = x_acc

    bd_v = [bd(x) for x in v]
    y_loc = _every(_dot, l_ak, bd_v)
    wu = _every(lambda t_, a_, y_: _dot(t_, jnp.concatenate([bd(a_), bd(y_)], axis=1)),
                t_inv, a_t, y_loc)
    w_t = [x[:, :gl] for x in wu]
    u_loc = [x[:, gl:] for x in wu]

    q_c = _every(lambda r_, m_, w_: r_ + _dot(m_, bd(w_)), r_t, m_rb, w_t)
    o_loc = _every(lambda mb, u_, mk, bv: _dot(jnp.concatenate([mb, mk], axis=1),
                                               jnp.concatenate([bd(u_), bv], axis=0)),
                   m_rb, u_loc, m_rk, bd_v)
    m_low = _every(lambda w_, b_: jnp.where(bd_mask, _dot_tn(w_, b_), 0.0).astype(BF16),
                   w_t, b_end)
    n_loc = _every(lambda u_, v_, b_, k_: fold(_dot_tn(jnp.concatenate([u_, v_], axis=0),
                                                       jnp.concatenate([b_, k_], axis=0))),
                   u_loc, v, b_end, k_end)

    state = [s_ref[g] for g in range(n_groups)]
    starts = []
    for c in range(n_chunks):
        starts.append(state)
        idx = [c * n_groups + g for g in range(n_groups)]
        state = [state[g] * gamma[i] + _dot(state[g], m_low[i]) + n_loc[i]
                 for g, i in enumerate(idx)]
    for g in range(n_groups):
        s_ref[g] = state[g]

    s0 = [starts[c][g] for (c, g) in probs]
    o = _every(lambda q_, s_, ol: _dot_nt(q_, bd(s_)) + ol, q_c, s0, o_loc)

    n_p = len(probs)
    unstack = lambda x: [x[i * CHUNK:(i + 1) * CHUNK] for i in range(n_p)]
    mu = unstack(_dot(jnp.concatenate(o, axis=0), seg) * (1.0 / HEAD_DIM))
    d = _every(lambda x, m_: x - m_, o, mu)
    var = unstack(_dot(jnp.concatenate([x * x for x in d], axis=0), seg) * (1.0 / HEAD_DIM))
    for i, (c, g) in enumerate(probs):
        ls = slice(g * gl, (g + 1) * gl)
        gn = d[i] * lax.rsqrt(var[i] + RWKV_GN_EPS) * ln_w[:, ls] + ln_b[:, ls]
        y = (gn + ld(7, (c, g))) * ld(6, (c, g))
        y_ref[0, c * CHUNK:(c + 1) * CHUNK, ls] = y.astype(y_ref.dtype)


def _wkv_sample_kernel(feat_ref, s_ref, s_out_ref, o_ref):
    nb = s_ref.shape[0]
    hd = HEAD_DIM
    ones = jnp.ones((hd, hd), BF16)
    ri = lax.broadcasted_iota(jnp.int32, (hd, hd), 0)
    ci = lax.broadcasted_iota(jnp.int32, (hd, hd), 1)
    eye = (ri == ci).astype(F32)

    def rows(i, b):
        full = feat_ref[i, b:b + 1, :]
        return jnp.concatenate(
            [jnp.broadcast_to(full[:, h * hd:(h + 1) * hd], (hd, hd)) for h in range(RWKV_HEADS)],
            axis=0)

    def split_dot(x):
        hi = x.astype(BF16)
        lo = (x - hi.astype(F32)).astype(BF16)
        return (jnp.dot(hi, ones, preferred_element_type=F32)
                + jnp.dot(lo, ones, preferred_element_type=F32))

    bs = list(range(nb))
    eye8 = jnp.concatenate([eye] * RWKV_HEADS, axis=0)
    s = [s_ref[b].reshape(RWKV_HEADS * hd, hd) for b in bs]
    sa = _every(lambda s_, b: _dot(s_ * rows(4, b), ones), s, bs)
    vx = [split_dot(eye8 * rows(3, b)) for b in bs]
    s_new = _every(lambda s_, sa_, vx_, b: s_ * jnp.exp(rows(1, b)) + sa_ * rows(5, b)
                   + vx_ * rows(2, b), s, sa, vx, bs)
    ox = _every(lambda s_, b: _dot(s_ * rows(0, b), ones), s_new, bs)
    for b in bs:
        s_out_ref[b] = s_new[b].reshape(RWKV_HEADS, hd, hd)
        d = eye8 * ox[b]
        o_ref[b:b + 1, :] = jnp.concatenate(
            [jnp.sum(d[h * hd:(h + 1) * hd], axis=0, keepdims=True) for h in range(RWKV_HEADS)],
            axis=1)


def _wkv_sample_call(feat_s, state, nb):
    n = state.shape[0]
    return pl.pallas_call(
        _wkv_sample_kernel,
        grid=(n // nb,),
        in_specs=[
            pl.BlockSpec((8, nb, RWKV_WIDTH), lambda i: (0, i, 0)),
            pl.BlockSpec((nb, RWKV_HEADS, HEAD_DIM, HEAD_DIM), lambda i: (i, 0, 0, 0)),
        ],
        out_specs=(
            pl.BlockSpec((nb, RWKV_HEADS, HEAD_DIM, HEAD_DIM), lambda i: (i, 0, 0, 0)),
            pl.BlockSpec((nb, RWKV_WIDTH), lambda i: (i, 0)),
        ),
        out_shape=(
            jax.ShapeDtypeStruct(state.shape, F32),
            jax.ShapeDtypeStruct((n, RWKV_WIDTH), F32),
        ),
        compiler_params=pltpu.CompilerParams(
            dimension_semantics=("arbitrary",), vmem_limit_bytes=VMEM_LIMIT),
        name="wkv_sample",
    )(feat_s, state)


def _swa_prompt_kernel(sink_ref, q_ref, kp_ref, kc_ref, vp_ref, vc_ref, o_ref):
    j = pl.program_id(1)
    w = WINDOW
    tq = q_ref.shape[1]
    n_blk = tq // w
    gl = GROUP_LANES
    rows = SWA_GROUP * w
    probs = [(qi, g) for qi in range(n_blk) for g in range(SWA_KV_HEADS)]

    lane_head = lax.broadcasted_iota(jnp.int32, (w, gl), 1) // HEAD_DIM
    head_mask = [lane_head == h for h in range(SWA_GROUP)]
    qi_ = lax.broadcasted_iota(jnp.int32, (rows, 2 * w), 0) % w
    ki_ = lax.broadcasted_iota(jnp.int32, (rows, 2 * w), 1)
    diff = qi_ - (ki_ - w)
    in_window = (diff >= 0) & (diff < WINDOW)
    first_valid = in_window & ((j * tq + ki_ - w) >= 0)

    def keys(ref_prev, ref_cur, qi, g):
        ls = slice(g * gl, (g + 1) * gl)
        prev = ref_prev[0, :, ls] if qi == 0 else ref_cur[0, (qi - 1) * w:qi * w, ls]
        return jnp.concatenate([prev, ref_cur[0, qi * w:(qi + 1) * w, ls]], axis=0)

    def lhs(qi, g):
        qg = q_ref[0, qi * w:(qi + 1) * w, g * gl:(g + 1) * gl]
        return jnp.concatenate([jnp.where(m, qg, jnp.zeros((), BF16)) for m in head_mask], axis=0)

    sinks = []
    for g in range(SWA_KV_HEADS):
        sinks.append(jnp.concatenate(
            [jnp.full((w, 1), sink_ref[g * SWA_GROUP + h], F32) for h in range(SWA_GROUP)], axis=0))

    s = [_dot_nt(lhs(qi, g), keys(kp_ref, kc_ref, qi, g)) for qi, g in probs]
    s = [jnp.where(first_valid if qi == 0 else in_window, x, MASK_VALUE)
         for x, (qi, g) in zip(s, probs)]
    sink = [sinks[g] for qi, g in probs]
    m = _every(lambda x, sk: jnp.maximum(jnp.max(x, axis=-1, keepdims=True), sk), s, sink)
    e = _every(lambda x, m_: jnp.exp(x - m_), s, m)
    inv = _every(lambda e_, sk, m_: 1.0 / (jnp.sum(e_, axis=-1, keepdims=True) + jnp.exp(sk - m_)),
                 e, sink, m)
    prob = _every(lambda e_, i_: (e_ * i_).astype(BF16), e, inv)
    og = [_dot(p_, keys(vp_ref, vc_ref, qi, g)) for p_, (qi, g) in zip(prob, probs)]
    for x, (qi, g) in zip(og, probs):
        y = jnp.where(head_mask[0], x[0:w], 0.0)
        for h in range(1, SWA_GROUP):
            y = y + jnp.where(head_mask[h], x[h * w:(h + 1) * w], 0.0)
        o_ref[0, qi * w:(qi + 1) * w, g * gl:(g + 1) * gl] = y.astype(o_ref.dtype)


def _swa_prompt_call(sinks, q, k, v, tq):
    b, t, _ = q.shape
    w = WINDOW
    per = tq // w
    prev = lambda i, j: (i, jnp.maximum(j * per - 1, 0), 0)
    cur = lambda i, j: (i, j, 0)
    return pl.pallas_call(
        _swa_prompt_kernel,
        grid=(b, t // tq),
        in_specs=[
            pl.BlockSpec(memory_space=pltpu.SMEM),
            pl.BlockSpec((1, tq, SWA_WIDTH), cur),
            pl.BlockSpec((1, w, SWA_WIDTH), prev),
            pl.BlockSpec((1, tq, SWA_WIDTH), cur),
            pl.BlockSpec((1, w, SWA_WIDTH), prev),
            pl.BlockSpec((1, tq, SWA_WIDTH), cur),
        ],
        out_specs=pl.BlockSpec((1, tq, SWA_WIDTH), cur),
        out_shape=jax.ShapeDtypeStruct((b, t, SWA_WIDTH), BF16),
        compiler_params=pltpu.CompilerParams(
            dimension_semantics=("arbitrary", "arbitrary"), vmem_limit_bytes=VMEM_LIMIT),
        name="swa_prompt",
    )(sinks, q, k, k, v, v)


def _swa_sample_kernel(sink_ref, qbd_ref, kn_ref, vn_ref, ck_ref, cv_ref,
                       o_ref, ko_ref, vo_ref):
    l = ck_ref.shape[1]
    qbd = qbd_ref[...]
    kn = kn_ref[...]
    vn = vn_ref[...]
    ck = ck_ref[...]
    cv = cv_ref[...]
    ki = lax.broadcasted_iota(jnp.int32, (1, 1, l), 2)
    kpos = PAST_LEN - l + ki
    diff = PAST_LEN - kpos
    valid = (diff >= 0) & (diff < WINDOW) & (kpos >= 0)
    s_c = lax.dot_general(qbd, ck.astype(BF16), (((2,), (2,)), ((0,), (0,))),
                          preferred_element_type=F32)
    s_c = jnp.where(valid, s_c, MASK_VALUE)
    s_n = jnp.sum(qbd.astype(F32) * kn.astype(BF16).astype(F32)[:, None, :],
                  axis=-1, keepdims=True)
    sink = sink_ref[...][None, :, 0:1]
    m = jnp.maximum(jnp.maximum(jnp.max(s_c, axis=-1, keepdims=True), s_n), sink)
    e_c = jnp.exp(s_c - m)
    e_n = jnp.exp(s_n - m)
    denom = jnp.sum(e_c, axis=-1, keepdims=True) + e_n + jnp.exp(sink - m)
    p_c = (e_c / denom).astype(BF16)
    p_n = (e_n / denom).astype(BF16).astype(F32)
    o = lax.dot_general(p_c, cv.astype(BF16), (((2,), (1,)), ((0,), (0,))),
                        preferred_element_type=F32)
    o = o + p_n * vn.astype(BF16).astype(F32)[:, None, :]
    o_ref[...] = o
    ko_ref[:, 0:l - 1, :] = ck_ref[:, 1:l, :]
    ko_ref[:, l - 1:l, :] = kn[:, None, :]
    vo_ref[:, 0:l - 1, :] = cv_ref[:, 1:l, :]
    vo_ref[:, l - 1:l, :] = vn[:, None, :]


def _swa_sample_call(sink_b, qbd, kn, vn, ck, cv, nb):
    n, l, _ = ck.shape
    return pl.pallas_call(
        _swa_sample_kernel,
        grid=(n // nb,),
        in_specs=[
            pl.BlockSpec((SWA_HEADS, LANES), lambda i: (0, 0)),
            pl.BlockSpec((nb, SWA_HEADS, KV_WIDTH), lambda i: (i, 0, 0)),
            pl.BlockSpec((nb, KV_WIDTH), lambda i: (i, 0)),
            pl.BlockSpec((nb, KV_WIDTH), lambda i: (i, 0)),
            pl.BlockSpec((nb, l, KV_WIDTH), lambda i: (i, 0, 0)),
            pl.BlockSpec((nb, l, KV_WIDTH), lambda i: (i, 0, 0)),
        ],
        out_specs=(
            pl.BlockSpec((nb, SWA_HEADS, KV_WIDTH), lambda i: (i, 0, 0)),
            pl.BlockSpec((nb, l, KV_WIDTH), lambda i: (i, 0, 0)),
            pl.BlockSpec((nb, l, KV_WIDTH), lambda i: (i, 0, 0)),
        ),
        out_shape=(
            jax.ShapeDtypeStruct((n, SWA_HEADS, KV_WIDTH), F32),
            jax.ShapeDtypeStruct((n, l, KV_WIDTH), F32),
            jax.ShapeDtypeStruct((n, l, KV_WIDTH), F32),
        ),
        compiler_params=pltpu.CompilerParams(
            dimension_semantics=("arbitrary",), vmem_limit_bytes=VMEM_LIMIT),
        name="swa_sample",
    )(sink_b, qbd, kn, vn, ck, cv)


def _tail_mix(x, y_rwkv, y_swa, g1, sh2, sc2, w_out, n_post_mix, n_pre_ffn):
    mix = _dot(y_rwkv, w_out[:RWKV_WIDTH]) + _dot(y_swa, w_out[RWKV_WIDTH:])
    x1 = x + g1 * _rmsnorm(mix, n_post_mix)
    h2 = (_rmsnorm(x1, n_pre_ffn) * (1.0 + sc2) + sh2).astype(BF16)
    return x1, h2


def _ffn_gate(act_ref, cw, cb, taps_fn, n_rows):
    def row_block(rb, carry):
        r0 = pl.multiple_of(rb * FF_ROWS, FF_ROWS)
        for c in range(N_FF_CHUNKS):
            z = []
            for half in range(2):
                cols = slice(half * D_FF + c * FF_CHUNK, half * D_FF + (c + 1) * FF_CHUNK)
                p1, p2, u = taps_fn(r0, cols)
                z.append(cb[:, cols] + p2 * cw[0:1, cols] + p1 * cw[1:2, cols] + u * cw[2:3, cols])
            act_ref[pl.ds(r0, FF_ROWS), c * FF_CHUNK:(c + 1) * FF_CHUNK] = (
                _silu(z[0]) * z[1]).astype(BF16)
        return carry

    lax.fori_loop(0, n_rows // FF_ROWS, row_block, 0)


def _tail_prompt_kernel(x_ref, yr_ref, ys_ref, mod_ref, w_out, n_post_mix, n_pre_ffn,
                        n_post_ffn, w_up, w_down, cw_ref, cb_ref,
                        y_ref, cp_ref, carry_ref, act_ref):
    t = pl.program_id(1)
    tm = x_ref.shape[1]

    @pl.when(t == 0)
    def _():
        carry_ref[...] = jnp.zeros_like(carry_ref)

    g1, sh2, sc2, g2 = mod_ref[2, 0], mod_ref[3, 0], mod_ref[4, 0], mod_ref[5, 0]
    x1, h2 = _tail_mix(x_ref[0], yr_ref[0], ys_ref[0], g1, sh2, sc2, w_out[...],
                       n_post_mix[...], n_pre_ffn[...])

    def cols_of(c, half):
        return slice(half * D_FF + c * FF_CHUNK, half * D_FF + (c + 1) * FF_CHUNK)

    def up(c):
        return [jnp.dot(h2, w_up[:, cols_of(c, half)], preferred_element_type=F32)
                for half in range(2)]

    def conv(u, cols):
        ext = jnp.concatenate([carry_ref[:, cols], u], axis=0)
        last = u[tm - SUBLANES:]
        carry_ref[:, cols] = last
        cp_ref[0, :, cols] = last
        return (cb_ref[:, cols] + pltpu.roll(ext, 2, 0)[SUBLANES:] * cw_ref[0:1, cols]
                + pltpu.roll(ext, 1, 0)[SUBLANES:] * cw_ref[1:2, cols] + u * cw_ref[2:3, cols])

    u_next = up(0)
    for c in range(N_FF_CHUNKS):
        u_cur = u_next
        if c + 1 < N_FF_CHUNKS:
            u_next = up(c + 1)
        za, zb = [conv(u_cur[half], cols_of(c, half)) for half in range(2)]
        act_ref[:, c * FF_CHUNK:(c + 1) * FF_CHUNK] = (_silu(za) * zb).astype(BF16)
    ff = jnp.dot(act_ref[...], w_down[...], preferred_element_type=F32)
    y_ref[0] = x1 + g2 * _rmsnorm(ff, n_post_ffn[...])


def _tail_sample_kernel(x_ref, o_ref, feat_ref, ys_ref, mod_ref, p0_ref, p1_ref, ln_w, ln_b,
                        w_out, n_post_mix, n_pre_ffn, n_post_ffn, w_up, w_down,
                        cw_ref, cb_ref, y_ref, u_ref, act_ref):
    n = x_ref.shape[0]
    seg = _seg_ones(RWKV_WIDTH)
    y_rwkv = _gn_epilogue(o_ref[...], feat_ref[7], feat_ref[6], ln_w[...], ln_b[...], seg)
    g1, sh2, sc2, g2 = mod_ref[2], mod_ref[3], mod_ref[4], mod_ref[5]
    x1, h2 = _tail_mix(x_ref[...], y_rwkv, ys_ref[...], g1, sh2, sc2, w_out[...],
                       n_post_mix[...], n_pre_ffn[...])
    u_ref[...] = jnp.dot(h2, w_up[...], preferred_element_type=F32)

    def taps_fn(r0, cols):
        rows = pl.ds(r0, FF_ROWS)
        return p1_ref[rows, cols], p0_ref[rows, cols], u_ref[rows, cols]

    _ffn_gate(act_ref, cw_ref, cb_ref, taps_fn, n)
    ff = jnp.dot(act_ref[...], w_down[...], preferred_element_type=F32)
    y_ref[...] = x1 + g2 * _rmsnorm(ff, n_post_ffn[...])


def _tail_prompt_call(x, y_rwkv, y_swa, mod_p, consts, tm):
    b, t, _ = x.shape
    tok = lambda i, j: (i, j, 0)
    in_specs = [
        pl.BlockSpec((1, tm, D_MODEL), tok),
        pl.BlockSpec((1, tm, RWKV_WIDTH), tok),
        pl.BlockSpec((1, tm, SWA_WIDTH), tok),
        pl.BlockSpec((6, 1, 1, D_MODEL), lambda i, j: (0, i, 0, 0)),
    ] + [_const_spec(c, 2) for c in consts]
    return pl.pallas_call(
        _tail_prompt_kernel,
        grid=(b, t // tm),
        in_specs=in_specs,
        out_specs=(
            pl.BlockSpec((1, tm, D_MODEL), tok),
            pl.BlockSpec((1, SUBLANES, 2 * D_FF), lambda i, j: (i, 0, 0)),
        ),
        out_shape=(
            jax.ShapeDtypeStruct((b, t, D_MODEL), F32),
            jax.ShapeDtypeStruct((b, SUBLANES, 2 * D_FF), F32),
        ),
        scratch_shapes=[
            pltpu.VMEM((SUBLANES, 2 * D_FF), F32),
            pltpu.VMEM((tm, D_FF), BF16),
        ],
        compiler_params=pltpu.CompilerParams(
            dimension_semantics=("arbitrary", "arbitrary"), vmem_limit_bytes=VMEM_LIMIT),
        name="tail_prompt",
    )(x, y_rwkv, y_swa, mod_p, *consts)


def _tail_sample_call(x, o_s, feat_s, y_swa, mod_s, p0, p1, ln_w, ln_b, consts):
    n = x.shape[0]
    args = (x, o_s, feat_s, y_swa, mod_s, p0, p1, ln_w, ln_b) + tuple(consts)
    out_shape = (
        jax.ShapeDtypeStruct((n, D_MODEL), F32),
        jax.ShapeDtypeStruct((n, 2 * D_FF), F32),
    )
    return pl.pallas_call(
        _tail_sample_kernel,
        grid=(1,),
        in_specs=[_const_spec(a, 1) for a in args],
        out_specs=tuple(pl.BlockSpec(s.shape, lambda i, nd=len(s.shape): (0,) * nd)
                        for s in out_shape),
        out_shape=out_shape,
        scratch_shapes=[pltpu.VMEM((n, D_FF), BF16)],
        compiler_params=pltpu.CompilerParams(
            dimension_semantics=("arbitrary",), vmem_limit_bytes=VMEM_LIMIT),
        name="tail_sample",
    )(*args)


def _rope_tables(pos):
    half = HEAD_DIM // 2
    inv = ROPE_THETA ** (-jnp.arange(half, dtype=F32) / half)
    ang = pos.astype(F32)[:, None] * inv[None, :]
    cos, sin = jnp.cos(ang), jnp.sin(ang)
    cos_h = jnp.concatenate([cos, cos], axis=-1)
    sin_h = jnp.concatenate([-sin, sin], axis=-1)
    return jnp.tile(cos_h, (1, LANES // HEAD_DIM)), jnp.tile(sin_h, (1, LANES // HEAD_DIM))


def kernel(x_prompt, x_sample, state_rwkv_wkv, state_rwkv_shift, cache_swa_k, cache_swa_v,
           state_ffn_conv, c_prompt, c_sample, w_ada, b_ada, norm_pre_mix, norm_post_mix,
           norm_pre_ffn, norm_post_ffn, w_in, rwkv_mu, rwkv_w0, rwkv_w_up, rwkv_a0, rwkv_a_up,
           rwkv_g_up, rwkv_k_k, rwkv_k_a, rwkv_r_k, rwkv_ln_w, rwkv_ln_b, swa_sinks, w_out,
           ffn_w_up, ffn_conv_w, ffn_conv_b, ffn_w_down):
    depth = w_ada.shape[0]
    assert depth == 1 and x_sample.shape[1] == 1
    b, t, _ = x_prompt.shape
    n = x_sample.shape[0]
    l = cache_swa_k.shape[2]
    tm = min(256, t)
    assert t % tm == 0 and tm % WINDOW == 0 and tm % CHUNK == 0
    assert tm % FF_ROWS == 0 and n % FF_ROWS == 0
    nb_wkv = 8 if n % 8 == 0 else n
    nb_swa = 8 if n % 8 == 0 else n
    li = 0

    row = lambda v: v.reshape(1, -1)
    w_in_b = w_in[li].astype(BF16)
    zeros_l = jnp.zeros((64, RWKV_WIDTH), F32)
    wa_up = jnp.concatenate([
        jnp.concatenate([rwkv_w_up[li], zeros_l], axis=1),
        jnp.concatenate([zeros_l, rwkv_a_up[li]], axis=1)], axis=0).astype(BF16)
    hid = jnp.arange(RWKV_WIDTH) // HEAD_DIM
    seg512 = (hid[:, None] == hid[None, :]).astype(BF16)
    proj_consts = (row(norm_pre_mix[li]), w_in_b, row(rwkv_mu[li]), row(rwkv_w0[li]), wa_up,
                   row(rwkv_a0[li]), rwkv_g_up[li].astype(BF16), row(rwkv_k_k[li]),
                   row(rwkv_k_a[li]), row(rwkv_r_k[li]), seg512)
    tail_consts = (w_out[li].astype(BF16), row(norm_post_mix[li]), row(norm_pre_ffn[li]),
                   row(norm_post_ffn[li]), ffn_w_up[li].astype(BF16), ffn_w_down[li].astype(BF16),
                   ffn_conv_w[li], row(ffn_conv_b[li]))
    ln_w, ln_b = row(rwkv_ln_w[li]), row(rwkv_ln_b[li])

    mod = _ada_call(jnp.concatenate([c_prompt, c_sample], axis=0), w_ada[li], row(b_ada[li]))
    mod_p = mod[:, :b].reshape(6, b, 1, D_MODEL)
    mod_s = mod[:, b:]

    cos_p, sin_p = _rope_tables(jnp.arange(t, dtype=jnp.int32))
    y_rwkv_p, s_cat, q_p, kx_p, vx_p, klast, vlast, plast = _mix_prompt_call(
        x_prompt, mod_p, cos_p, sin_p, proj_consts + (ln_w, ln_b), tm)
    y_swa_p = _swa_prompt_call(swa_sinks[li], q_p, kx_p, vx_p, min(2 * tm, t))
    y_p, cp = _tail_prompt_call(x_prompt, y_rwkv_p, y_swa_p, mod_p, tail_consts, tm)

    wkv_p = s_cat.reshape(b, HEAD_DIM, RWKV_HEADS, HEAD_DIM).transpose(0, 2, 1, 3)
    shift_p = plast[:, SUBLANES - 1]
    k_p = klast.reshape(b, WINDOW, SWA_KV_HEADS, HEAD_DIM)
    v_p = vlast.reshape(b, WINDOW, SWA_KV_HEADS, HEAD_DIM)
    conv_p = cp[:, SUBLANES - 2:]

    cos_s, sin_s = _rope_tables(jnp.full((1,), PAST_LEN, jnp.int32))
    feat_s, q_s, kn_s, vn_s, p_s = _proj_sample_call(
        x_sample[:, 0], mod_s, cos_s, sin_s, state_rwkv_shift[li], proj_consts)
    wkv_s, o_s = _wkv_sample_call(feat_s, state_rwkv_wkv[li], nb_wkv)

    head_group = jnp.arange(SWA_HEADS) // SWA_GROUP
    sel = (head_group[:, None] == jnp.arange(SWA_KV_HEADS)[None, :])
    q4 = q_s.reshape(n, SWA_HEADS, 1, HEAD_DIM)
    qbd = jnp.where(sel[None, :, :, None], q4, jnp.zeros((), BF16)).reshape(n, SWA_HEADS, KV_WIDTH)
    sink_b = jnp.broadcast_to(swa_sinks[li][:, None], (SWA_HEADS, LANES))
    ck = cache_swa_k[li].reshape(n, l, KV_WIDTH)
    cv = cache_swa_v[li].reshape(n, l, KV_WIDTH)
    o_att, k_s, v_s = _swa_sample_call(sink_b, qbd, kn_s, vn_s, ck, cv, nb_swa)
    o4 = o_att.reshape(n, SWA_HEADS, SWA_KV_HEADS, HEAD_DIM)
    y_swa_s = jnp.sum(jnp.where(sel[None, :, :, None], o4, 0.0), axis=2).reshape(n, SWA_WIDTH)

    conv0 = state_ffn_conv[li]
    y_s, u_s = _tail_sample_call(x_sample[:, 0], o_s, feat_s, y_swa_s.astype(BF16), mod_s,
                                 conv0[:, 0], conv0[:, 1], ln_w, ln_b, tail_consts)
    conv_s = jnp.stack([conv0[:, 1], u_s], axis=1)

    expand = lambda a: a[None]
    return (y_p, y_s[:, None, :], expand(wkv_p), expand(shift_p), expand(k_p), expand(v_p),
            expand(conv_p), expand(wkv_s), expand(p_s),
            expand(k_s.reshape(n, l, SWA_KV_HEADS, HEAD_DIM)),
            expand(v_s.reshape(n, l, SWA_KV_HEADS, HEAD_DIM)), expand(conv_s))
```

```python
import math

import jax
import jax.numpy as jnp
from jax import lax
from jax.experimental import pallas as pl
from jax.experimental.pallas import tpu as pltpu

D_MODEL = 1024
HEAD_DIM = 64
RWKV_WIDTH = 512
RWKV_HEADS = 8
RWKV_COLS = 1792
RWKV_GN_EPS = 64e-5
SWA_WIDTH = 512
SWA_HEADS = 8
SWA_KV_HEADS = 2
SWA_GROUP = 4
KV_WIDTH = SWA_KV_HEADS * HEAD_DIM
WINDOW = 128
PAST_LEN = 16384
ROPE_THETA = 10000.0
ATTN_SCALE = HEAD_DIM ** -0.5
D_FF = 2816
NORM_EPS = 1e-6
MASK_VALUE = -1e30
PROJ_COLS = RWKV_COLS + SWA_WIDTH + 2 * KV_WIDTH

LANES = 128
SUBLANES = 8
CHUNK = 64
GROUP_LANES = 256
FF_CHUNK = 256
N_FF_CHUNKS = D_FF // FF_CHUNK
FF_ROWS = 32
VMEM_LIMIT = 56 * 1024 * 1024

F32 = jnp.float32
BF16 = jnp.bfloat16


def _sigmoid(x):
    return 1.0 / (1.0 + jnp.exp(-x))


def _silu(x):
    return x * _sigmoid(x)


def _rmsnorm(x, g):
    return x * lax.rsqrt(jnp.mean(x * x, axis=-1, keepdims=True) + NORM_EPS) * g


def _dot(a, b):
    return jnp.dot(a.astype(BF16), b.astype(BF16), preferred_element_type=F32)


def _dot_nt(a, b):
    return lax.dot_general(a.astype(BF16), b.astype(BF16), (((1,), (1,)), ((), ())),
                           preferred_element_type=F32)


def _dot_tn(a, b):
    return lax.dot_general(a.astype(BF16), b.astype(BF16), (((0,), (0,)), ((), ())),
                           preferred_element_type=F32)


def _every(fn, *lists):
    return [fn(*a) for a in zip(*lists)]


def _swap_halves(x):
    w = x.shape[-1]
    lane = lax.broadcasted_iota(jnp.int32, x.shape, x.ndim - 1)
    lo = (lane & (HEAD_DIM // 2)) == 0
    return jnp.where(lo, pltpu.roll(x, w - HEAD_DIM // 2, x.ndim - 1),
                     pltpu.roll(x, HEAD_DIM // 2, x.ndim - 1))


def _rope(x, cos, sin):
    reps = x.shape[-1] // LANES
    cos_w = jnp.concatenate([cos] * reps, axis=-1) if reps > 1 else cos
    sin_w = jnp.concatenate([sin] * reps, axis=-1) if reps > 1 else sin
    return x * cos_w + _swap_halves(x) * sin_w


def _proj_features(x, shift, scale, g_pre, w_in, prev_fn, mu, w0, wa_up, a0, g_up,
                   k_k, k_a, r_k, seg512, cos, sin):
    h = _rmsnorm(x, g_pre) * (1.0 + scale) + shift
    p = _dot(h, w_in)
    p_rwkv = p[:, :RWKV_COLS]
    prev = prev_fn(p_rwkv)
    xm = p_rwkv + (prev - p_rwkv) * mu
    r = xm[:, 0:512]
    k = xm[:, 512:1024]
    v = xm[:, 1024:1536]
    wa = xm[:, 1536:1664]
    gd = xm[:, 1664:1792]
    lane = lax.broadcasted_iota(jnp.int32, wa.shape, 1)
    wa_act = jnp.where(lane < 64, jnp.tanh(wa), wa)
    lora = _dot(wa_act, wa_up)
    lw = -math.exp(-0.5) * _sigmoid(w0 + lora[:, :512])
    a = _sigmoid(a0 + lora[:, 512:])
    g = _dot(_sigmoid(gd), g_up)
    kk = k * k_k
    ss = _dot(kk * kk, seg512)
    kk = kk / jnp.maximum(jnp.sqrt(ss), 1e-12)
    kf = k * (1.0 + (a - 1.0) * k_a)
    bonus = _dot(r * kf * r_k, seg512) * v
    feats = (r, lw, kf, v, -kk, kk * a, g, bonus)
    q = _rope(p[:, RWKV_COLS:RWKV_COLS + SWA_WIDTH], cos, sin)
    ks = _rope(p[:, RWKV_COLS + SWA_WIDTH:RWKV_COLS + SWA_WIDTH + KV_WIDTH], cos, sin)
    vs = p[:, RWKV_COLS + SWA_WIDTH + KV_WIDTH:]
    return feats, q, ks, vs, p_rwkv


def _gn_epilogue(o, bonus, g, ln_w, ln_b, seg):
    mu = _dot(o, seg) * (1.0 / HEAD_DIM)
    d = o - mu
    var = _dot(d * d, seg) * (1.0 / HEAD_DIM)
    gn = d * lax.rsqrt(var + RWKV_GN_EPS) * ln_w + ln_b
    return (gn + bonus) * g


def _seg_ones(n):
    r = lax.broadcasted_iota(jnp.int32, (n, n), 0) // HEAD_DIM
    c = lax.broadcasted_iota(jnp.int32, (n, n), 1) // HEAD_DIM
    return (r == c).astype(BF16)


def _ada_kernel(c_ref, w_ref, b_ref, o_ref):
    o_ref[0] = _dot(_silu(c_ref[...]), w_ref[...]) + b_ref[...]


def _ada_call(c_all, w_ada, b_ada):
    rows = c_all.shape[0]
    return pl.pallas_call(
        _ada_kernel,
        grid=(6,),
        in_specs=[
            pl.BlockSpec((rows, D_MODEL), lambda j: (0, 0)),
            pl.BlockSpec((D_MODEL, D_MODEL), lambda j: (0, j)),
            pl.BlockSpec((1, D_MODEL), lambda j: (0, j)),
        ],
        out_specs=pl.BlockSpec((1, rows, D_MODEL), lambda j: (j, 0, 0)),
        out_shape=jax.ShapeDtypeStruct((6, rows, D_MODEL), F32),
        compiler_params=pltpu.CompilerParams(
            dimension_semantics=("arbitrary",), vmem_limit_bytes=VMEM_LIMIT),
        name="ada",
    )(c_all, w_ada, b_ada)


def _expand_kv(x):
    lane = lax.broadcasted_iota(jnp.int32, x.shape, 1)
    rolled = pltpu.roll(x, HEAD_DIM, 1)
    g0 = jnp.where(lane < HEAD_DIM, x, rolled)
    g1 = jnp.where(lane < HEAD_DIM, rolled, x)
    return jnp.concatenate([g0, g0, g1, g1], axis=1)


def _mix_prompt_kernel(x_ref, mod_ref, cos_ref, sin_ref, g_pre, w_in, mu, w0, wa_up, a0,
                       g_up, k_k, k_a, r_k, seg512, ln_w, ln_b,
                       y_ref, s_out_ref, q_ref, k_ref, v_ref, klast_ref, vlast_ref, plast_ref,
                       carry_ref, s_ref):
    t = pl.program_id(1)

    @pl.when(t == 0)
    def _():
        carry_ref[...] = jnp.zeros_like(carry_ref)
        s_ref[...] = jnp.zeros_like(s_ref)

    tm = x_ref.shape[1]
    carry_row = carry_ref[SUBLANES - 1:SUBLANES, :]

    def prev_fn(p_rwkv):
        row = lax.broadcasted_iota(jnp.int32, p_rwkv.shape, 0)
        return jnp.where(row == 0, carry_row, pltpu.roll(p_rwkv, 1, 0))

    feats, q, ks, vs, p_rwkv = _proj_features(
        x_ref[0], mod_ref[0, 0], mod_ref[1, 0], g_pre[...], w_in[...], prev_fn, mu[...],
        w0[...], wa_up[...], a0[...], g_up[...], k_k[...], k_a[...], r_k[...], seg512[...],
        cos_ref[...], sin_ref[...])
    q_ref[0] = (q * ATTN_SCALE).astype(BF16)
    k_ref[0] = _expand_kv(ks).astype(BF16)
    v_ref[0] = _expand_kv(vs).astype(BF16)
    last = p_rwkv[tm - SUBLANES:, :]
    carry_ref[...] = last
    plast_ref[0] = last
    _wkv_tile(feats, ln_w, ln_b, y_ref, s_ref)

    @pl.when(t == pl.num_programs(1) - 1)
    def _():
        klast_ref[0] = ks[tm - WINDOW:, :]
        vlast_ref[0] = vs[tm - WINDOW:, :]
        for g in range(RWKV_WIDTH // GROUP_LANES):
            s_out_ref[0, :, g * GROUP_LANES:(g + 1) * GROUP_LANES] = s_ref[g]


def _proj_sample_kernel(x_ref, mod_ref, cos_ref, sin_ref, prev_ref, g_pre, w_in, mu, w0,
                        wa_up, a0, g_up, k_k, k_a, r_k, seg512,
                        feat_ref, ft_ref, q_ref, k_ref, v_ref, p_ref):
    feats, q, ks, vs, p_rwkv = _proj_features(
        x_ref[...], mod_ref[0], mod_ref[1], g_pre[...], w_in[...], lambda p: prev_ref[...],
        mu[...], w0[...], wa_up[...], a0[...], g_up[...], k_k[...], k_a[...], r_k[...],
        seg512[...], cos_ref[...], sin_ref[...])
    for i, f in enumerate(feats):
        feat_ref[i] = f
    for i in range(6):
        ft_ref[i] = feats[i].T
    q_ref[...] = (q * ATTN_SCALE).astype(BF16)
    k_ref[...] = ks
    v_ref[...] = vs
    p_ref[...] = p_rwkv


def _const_spec(arr, grid_rank):
    zeros = (0,) * arr.ndim
    if grid_rank == 1:
        return pl.BlockSpec(arr.shape, lambda i: zeros)
    return pl.BlockSpec(arr.shape, lambda i, j: zeros)


def _mix_prompt_call(x, mod_p, cos, sin, consts, tm):
    b, t, _ = x.shape
    nt = t // tm
    in_specs = [
        pl.BlockSpec((1, tm, D_MODEL), lambda i, j: (i, j, 0)),
        pl.BlockSpec((6, 1, 1, D_MODEL), lambda i, j: (0, i, 0, 0)),
        pl.BlockSpec((tm, LANES), lambda i, j: (j, 0)),
        pl.BlockSpec((tm, LANES), lambda i, j: (j, 0)),
    ] + [_const_spec(c, 2) for c in consts]
    out_shape = (
        jax.ShapeDtypeStruct((b, t, RWKV_WIDTH), BF16),
        jax.ShapeDtypeStruct((b, HEAD_DIM, RWKV_WIDTH), F32),
        jax.ShapeDtypeStruct((b, t, SWA_WIDTH), BF16),
        jax.ShapeDtypeStruct((b, t, SWA_WIDTH), BF16),
        jax.ShapeDtypeStruct((b, t, SWA_WIDTH), BF16),
        jax.ShapeDtypeStruct((b, WINDOW, KV_WIDTH), F32),
        jax.ShapeDtypeStruct((b, WINDOW, KV_WIDTH), F32),
        jax.ShapeDtypeStruct((b, SUBLANES, RWKV_COLS), F32),
    )
    out_specs = (
        pl.BlockSpec((1, tm, RWKV_WIDTH), lambda i, j: (i, j, 0)),
        pl.BlockSpec((1, HEAD_DIM, RWKV_WIDTH), lambda i, j: (i, 0, 0)),
        pl.BlockSpec((1, tm, SWA_WIDTH), lambda i, j: (i, j, 0)),
        pl.BlockSpec((1, tm, SWA_WIDTH), lambda i, j: (i, j, 0)),
        pl.BlockSpec((1, tm, SWA_WIDTH), lambda i, j: (i, j, 0)),
        pl.BlockSpec((1, WINDOW, KV_WIDTH), lambda i, j: (i, 0, 0)),
        pl.BlockSpec((1, WINDOW, KV_WIDTH), lambda i, j: (i, 0, 0)),
        pl.BlockSpec((1, SUBLANES, RWKV_COLS), lambda i, j: (i, 0, 0)),
    )
    return pl.pallas_call(
        _mix_prompt_kernel,
        grid=(b, nt),
        in_specs=in_specs,
        out_specs=out_specs,
        out_shape=out_shape,
        scratch_shapes=[
            pltpu.VMEM((SUBLANES, RWKV_COLS), F32),
            pltpu.VMEM((RWKV_WIDTH // GROUP_LANES, HEAD_DIM, GROUP_LANES), F32),
        ],
        compiler_params=pltpu.CompilerParams(
            dimension_semantics=("arbitrary", "arbitrary"), vmem_limit_bytes=VMEM_LIMIT),
        name="mix_prompt",
    )(x, mod_p, cos, sin, *consts)


def _proj_sample_call(x, mod_s, cos, sin, prev, consts):
    n = x.shape[0]
    args = (x, mod_s, cos, sin, prev) + tuple(consts)
    out_shape = (
        jax.ShapeDtypeStruct((8, n, RWKV_WIDTH), F32),
        jax.ShapeDtypeStruct((6, RWKV_WIDTH, n), F32),
        jax.ShapeDtypeStruct((n, SWA_WIDTH), BF16),
        jax.ShapeDtypeStruct((n, KV_WIDTH), F32),
        jax.ShapeDtypeStruct((n, KV_WIDTH), F32),
        jax.ShapeDtypeStruct((n, RWKV_COLS), F32),
    )
    return pl.pallas_call(
        _proj_sample_kernel,
        grid=(1,),
        in_specs=[_const_spec(a, 1) for a in args],
        out_specs=tuple(pl.BlockSpec(s.shape, lambda i, nd=len(s.shape): (0,) * nd)
                        for s in out_shape),
        out_shape=out_shape,
        compiler_params=pltpu.CompilerParams(
            dimension_semantics=("arbitrary",), vmem_limit_bytes=VMEM_LIMIT),
        name="proj_sample",
    )(*args)


def _wkv_tile(feats, ln_w, ln_b, y_ref, s_ref):
    tt = feats[0].shape[0]
    n_chunks = tt // CHUNK
    gl = GROUP_LANES
    n_groups = RWKV_WIDTH // gl
    heads_per_group = gl // HEAD_DIM
    probs = [(c, g) for c in range(n_chunks) for g in range(n_groups)]

    row_c = lax.broadcasted_iota(jnp.int32, (CHUNK, gl), 0)
    col_c = lax.broadcasted_iota(jnp.int32, (CHUNK, gl), 1) % CHUNK
    strict = row_c > col_c
    incl = row_c >= col_c
    eye_cat = (row_c == col_c).astype(F32)
    rb = lax.broadcasted_iota(jnp.int32, (gl, gl), 0) // HEAD_DIM
    cb = lax.broadcasted_iota(jnp.int32, (gl, gl), 1) // HEAD_DIM
    bd_mask = rb == cb
    tri_r = lax.broadcasted_iota(jnp.int32, (CHUNK, CHUNK), 0)
    tri_c = lax.broadcasted_iota(jnp.int32, (CHUNK, CHUNK), 1)
    tril_ones = (tri_r >= tri_c).astype(BF16)
    seg = bd_mask.astype(BF16)

    def bd(x):
        xb = x.astype(BF16)
        return jnp.where(bd_mask, jnp.concatenate([xb] * heads_per_group, axis=0),
                         jnp.zeros((), BF16))

    def fold(x):
        xm = jnp.where(bd_mask, x, 0.0)
        acc = xm[0:HEAD_DIM]
        for hh in range(1, heads_per_group):
            acc = acc + xm[hh * HEAD_DIM:(hh + 1) * HEAD_DIM]
        return acc

    def ld(i, p):
        c, g = p
        return feats[i][c * CHUNK:(c + 1) * CHUNK, g * gl:(g + 1) * gl]

    lw = [ld(1, p) for p in probs]
    na = [ld(4, p) for p in probs]
    bb = [ld(5, p) for p in probs]
    kf = [ld(2, p) for p in probs]
    r = [ld(0, p) for p in probs]
    v = [ld(3, p) for p in probs]

    def cumsum(x):
        hi = x.astype(BF16)
        lo = (x - hi.astype(F32)).astype(BF16)
        both = jnp.dot(tril_ones, jnp.concatenate([hi, lo], axis=1), preferred_element_type=F32)
        return both[:, :gl] + both[:, gl:]

    cum = _every(cumsum, lw)
    cum_last = [x[CHUNK - 1:CHUNK, :] for x in cum]
    e_out = [jnp.exp(-x) for x in cum]
    e_end = _every(lambda cl, x: jnp.exp(cl - x), cum_last, cum)
    a_t = _every(lambda n_, x, l_: n_ * jnp.exp(x - l_), na, cum, lw)
    r_t = _every(lambda r_, x: r_ * jnp.exp(x), r, cum)
    b_t = _every(lambda b_, e: b_ * e, bb, e_out)
    k_t = _every(lambda k_, e: k_ * e, kf, e_out)
    b_end = _every(lambda b_, e: b_ * e, bb, e_end)
    k_end = _every(lambda k_, e: k_ * e, kf, e_end)
    gamma = [jnp.exp(x) for x in cum_last]

    ar = _every(lambda a_, r_: jnp.concatenate([a_, r_], axis=0), a_t, r_t)
    pb = _every(lambda x, y: _dot_nt(x, bd(y)), ar, b_t)
    pk = _every(lambda x, y: _dot_nt(x, bd(y)), ar, k_t)
    l_ab = [jnp.where(strict, x[:CHUNK], 0.0) for x in pb]
    l_ak = [jnp.where(strict, x[:CHUNK], 0.0) for x in pk]
    m_rb = [jnp.where(incl, x[CHUNK:], 0.0) for x in pb]
    m_rk = [jnp.where(incl, x[CHUNK:], 0.0) for x in pk]

    x_acc = [eye_cat + l for l in l_ab]
    pw = _every(lambda l: _dot(l, bd(l)), l_ab)
    n_sq = int(math.log2(CHUNK)) - 1
    for lvl in range(n_sq):
        rhs = [bd(p_) for p_ in pw]
        if lvl < n_sq - 1:
            both = _every(lambda x, p_, w_: _dot(jnp.concatenate([x, p_], axis=0), w_),
                          x_acc, pw, rhs)
            x_acc = _every(lambda x, b_: x + b_[:CHUNK], x_acc, both)
            pw = [b_[CHUNK:] for b_ in both]
        else:
            x_acc = _every(lambda x, w_: x + _dot(x, w_), x_acc, rhs)
    t_inv = x_acc

    bd_v = [bd(x) for x in v]
    y_loc = _every(_dot, l_ak, bd_v)
    wu = _every(lambda t_, a_, y_: _dot(t_, jnp.concatenate([bd(a_), bd(y_)], axis=1)),
                t_inv, a_t, y_loc)
    w_t = [x[:, :gl] for x in wu]
    u_loc = [x[:, gl:] for x in wu]

    q_c = _every(lambda r_, m_, w_: r_ + _dot(m_, bd(w_)), r_t, m_rb, w_t)
    o_loc = _every(lambda mb, u_, mk, bv: _dot(jnp.concatenate([mb, mk], axis=1),
                                               jnp.concatenate([bd(u_), bv], axis=0)),
                   m_rb, u_loc, m_rk, bd_v)
    m_low = _every(lambda w_, b_: jnp.where(bd_mask, _dot_tn(w_, b_), 0.0).astype(BF16),
                   w_t, b_end)
    n_loc = _every(lambda u_, v_, b_, k_: fold(_dot_tn(jnp.concatenate([u_, v_], axis=0),
                                                       jnp.concatenate([b_, k_], axis=0))),
                   u_loc, v, b_end, k_end)

    state = [s_ref[g] for g in range(n_groups)]
    starts = []
    for c in range(n_chunks):
        starts.append(state)
        idx = [c * n_groups + g for g in range(n_groups)]
        state = [state[g] * gamma[i] + _dot(state[g], m_low[i]) + n_loc[i]
                 for g, i in enumerate(idx)]
    for g in range(n_groups):
        s_ref[g] = state[g]

    s0 = [starts[c][g] for (c, g) in probs]
    o = _every(lambda q_, s_, ol: _dot_nt(q_, bd(s_)) + ol, q_c, s0, o_loc)

    n_p = len(probs)
    unstack = lambda x: [x[i * CHUNK:(i + 1) * CHUNK] for i in range(n_p)]
    mu = unstack(_dot(jnp.concatenate(o, axis=0), seg) * (1.0 / HEAD_DIM))
    d = _every(lambda x, m_: x - m_, o, mu)
    var = unstack(_dot(jnp.concatenate([x * x for x in d], axis=0), seg) * (1.0 / HEAD_DIM))
    for i, (c, g) in enumerate(probs):
        ls = slice(g * gl, (g + 1) * gl)
        gn = d[i] * lax.rsqrt(var[i] + RWKV_GN_EPS) * ln_w[:, ls] + ln_b[:, ls]
        y = (gn + ld(7, (c, g))) * ld(6, (c, g))
        y_ref[0, c * CHUNK:(c + 1) * CHUNK, ls] = y.astype(y_ref.dtype)


def _wkv_sample_kernel(ft_ref, s_ref, s_out_ref, o_ref):
    hd = HEAD_DIM
    r, kf, na, bb = ft_ref[0], ft_ref[2], ft_ref[4], ft_ref[5]
    w = jnp.exp(ft_ref[1])

    def value_block(vb, carry):
        v0 = pl.multiple_of(vb * SUBLANES, SUBLANES)
        v_rows = ft_ref[3, pl.ds(v0, SUBLANES), :]
        outs = []
        for j in range(SUBLANES):
            s = s_ref[0, v0 + j]
            sa = jnp.sum(s * na, axis=0, keepdims=True)
            s_new = s * w + sa * bb + v_rows[j:j + 1] * kf
            s_out_ref[0, v0 + j] = s_new
            outs.append(jnp.sum(s_new * r, axis=0, keepdims=True))
        o_ref[pl.ds(v0, SUBLANES), :] = jnp.concatenate(outs, axis=0)
        return carry

    lax.fori_loop(0, hd // SUBLANES, value_block, 0)


def _wkv_sample_call(ft_s, state_t):
    h, hd, _, n = state_t.shape
    return pl.pallas_call(
        _wkv_sample_kernel,
        grid=(h,),
        in_specs=[
            pl.BlockSpec((6, hd, n), lambda i: (0, i, 0)),
            pl.BlockSpec((1, hd, hd, n), lambda i: (i, 0, 0, 0)),
        ],
        out_specs=(
            pl.BlockSpec((1, hd, hd, n), lambda i: (i, 0, 0, 0)),
            pl.BlockSpec((hd, n), lambda i: (i, 0)),
        ),
        out_shape=(
            jax.ShapeDtypeStruct(state_t.shape, F32),
            jax.ShapeDtypeStruct((h * hd, n), F32),
        ),
        compiler_params=pltpu.CompilerParams(
            dimension_semantics=("arbitrary",), vmem_limit_bytes=VMEM_LIMIT),
        name="wkv_sample",
    )(ft_s, state_t)


def _swa_prompt_kernel(sink_ref, q_ref, kp_ref, kc_ref, vp_ref, vc_ref, o_ref):
    j = pl.program_id(1)
    w = WINDOW
    tq = q_ref.shape[1]
    n_blk = tq // w
    gl = GROUP_LANES
    rows = SWA_GROUP * w
    probs = [(qi, g) for qi in range(n_blk) for g in range(SWA_KV_HEADS)]

    lane_head = lax.broadcasted_iota(jnp.int32, (w, gl), 1) // HEAD_DIM
    head_mask = [lane_head == h for h in range(SWA_GROUP)]
    qi_ = lax.broadcasted_iota(jnp.int32, (rows, 2 * w), 0) % w
    ki_ = lax.broadcasted_iota(jnp.int32, (rows, 2 * w), 1)
    diff = qi_ - (ki_ - w)
    in_window = (diff >= 0) & (diff < WINDOW)
    first_valid = in_window & ((j * tq + ki_ - w) >= 0)

    def keys(ref_prev, ref_cur, qi, g):
        ls = slice(g * gl, (g + 1) * gl)
        prev = ref_prev[0, :, ls] if qi == 0 else ref_cur[0, (qi - 1) * w:qi * w, ls]
        return jnp.concatenate([prev, ref_cur[0, qi * w:(qi + 1) * w, ls]], axis=0)

    def lhs(qi, g):
        qg = q_ref[0, qi * w:(qi + 1) * w, g * gl:(g + 1) * gl]
        return jnp.concatenate([jnp.where(m, qg, jnp.zeros((), BF16)) for m in head_mask], axis=0)

    sinks = []
    for g in range(SWA_KV_HEADS):
        sinks.append(jnp.concatenate(
            [jnp.full((w, 1), sink_ref[g * SWA_GROUP + h], F32) for h in range(SWA_GROUP)], axis=0))

    s = [_dot_nt(lhs(qi, g), keys(kp_ref, kc_ref, qi, g)) for qi, g in probs]
    s = [jnp.where(first_valid if qi == 0 else in_window, x, MASK_VALUE)
         for x, (qi, g) in zip(s, probs)]
    sink = [sinks[g] for qi, g in probs]
    m = _every(lambda x, sk: jnp.maximum(jnp.max(x, axis=-1, keepdims=True), sk), s, sink)
    e = _every(lambda x, m_: jnp.exp(x - m_), s, m)
    inv = _every(lambda e_, sk, m_: 1.0 / (jnp.sum(e_, axis=-1, keepdims=True) + jnp.exp(sk - m_)),
                 e, sink, m)
    prob = _every(lambda e_, i_: (e_ * i_).astype(BF16), e, inv)
    og = [_dot(p_, keys(vp_ref, vc_ref, qi, g)) for p_, (qi, g) in zip(prob, probs)]
    for x, (qi, g) in zip(og, probs):
        y = jnp.where(head_mask[0], x[0:w], 0.0)
        for h in range(1, SWA_GROUP):
            y = y + jnp.where(head_mask[h], x[h * w:(h + 1) * w], 0.0)
        o_ref[0, qi * w:(qi + 1) * w, g * gl:(g + 1) * gl] = y.astype(o_ref.dtype)


def _swa_prompt_call(sinks, q, k, v, tq):
    b, t, _ = q.shape
    w = WINDOW
    per = tq // w
    prev = lambda i, j: (i, jnp.maximum(j * per - 1, 0), 0)
    cur = lambda i, j: (i, j, 0)
    return pl.pallas_call(
        _swa_prompt_kernel,
        grid=(b, t // tq),
        in_specs=[
            pl.BlockSpec(memory_space=pltpu.SMEM),
            pl.BlockSpec((1, tq, SWA_WIDTH), cur),
            pl.BlockSpec((1, w, SWA_WIDTH), prev),
            pl.BlockSpec((1, tq, SWA_WIDTH), cur),
            pl.BlockSpec((1, w, SWA_WIDTH), prev),
            pl.BlockSpec((1, tq, SWA_WIDTH), cur),
        ],
        out_specs=pl.BlockSpec((1, tq, SWA_WIDTH), cur),
        out_shape=jax.ShapeDtypeStruct((b, t, SWA_WIDTH), BF16),
        compiler_params=pltpu.CompilerParams(
            dimension_semantics=("arbitrary", "arbitrary"), vmem_limit_bytes=VMEM_LIMIT),
        name="swa_prompt",
    )(sinks, q, k, k, v, v)


def _swa_sample_kernel(sink_ref, q_ref, knr_ref, vnr_ref, knt_ref, vnt_ref, ck_ref, cv_ref,
                       o_ref, ko_ref, vo_ref):
    nb, _, hd, l = ck_ref.shape
    q = q_ref[...]
    ck = ck_ref[...].reshape(nb * SWA_KV_HEADS, hd, l)
    cv = cv_ref[...].reshape(nb * SWA_KV_HEADS, hd, l)
    knr = knr_ref[...].astype(BF16).astype(F32)
    vnr = vnr_ref[...].astype(BF16).astype(F32)
    ki = lax.broadcasted_iota(jnp.int32, (1, 1, l), 2)
    kpos = PAST_LEN - l + ki
    diff = PAST_LEN - kpos
    valid = (diff >= 0) & (diff < WINDOW) & (kpos >= 0)
    s_c = lax.dot_general(q, ck.astype(BF16), (((2,), (1,)), ((0,), (0,))),
                          preferred_element_type=F32)
    s_c = jnp.where(valid, s_c, MASK_VALUE)
    s_n = jnp.sum(q.astype(F32) * knr, axis=-1, keepdims=True)
    sink = sink_ref[:, :, 0:1]
    m = jnp.maximum(jnp.maximum(jnp.max(s_c, axis=-1, keepdims=True), s_n), sink)
    e_c = jnp.exp(s_c - m)
    e_n = jnp.exp(s_n - m)
    denom = jnp.sum(e_c, axis=-1, keepdims=True) + e_n + jnp.exp(sink - m)
    p_c = (e_c / denom).astype(BF16)
    p_n = (e_n / denom).astype(BF16).astype(F32)
    o = lax.dot_general(p_c, cv.astype(BF16), (((2,), (2,)), ((0,), (0,))),
                        preferred_element_type=F32)
    o_ref[...] = o + p_n * vnr
    lane = lax.broadcasted_iota(jnp.int32, (hd, l), 1)
    for b in range(nb):
        for g in range(SWA_KV_HEADS):
            rows = slice(g * hd, (g + 1) * hd)
            ko_ref[b, g] = jnp.where(lane == l - 1, knt_ref[0, rows, b:b + 1],
                                     pltpu.roll(ck_ref[b, g], l - 1, 1))
            vo_ref[b, g] = jnp.where(lane == l - 1, vnt_ref[0, rows, b:b + 1],
                                     pltpu.roll(cv_ref[b, g], l - 1, 1))


def _swa_sample_call(sink_t, q8, knr, vnr, knt, vnt, ck, cv, nb):
    n, g, hd, l = ck.shape
    rows = nb * g
    blk3 = lambda i: (i, 0, 0)
    blk4 = lambda i: (i, 0, 0, 0)
    return pl.pallas_call(
        _swa_sample_kernel,
        grid=(n // nb,),
        in_specs=[
            pl.BlockSpec((rows, SUBLANES, LANES), lambda i: (0, 0, 0)),
            pl.BlockSpec((rows, SUBLANES, hd), blk3),
            pl.BlockSpec((rows, 1, hd), blk3),
            pl.BlockSpec((rows, 1, hd), blk3),
            pl.BlockSpec((1, g * hd, nb), blk3),
            pl.BlockSpec((1, g * hd, nb), blk3),
            pl.BlockSpec((nb, g, hd, l), blk4),
            pl.BlockSpec((nb, g, hd, l), blk4),
        ],
        out_specs=(
            pl.BlockSpec((rows, SUBLANES, hd), blk3),
            pl.BlockSpec((nb, g, hd, l), blk4),
            pl.BlockSpec((nb, g, hd, l), blk4),
        ),
        out_shape=(
            jax.ShapeDtypeStruct((n * g, SUBLANES, hd), F32),
            jax.ShapeDtypeStruct(ck.shape, F32),
            jax.ShapeDtypeStruct(cv.shape, F32),
        ),
        compiler_params=pltpu.CompilerParams(
            dimension_semantics=("arbitrary",), vmem_limit_bytes=VMEM_LIMIT),
        name="swa_sample",
    )(sink_t, q8, knr, vnr, knt, vnt, ck, cv)


def _tail_mix(x, y_rwkv, y_swa, g1, sh2, sc2, w_out, n_post_mix, n_pre_ffn):
    mix = _dot(y_rwkv, w_out[:RWKV_WIDTH]) + _dot(y_swa, w_out[RWKV_WIDTH:])
    x1 = x + g1 * _rmsnorm(mix, n_post_mix)
    h2 = (_rmsnorm(x1, n_pre_ffn) * (1.0 + sc2) + sh2).astype(BF16)
    return x1, h2


def _ffn_gate(act_ref, cw, cb, taps_fn, n_rows):
    def row_block(rb, carry):
        r0 = pl.multiple_of(rb * FF_ROWS, FF_ROWS)
        for c in range(N_FF_CHUNKS):
            z = []
            for half in range(2):
                cols = slice(half * D_FF + c * FF_CHUNK, half * D_FF + (c + 1) * FF_CHUNK)
                p1, p2, u = taps_fn(r0, cols)
                z.append(cb[:, cols] + p2 * cw[0:1, cols] + p1 * cw[1:2, cols] + u * cw[2:3, cols])
            act_ref[pl.ds(r0, FF_ROWS), c * FF_CHUNK:(c + 1) * FF_CHUNK] = (
                _silu(z[0]) * z[1]).astype(BF16)
        return carry

    lax.fori_loop(0, n_rows // FF_ROWS, row_block, 0)


def _tail_prompt_kernel(x_ref, yr_ref, ys_ref, mod_ref, w_out, n_post_mix, n_pre_ffn,
                        n_post_ffn, w_up, w_down, cw_ref, cb_ref,
                        y_ref, cp_ref, carry_ref, act_ref):
    t = pl.program_id(1)
    tm = x_ref.shape[1]

    @pl.when(t == 0)
    def _():
        carry_ref[...] = jnp.zeros_like(carry_ref)

    g1, sh2, sc2, g2 = mod_ref[2, 0], mod_ref[3, 0], mod_ref[4, 0], mod_ref[5, 0]
    x1, h2 = _tail_mix(x_ref[0], yr_ref[0], ys_ref[0], g1, sh2, sc2, w_out[...],
                       n_post_mix[...], n_pre_ffn[...])

    def cols_of(c, half):
        return slice(half * D_FF + c * FF_CHUNK, half * D_FF + (c + 1) * FF_CHUNK)

    def up(c):
        return [jnp.dot(h2, w_up[:, cols_of(c, half)], preferred_element_type=F32)
                for half in range(2)]

    def conv(u, cols):
        ext = jnp.concatenate([carry_ref[:, cols], u], axis=0)
        last = u[tm - SUBLANES:]
        carry_ref[:, cols] = last
        cp_ref[0, :, cols] = last
        return (cb_ref[:, cols] + pltpu.roll(ext, 2, 0)[SUBLANES:] * cw_ref[0:1, cols]
                + pltpu.roll(ext, 1, 0)[SUBLANES:] * cw_ref[1:2, cols] + u * cw_ref[2:3, cols])

    u_next = up(0)
    for c in range(N_FF_CHUNKS):
        u_cur = u_next
        if c + 1 < N_FF_CHUNKS:
            u_next = up(c + 1)
        za, zb = [conv(u_cur[half], cols_of(c, half)) for half in range(2)]
        act_ref[:, c * FF_CHUNK:(c + 1) * FF_CHUNK] = (_silu(za) * zb).astype(BF16)
    ff = jnp.dot(act_ref[...], w_down[...], preferred_element_type=F32)
    y_ref[0] = x1 + g2 * _rmsnorm(ff, n_post_ffn[...])


def _tail_sample_kernel(x_ref, o_ref, feat_ref, ys_ref, mod_ref, p0_ref, p1_ref, ln_w, ln_b,
                        w_out, n_post_mix, n_pre_ffn, n_post_ffn, w_up, w_down,
                        cw_ref, cb_ref, y_ref, u_ref, act_ref):
    n = x_ref.shape[0]
    seg = _seg_ones(RWKV_WIDTH)
    y_rwkv = _gn_epilogue(o_ref[...].T, feat_ref[7], feat_ref[6], ln_w[...], ln_b[...], seg)
    g1, sh2, sc2, g2 = mod_ref[2], mod_ref[3], mod_ref[4], mod_ref[5]
    x1, h2 = _tail_mix(x_ref[...], y_rwkv, ys_ref[...], g1, sh2, sc2, w_out[...],
                       n_post_mix[...], n_pre_ffn[...])
    u_ref[...] = jnp.dot(h2, w_up[...], preferred_element_type=F32)

    def taps_fn(r0, cols):
        rows = pl.ds(r0, FF_ROWS)
        return p1_ref[rows, cols], p0_ref[rows, cols], u_ref[rows, cols]

    _ffn_gate(act_ref, cw_ref, cb_ref, taps_fn, n)
    ff = jnp.dot(act_ref[...], w_down[...], preferred_element_type=F32)
    y_ref[...] = x1 + g2 * _rmsnorm(ff, n_post_ffn[...])


def _tail_prompt_call(x, y_rwkv, y_swa, mod_p, consts, tm):
    b, t, _ = x.shape
    tok = lambda i, j: (i, j, 0)
    in_specs = [
        pl.BlockSpec((1, tm, D_MODEL), tok),
        pl.BlockSpec((1, tm, RWKV_WIDTH), tok),
        pl.BlockSpec((1, tm, SWA_WIDTH), tok),
        pl.BlockSpec((6, 1, 1, D_MODEL), lambda i, j: (0, i, 0, 0)),
    ] + [_const_spec(c, 2) for c in consts]
    return pl.pallas_call(
        _tail_prompt_kernel,
        grid=(b, t // tm),
        in_specs=in_specs,
        out_specs=(
            pl.BlockSpec((1, tm, D_MODEL), tok),
            pl.BlockSpec((1, SUBLANES, 2 * D_FF), lambda i, j: (i, 0, 0)),
        ),
        out_shape=(
            jax.ShapeDtypeStruct((b, t, D_MODEL), F32),
            jax.ShapeDtypeStruct((b, SUBLANES, 2 * D_FF), F32),
        ),
        scratch_shapes=[
            pltpu.VMEM((SUBLANES, 2 * D_FF), F32),
            pltpu.VMEM((tm, D_FF), BF16),
        ],
        compiler_params=pltpu.CompilerParams(
            dimension_semantics=("arbitrary", "arbitrary"), vmem_limit_bytes=VMEM_LIMIT),
        name="tail_prompt",
    )(x, y_rwkv, y_swa, mod_p, *consts)


def _tail_sample_call(x, o_s, feat_s, y_swa, mod_s, p0, p1, ln_w, ln_b, consts):
    n = x.shape[0]
    args = (x, o_s, feat_s, y_swa, mod_s, p0, p1, ln_w, ln_b) + tuple(consts)
    out_shape = (
        jax.ShapeDtypeStruct((n, D_MODEL), F32),
        jax.ShapeDtypeStruct((n, 2 * D_FF), F32),
    )
    return pl.pallas_call(
        _tail_sample_kernel,
        grid=(1,),
        in_specs=[_const_spec(a, 1) for a in args],
        out_specs=tuple(pl.BlockSpec(s.shape, lambda i, nd=len(s.shape): (0,) * nd)
                        for s in out_shape),
        out_shape=out_shape,
        scratch_shapes=[pltpu.VMEM((n, D_FF), BF16)],
        compiler_params=pltpu.CompilerParams(
            dimension_semantics=("arbitrary",), vmem_limit_bytes=VMEM_LIMIT),
        name="tail_sample",
    )(*args)


def _rope_tables(pos):
    half = HEAD_DIM // 2
    inv = ROPE_THETA ** (-jnp.arange(half, dtype=F32) / half)
    ang = pos.astype(F32)[:, None] * inv[None, :]
    cos, sin = jnp.cos(ang), jnp.sin(ang)
    cos_h = jnp.concatenate([cos, cos], axis=-1)
    sin_h = jnp.concatenate([-sin, sin], axis=-1)
    return jnp.tile(cos_h, (1, LANES // HEAD_DIM)), jnp.tile(sin_h, (1, LANES // HEAD_DIM))


def kernel(x_prompt, x_sample, state_rwkv_wkv, state_rwkv_shift, cache_swa_k, cache_swa_v,
           state_ffn_conv, c_prompt, c_sample, w_ada, b_ada, norm_pre_mix, norm_post_mix,
           norm_pre_ffn, norm_post_ffn, w_in, rwkv_mu, rwkv_w0, rwkv_w_up, rwkv_a0, rwkv_a_up,
           rwkv_g_up, rwkv_k_k, rwkv_k_a, rwkv_r_k, rwkv_ln_w, rwkv_ln_b, swa_sinks, w_out,
           ffn_w_up, ffn_conv_w, ffn_conv_b, ffn_w_down):
    depth = w_ada.shape[0]
    assert depth == 1 and x_sample.shape[1] == 1
    b, t, _ = x_prompt.shape
    n = x_sample.shape[0]
    l = cache_swa_k.shape[2]
    tm = min(256, t)
    assert t % tm == 0 and tm % WINDOW == 0 and tm % CHUNK == 0
    assert tm % FF_ROWS == 0 and n % FF_ROWS == 0
    nb_swa = 8 if n % 8 == 0 else n
    li = 0

    row = lambda v: v.reshape(1, -1)
    w_in_b = w_in[li].astype(BF16)
    zeros_l = jnp.zeros((64, RWKV_WIDTH), F32)
    wa_up = jnp.concatenate([
        jnp.concatenate([rwkv_w_up[li], zeros_l], axis=1),
        jnp.concatenate([zeros_l, rwkv_a_up[li]], axis=1)], axis=0).astype(BF16)
    hid = jnp.arange(RWKV_WIDTH) // HEAD_DIM
    seg512 = (hid[:, None] == hid[None, :]).astype(BF16)
    proj_consts = (row(norm_pre_mix[li]), w_in_b, row(rwkv_mu[li]), row(rwkv_w0[li]), wa_up,
                   row(rwkv_a0[li]), rwkv_g_up[li].astype(BF16), row(rwkv_k_k[li]),
                   row(rwkv_k_a[li]), row(rwkv_r_k[li]), seg512)
    tail_consts = (w_out[li].astype(BF16), row(norm_post_mix[li]), row(norm_pre_ffn[li]),
                   row(norm_post_ffn[li]), ffn_w_up[li].astype(BF16), ffn_w_down[li].astype(BF16),
                   ffn_conv_w[li], row(ffn_conv_b[li]))
    ln_w, ln_b = row(rwkv_ln_w[li]), row(rwkv_ln_b[li])

    mod = _ada_call(jnp.concatenate([c_prompt, c_sample], axis=0), w_ada[li], row(b_ada[li]))
    mod_p = mod[:, :b].reshape(6, b, 1, D_MODEL)
    mod_s = mod[:, b:]

    cos_p, sin_p = _rope_tables(jnp.arange(t, dtype=jnp.int32))
    y_rwkv_p, s_cat, q_p, kx_p, vx_p, klast, vlast, plast = _mix_prompt_call(
        x_prompt, mod_p, cos_p, sin_p, proj_consts + (ln_w, ln_b), tm)
    y_swa_p = _swa_prompt_call(swa_sinks[li], q_p, kx_p, vx_p, min(2 * tm, t))
    y_p, cp = _tail_prompt_call(x_prompt, y_rwkv_p, y_swa_p, mod_p, tail_consts, tm)

    wkv_p = s_cat.reshape(b, HEAD_DIM, RWKV_HEADS, HEAD_DIM).transpose(0, 2, 1, 3)
    shift_p = plast[:, SUBLANES - 1]
    k_p = klast.reshape(b, WINDOW, SWA_KV_HEADS, HEAD_DIM)
    v_p = vlast.reshape(b, WINDOW, SWA_KV_HEADS, HEAD_DIM)
    conv_p = cp[:, SUBLANES - 2:]

    cos_s, sin_s = _rope_tables(jnp.full((1,), PAST_LEN, jnp.int32))
    feat_s, ft_s, q_s, kn_s, vn_s, p_s = _proj_sample_call(
        x_sample[:, 0], mod_s, cos_s, sin_s, state_rwkv_shift[li], proj_consts)
    state_t = jnp.transpose(state_rwkv_wkv[li], (1, 2, 3, 0))
    wkv_t, o_t = _wkv_sample_call(ft_s, state_t)
    wkv_s = jnp.transpose(wkv_t, (3, 0, 1, 2))

    g2 = SWA_KV_HEADS
    q4 = q_s.reshape(n * g2, SWA_GROUP, HEAD_DIM)
    q8 = jnp.concatenate([q4, jnp.zeros_like(q4)], axis=1)
    sink_t = jnp.broadcast_to(
        jnp.concatenate([swa_sinks[li].reshape(g2, SWA_GROUP),
                         jnp.full((g2, SWA_GROUP), MASK_VALUE, F32)], axis=1)[None, :, :, None],
        (nb_swa, g2, SUBLANES, LANES)).reshape(nb_swa * g2, SUBLANES, LANES)
    cols = lambda a: a.reshape(n // nb_swa, nb_swa, KV_WIDTH).transpose(0, 2, 1)
    ck = jnp.transpose(cache_swa_k[li], (0, 2, 3, 1))
    cv = jnp.transpose(cache_swa_v[li], (0, 2, 3, 1))
    o_att, k_t, v_t = _swa_sample_call(
        sink_t, q8, kn_s.reshape(n * g2, 1, HEAD_DIM), vn_s.reshape(n * g2, 1, HEAD_DIM),
        cols(kn_s), cols(vn_s), ck, cv, nb_swa)
    y_swa_s = o_att[:, :SWA_GROUP].reshape(n, SWA_WIDTH)
    k_s = jnp.transpose(k_t, (0, 3, 1, 2))
    v_s = jnp.transpose(v_t, (0, 3, 1, 2))

    conv0 = state_ffn_conv[li]
    y_s, u_s = _tail_sample_call(x_sample[:, 0], o_t, feat_s, y_swa_s.astype(BF16), mod_s,
                                 conv0[:, 0], conv0[:, 1], ln_w, ln_b, tail_consts)
    conv_s = jnp.stack([conv0[:, 1], u_s], axis=1)

    expand = lambda a: a[None]
    return (y_p, y_s[:, None, :], expand(wkv_p), expand(shift_p), expand(k_p), expand(v_p),
            expand(conv_p), expand(wkv_s), expand(p_s),
            expand(k_s), expand(v_s), expand(conv_s))
```

```python
import math

import jax
import jax.numpy as jnp
from jax import lax
from jax.experimental import pallas as pl
from jax.experimental.pallas import tpu as pltpu

D_MODEL = 1024
HEAD_DIM = 64
RWKV_WIDTH = 512
RWKV_HEADS = 8
RWKV_COLS = 1792
RWKV_GN_EPS = 64e-5
SWA_WIDTH = 512
SWA_HEADS = 8
SWA_KV_HEADS = 2
SWA_GROUP = 4
KV_WIDTH = SWA_KV_HEADS * HEAD_DIM
WINDOW = 128
PAST_LEN = 16384
ROPE_THETA = 10000.0
ATTN_SCALE = HEAD_DIM ** -0.5
D_FF = 2816
NORM_EPS = 1e-6
MASK_VALUE = -1e30
PROJ_COLS = RWKV_COLS + SWA_WIDTH + 2 * KV_WIDTH

LANES = 128
SUBLANES = 8
CHUNK = 64
GROUP_LANES = 256
FF_CHUNK = 256
N_FF_CHUNKS = D_FF // FF_CHUNK
FF_ROWS = 32
TAIL_SUB = 256
VMEM_LIMIT = 56 * 1024 * 1024

F32 = jnp.float32
BF16 = jnp.bfloat16


def _sigmoid(x):
    return 1.0 / (1.0 + jnp.exp(-x))


def _silu(x):
    return x * _sigmoid(x)


def _rmsnorm(x, g):
    return x * lax.rsqrt(jnp.mean(x * x, axis=-1, keepdims=True) + NORM_EPS) * g


def _dot(a, b):
    return jnp.dot(a.astype(BF16), b.astype(BF16), preferred_element_type=F32)


def _dot_nt(a, b):
    return lax.dot_general(a.astype(BF16), b.astype(BF16), (((1,), (1,)), ((), ())),
                           preferred_element_type=F32)


def _dot_tn(a, b):
    return lax.dot_general(a.astype(BF16), b.astype(BF16), (((0,), (0,)), ((), ())),
                           preferred_element_type=F32)


def _every(fn, *lists):
    return [fn(*a) for a in zip(*lists)]


def _swap_halves(x):
    w = x.shape[-1]
    lane = lax.broadcasted_iota(jnp.int32, x.shape, x.ndim - 1)
    lo = (lane & (HEAD_DIM // 2)) == 0
    return jnp.where(lo, pltpu.roll(x, w - HEAD_DIM // 2, x.ndim - 1),
                     pltpu.roll(x, HEAD_DIM // 2, x.ndim - 1))


def _rope(x, cos, sin):
    reps = x.shape[-1] // LANES
    cos_w = jnp.concatenate([cos] * reps, axis=-1) if reps > 1 else cos
    sin_w = jnp.concatenate([sin] * reps, axis=-1) if reps > 1 else sin
    return x * cos_w + _swap_halves(x) * sin_w


def _proj_features(x, shift, scale, g_pre, w_in, prev_fn, mu, w0, wa_up, a0, g_up,
                   k_k, k_a, r_k, seg512, cos, sin):
    h = _rmsnorm(x, g_pre) * (1.0 + scale) + shift
    p = _dot(h, w_in)
    p_rwkv = p[:, :RWKV_COLS]
    prev = prev_fn(p_rwkv)
    xm = p_rwkv + (prev - p_rwkv) * mu
    r = xm[:, 0:512]
    k = xm[:, 512:1024]
    v = xm[:, 1024:1536]
    wa = xm[:, 1536:1664]
    gd = xm[:, 1664:1792]
    lane = lax.broadcasted_iota(jnp.int32, wa.shape, 1)
    wa_act = jnp.where(lane < 64, jnp.tanh(wa), wa)
    lora = _dot(wa_act, wa_up)
    lw = -math.exp(-0.5) * _sigmoid(w0 + lora[:, :512])
    a = _sigmoid(a0 + lora[:, 512:])
    g = _dot(_sigmoid(gd), g_up)
    kk = k * k_k
    ss = _dot(kk * kk, seg512)
    kk = kk / jnp.maximum(jnp.sqrt(ss), 1e-12)
    kf = k * (1.0 + (a - 1.0) * k_a)
    bonus = _dot(r * kf * r_k, seg512) * v
    feats = (r, lw, kf, v, -kk, kk * a, g, bonus)
    q = _rope(p[:, RWKV_COLS:RWKV_COLS + SWA_WIDTH], cos, sin)
    ks = _rope(p[:, RWKV_COLS + SWA_WIDTH:RWKV_COLS + SWA_WIDTH + KV_WIDTH], cos, sin)
    vs = p[:, RWKV_COLS + SWA_WIDTH + KV_WIDTH:]
    return feats, q, ks, vs, p_rwkv


def _gn_epilogue(o, bonus, g, ln_w, ln_b, seg):
    mu = _dot(o, seg) * (1.0 / HEAD_DIM)
    d = o - mu
    var = _dot(d * d, seg) * (1.0 / HEAD_DIM)
    gn = d * lax.rsqrt(var + RWKV_GN_EPS) * ln_w + ln_b
    return (gn + bonus) * g


def _seg_ones(n):
    r = lax.broadcasted_iota(jnp.int32, (n, n), 0) // HEAD_DIM
    c = lax.broadcasted_iota(jnp.int32, (n, n), 1) // HEAD_DIM
    return (r == c).astype(BF16)


def _ada_kernel(c_ref, w_ref, b_ref, o_ref):
    o_ref[0] = _dot(_silu(c_ref[...]), w_ref[...]) + b_ref[...]


def _ada_call(c_all, w_ada, b_ada):
    rows = c_all.shape[0]
    return pl.pallas_call(
        _ada_kernel,
        grid=(6,),
        in_specs=[
            pl.BlockSpec((rows, D_MODEL), lambda j: (0, 0)),
            pl.BlockSpec((D_MODEL, D_MODEL), lambda j: (0, j)),
            pl.BlockSpec((1, D_MODEL), lambda j: (0, j)),
        ],
        out_specs=pl.BlockSpec((1, rows, D_MODEL), lambda j: (j, 0, 0)),
        out_shape=jax.ShapeDtypeStruct((6, rows, D_MODEL), F32),
        compiler_params=pltpu.CompilerParams(
            dimension_semantics=("arbitrary",), vmem_limit_bytes=VMEM_LIMIT),
        name="ada",
    )(c_all, w_ada, b_ada)


def _expand_kv(x):
    lane = lax.broadcasted_iota(jnp.int32, x.shape, 1)
    rolled = pltpu.roll(x, HEAD_DIM, 1)
    g0 = jnp.where(lane < HEAD_DIM, x, rolled)
    g1 = jnp.where(lane < HEAD_DIM, rolled, x)
    return jnp.concatenate([g0, g0, g1, g1], axis=1)


def _mix_prompt_kernel(x_ref, mod_ref, cos_ref, sin_ref, g_pre, w_in, mu, w0, wa_up, a0,
                       g_up, k_k, k_a, r_k, seg512, ln_w, ln_b,
                       y_ref, s_out_ref, q_ref, k_ref, v_ref, klast_ref, vlast_ref, plast_ref,
                       carry_ref, s_ref):
    t = pl.program_id(1)

    @pl.when(t == 0)
    def _():
        carry_ref[...] = jnp.zeros_like(carry_ref)
        s_ref[...] = jnp.zeros_like(s_ref)

    tm = x_ref.shape[1]
    carry_row = carry_ref[SUBLANES - 1:SUBLANES, :]

    def prev_fn(p_rwkv):
        row = lax.broadcasted_iota(jnp.int32, p_rwkv.shape, 0)
        return jnp.where(row == 0, carry_row, pltpu.roll(p_rwkv, 1, 0))

    feats, q, ks, vs, p_rwkv = _proj_features(
        x_ref[0], mod_ref[0, 0], mod_ref[1, 0], g_pre[...], w_in[...], prev_fn, mu[...],
        w0[...], wa_up[...], a0[...], g_up[...], k_k[...], k_a[...], r_k[...], seg512[...],
        cos_ref[...], sin_ref[...])
    q_ref[0] = (q * ATTN_SCALE).astype(BF16)
    k_ref[0] = _expand_kv(ks).astype(BF16)
    v_ref[0] = _expand_kv(vs).astype(BF16)
    last = p_rwkv[tm - SUBLANES:, :]
    carry_ref[...] = last
    plast_ref[0] = last
    _wkv_tile(feats, ln_w, ln_b, y_ref, s_ref)

    @pl.when(t == pl.num_programs(1) - 1)
    def _():
        klast_ref[0] = ks[tm - WINDOW:, :]
        vlast_ref[0] = vs[tm - WINDOW:, :]
        for g in range(RWKV_WIDTH // GROUP_LANES):
            s_out_ref[0, :, g * GROUP_LANES:(g + 1) * GROUP_LANES] = s_ref[g]


def _proj_sample_kernel(x_ref, mod_ref, cos_ref, sin_ref, prev_ref, g_pre, w_in, mu, w0,
                        wa_up, a0, g_up, k_k, k_a, r_k, seg512,
                        feat_ref, ft_ref, q_ref, k_ref, v_ref, p_ref):
    feats, q, ks, vs, p_rwkv = _proj_features(
        x_ref[...], mod_ref[0], mod_ref[1], g_pre[...], w_in[...], lambda p: prev_ref[...],
        mu[...], w0[...], wa_up[...], a0[...], g_up[...], k_k[...], k_a[...], r_k[...],
        seg512[...], cos_ref[...], sin_ref[...])
    for i, f in enumerate(feats):
        feat_ref[i] = f
    for i in range(6):
        ft_ref[i] = feats[i].T
    q_ref[...] = (q * ATTN_SCALE).astype(BF16)
    k_ref[...] = ks
    v_ref[...] = vs
    p_ref[...] = p_rwkv


def _const_spec(arr, grid_rank):
    zeros = (0,) * arr.ndim
    if grid_rank == 1:
        return pl.BlockSpec(arr.shape, lambda i: zeros)
    return pl.BlockSpec(arr.shape, lambda i, j: zeros)


def _mix_prompt_call(x, mod_p, cos, sin, consts, tm):
    b, t, _ = x.shape
    nt = t // tm
    in_specs = [
        pl.BlockSpec((1, tm, D_MODEL), lambda i, j: (i, j, 0)),
        pl.BlockSpec((6, 1, 1, D_MODEL), lambda i, j: (0, i, 0, 0)),
        pl.BlockSpec((tm, LANES), lambda i, j: (j, 0)),
        pl.BlockSpec((tm, LANES), lambda i, j: (j, 0)),
    ] + [_const_spec(c, 2) for c in consts]
    out_shape = (
        jax.ShapeDtypeStruct((b, t, RWKV_WIDTH), BF16),
        jax.ShapeDtypeStruct((b, HEAD_DIM, RWKV_WIDTH), F32),
        jax.ShapeDtypeStruct((b, t, SWA_WIDTH), BF16),
        jax.ShapeDtypeStruct((b, t, SWA_WIDTH), BF16),
        jax.ShapeDtypeStruct((b, t, SWA_WIDTH), BF16),
        jax.ShapeDtypeStruct((b, WINDOW, KV_WIDTH), F32),
        jax.ShapeDtypeStruct((b, WINDOW, KV_WIDTH), F32),
        jax.ShapeDtypeStruct((b, SUBLANES, RWKV_COLS), F32),
    )
    out_specs = (
        pl.BlockSpec((1, tm, RWKV_WIDTH), lambda i, j: (i, j, 0)),
        pl.BlockSpec((1, HEAD_DIM, RWKV_WIDTH), lambda i, j: (i, 0, 0)),
        pl.BlockSpec((1, tm, SWA_WIDTH), lambda i, j: (i, j, 0)),
        pl.BlockSpec((1, tm, SWA_WIDTH), lambda i, j: (i, j, 0)),
        pl.BlockSpec((1, tm, SWA_WIDTH), lambda i, j: (i, j, 0)),
        pl.BlockSpec((1, WINDOW, KV_WIDTH), lambda i, j: (i, 0, 0)),
        pl.BlockSpec((1, WINDOW, KV_WIDTH), lambda i, j: (i, 0, 0)),
        pl.BlockSpec((1, SUBLANES, RWKV_COLS), lambda i, j: (i, 0, 0)),
    )
    return pl.pallas_call(
        _mix_prompt_kernel,
        grid=(b, nt),
        in_specs=in_specs,
        out_specs=out_specs,
        out_shape=out_shape,
        scratch_shapes=[
            pltpu.VMEM((SUBLANES, RWKV_COLS), F32),
            pltpu.VMEM((RWKV_WIDTH // GROUP_LANES, HEAD_DIM, GROUP_LANES), F32),
        ],
        compiler_params=pltpu.CompilerParams(
            dimension_semantics=("arbitrary", "arbitrary"), vmem_limit_bytes=VMEM_LIMIT),
        name="mix_prompt",
    )(x, mod_p, cos, sin, *consts)


def _proj_sample_call(x, mod_s, cos, sin, prev, consts):
    n = x.shape[0]
    args = (x, mod_s, cos, sin, prev) + tuple(consts)
    out_shape = (
        jax.ShapeDtypeStruct((8, n, RWKV_WIDTH), F32),
        jax.ShapeDtypeStruct((6, RWKV_WIDTH, n), F32),
        jax.ShapeDtypeStruct((n, SWA_WIDTH), BF16),
        jax.ShapeDtypeStruct((n, KV_WIDTH), F32),
        jax.ShapeDtypeStruct((n, KV_WIDTH), F32),
        jax.ShapeDtypeStruct((n, RWKV_COLS), F32),
    )
    return pl.pallas_call(
        _proj_sample_kernel,
        grid=(1,),
        in_specs=[_const_spec(a, 1) for a in args],
        out_specs=tuple(pl.BlockSpec(s.shape, lambda i, nd=len(s.shape): (0,) * nd)
                        for s in out_shape),
        out_shape=out_shape,
        compiler_params=pltpu.CompilerParams(
            dimension_semantics=("arbitrary",), vmem_limit_bytes=VMEM_LIMIT),
        name="proj_sample",
    )(*args)


def _wkv_tile(feats, ln_w, ln_b, y_ref, s_ref):
    tt = feats[0].shape[0]
    n_chunks = tt // CHUNK
    gl = GROUP_LANES
    n_groups = RWKV_WIDTH // gl
    heads_per_group = gl // HEAD_DIM
    probs = [(c, g) for c in range(n_chunks) for g in range(n_groups)]

    row_c = lax.broadcasted_iota(jnp.int32, (CHUNK, gl), 0)
    col_c = lax.broadcasted_iota(jnp.int32, (CHUNK, gl), 1) % CHUNK
    strict = row_c > col_c
    incl = row_c >= col_c
    eye_cat = (row_c == col_c).astype(F32)
    rb = lax.broadcasted_iota(jnp.int32, (gl, gl), 0) // HEAD_DIM
    cb = lax.broadcasted_iota(jnp.int32, (gl, gl), 1) // HEAD_DIM
    bd_mask = rb == cb
    tri_r = lax.broadcasted_iota(jnp.int32, (CHUNK, CHUNK), 0)
    tri_c = lax.broadcasted_iota(jnp.int32, (CHUNK, CHUNK), 1)
    tril_ones = (tri_r >= tri_c).astype(BF16)
    seg = bd_mask.astype(BF16)

    def bd(x):
        xb = x.astype(BF16)
        return jnp.where(bd_mask, jnp.concatenate([xb] * heads_per_group, axis=0),
                         jnp.zeros((), BF16))

    def fold(x):
        xm = jnp.where(bd_mask, x, 0.0)
        acc = xm[0:HEAD_DIM]
        for hh in range(1, heads_per_group):
            acc = acc + xm[hh * HEAD_DIM:(hh + 1) * HEAD_DIM]
        return acc

    def ld(i, p):
        c, g = p
        return feats[i][c * CHUNK:(c + 1) * CHUNK, g * gl:(g + 1) * gl]

    lw = [ld(1, p) for p in probs]
    na = [ld(4, p) for p in probs]
    bb = [ld(5, p) for p in probs]
    kf = [ld(2, p) for p in probs]
    r = [ld(0, p) for p in probs]
    v = [ld(3, p) for p in probs]

    def cumsum(x):
        hi = x.astype(BF16)
        lo = (x - hi.astype(F32)).astype(BF16)
        both = jnp.dot(tril_ones, jnp.concatenate([hi, lo], axis=1), preferred_element_type=F32)
        return both[:, :gl] + both[:, gl:]

    cum = _every(cumsum, lw)
    cum_last = [x[CHUNK - 1:CHUNK, :] for x in cum]
    e_out = [jnp.exp(-x) for x in cum]
    e_end = _every(lambda cl, x: jnp.exp(cl - x), cum_last, cum)
    a_t = _every(lambda n_, x, l_: n_ * jnp.exp(x - l_), na, cum, lw)
    r_t = _every(lambda r_, x: r_ * jnp.exp(x), r, cum)
    b_t = _every(lambda b_, e: b_ * e, bb, e_out)
    k_t = _every(lambda k_, e: k_ * e, kf, e_out)
    b_end = _every(lambda b_, e: b_ * e, bb, e_end)
    k_end = _every(lambda k_, e: k_ * e, kf, e_end)
    gamma = [jnp.exp(x) for x in cum_last]

    ar = _every(lambda a_, r_: jnp.concatenate([a_, r_], axis=0), a_t, r_t)
    pb = _every(lambda x, y: _dot_nt(x, bd(y)), ar, b_t)
    pk = _every(lambda x, y: _dot_nt(x, bd(y)), ar, k_t)
    l_ab = [jnp.where(strict, x[:CHUNK], 0.0) for x in pb]
    l_ak = [jnp.where(strict, x[:CHUNK], 0.0) for x in pk]
    m_rb = [jnp.where(incl, x[CHUNK:], 0.0) for x in pb]
    m_rk = [jnp.where(incl, x[CHUNK:], 0.0) for x in pk]

    x_acc = [eye_cat + l for l in l_ab]
    pw = _every(lambda l: _dot(l, bd(l)), l_ab)
    n_sq = int(math.log2(CHUNK)) - 1
    for lvl in range(n_sq):
        rhs = [bd(p_) for p_ in pw]
        if lvl < n_sq - 1:
            both = _every(lambda x, p_, w_: _dot(jnp.concatenate([x, p_], axis=0), w_),
                          x_acc, pw, rhs)
            x_acc = _every(lambda x, b_: x + b_[:CHUNK], x_acc, both)
            pw = [b_[CHUNK:] for b_ in both]
        else:
            x_acc = _every(lambda x, w_: x + _dot(x, w_), x_acc, rhs)
    t_inv = x_acc

    kv = _every(lambda la, mk, x: _dot(jnp.concatenate([la, mk], axis=0), bd(x)), l_ak, m_rk, v)
    y_loc = [x[:CHUNK] for x in kv]
    wu = _every(lambda t_, a_, y_: _dot(t_, jnp.concatenate([bd(a_), bd(y_)], axis=1)),
                t_inv, a_t, y_loc)
    w_t = [x[:, :gl] for x in wu]
    u_loc = [x[:, gl:] for x in wu]

    mwu = _every(lambda mb, w_, u_: _dot(mb, jnp.concatenate([bd(w_), bd(u_)], axis=1)),
                 m_rb, w_t, u_loc)
    q_c = _every(lambda r_, x: r_ + x[:, :gl], r_t, mwu)
    o_loc = _every(lambda x, y_: x[:, gl:] + y_[CHUNK:], mwu, kv)
    m_low = _every(lambda w_, b_: jnp.where(bd_mask, _dot_tn(w_, b_), 0.0).astype(BF16),
                   w_t, b_end)
    n_loc = _every(lambda u_, v_, b_, k_: fold(_dot_tn(jnp.concatenate([u_, v_], axis=0),
                                                       jnp.concatenate([b_, k_], axis=0))),
                   u_loc, v, b_end, k_end)

    state = [s_ref[g] for g in range(n_groups)]
    starts = []
    for c in range(n_chunks):
        starts.append(state)
        idx = [c * n_groups + g for g in range(n_groups)]
        state = [state[g] * gamma[i] + _dot(state[g], m_low[i]) + n_loc[i]
                 for g, i in enumerate(idx)]
    for g in range(n_groups):
        s_ref[g] = state[g]

    s0 = [starts[c][g] for (c, g) in probs]
    o = _every(lambda q_, s_, ol: _dot_nt(q_, bd(s_)) + ol, q_c, s0, o_loc)

    n_p = len(probs)
    unstack = lambda x: [x[i * CHUNK:(i + 1) * CHUNK] for i in range(n_p)]
    mu = unstack(_dot(jnp.concatenate(o, axis=0), seg) * (1.0 / HEAD_DIM))
    d = _every(lambda x, m_: x - m_, o, mu)
    var = unstack(_dot(jnp.concatenate([x * x for x in d], axis=0), seg) * (1.0 / HEAD_DIM))
    for i, (c, g) in enumerate(probs):
        ls = slice(g * gl, (g + 1) * gl)
        gn = d[i] * lax.rsqrt(var[i] + RWKV_GN_EPS) * ln_w[:, ls] + ln_b[:, ls]
        y = (gn + ld(7, (c, g))) * ld(6, (c, g))
        y_ref[0, c * CHUNK:(c + 1) * CHUNK, ls] = y.astype(y_ref.dtype)


def _wkv_sample_kernel(ft_ref, s_ref, s_out_ref, o_ref):
    hd = HEAD_DIM
    r, kf, na, bb = ft_ref[0], ft_ref[2], ft_ref[4], ft_ref[5]
    w = jnp.exp(ft_ref[1])

    def value_block(vb, carry):
        v0 = pl.multiple_of(vb * SUBLANES, SUBLANES)
        v_rows = ft_ref[3, pl.ds(v0, SUBLANES), :]
        outs = []
        for j in range(SUBLANES):
            s = s_ref[0, v0 + j]
            sa = jnp.sum(s * na, axis=0, keepdims=True)
            s_new = s * w + sa * bb + v_rows[j:j + 1] * kf
            s_out_ref[0, v0 + j] = s_new
            outs.append(jnp.sum(s_new * r, axis=0, keepdims=True))
        o_ref[pl.ds(v0, SUBLANES), :] = jnp.concatenate(outs, axis=0)
        return carry

    lax.fori_loop(0, hd // SUBLANES, value_block, 0)


def _wkv_sample_call(ft_s, state_t):
    h, hd, _, n = state_t.shape
    return pl.pallas_call(
        _wkv_sample_kernel,
        grid=(h,),
        in_specs=[
            pl.BlockSpec((6, hd, n), lambda i: (0, i, 0)),
            pl.BlockSpec((1, hd, hd, n), lambda i: (i, 0, 0, 0)),
        ],
        out_specs=(
            pl.BlockSpec((1, hd, hd, n), lambda i: (i, 0, 0, 0)),
            pl.BlockSpec((hd, n), lambda i: (i, 0)),
        ),
        out_shape=(
            jax.ShapeDtypeStruct(state_t.shape, F32),
            jax.ShapeDtypeStruct((h * hd, n), F32),
        ),
        compiler_params=pltpu.CompilerParams(
            dimension_semantics=("arbitrary",), vmem_limit_bytes=VMEM_LIMIT),
        name="wkv_sample",
    )(ft_s, state_t)


def _swa_prompt_kernel(sink_ref, q_ref, kp_ref, kc_ref, vp_ref, vc_ref, o_ref):
    j = pl.program_id(1)
    w = WINDOW
    tq = q_ref.shape[1]
    n_blk = tq // w
    gl = GROUP_LANES
    rows = SWA_GROUP * w
    probs = [(qi, g) for qi in range(n_blk) for g in range(SWA_KV_HEADS)]

    lane_head = lax.broadcasted_iota(jnp.int32, (w, gl), 1) // HEAD_DIM
    head_mask = [lane_head == h for h in range(SWA_GROUP)]
    qi_ = lax.broadcasted_iota(jnp.int32, (rows, 2 * w), 0) % w
    ki_ = lax.broadcasted_iota(jnp.int32, (rows, 2 * w), 1)
    diff = qi_ - (ki_ - w)
    in_window = (diff >= 0) & (diff < WINDOW)
    first_valid = in_window & ((j * tq + ki_ - w) >= 0)

    def keys(ref_prev, ref_cur, qi, g):
        ls = slice(g * gl, (g + 1) * gl)
        prev = ref_prev[0, :, ls] if qi == 0 else ref_cur[0, (qi - 1) * w:qi * w, ls]
        return jnp.concatenate([prev, ref_cur[0, qi * w:(qi + 1) * w, ls]], axis=0)

    def lhs(qi, g):
        qg = q_ref[0, qi * w:(qi + 1) * w, g * gl:(g + 1) * gl]
        return jnp.concatenate([jnp.where(m, qg, jnp.zeros((), BF16)) for m in head_mask], axis=0)

    sinks = []
    for g in range(SWA_KV_HEADS):
        sinks.append(jnp.concatenate(
            [jnp.full((w, 1), sink_ref[g * SWA_GROUP + h], F32) for h in range(SWA_GROUP)], axis=0))

    s = [_dot_nt(lhs(qi, g), keys(kp_ref, kc_ref, qi, g)) for qi, g in probs]
    s = [jnp.where(first_valid if qi == 0 else in_window, x, MASK_VALUE)
         for x, (qi, g) in zip(s, probs)]
    sink = [sinks[g] for qi, g in probs]
    m = _every(lambda x, sk: jnp.maximum(jnp.max(x, axis=-1, keepdims=True), sk), s, sink)
    e = _every(lambda x, m_: jnp.exp(x - m_), s, m)
    inv = _every(lambda e_, sk, m_: 1.0 / (jnp.sum(e_, axis=-1, keepdims=True) + jnp.exp(sk - m_)),
                 e, sink, m)
    prob = _every(lambda e_, i_: (e_ * i_).astype(BF16), e, inv)
    og = [_dot(p_, keys(vp_ref, vc_ref, qi, g)) for p_, (qi, g) in zip(prob, probs)]
    for x, (qi, g) in zip(og, probs):
        y = jnp.where(head_mask[0], x[0:w], 0.0)
        for h in range(1, SWA_GROUP):
            y = y + jnp.where(head_mask[h], x[h * w:(h + 1) * w], 0.0)
        o_ref[0, qi * w:(qi + 1) * w, g * gl:(g + 1) * gl] = y.astype(o_ref.dtype)


def _swa_prompt_call(sinks, q, k, v, tq):
    b, t, _ = q.shape
    w = WINDOW
    per = tq // w
    prev = lambda i, j: (i, jnp.maximum(j * per - 1, 0), 0)
    cur = lambda i, j: (i, j, 0)
    return pl.pallas_call(
        _swa_prompt_kernel,
        grid=(b, t // tq),
        in_specs=[
            pl.BlockSpec(memory_space=pltpu.SMEM),
            pl.BlockSpec((1, tq, SWA_WIDTH), cur),
            pl.BlockSpec((1, w, SWA_WIDTH), prev),
            pl.BlockSpec((1, tq, SWA_WIDTH), cur),
            pl.BlockSpec((1, w, SWA_WIDTH), prev),
            pl.BlockSpec((1, tq, SWA_WIDTH), cur),
        ],
        out_specs=pl.BlockSpec((1, tq, SWA_WIDTH), cur),
        out_shape=jax.ShapeDtypeStruct((b, t, SWA_WIDTH), BF16),
        compiler_params=pltpu.CompilerParams(
            dimension_semantics=("arbitrary", "arbitrary"), vmem_limit_bytes=VMEM_LIMIT),
        name="swa_prompt",
    )(sinks, q, k, k, v, v)


def _swa_sample_kernel(sink_ref, q_ref, knr_ref, vnr_ref, knt_ref, vnt_ref, ck_ref, cv_ref,
                       o_ref, ko_ref, vo_ref):
    nb, _, hd, l = ck_ref.shape
    q = q_ref[...]
    ck = ck_ref[...].reshape(nb * SWA_KV_HEADS, hd, l)
    cv = cv_ref[...].reshape(nb * SWA_KV_HEADS, hd, l)
    knr = knr_ref[...].astype(BF16).astype(F32)
    vnr = vnr_ref[...].astype(BF16).astype(F32)
    ki = lax.broadcasted_iota(jnp.int32, (1, 1, l), 2)
    kpos = PAST_LEN - l + ki
    diff = PAST_LEN - kpos
    valid = (diff >= 0) & (diff < WINDOW) & (kpos >= 0)
    s_c = lax.dot_general(q, ck.astype(BF16), (((2,), (1,)), ((0,), (0,))),
                          preferred_element_type=F32)
    s_c = jnp.where(valid, s_c, MASK_VALUE)
    s_n = jnp.sum(q.astype(F32) * knr, axis=-1, keepdims=True)
    sink = sink_ref[:, :, 0:1]
    m = jnp.maximum(jnp.maximum(jnp.max(s_c, axis=-1, keepdims=True), s_n), sink)
    e_c = jnp.exp(s_c - m)
    e_n = jnp.exp(s_n - m)
    denom = jnp.sum(e_c, axis=-1, keepdims=True) + e_n + jnp.exp(sink - m)
    p_c = (e_c / denom).astype(BF16)
    p_n = (e_n / denom).astype(BF16).astype(F32)
    o = lax.dot_general(p_c, cv.astype(BF16), (((2,), (2,)), ((0,), (0,))),
                        preferred_element_type=F32)
    o_ref[...] = o + p_n * vnr
    lane = lax.broadcasted_iota(jnp.int32, (hd, l), 1)
    for b in range(nb):
        for g in range(SWA_KV_HEADS):
            rows = slice(g * hd, (g + 1) * hd)
            ko_ref[b, g] = jnp.where(lane == l - 1, knt_ref[0, rows, b:b + 1],
                                     pltpu.roll(ck_ref[b, g], l - 1, 1))
            vo_ref[b, g] = jnp.where(lane == l - 1, vnt_ref[0, rows, b:b + 1],
                                     pltpu.roll(cv_ref[b, g], l - 1, 1))


def _swa_sample_call(sink_t, q8, knr, vnr, knt, vnt, ck, cv, nb):
    n, g, hd, l = ck.shape
    rows = nb * g
    blk3 = lambda i: (i, 0, 0)
    blk4 = lambda i: (i, 0, 0, 0)
    return pl.pallas_call(
        _swa_sample_kernel,
        grid=(n // nb,),
        in_specs=[
            pl.BlockSpec((rows, SUBLANES, LANES), lambda i: (0, 0, 0)),
            pl.BlockSpec((rows, SUBLANES, hd), blk3),
            pl.BlockSpec((rows, 1, hd), blk3),
            pl.BlockSpec((rows, 1, hd), blk3),
            pl.BlockSpec((1, g * hd, nb), blk3),
            pl.BlockSpec((1, g * hd, nb), blk3),
            pl.BlockSpec((nb, g, hd, l), blk4),
            pl.BlockSpec((nb, g, hd, l), blk4),
        ],
        out_specs=(
            pl.BlockSpec((rows, SUBLANES, hd), blk3),
            pl.BlockSpec((nb, g, hd, l), blk4),
            pl.BlockSpec((nb, g, hd, l), blk4),
        ),
        out_shape=(
            jax.ShapeDtypeStruct((n * g, SUBLANES, hd), F32),
            jax.ShapeDtypeStruct(ck.shape, F32),
            jax.ShapeDtypeStruct(cv.shape, F32),
        ),
        compiler_params=pltpu.CompilerParams(
            dimension_semantics=("arbitrary",), vmem_limit_bytes=VMEM_LIMIT),
        name="swa_sample",
    )(sink_t, q8, knr, vnr, knt, vnt, ck, cv)


def _tail_mix(x, y_rwkv, y_swa, g1, sh2, sc2, w_out, n_post_mix, n_pre_ffn):
    mix = _dot(y_rwkv, w_out[:RWKV_WIDTH]) + _dot(y_swa, w_out[RWKV_WIDTH:])
    x1 = x + g1 * _rmsnorm(mix, n_post_mix)
    h2 = (_rmsnorm(x1, n_pre_ffn) * (1.0 + sc2) + sh2).astype(BF16)
    return x1, h2


def _ffn_gate(act_ref, cw, cb, taps_fn, n_rows):
    def row_block(rb, carry):
        r0 = pl.multiple_of(rb * FF_ROWS, FF_ROWS)
        for c in range(N_FF_CHUNKS):
            z = []
            for half in range(2):
                cols = slice(half * D_FF + c * FF_CHUNK, half * D_FF + (c + 1) * FF_CHUNK)
                p1, p2, u = taps_fn(r0, cols)
                z.append(cb[:, cols] + p2 * cw[0:1, cols] + p1 * cw[1:2, cols] + u * cw[2:3, cols])
            act_ref[pl.ds(r0, FF_ROWS), c * FF_CHUNK:(c + 1) * FF_CHUNK] = (
                _silu(z[0]) * z[1]).astype(BF16)
        return carry

    lax.fori_loop(0, n_rows // FF_ROWS, row_block, 0)


def _tail_prompt_kernel(x_ref, yr_ref, ys_ref, mod_ref, w_out, n_post_mix, n_pre_ffn,
                        n_post_ffn, w_up, w_down, cw_ref, cb_ref,
                        y_ref, cp_ref, carry_ref, act_ref):
    t = pl.program_id(1)
    tm = x_ref.shape[1]
    sub = min(tm, TAIL_SUB)
    blocks = [slice(i * sub, (i + 1) * sub) for i in range(tm // sub)]

    @pl.when(t == 0)
    def _():
        carry_ref[...] = jnp.zeros_like(carry_ref)

    g1, sh2, sc2, g2 = mod_ref[2, 0], mod_ref[3, 0], mod_ref[4, 0], mod_ref[5, 0]
    mixes = [_dot(yr_ref[0, rb, :], w_out[:RWKV_WIDTH]) + _dot(ys_ref[0, rb, :], w_out[RWKV_WIDTH:])
             for rb in blocks]

    def cols_of(c, half):
        return slice(half * D_FF + c * FF_CHUNK, half * D_FF + (c + 1) * FF_CHUNK)

    def conv(u, cols):
        ext = jnp.concatenate([carry_ref[:, cols], u], axis=0)
        last = u[sub - SUBLANES:]
        carry_ref[:, cols] = last
        cp_ref[0, :, cols] = last
        return (cb_ref[:, cols] + pltpu.roll(ext, 2, 0)[SUBLANES:] * cw_ref[0:1, cols]
                + pltpu.roll(ext, 1, 0)[SUBLANES:] * cw_ref[1:2, cols] + u * cw_ref[2:3, cols])

    for rb, mix in zip(blocks, mixes):
        x1 = x_ref[0, rb, :] + g1 * _rmsnorm(mix, n_post_mix[...])
        h2 = (_rmsnorm(x1, n_pre_ffn[...]) * (1.0 + sc2) + sh2).astype(BF16)

        def up(c):
            return [jnp.dot(h2, w_up[:, cols_of(c, half)], preferred_element_type=F32)
                    for half in range(2)]

        def down(c):
            rows = slice(c * FF_CHUNK, (c + 1) * FF_CHUNK)
            return jnp.dot(act_ref[rb, rows], w_down[rows, :], preferred_element_type=F32)

        u_next = up(0)
        ff = None
        for c in range(N_FF_CHUNKS):
            u_cur = u_next
            if c + 1 < N_FF_CHUNKS:
                u_next = up(c + 1)
            if c >= 1:
                part = down(c - 1)
                ff = part if ff is None else ff + part
            za, zb = [conv(u_cur[half], cols_of(c, half)) for half in range(2)]
            act_ref[rb, c * FF_CHUNK:(c + 1) * FF_CHUNK] = (_silu(za) * zb).astype(BF16)
        ff = ff + down(N_FF_CHUNKS - 1)
        y_ref[0, rb, :] = x1 + g2 * _rmsnorm(ff, n_post_ffn[...])


def _tail_sample_kernel(x_ref, o_ref, feat_ref, ys_ref, mod_ref, p0_ref, p1_ref, ln_w, ln_b,
                        w_out, n_post_mix, n_pre_ffn, n_post_ffn, w_up, w_down,
                        cw_ref, cb_ref, y_ref, u_ref, act_ref):
    n = x_ref.shape[0]
    seg = _seg_ones(RWKV_WIDTH)
    y_rwkv = _gn_epilogue(o_ref[...].T, feat_ref[7], feat_ref[6], ln_w[...], ln_b[...], seg)
    g1, sh2, sc2, g2 = mod_ref[2], mod_ref[3], mod_ref[4], mod_ref[5]
    x1, h2 = _tail_mix(x_ref[...], y_rwkv, ys_ref[...], g1, sh2, sc2, w_out[...],
                       n_post_mix[...], n_pre_ffn[...])
    u_ref[...] = jnp.dot(h2, w_up[...], preferred_element_type=F32)

    def taps_fn(r0, cols):
        rows = pl.ds(r0, FF_ROWS)
        return p1_ref[rows, cols], p0_ref[rows, cols], u_ref[rows, cols]

    _ffn_gate(act_ref, cw_ref, cb_ref, taps_fn, n)
    ff = jnp.dot(act_ref[...], w_down[...], preferred_element_type=F32)
    y_ref[...] = x1 + g2 * _rmsnorm(ff, n_post_ffn[...])


def _tail_prompt_call(x, y_rwkv, y_swa, mod_p, consts, tm):
    b, t, _ = x.shape
    tok = lambda i, j: (i, j, 0)
    in_specs = [
        pl.BlockSpec((1, tm, D_MODEL), tok),
        pl.BlockSpec((1, tm, RWKV_WIDTH), tok),
        pl.BlockSpec((1, tm, SWA_WIDTH), tok),
        pl.BlockSpec((6, 1, 1, D_MODEL), lambda i, j: (0, i, 0, 0)),
    ] + [_const_spec(c, 2) for c in consts]
    return pl.pallas_call(
        _tail_prompt_kernel,
        grid=(b, t // tm),
        in_specs=in_specs,
        out_specs=(
            pl.BlockSpec((1, tm, D_MODEL), tok),
            pl.BlockSpec((1, SUBLANES, 2 * D_FF), lambda i, j: (i, 0, 0)),
        ),
        out_shape=(
            jax.ShapeDtypeStruct((b, t, D_MODEL), F32),
            jax.ShapeDtypeStruct((b, SUBLANES, 2 * D_FF), F32),
        ),
        scratch_shapes=[
            pltpu.VMEM((SUBLANES, 2 * D_FF), F32),
            pltpu.VMEM((tm, D_FF), BF16),
        ],
        compiler_params=pltpu.CompilerParams(
            dimension_semantics=("arbitrary", "arbitrary"), vmem_limit_bytes=VMEM_LIMIT),
        name="tail_prompt",
    )(x, y_rwkv, y_swa, mod_p, *consts)


def _tail_sample_call(x, o_s, feat_s, y_swa, mod_s, p0, p1, ln_w, ln_b, consts):
    n = x.shape[0]
    args = (x, o_s, feat_s, y_swa, mod_s, p0, p1, ln_w, ln_b) + tuple(consts)
    out_shape = (
        jax.ShapeDtypeStruct((n, D_MODEL), F32),
        jax.ShapeDtypeStruct((n, 2 * D_FF), F32),
    )
    return pl.pallas_call(
        _tail_sample_kernel,
        grid=(1,),
        in_specs=[_const_spec(a, 1) for a in args],
        out_specs=tuple(pl.BlockSpec(s.shape, lambda i, nd=len(s.shape): (0,) * nd)
                        for s in out_shape),
        out_shape=out_shape,
        scratch_shapes=[pltpu.VMEM((n, D_FF), BF16)],
        compiler_params=pltpu.CompilerParams(
            dimension_semantics=("arbitrary",), vmem_limit_bytes=VMEM_LIMIT),
        name="tail_sample",
    )(*args)


def _rope_tables(pos):
    half = HEAD_DIM // 2
    inv = ROPE_THETA ** (-jnp.arange(half, dtype=F32) / half)
    ang = pos.astype(F32)[:, None] * inv[None, :]
    cos, sin = jnp.cos(ang), jnp.sin(ang)
    cos_h = jnp.concatenate([cos, cos], axis=-1)
    sin_h = jnp.concatenate([-sin, sin], axis=-1)
    return jnp.tile(cos_h, (1, LANES // HEAD_DIM)), jnp.tile(sin_h, (1, LANES // HEAD_DIM))


def kernel(x_prompt, x_sample, state_rwkv_wkv, state_rwkv_shift, cache_swa_k, cache_swa_v,
           state_ffn_conv, c_prompt, c_sample, w_ada, b_ada, norm_pre_mix, norm_post_mix,
           norm_pre_ffn, norm_post_ffn, w_in, rwkv_mu, rwkv_w0, rwkv_w_up, rwkv_a0, rwkv_a_up,
           rwkv_g_up, rwkv_k_k, rwkv_k_a, rwkv_r_k, rwkv_ln_w, rwkv_ln_b, swa_sinks, w_out,
           ffn_w_up, ffn_conv_w, ffn_conv_b, ffn_w_down):
    depth = w_ada.shape[0]
    assert depth == 1 and x_sample.shape[1] == 1
    b, t, _ = x_prompt.shape
    n = x_sample.shape[0]
    l = cache_swa_k.shape[2]
    tm = min(256, t)
    assert t % tm == 0 and tm % WINDOW == 0 and tm % CHUNK == 0
    assert tm % FF_ROWS == 0 and n % FF_ROWS == 0
    nb_swa = 8 if n % 8 == 0 else n
    li = 0

    row = lambda v: v.reshape(1, -1)
    w_in_b = w_in[li].astype(BF16)
    zeros_l = jnp.zeros((64, RWKV_WIDTH), F32)
    wa_up = jnp.concatenate([
        jnp.concatenate([rwkv_w_up[li], zeros_l], axis=1),
        jnp.concatenate([zeros_l, rwkv_a_up[li]], axis=1)], axis=0).astype(BF16)
    hid = jnp.arange(RWKV_WIDTH) // HEAD_DIM
    seg512 = (hid[:, None] == hid[None, :]).astype(BF16)
    proj_consts = (row(norm_pre_mix[li]), w_in_b, row(rwkv_mu[li]), row(rwkv_w0[li]), wa_up,
                   row(rwkv_a0[li]), rwkv_g_up[li].astype(BF16), row(rwkv_k_k[li]),
                   row(rwkv_k_a[li]), row(rwkv_r_k[li]), seg512)
    tail_consts = (w_out[li].astype(BF16), row(norm_post_mix[li]), row(norm_pre_ffn[li]),
                   row(norm_post_ffn[li]), ffn_w_up[li].astype(BF16), ffn_w_down[li].astype(BF16),
                   ffn_conv_w[li], row(ffn_conv_b[li]))
    ln_w, ln_b = row(rwkv_ln_w[li]), row(rwkv_ln_b[li])

    mod = _ada_call(jnp.concatenate([c_prompt, c_sample], axis=0), w_ada[li], row(b_ada[li]))
    mod_p = mod[:, :b].reshape(6, b, 1, D_MODEL)
    mod_s = mod[:, b:]

    cos_p, sin_p = _rope_tables(jnp.arange(t, dtype=jnp.int32))
    y_rwkv_p, s_cat, q_p, kx_p, vx_p, klast, vlast, plast = _mix_prompt_call(
        x_prompt, mod_p, cos_p, sin_p, proj_consts + (ln_w, ln_b), tm)
    y_swa_p = _swa_prompt_call(swa_sinks[li], q_p, kx_p, vx_p, min(2 * tm, t))
    y_p, cp = _tail_prompt_call(x_prompt, y_rwkv_p, y_swa_p, mod_p, tail_consts, min(2 * tm, t))

    wkv_p = s_cat.reshape(b, HEAD_DIM, RWKV_HEADS, HEAD_DIM).transpose(0, 2, 1, 3)
    shift_p = plast[:, SUBLANES - 1]
    k_p = klast.reshape(b, WINDOW, SWA_KV_HEADS, HEAD_DIM)
    v_p = vlast.reshape(b, WINDOW, SWA_KV_HEADS, HEAD_DIM)
    conv_p = cp[:, SUBLANES - 2:]

    cos_s, sin_s = _rope_tables(jnp.full((1,), PAST_LEN, jnp.int32))
    feat_s, ft_s, q_s, kn_s, vn_s, p_s = _proj_sample_call(
        x_sample[:, 0], mod_s, cos_s, sin_s, state_rwkv_shift[li], proj_consts)
    state_t = jnp.transpose(state_rwkv_wkv[li], (1, 2, 3, 0))
    wkv_t, o_t = _wkv_sample_call(ft_s, state_t)
    wkv_s = jnp.transpose(wkv_t, (3, 0, 1, 2))

    g2 = SWA_KV_HEADS
    q4 = q_s.reshape(n * g2, SWA_GROUP, HEAD_DIM)
    q8 = jnp.concatenate([q4, jnp.zeros_like(q4)], axis=1)
    sink_t = jnp.broadcast_to(
        jnp.concatenate([swa_sinks[li].reshape(g2, SWA_GROUP),
                         jnp.full((g2, SWA_GROUP), MASK_VALUE, F32)], axis=1)[None, :, :, None],
        (nb_swa, g2, SUBLANES, LANES)).reshape(nb_swa * g2, SUBLANES, LANES)
    cols = lambda a: a.reshape(n // nb_swa, nb_swa, KV_WIDTH).transpose(0, 2, 1)
    ck = jnp.transpose(cache_swa_k[li], (0, 2, 3, 1))
    cv = jnp.transpose(cache_swa_v[li], (0, 2, 3, 1))
    o_att, k_t, v_t = _swa_sample_call(
        sink_t, q8, kn_s.reshape(n * g2, 1, HEAD_DIM), vn_s.reshape(n * g2, 1, HEAD_DIM),
        cols(kn_s), cols(vn_s), ck, cv, nb_swa)
    y_swa_s = o_att[:, :SWA_GROUP].reshape(n, SWA_WIDTH)
    k_s = jnp.transpose(k_t, (0, 3, 1, 2))
    v_s = jnp.transpose(v_t, (0, 3, 1, 2))

    conv0 = state_ffn_conv[li]
    y_s, u_s = _tail_sample_call(x_sample[:, 0], o_t, feat_s, y_swa_s.astype(BF16), mod_s,
                                 conv0[:, 0], conv0[:, 1], ln_w, ln_b, tail_consts)
    conv_s = jnp.stack([conv0[:, 1], u_s], axis=1)

    expand = lambda a: a[None]
    return (y_p, y_s[:, None, :], expand(wkv_p), expand(shift_p), expand(k_p), expand(v_p),
            expand(conv_p), expand(wkv_s), expand(p_s),
            expand(k_s), expand(v_s), expand(conv_s))
```

```python
import math

import jax
import jax.numpy as jnp
from jax import lax
from jax.experimental import pallas as pl
from jax.experimental.pallas import tpu as pltpu

D_MODEL = 1024
HEAD_DIM = 64
RWKV_WIDTH = 512
RWKV_HEADS = 8
RWKV_COLS = 1792
RWKV_GN_EPS = 64e-5
SWA_WIDTH = 512
SWA_HEADS = 8
SWA_KV_HEADS = 2
SWA_GROUP = 4
KV_WIDTH = SWA_KV_HEADS * HEAD_DIM
WINDOW = 128
PAST_LEN = 16384
ROPE_THETA = 10000.0
ATTN_SCALE = HEAD_DIM ** -0.5
D_FF = 2816
NORM_EPS = 1e-6
MASK_VALUE = -1e30
PROJ_COLS = RWKV_COLS + SWA_WIDTH + 2 * KV_WIDTH

LANES = 128
SUBLANES = 8
CHUNK = 64
GROUP_LANES = 256
FF_CHUNK = 256
N_FF_CHUNKS = D_FF // FF_CHUNK
FF_ROWS = 32
TAIL_SUB = 256
VMEM_LIMIT = 56 * 1024 * 1024

F32 = jnp.float32
BF16 = jnp.bfloat16


def _sigmoid(x):
    return 1.0 / (1.0 + jnp.exp(-x))


def _silu(x):
    return x * _sigmoid(x)


def _rmsnorm(x, g):
    return x * lax.rsqrt(jnp.mean(x * x, axis=-1, keepdims=True) + NORM_EPS) * g


def _dot(a, b):
    return jnp.dot(a.astype(BF16), b.astype(BF16), preferred_element_type=F32)


def _dot_nt(a, b):
    return lax.dot_general(a.astype(BF16), b.astype(BF16), (((1,), (1,)), ((), ())),
                           preferred_element_type=F32)


def _dot_tn(a, b):
    return lax.dot_general(a.astype(BF16), b.astype(BF16), (((0,), (0,)), ((), ())),
                           preferred_element_type=F32)


def _every(fn, *lists):
    return [fn(*a) for a in zip(*lists)]


def _swap_halves(x):
    w = x.shape[-1]
    lane = lax.broadcasted_iota(jnp.int32, x.shape, x.ndim - 1)
    lo = (lane & (HEAD_DIM // 2)) == 0
    return jnp.where(lo, pltpu.roll(x, w - HEAD_DIM // 2, x.ndim - 1),
                     pltpu.roll(x, HEAD_DIM // 2, x.ndim - 1))


def _rope(x, cos, sin):
    reps = x.shape[-1] // LANES
    cos_w = jnp.concatenate([cos] * reps, axis=-1) if reps > 1 else cos
    sin_w = jnp.concatenate([sin] * reps, axis=-1) if reps > 1 else sin
    return x * cos_w + _swap_halves(x) * sin_w


def _proj_features(x, shift, scale, g_pre, w_in, prev_fn, mu, w0, wa_up, a0, g_up,
                   k_k, k_a, r_k, seg_blk, cos, sin):
    h = _rmsnorm(x, g_pre) * (1.0 + scale) + shift
    p = _dot(h, w_in)
    p_rwkv = p[:, :RWKV_COLS]
    prev = prev_fn(p_rwkv)
    xm = p_rwkv + (prev - p_rwkv) * mu
    r = xm[:, 0:512]
    k = xm[:, 512:1024]
    v = xm[:, 1024:1536]
    wa = xm[:, 1536:1664]
    gd = xm[:, 1664:1792]
    lane = lax.broadcasted_iota(jnp.int32, wa.shape, 1)
    wa_act = jnp.where(lane < 64, jnp.tanh(wa), wa)
    lora = _dot(wa_act, wa_up)
    lw = -math.exp(-0.5) * _sigmoid(w0 + lora[:, :512])
    a = _sigmoid(a0 + lora[:, 512:])
    g = _dot(_sigmoid(gd), g_up)
    kk = k * k_k
    kf = k * (1.0 + (a - 1.0) * k_a)
    gl = seg_blk.shape[0]
    n = x.shape[0]
    sums = [_dot(jnp.concatenate([(kk * kk)[:, i:i + gl], (r * kf * r_k)[:, i:i + gl]], axis=0),
                 seg_blk) for i in range(0, RWKV_WIDTH, gl)]
    ss = jnp.concatenate([s_[:n] for s_ in sums], axis=1)
    kk = kk / jnp.maximum(jnp.sqrt(ss), 1e-12)
    bonus = jnp.concatenate([s_[n:] for s_ in sums], axis=1) * v
    feats = (r, lw, kf, v, -kk, kk * a, g, bonus)
    q = _rope(p[:, RWKV_COLS:RWKV_COLS + SWA_WIDTH], cos, sin)
    ks = _rope(p[:, RWKV_COLS + SWA_WIDTH:RWKV_COLS + SWA_WIDTH + KV_WIDTH], cos, sin)
    vs = p[:, RWKV_COLS + SWA_WIDTH + KV_WIDTH:]
    return feats, q, ks, vs, p_rwkv


def _gn_epilogue(o, bonus, g, ln_w, ln_b, seg):
    mu = _dot(o, seg) * (1.0 / HEAD_DIM)
    d = o - mu
    var = _dot(d * d, seg) * (1.0 / HEAD_DIM)
    gn = d * lax.rsqrt(var + RWKV_GN_EPS) * ln_w + ln_b
    return (gn + bonus) * g


def _seg_ones(n):
    r = lax.broadcasted_iota(jnp.int32, (n, n), 0) // HEAD_DIM
    c = lax.broadcasted_iota(jnp.int32, (n, n), 1) // HEAD_DIM
    return (r == c).astype(BF16)


def _ada_kernel(c_ref, w_ref, b_ref, o_ref):
    o_ref[0] = _dot(_silu(c_ref[...]), w_ref[...]) + b_ref[...]


def _ada_call(c_all, w_ada, b_ada):
    rows = c_all.shape[0]
    return pl.pallas_call(
        _ada_kernel,
        grid=(6,),
        in_specs=[
            pl.BlockSpec((rows, D_MODEL), lambda j: (0, 0)),
            pl.BlockSpec((D_MODEL, D_MODEL), lambda j: (0, j)),
            pl.BlockSpec((1, D_MODEL), lambda j: (0, j)),
        ],
        out_specs=pl.BlockSpec((1, rows, D_MODEL), lambda j: (j, 0, 0)),
        out_shape=jax.ShapeDtypeStruct((6, rows, D_MODEL), F32),
        compiler_params=pltpu.CompilerParams(
            dimension_semantics=("arbitrary",), vmem_limit_bytes=VMEM_LIMIT),
        name="ada",
    )(c_all, w_ada, b_ada)


def _expand_kv(x):
    lane = lax.broadcasted_iota(jnp.int32, x.shape, 1)
    rolled = pltpu.roll(x, HEAD_DIM, 1)
    g0 = jnp.where(lane < HEAD_DIM, x, rolled)
    g1 = jnp.where(lane < HEAD_DIM, rolled, x)
    return jnp.concatenate([g0, g0, g1, g1], axis=1)


def _mix_prompt_kernel(x_ref, mod_ref, cos_ref, sin_ref, g_pre, w_in, mu, w0, wa_up, a0,
                       g_up, k_k, k_a, r_k, seg_blk, ln_w, ln_b,
                       y_ref, s_out_ref, q_ref, k_ref, v_ref, klast_ref, vlast_ref, plast_ref,
                       carry_ref, s_ref):
    t = pl.program_id(1)

    @pl.when(t == 0)
    def _():
        carry_ref[...] = jnp.zeros_like(carry_ref)
        s_ref[...] = jnp.zeros_like(s_ref)

    tm = x_ref.shape[1]
    carry_row = carry_ref[SUBLANES - 1:SUBLANES, :]

    def prev_fn(p_rwkv):
        row = lax.broadcasted_iota(jnp.int32, p_rwkv.shape, 0)
        return jnp.where(row == 0, carry_row, pltpu.roll(p_rwkv, 1, 0))

    feats, q, ks, vs, p_rwkv = _proj_features(
        x_ref[0], mod_ref[0, 0], mod_ref[1, 0], g_pre[...], w_in[...], prev_fn, mu[...],
        w0[...], wa_up[...], a0[...], g_up[...], k_k[...], k_a[...], r_k[...], seg_blk[...],
        cos_ref[...], sin_ref[...])
    q_ref[0] = (q * ATTN_SCALE).astype(BF16)
    k_ref[0] = _expand_kv(ks).astype(BF16)
    v_ref[0] = _expand_kv(vs).astype(BF16)
    last = p_rwkv[tm - SUBLANES:, :]
    carry_ref[...] = last
    plast_ref[0] = last
    _wkv_tile(feats, ln_w, ln_b, y_ref, s_ref)

    @pl.when(t == pl.num_programs(1) - 1)
    def _():
        klast_ref[0] = ks[tm - WINDOW:, :]
        vlast_ref[0] = vs[tm - WINDOW:, :]
        for g in range(RWKV_WIDTH // GROUP_LANES):
            s_out_ref[0, :, g * GROUP_LANES:(g + 1) * GROUP_LANES] = s_ref[g]


def _proj_sample_kernel(x_ref, mod_ref, cos_ref, sin_ref, prev_ref, g_pre, w_in, mu, w0,
                        wa_up, a0, g_up, k_k, k_a, r_k, seg_blk,
                        feat_ref, ft_ref, q_ref, k_ref, v_ref, p_ref):
    feats, q, ks, vs, p_rwkv = _proj_features(
        x_ref[...], mod_ref[0], mod_ref[1], g_pre[...], w_in[...], lambda p: prev_ref[...],
        mu[...], w0[...], wa_up[...], a0[...], g_up[...], k_k[...], k_a[...], r_k[...],
        seg_blk[...], cos_ref[...], sin_ref[...])
    for i, f in enumerate(feats):
        feat_ref[i] = f
    for i in range(6):
        ft_ref[i] = feats[i].T
    q_ref[...] = (q * ATTN_SCALE).astype(BF16)
    k_ref[...] = ks
    v_ref[...] = vs
    p_ref[...] = p_rwkv


def _const_spec(arr, grid_rank):
    zeros = (0,) * arr.ndim
    if grid_rank == 1:
        return pl.BlockSpec(arr.shape, lambda i: zeros)
    return pl.BlockSpec(arr.shape, lambda i, j: zeros)


def _mix_prompt_call(x, mod_p, cos, sin, consts, tm):
    b, t, _ = x.shape
    nt = t // tm
    in_specs = [
        pl.BlockSpec((1, tm, D_MODEL), lambda i, j: (i, j, 0)),
        pl.BlockSpec((6, 1, 1, D_MODEL), lambda i, j: (0, i, 0, 0)),
        pl.BlockSpec((tm, LANES), lambda i, j: (j, 0)),
        pl.BlockSpec((tm, LANES), lambda i, j: (j, 0)),
    ] + [_const_spec(c, 2) for c in consts]
    out_shape = (
        jax.ShapeDtypeStruct((b, t, RWKV_WIDTH), BF16),
        jax.ShapeDtypeStruct((b, HEAD_DIM, RWKV_WIDTH), F32),
        jax.ShapeDtypeStruct((b, t, SWA_WIDTH), BF16),
        jax.ShapeDtypeStruct((b, t, SWA_WIDTH), BF16),
        jax.ShapeDtypeStruct((b, t, SWA_WIDTH), BF16),
        jax.ShapeDtypeStruct((b, WINDOW, KV_WIDTH), F32),
        jax.ShapeDtypeStruct((b, WINDOW, KV_WIDTH), F32),
        jax.ShapeDtypeStruct((b, SUBLANES, RWKV_COLS), F32),
    )
    out_specs = (
        pl.BlockSpec((1, tm, RWKV_WIDTH), lambda i, j: (i, j, 0)),
        pl.BlockSpec((1, HEAD_DIM, RWKV_WIDTH), lambda i, j: (i, 0, 0)),
        pl.BlockSpec((1, tm, SWA_WIDTH), lambda i, j: (i, j, 0)),
        pl.BlockSpec((1, tm, SWA_WIDTH), lambda i, j: (i, j, 0)),
        pl.BlockSpec((1, tm, SWA_WIDTH), lambda i, j: (i, j, 0)),
        pl.BlockSpec((1, WINDOW, KV_WIDTH), lambda i, j: (i, 0, 0)),
        pl.BlockSpec((1, WINDOW, KV_WIDTH), lambda i, j: (i, 0, 0)),
        pl.BlockSpec((1, SUBLANES, RWKV_COLS), lambda i, j: (i, 0, 0)),
    )
    return pl.pallas_call(
        _mix_prompt_kernel,
        grid=(b, nt),
        in_specs=in_specs,
        out_specs=out_specs,
        out_shape=out_shape,
        scratch_shapes=[
            pltpu.VMEM((SUBLANES, RWKV_COLS), F32),
            pltpu.VMEM((RWKV_WIDTH // GROUP_LANES, HEAD_DIM, GROUP_LANES), F32),
        ],
        compiler_params=pltpu.CompilerParams(
            dimension_semantics=("arbitrary", "arbitrary"), vmem_limit_bytes=VMEM_LIMIT),
        name="mix_prompt",
    )(x, mod_p, cos, sin, *consts)


def _proj_sample_call(x, mod_s, cos, sin, prev, consts):
    n = x.shape[0]
    args = (x, mod_s, cos, sin, prev) + tuple(consts)
    out_shape = (
        jax.ShapeDtypeStruct((8, n, RWKV_WIDTH), F32),
        jax.ShapeDtypeStruct((6, RWKV_WIDTH, n), F32),
        jax.ShapeDtypeStruct((n, SWA_WIDTH), BF16),
        jax.ShapeDtypeStruct((n, KV_WIDTH), F32),
        jax.ShapeDtypeStruct((n, KV_WIDTH), F32),
        jax.ShapeDtypeStruct((n, RWKV_COLS), F32),
    )
    return pl.pallas_call(
        _proj_sample_kernel,
        grid=(1,),
        in_specs=[_const_spec(a, 1) for a in args],
        out_specs=tuple(pl.BlockSpec(s.shape, lambda i, nd=len(s.shape): (0,) * nd)
                        for s in out_shape),
        out_shape=out_shape,
        compiler_params=pltpu.CompilerParams(
            dimension_semantics=("arbitrary",), vmem_limit_bytes=VMEM_LIMIT),
        name="proj_sample",
    )(*args)


def _wkv_tile(feats, ln_w, ln_b, y_ref, s_ref):
    tt = feats[0].shape[0]
    n_chunks = tt // CHUNK
    gl = GROUP_LANES
    n_groups = RWKV_WIDTH // gl
    heads_per_group = gl // HEAD_DIM
    probs = [(c, g) for c in range(n_chunks) for g in range(n_groups)]

    row_c = lax.broadcasted_iota(jnp.int32, (CHUNK, gl), 0)
    col_c = lax.broadcasted_iota(jnp.int32, (CHUNK, gl), 1) % CHUNK
    strict = row_c > col_c
    incl = row_c >= col_c
    eye_cat = (row_c == col_c).astype(F32)
    rb = lax.broadcasted_iota(jnp.int32, (gl, gl), 0) // HEAD_DIM
    cb = lax.broadcasted_iota(jnp.int32, (gl, gl), 1) // HEAD_DIM
    bd_mask = rb == cb
    tri_r = lax.broadcasted_iota(jnp.int32, (CHUNK, CHUNK), 0)
    tri_c = lax.broadcasted_iota(jnp.int32, (CHUNK, CHUNK), 1)
    tril_ones = (tri_r >= tri_c).astype(BF16)
    seg = bd_mask.astype(BF16)

    def bd(x):
        xb = x.astype(BF16)
        return jnp.where(bd_mask, jnp.concatenate([xb] * heads_per_group, axis=0),
                         jnp.zeros((), BF16))

    def fold(x):
        xm = jnp.where(bd_mask, x, 0.0)
        acc = xm[0:HEAD_DIM]
        for hh in range(1, heads_per_group):
            acc = acc + xm[hh * HEAD_DIM:(hh + 1) * HEAD_DIM]
        return acc

    def ld(i, p):
        c, g = p
        return feats[i][c * CHUNK:(c + 1) * CHUNK, g * gl:(g + 1) * gl]

    lw = [ld(1, p) for p in probs]
    na = [ld(4, p) for p in probs]
    bb = [ld(5, p) for p in probs]
    kf = [ld(2, p) for p in probs]
    r = [ld(0, p) for p in probs]
    v = [ld(3, p) for p in probs]

    def cumsum(x):
        hi = x.astype(BF16)
        lo = (x - hi.astype(F32)).astype(BF16)
        both = jnp.dot(tril_ones, jnp.concatenate([hi, lo], axis=1), preferred_element_type=F32)
        return both[:, :gl] + both[:, gl:]

    cum = _every(cumsum, lw)
    cum_last = [x[CHUNK - 1:CHUNK, :] for x in cum]
    e_out = [jnp.exp(-x) for x in cum]
    e_end = _every(lambda cl, x: jnp.exp(cl - x), cum_last, cum)
    a_t = _every(lambda n_, x, l_: n_ * jnp.exp(x - l_), na, cum, lw)
    r_t = _every(lambda r_, x: r_ * jnp.exp(x), r, cum)
    b_t = _every(lambda b_, e: b_ * e, bb, e_out)
    k_t = _every(lambda k_, e: k_ * e, kf, e_out)
    b_end = _every(lambda b_, e: b_ * e, bb, e_end)
    k_end = _every(lambda k_, e: k_ * e, kf, e_end)
    gamma = [jnp.exp(x) for x in cum_last]

    ar = _every(lambda a_, r_: jnp.concatenate([a_, r_], axis=0), a_t, r_t)
    pb = _every(lambda x, y: _dot_nt(x, bd(y)), ar, b_t)
    pk = _every(lambda x, y: _dot_nt(x, bd(y)), ar, k_t)
    l_ab = [jnp.where(strict, x[:CHUNK], 0.0) for x in pb]
    l_ak = [jnp.where(strict, x[:CHUNK], 0.0) for x in pk]
    m_rb = [jnp.where(incl, x[CHUNK:], 0.0) for x in pb]
    m_rk = [jnp.where(incl, x[CHUNK:], 0.0) for x in pk]

    x_acc = [eye_cat + l for l in l_ab]
    pw = _every(lambda l: _dot(l, bd(l)), l_ab)
    n_sq = int(math.log2(CHUNK)) - 1
    for lvl in range(n_sq):
        rhs = [bd(p_) for p_ in pw]
        if lvl < n_sq - 1:
            both = _every(lambda x, p_, w_: _dot(jnp.concatenate([x, p_], axis=0), w_),
                          x_acc, pw, rhs)
            x_acc = _every(lambda x, b_: x + b_[:CHUNK], x_acc, both)
            pw = [b_[CHUNK:] for b_ in both]
        else:
            x_acc = _every(lambda x, w_: x + _dot(x, w_), x_acc, rhs)
    t_inv = x_acc

    kv = _every(lambda la, mk, x: _dot(jnp.concatenate([la, mk], axis=0), bd(x)), l_ak, m_rk, v)
    y_loc = [x[:CHUNK] for x in kv]
    wu = _every(lambda t_, a_, y_: _dot(t_, jnp.concatenate([bd(a_), bd(y_)], axis=1)),
                t_inv, a_t, y_loc)
    w_t = [x[:, :gl] for x in wu]
    u_loc = [x[:, gl:] for x in wu]

    mwu = _every(lambda mb, w_, u_: _dot(mb, jnp.concatenate([bd(w_), bd(u_)], axis=1)),
                 m_rb, w_t, u_loc)
    q_c = _every(lambda r_, x: r_ + x[:, :gl], r_t, mwu)
    o_loc = _every(lambda x, y_: x[:, gl:] + y_[CHUNK:], mwu, kv)
    m_low = _every(lambda w_, b_: jnp.where(bd_mask, _dot_tn(w_, b_), 0.0).astype(BF16),
                   w_t, b_end)
    n_loc = _every(lambda u_, v_, b_, k_: fold(_dot_tn(jnp.concatenate([u_, v_], axis=0),
                                                       jnp.concatenate([b_, k_], axis=0))),
                   u_loc, v, b_end, k_end)

    state = [s_ref[g] for g in range(n_groups)]
    starts = []
    for c in range(n_chunks):
        starts.append(state)
        idx = [c * n_groups + g for g in range(n_groups)]
        state = [state[g] * gamma[i] + _dot(state[g], m_low[i]) + n_loc[i]
                 for g, i in enumerate(idx)]
    for g in range(n_groups):
        s_ref[g] = state[g]

    s0 = [starts[c][g] for (c, g) in probs]
    o = _every(lambda q_, s_, ol: _dot_nt(q_, bd(s_)) + ol, q_c, s0, o_loc)

    n_p = len(probs)
    unstack = lambda x: [x[i * CHUNK:(i + 1) * CHUNK] for i in range(n_p)]
    mu = unstack(_dot(jnp.concatenate(o, axis=0), seg) * (1.0 / HEAD_DIM))
    d = _every(lambda x, m_: x - m_, o, mu)
    var = unstack(_dot(jnp.concatenate([x * x for x in d], axis=0), seg) * (1.0 / HEAD_DIM))
    for i, (c, g) in enumerate(probs):
        ls = slice(g * gl, (g + 1) * gl)
        gn = d[i] * lax.rsqrt(var[i] + RWKV_GN_EPS) * ln_w[:, ls] + ln_b[:, ls]
        y = (gn + ld(7, (c, g))) * ld(6, (c, g))
        y_ref[0, c * CHUNK:(c + 1) * CHUNK, ls] = y.astype(y_ref.dtype)


def _wkv_sample_kernel(ft_ref, s_ref, s_out_ref, o_ref):
    hd = HEAD_DIM
    r, kf, na, bb = ft_ref[0], ft_ref[2], ft_ref[4], ft_ref[5]
    w = jnp.exp(ft_ref[1])

    def value_block(vb, carry):
        v0 = pl.multiple_of(vb * SUBLANES, SUBLANES)
        v_rows = ft_ref[3, pl.ds(v0, SUBLANES), :]
        outs = []
        for j in range(SUBLANES):
            s = s_ref[0, v0 + j]
            sa = jnp.sum(s * na, axis=0, keepdims=True)
            s_new = s * w + sa * bb + v_rows[j:j + 1] * kf
            s_out_ref[0, v0 + j] = s_new
            outs.append(jnp.sum(s_new * r, axis=0, keepdims=True))
        o_ref[pl.ds(v0, SUBLANES), :] = jnp.concatenate(outs, axis=0)
        return carry

    lax.fori_loop(0, hd // SUBLANES, value_block, 0)


def _wkv_sample_call(ft_s, state_t):
    h, hd, _, n = state_t.shape
    return pl.pallas_call(
        _wkv_sample_kernel,
        grid=(h,),
        in_specs=[
            pl.BlockSpec((6, hd, n), lambda i: (0, i, 0)),
            pl.BlockSpec((1, hd, hd, n), lambda i: (i, 0, 0, 0)),
        ],
        out_specs=(
            pl.BlockSpec((1, hd, hd, n), lambda i: (i, 0, 0, 0)),
            pl.BlockSpec((hd, n), lambda i: (i, 0)),
        ),
        out_shape=(
            jax.ShapeDtypeStruct(state_t.shape, F32),
            jax.ShapeDtypeStruct((h * hd, n), F32),
        ),
        compiler_params=pltpu.CompilerParams(
            dimension_semantics=("arbitrary",), vmem_limit_bytes=VMEM_LIMIT),
        name="wkv_sample",
    )(ft_s, state_t)


def _swa_prompt_kernel(sink_ref, q_ref, kp_ref, kc_ref, vp_ref, vc_ref, o_ref):
    j = pl.program_id(1)
    w = WINDOW
    tq = q_ref.shape[1]
    n_blk = tq // w
    gl = GROUP_LANES
    rows = SWA_GROUP * w
    probs = [(qi, g) for qi in range(n_blk) for g in range(SWA_KV_HEADS)]

    lane_head = lax.broadcasted_iota(jnp.int32, (w, gl), 1) // HEAD_DIM
    head_mask = [lane_head == h for h in range(SWA_GROUP)]
    qi_ = lax.broadcasted_iota(jnp.int32, (rows, 2 * w), 0) % w
    ki_ = lax.broadcasted_iota(jnp.int32, (rows, 2 * w), 1)
    diff = qi_ - (ki_ - w)
    in_window = (diff >= 0) & (diff < WINDOW)
    first_valid = in_window & ((j * tq + ki_ - w) >= 0)

    def keys(ref_prev, ref_cur, qi, g):
        ls = slice(g * gl, (g + 1) * gl)
        prev = ref_prev[0, :, ls] if qi == 0 else ref_cur[0, (qi - 1) * w:qi * w, ls]
        return jnp.concatenate([prev, ref_cur[0, qi * w:(qi + 1) * w, ls]], axis=0)

    def lhs(qi, g):
        qg = q_ref[0, qi * w:(qi + 1) * w, g * gl:(g + 1) * gl]
        return jnp.concatenate([jnp.where(m, qg, jnp.zeros((), BF16)) for m in head_mask], axis=0)

    sinks = []
    for g in range(SWA_KV_HEADS):
        sinks.append(jnp.concatenate(
            [jnp.full((w, 1), sink_ref[g * SWA_GROUP + h], F32) for h in range(SWA_GROUP)], axis=0))

    s = [_dot_nt(lhs(qi, g), keys(kp_ref, kc_ref, qi, g)) for qi, g in probs]
    s = [jnp.where(first_valid if qi == 0 else in_window, x, MASK_VALUE)
         for x, (qi, g) in zip(s, probs)]
    sink = [sinks[g] for qi, g in probs]
    m = _every(lambda x, sk: jnp.maximum(jnp.max(x, axis=-1, keepdims=True), sk), s, sink)
    e = _every(lambda x, m_: jnp.exp(x - m_), s, m)
    inv = _every(lambda e_, sk, m_: 1.0 / (jnp.sum(e_, axis=-1, keepdims=True) + jnp.exp(sk - m_)),
                 e, sink, m)
    prob = _every(lambda e_, i_: (e_ * i_).astype(BF16), e, inv)
    og = [_dot(p_, keys(vp_ref, vc_ref, qi, g)) for p_, (qi, g) in zip(prob, probs)]
    for x, (qi, g) in zip(og, probs):
        y = jnp.where(head_mask[0], x[0:w], 0.0)
        for h in range(1, SWA_GROUP):
            y = y + jnp.where(head_mask[h], x[h * w:(h + 1) * w], 0.0)
        o_ref[0, qi * w:(qi + 1) * w, g * gl:(g + 1) * gl] = y.astype(o_ref.dtype)


def _swa_prompt_call(sinks, q, k, v, tq):
    b, t, _ = q.shape
    w = WINDOW
    per = tq // w
    prev = lambda i, j: (i, jnp.maximum(j * per - 1, 0), 0)
    cur = lambda i, j: (i, j, 0)
    return pl.pallas_call(
        _swa_prompt_kernel,
        grid=(b, t // tq),
        in_specs=[
            pl.BlockSpec(memory_space=pltpu.SMEM),
            pl.BlockSpec((1, tq, SWA_WIDTH), cur),
            pl.BlockSpec((1, w, SWA_WIDTH), prev),
            pl.BlockSpec((1, tq, SWA_WIDTH), cur),
            pl.BlockSpec((1, w, SWA_WIDTH), prev),
            pl.BlockSpec((1, tq, SWA_WIDTH), cur),
        ],
        out_specs=pl.BlockSpec((1, tq, SWA_WIDTH), cur),
        out_shape=jax.ShapeDtypeStruct((b, t, SWA_WIDTH), BF16),
        compiler_params=pltpu.CompilerParams(
            dimension_semantics=("arbitrary", "arbitrary"), vmem_limit_bytes=VMEM_LIMIT),
        name="swa_prompt",
    )(sinks, q, k, k, v, v)


def _swa_sample_kernel(sink_ref, q_ref, knr_ref, vnr_ref, knt_ref, vnt_ref, ck_ref, cv_ref,
                       o_ref, ko_ref, vo_ref):
    nb, _, hd, l = ck_ref.shape
    q = q_ref[...]
    ck = ck_ref[...].reshape(nb * SWA_KV_HEADS, hd, l)
    cv = cv_ref[...].reshape(nb * SWA_KV_HEADS, hd, l)
    knr = knr_ref[...].astype(BF16).astype(F32)
    vnr = vnr_ref[...].astype(BF16).astype(F32)
    ki = lax.broadcasted_iota(jnp.int32, (1, 1, l), 2)
    kpos = PAST_LEN - l + ki
    diff = PAST_LEN - kpos
    valid = (diff >= 0) & (diff < WINDOW) & (kpos >= 0)
    s_c = lax.dot_general(q, ck.astype(BF16), (((2,), (1,)), ((0,), (0,))),
                          preferred_element_type=F32)
    s_c = jnp.where(valid, s_c, MASK_VALUE)
    s_n = jnp.sum(q.astype(F32) * knr, axis=-1, keepdims=True)
    sink = sink_ref[:, :, 0:1]
    m = jnp.maximum(jnp.maximum(jnp.max(s_c, axis=-1, keepdims=True), s_n), sink)
    e_c = jnp.exp(s_c - m)
    e_n = jnp.exp(s_n - m)
    denom = jnp.sum(e_c, axis=-1, keepdims=True) + e_n + jnp.exp(sink - m)
    p_c = (e_c / denom).astype(BF16)
    p_n = (e_n / denom).astype(BF16).astype(F32)
    o = lax.dot_general(p_c, cv.astype(BF16), (((2,), (2,)), ((0,), (0,))),
                        preferred_element_type=F32)
    o_ref[...] = o + p_n * vnr
    lane = lax.broadcasted_iota(jnp.int32, (hd, l), 1)
    for b in range(nb):
        for g in range(SWA_KV_HEADS):
            rows = slice(g * hd, (g + 1) * hd)
            ko_ref[b, g] = jnp.where(lane == l - 1, knt_ref[0, rows, b:b + 1],
                                     pltpu.roll(ck_ref[b, g], l - 1, 1))
            vo_ref[b, g] = jnp.where(lane == l - 1, vnt_ref[0, rows, b:b + 1],
                                     pltpu.roll(cv_ref[b, g], l - 1, 1))


def _swa_sample_call(sink_t, q8, knr, vnr, knt, vnt, ck, cv, nb):
    n, g, hd, l = ck.shape
    rows = nb * g
    blk3 = lambda i: (i, 0, 0)
    blk4 = lambda i: (i, 0, 0, 0)
    return pl.pallas_call(
        _swa_sample_kernel,
        grid=(n // nb,),
        in_specs=[
            pl.BlockSpec((rows, SUBLANES, LANES), lambda i: (0, 0, 0)),
            pl.BlockSpec((rows, SUBLANES, hd), blk3),
            pl.BlockSpec((rows, 1, hd), blk3),
            pl.BlockSpec((rows, 1, hd), blk3),
            pl.BlockSpec((1, g * hd, nb), blk3),
            pl.BlockSpec((1, g * hd, nb), blk3),
            pl.BlockSpec((nb, g, hd, l), blk4),
            pl.BlockSpec((nb, g, hd, l), blk4),
        ],
        out_specs=(
            pl.BlockSpec((rows, SUBLANES, hd), blk3),
            pl.BlockSpec((nb, g, hd, l), blk4),
            pl.BlockSpec((nb, g, hd, l), blk4),
        ),
        out_shape=(
            jax.ShapeDtypeStruct((n * g, SUBLANES, hd), F32),
            jax.ShapeDtypeStruct(ck.shape, F32),
            jax.ShapeDtypeStruct(cv.shape, F32),
        ),
        compiler_params=pltpu.CompilerParams(
            dimension_semantics=("arbitrary",), vmem_limit_bytes=VMEM_LIMIT),
        name="swa_sample",
    )(sink_t, q8, knr, vnr, knt, vnt, ck, cv)


def _tail_mix(x, y_rwkv, y_swa, g1, sh2, sc2, w_out, n_post_mix, n_pre_ffn):
    mix = _dot(y_rwkv, w_out[:RWKV_WIDTH]) + _dot(y_swa, w_out[RWKV_WIDTH:])
    x1 = x + g1 * _rmsnorm(mix, n_post_mix)
    h2 = (_rmsnorm(x1, n_pre_ffn) * (1.0 + sc2) + sh2).astype(BF16)
    return x1, h2


def _ffn_gate(act_ref, cw, cb, taps_fn, n_rows):
    def row_block(rb, carry):
        r0 = pl.multiple_of(rb * FF_ROWS, FF_ROWS)
        for c in range(N_FF_CHUNKS):
            z = []
            for half in range(2):
                cols = slice(half * D_FF + c * FF_CHUNK, half * D_FF + (c + 1) * FF_CHUNK)
                p1, p2, u = taps_fn(r0, cols)
                z.append(cb[:, cols] + p2 * cw[0:1, cols] + p1 * cw[1:2, cols] + u * cw[2:3, cols])
            act_ref[pl.ds(r0, FF_ROWS), c * FF_CHUNK:(c + 1) * FF_CHUNK] = (
                _silu(z[0]) * z[1]).astype(BF16)
        return carry

    lax.fori_loop(0, n_rows // FF_ROWS, row_block, 0)


def _tail_prompt_kernel(x_ref, yr_ref, ys_ref, mod_ref, w_out, n_post_mix, n_pre_ffn,
                        n_post_ffn, w_up, w_down, cw_ref, cb_ref,
                        y_ref, cp_ref, carry_ref, act_ref):
    t = pl.program_id(1)
    tm = x_ref.shape[1]
    sub = min(tm, TAIL_SUB)
    blocks = [slice(i * sub, (i + 1) * sub) for i in range(tm // sub)]

    @pl.when(t == 0)
    def _():
        carry_ref[...] = jnp.zeros_like(carry_ref)

    g1, sh2, sc2, g2 = mod_ref[2, 0], mod_ref[3, 0], mod_ref[4, 0], mod_ref[5, 0]
    mixes = [_dot(yr_ref[0, rb, :], w_out[:RWKV_WIDTH]) + _dot(ys_ref[0, rb, :], w_out[RWKV_WIDTH:])
             for rb in blocks]

    def cols_of(c, half):
        return slice(half * D_FF + c * FF_CHUNK, half * D_FF + (c + 1) * FF_CHUNK)

    def conv(u, cols):
        ext = jnp.concatenate([carry_ref[:, cols], u], axis=0)
        last = u[sub - SUBLANES:]
        carry_ref[:, cols] = last
        cp_ref[0, :, cols] = last
        return (cb_ref[:, cols] + pltpu.roll(ext, 2, 0)[SUBLANES:] * cw_ref[0:1, cols]
                + pltpu.roll(ext, 1, 0)[SUBLANES:] * cw_ref[1:2, cols] + u * cw_ref[2:3, cols])

    for rb, mix in zip(blocks, mixes):
        x1 = x_ref[0, rb, :] + g1 * _rmsnorm(mix, n_post_mix[...])
        h2 = (_rmsnorm(x1, n_pre_ffn[...]) * (1.0 + sc2) + sh2).astype(BF16)

        def up(c):
            return [jnp.dot(h2, w_up[:, cols_of(c, half)], preferred_element_type=F32)
                    for half in range(2)]

        def down(c):
            rows = slice(c * FF_CHUNK, (c + 1) * FF_CHUNK)
            return jnp.dot(act_ref[rb, rows], w_down[rows, :], preferred_element_type=F32)

        u_next = up(0)
        ff = None
        for c in range(N_FF_CHUNKS):
            u_cur = u_next
            if c + 1 < N_FF_CHUNKS:
                u_next = up(c + 1)
            if c >= 1:
                part = down(c - 1)
                ff = part if ff is None else ff + part
            za, zb = [conv(u_cur[half], cols_of(c, half)) for half in range(2)]
            act_ref[rb, c * FF_CHUNK:(c + 1) * FF_CHUNK] = (_silu(za) * zb).astype(BF16)
        ff = ff + down(N_FF_CHUNKS - 1)
        y_ref[0, rb, :] = x1 + g2 * _rmsnorm(ff, n_post_ffn[...])


def _tail_sample_kernel(x_ref, o_ref, feat_ref, ys_ref, mod_ref, p0_ref, p1_ref, ln_w, ln_b,
                        w_out, n_post_mix, n_pre_ffn, n_post_ffn, w_up, w_down,
                        cw_ref, cb_ref, y_ref, u_ref, act_ref):
    n = x_ref.shape[0]
    seg = _seg_ones(RWKV_WIDTH)
    y_rwkv = _gn_epilogue(o_ref[...].T, feat_ref[7], feat_ref[6], ln_w[...], ln_b[...], seg)
    g1, sh2, sc2, g2 = mod_ref[2], mod_ref[3], mod_ref[4], mod_ref[5]
    x1, h2 = _tail_mix(x_ref[...], y_rwkv, ys_ref[...], g1, sh2, sc2, w_out[...],
                       n_post_mix[...], n_pre_ffn[...])
    u_ref[...] = jnp.dot(h2, w_up[...], preferred_element_type=F32)

    def taps_fn(r0, cols):
        rows = pl.ds(r0, FF_ROWS)
        return p1_ref[rows, cols], p0_ref[rows, cols], u_ref[rows, cols]

    _ffn_gate(act_ref, cw_ref, cb_ref, taps_fn, n)
    ff = jnp.dot(act_ref[...], w_down[...], preferred_element_type=F32)
    y_ref[...] = x1 + g2 * _rmsnorm(ff, n_post_ffn[...])


def _tail_prompt_call(x, y_rwkv, y_swa, mod_p, consts, tm):
    b, t, _ = x.shape
    tok = lambda i, j: (i, j, 0)
    in_specs = [
        pl.BlockSpec((1, tm, D_MODEL), tok),
        pl.BlockSpec((1, tm, RWKV_WIDTH), tok),
        pl.BlockSpec((1, tm, SWA_WIDTH), tok),
        pl.BlockSpec((6, 1, 1, D_MODEL), lambda i, j: (0, i, 0, 0)),
    ] + [_const_spec(c, 2) for c in consts]
    return pl.pallas_call(
        _tail_prompt_kernel,
        grid=(b, t // tm),
        in_specs=in_specs,
        out_specs=(
            pl.BlockSpec((1, tm, D_MODEL), tok),
            pl.BlockSpec((1, SUBLANES, 2 * D_FF), lambda i, j: (i, 0, 0)),
        ),
        out_shape=(
            jax.ShapeDtypeStruct((b, t, D_MODEL), F32),
            jax.ShapeDtypeStruct((b, SUBLANES, 2 * D_FF), F32),
        ),
        scratch_shapes=[
            pltpu.VMEM((SUBLANES, 2 * D_FF), F32),
            pltpu.VMEM((tm, D_FF), BF16),
        ],
        compiler_params=pltpu.CompilerParams(
            dimension_semantics=("arbitrary", "arbitrary"), vmem_limit_bytes=VMEM_LIMIT),
        name="tail_prompt",
    )(x, y_rwkv, y_swa, mod_p, *consts)


def _tail_sample_call(x, o_s, feat_s, y_swa, mod_s, p0, p1, ln_w, ln_b, consts):
    n = x.shape[0]
    args = (x, o_s, feat_s, y_swa, mod_s, p0, p1, ln_w, ln_b) + tuple(consts)
    out_shape = (
        jax.ShapeDtypeStruct((n, D_MODEL), F32),
        jax.ShapeDtypeStruct((n, 2 * D_FF), F32),
    )
    return pl.pallas_call(
        _tail_sample_kernel,
        grid=(1,),
        in_specs=[_const_spec(a, 1) for a in args],
        out_specs=tuple(pl.BlockSpec(s.shape, lambda i, nd=len(s.shape): (0,) * nd)
                        for s in out_shape),
        out_shape=out_shape,
        scratch_shapes=[pltpu.VMEM((n, D_FF), BF16)],
        compiler_params=pltpu.CompilerParams(
            dimension_semantics=("arbitrary",), vmem_limit_bytes=VMEM_LIMIT),
        name="tail_sample",
    )(*args)


def _rope_tables(pos):
    half = HEAD_DIM // 2
    inv = ROPE_THETA ** (-jnp.arange(half, dtype=F32) / half)
    ang = pos.astype(F32)[:, None] * inv[None, :]
    cos, sin = jnp.cos(ang), jnp.sin(ang)
    cos_h = jnp.concatenate([cos, cos], axis=-1)
    sin_h = jnp.concatenate([-sin, sin], axis=-1)
    return jnp.tile(cos_h, (1, LANES // HEAD_DIM)), jnp.tile(sin_h, (1, LANES // HEAD_DIM))


def kernel(x_prompt, x_sample, state_rwkv_wkv, state_rwkv_shift, cache_swa_k, cache_swa_v,
           state_ffn_conv, c_prompt, c_sample, w_ada, b_ada, norm_pre_mix, norm_post_mix,
           norm_pre_ffn, norm_post_ffn, w_in, rwkv_mu, rwkv_w0, rwkv_w_up, rwkv_a0, rwkv_a_up,
           rwkv_g_up, rwkv_k_k, rwkv_k_a, rwkv_r_k, rwkv_ln_w, rwkv_ln_b, swa_sinks, w_out,
           ffn_w_up, ffn_conv_w, ffn_conv_b, ffn_w_down):
    depth = w_ada.shape[0]
    assert depth == 1 and x_sample.shape[1] == 1
    b, t, _ = x_prompt.shape
    n = x_sample.shape[0]
    l = cache_swa_k.shape[2]
    tm = min(256, t)
    assert t % tm == 0 and tm % WINDOW == 0 and tm % CHUNK == 0
    assert tm % FF_ROWS == 0 and n % FF_ROWS == 0
    nb_swa = 8 if n % 8 == 0 else n
    li = 0

    row = lambda v: v.reshape(1, -1)
    w_in_b = w_in[li].astype(BF16)
    zeros_l = jnp.zeros((64, RWKV_WIDTH), F32)
    wa_up = jnp.concatenate([
        jnp.concatenate([rwkv_w_up[li], zeros_l], axis=1),
        jnp.concatenate([zeros_l, rwkv_a_up[li]], axis=1)], axis=0).astype(BF16)
    hid = jnp.arange(GROUP_LANES) // HEAD_DIM
    seg_blk = (hid[:, None] == hid[None, :]).astype(BF16)
    proj_consts = (row(norm_pre_mix[li]), w_in_b, row(rwkv_mu[li]), row(rwkv_w0[li]), wa_up,
                   row(rwkv_a0[li]), rwkv_g_up[li].astype(BF16), row(rwkv_k_k[li]),
                   row(rwkv_k_a[li]), row(rwkv_r_k[li]), seg_blk)
    tail_consts = (w_out[li].astype(BF16), row(norm_post_mix[li]), row(norm_pre_ffn[li]),
                   row(norm_post_ffn[li]), ffn_w_up[li].astype(BF16), ffn_w_down[li].astype(BF16),
                   ffn_conv_w[li], row(ffn_conv_b[li]))
    ln_w, ln_b = row(rwkv_ln_w[li]), row(rwkv_ln_b[li])

    mod = _ada_call(jnp.concatenate([c_prompt, c_sample], axis=0), w_ada[li], row(b_ada[li]))
    mod_p = mod[:, :b].reshape(6, b, 1, D_MODEL)
    mod_s = mod[:, b:]

    cos_p, sin_p = _rope_tables(jnp.arange(t, dtype=jnp.int32))
    y_rwkv_p, s_cat, q_p, kx_p, vx_p, klast, vlast, plast = _mix_prompt_call(
        x_prompt, mod_p, cos_p, sin_p, proj_consts + (ln_w, ln_b), min(2 * tm, t))
    y_swa_p = _swa_prompt_call(swa_sinks[li], q_p, kx_p, vx_p, min(2 * tm, t))
    y_p, cp = _tail_prompt_call(x_prompt, y_rwkv_p, y_swa_p, mod_p, tail_consts, min(2 * tm, t))

    wkv_p = s_cat.reshape(b, HEAD_DIM, RWKV_HEADS, HEAD_DIM).transpose(0, 2, 1, 3)
    shift_p = plast[:, SUBLANES - 1]
    k_p = klast.reshape(b, WINDOW, SWA_KV_HEADS, HEAD_DIM)
    v_p = vlast.reshape(b, WINDOW, SWA_KV_HEADS, HEAD_DIM)
    conv_p = cp[:, SUBLANES - 2:]

    cos_s, sin_s = _rope_tables(jnp.full((1,), PAST_LEN, jnp.int32))
    feat_s, ft_s, q_s, kn_s, vn_s, p_s = _proj_sample_call(
        x_sample[:, 0], mod_s, cos_s, sin_s, state_rwkv_shift[li], proj_consts)
    state_t = jnp.transpose(state_rwkv_wkv[li], (1, 2, 3, 0))
    wkv_t, o_t = _wkv_sample_call(ft_s, state_t)
    wkv_s = jnp.transpose(wkv_t, (3, 0, 1, 2))

    g2 = SWA_KV_HEADS
    q4 = q_s.reshape(n * g2, SWA_GROUP, HEAD_DIM)
    q8 = jnp.concatenate([q4, jnp.zeros_like(q4)], axis=1)
    sink_t = jnp.broadcast_to(
        jnp.concatenate([swa_sinks[li].reshape(g2, SWA_GROUP),
                         jnp.full((g2, SWA_GROUP), MASK_VALUE, F32)], axis=1)[None, :, :, None],
        (nb_swa, g2, SUBLANES, LANES)).reshape(nb_swa * g2, SUBLANES, LANES)
    cols = lambda a: a.reshape(n // nb_swa, nb_swa, KV_WIDTH).transpose(0, 2, 1)
    ck = jnp.transpose(cache_swa_k[li], (0, 2, 3, 1))
    cv = jnp.transpose(cache_swa_v[li], (0, 2, 3, 1))
    o_att, k_t, v_t = _swa_sample_call(
        sink_t, q8, kn_s.reshape(n * g2, 1, HEAD_DIM), vn_s.reshape(n * g2, 1, HEAD_DIM),
        cols(kn_s), cols(vn_s), ck, cv, nb_swa)
    y_swa_s = o_att[:, :SWA_GROUP].reshape(n, SWA_WIDTH)
    k_s = jnp.transpose(k_t, (0, 3, 1, 2))
    v_s = jnp.transpose(v_t, (0, 3, 1, 2))

    conv0 = state_ffn_conv[li]
    y_s, u_s = _tail_sample_call(x_sample[:, 0], o_t, feat_s, y_swa_s.astype(BF16), mod_s,
                                 conv0[:, 0], conv0[:, 1], ln_w, ln_b, tail_consts)
    conv_s = jnp.stack([conv0[:, 1], u_s], axis=1)

    expand = lambda a: a[None]
    return (y_p, y_s[:, None, :], expand(wkv_p), expand(shift_p), expand(k_p), expand(v_p),
            expand(conv_p), expand(wkv_s), expand(p_s),
            expand(k_s), expand(v_s), expand(conv_s))
```

```python
import math

import jax
import jax.numpy as jnp
from jax import lax
from jax.experimental import pallas as pl
from jax.experimental.pallas import tpu as pltpu

D_MODEL = 1024
HEAD_DIM = 64
RWKV_WIDTH = 512
RWKV_HEADS = 8
RWKV_COLS = 1792
RWKV_GN_EPS = 64e-5
SWA_WIDTH = 512
SWA_HEADS = 8
SWA_KV_HEADS = 2
SWA_GROUP = 4
KV_WIDTH = SWA_KV_HEADS * HEAD_DIM
WINDOW = 128
PAST_LEN = 16384
ROPE_THETA = 10000.0
ATTN_SCALE = HEAD_DIM ** -0.5
D_FF = 2816
NORM_EPS = 1e-6
MASK_VALUE = -1e30
PROJ_COLS = RWKV_COLS + SWA_WIDTH + 2 * KV_WIDTH

LANES = 128
SUBLANES = 8
CHUNK = 64
GROUP_LANES = 256
FF_CHUNK = 256
N_FF_CHUNKS = D_FF // FF_CHUNK
FF_ROWS = 32
TAIL_SUB = 256
MIX_SUB = 512
TAIL_GROUP = 512
VMEM_LIMIT = 56 * 1024 * 1024

F32 = jnp.float32
BF16 = jnp.bfloat16


def _sigmoid(x):
    return 1.0 / (1.0 + jnp.exp(-x))


def _silu(x):
    return x * _sigmoid(x)


def _rmsnorm(x, g):
    return x * lax.rsqrt(jnp.mean(x * x, axis=-1, keepdims=True) + NORM_EPS) * g


def _dot(a, b):
    return jnp.dot(a.astype(BF16), b.astype(BF16), preferred_element_type=F32)


def _dot_nt(a, b):
    return lax.dot_general(a.astype(BF16), b.astype(BF16), (((1,), (1,)), ((), ())),
                           preferred_element_type=F32)


def _dot_tn(a, b):
    return lax.dot_general(a.astype(BF16), b.astype(BF16), (((0,), (0,)), ((), ())),
                           preferred_element_type=F32)


def _every(fn, *lists):
    return [fn(*a) for a in zip(*lists)]


def _swap_halves(x):
    w = x.shape[-1]
    lane = lax.broadcasted_iota(jnp.int32, x.shape, x.ndim - 1)
    lo = (lane & (HEAD_DIM // 2)) == 0
    return jnp.where(lo, pltpu.roll(x, w - HEAD_DIM // 2, x.ndim - 1),
                     pltpu.roll(x, HEAD_DIM // 2, x.ndim - 1))


def _rope(x, cos, sin):
    reps = x.shape[-1] // LANES
    cos_w = jnp.concatenate([cos] * reps, axis=-1) if reps > 1 else cos
    sin_w = jnp.concatenate([sin] * reps, axis=-1) if reps > 1 else sin
    return x * cos_w + _swap_halves(x) * sin_w


def _proj_features(x, shift, scale, g_pre, w_in, prev_fn, mu, w0, wa_up, a0, g_up,
                   k_k, k_a, r_k, seg_blk, cos, sin):
    h = _rmsnorm(x, g_pre) * (1.0 + scale) + shift
    p = _dot(h, w_in)
    p_rwkv = p[:, :RWKV_COLS]
    prev = prev_fn(p_rwkv)
    xm = p_rwkv + (prev - p_rwkv) * mu
    r = xm[:, 0:512]
    k = xm[:, 512:1024]
    v = xm[:, 1024:1536]
    wa = xm[:, 1536:1664]
    gd = xm[:, 1664:1792]
    lane = lax.broadcasted_iota(jnp.int32, wa.shape, 1)
    wa_act = jnp.where(lane < 64, jnp.tanh(wa), wa)
    lora = _dot(wa_act, wa_up)
    lw = -math.exp(-0.5) * _sigmoid(w0 + lora[:, :512])
    a = _sigmoid(a0 + lora[:, 512:])
    g = _dot(_sigmoid(gd), g_up)
    kk = k * k_k
    kf = k * (1.0 + (a - 1.0) * k_a)
    gl = seg_blk.shape[0]
    n = x.shape[0]
    sums = [_dot(jnp.concatenate([(kk * kk)[:, i:i + gl], (r * kf * r_k)[:, i:i + gl]], axis=0),
                 seg_blk) for i in range(0, RWKV_WIDTH, gl)]
    ss = jnp.concatenate([s_[:n] for s_ in sums], axis=1)
    kk = kk / jnp.maximum(jnp.sqrt(ss), 1e-12)
    bonus = jnp.concatenate([s_[n:] for s_ in sums], axis=1) * v
    feats = (r, lw, kf, v, -kk, kk * a, g, bonus)
    q = _rope(p[:, RWKV_COLS:RWKV_COLS + SWA_WIDTH], cos, sin)
    ks = _rope(p[:, RWKV_COLS + SWA_WIDTH:RWKV_COLS + SWA_WIDTH + KV_WIDTH], cos, sin)
    vs = p[:, RWKV_COLS + SWA_WIDTH + KV_WIDTH:]
    return feats, q, ks, vs, p_rwkv


def _gn_epilogue(o, bonus, g, ln_w, ln_b, seg):
    mu = _dot(o, seg) * (1.0 / HEAD_DIM)
    d = o - mu
    var = _dot(d * d, seg) * (1.0 / HEAD_DIM)
    gn = d * lax.rsqrt(var + RWKV_GN_EPS) * ln_w + ln_b
    return (gn + bonus) * g


def _seg_ones(n):
    r = lax.broadcasted_iota(jnp.int32, (n, n), 0) // HEAD_DIM
    c = lax.broadcasted_iota(jnp.int32, (n, n), 1) // HEAD_DIM
    return (r == c).astype(BF16)


def _ada_kernel(c_ref, w_ref, b_ref, o_ref):
    o_ref[0] = _dot(_silu(c_ref[...]), w_ref[...]) + b_ref[...]


def _ada_call(c_all, w_ada, b_ada):
    rows = c_all.shape[0]
    return pl.pallas_call(
        _ada_kernel,
        grid=(6,),
        in_specs=[
            pl.BlockSpec((rows, D_MODEL), lambda j: (0, 0)),
            pl.BlockSpec((D_MODEL, D_MODEL), lambda j: (0, j)),
            pl.BlockSpec((1, D_MODEL), lambda j: (0, j)),
        ],
        out_specs=pl.BlockSpec((1, rows, D_MODEL), lambda j: (j, 0, 0)),
        out_shape=jax.ShapeDtypeStruct((6, rows, D_MODEL), F32),
        compiler_params=pltpu.CompilerParams(
            dimension_semantics=("arbitrary",), vmem_limit_bytes=VMEM_LIMIT),
        name="ada",
    )(c_all, w_ada, b_ada)


def _expand_kv(x):
    lane = lax.broadcasted_iota(jnp.int32, x.shape, 1)
    rolled = pltpu.roll(x, HEAD_DIM, 1)
    g0 = jnp.where(lane < HEAD_DIM, x, rolled)
    g1 = jnp.where(lane < HEAD_DIM, rolled, x)
    return jnp.concatenate([g0, g0, g1, g1], axis=1)


def _mix_prompt_kernel(x_ref, mod_ref, cos_ref, sin_ref, g_pre, w_in, mu, w0, wa_up, a0,
                       g_up, k_k, k_a, r_k, seg_blk, ln_w, ln_b,
                       y_ref, s_out_ref, q_ref, k_ref, v_ref, klast_ref, vlast_ref, plast_ref,
                       carry_ref, s_ref):
    t = pl.program_id(1)

    @pl.when(t == 0)
    def _():
        carry_ref[...] = jnp.zeros_like(carry_ref)
        s_ref[...] = jnp.zeros_like(s_ref)

    sub = min(x_ref.shape[1], MIX_SUB)

    def sub_tile(i, carry):
        rows = pl.ds(pl.multiple_of(i * sub, sub), sub)
        carry_row = carry_ref[SUBLANES - 1:SUBLANES, :]

        def prev_fn(p_rwkv):
            row = lax.broadcasted_iota(jnp.int32, p_rwkv.shape, 0)
            return jnp.where(row == 0, carry_row, pltpu.roll(p_rwkv, 1, 0))

        feats, q, ks, vs, p_rwkv = _proj_features(
            x_ref[0, rows, :], mod_ref[0, 0], mod_ref[1, 0], g_pre[...], w_in[...], prev_fn,
            mu[...], w0[...], wa_up[...], a0[...], g_up[...], k_k[...], k_a[...], r_k[...],
            seg_blk[...], cos_ref[rows, :], sin_ref[rows, :])
        q_ref[0, rows, :] = (q * ATTN_SCALE).astype(BF16)
        k_ref[0, rows, :] = _expand_kv(ks).astype(BF16)
        v_ref[0, rows, :] = _expand_kv(vs).astype(BF16)
        last = p_rwkv[sub - SUBLANES:, :]
        carry_ref[...] = last
        plast_ref[0] = last
        klast_ref[0] = ks[sub - WINDOW:, :]
        vlast_ref[0] = vs[sub - WINDOW:, :]
        _wkv_tile(feats, ln_w, ln_b, y_ref.at[0, rows, :], s_ref)
        return carry

    lax.fori_loop(0, x_ref.shape[1] // sub, sub_tile, 0)

    @pl.when(t == pl.num_programs(1) - 1)
    def _():
        for g in range(RWKV_WIDTH // GROUP_LANES):
            s_out_ref[0, :, g * GROUP_LANES:(g + 1) * GROUP_LANES] = s_ref[g]


def _proj_sample_kernel(x_ref, mod_ref, cos_ref, sin_ref, prev_ref, g_pre, w_in, mu, w0,
                        wa_up, a0, g_up, k_k, k_a, r_k, seg_blk,
                        feat_ref, ft_ref, q_ref, k_ref, v_ref, p_ref):
    feats, q, ks, vs, p_rwkv = _proj_features(
        x_ref[...], mod_ref[0], mod_ref[1], g_pre[...], w_in[...], lambda p: prev_ref[...],
        mu[...], w0[...], wa_up[...], a0[...], g_up[...], k_k[...], k_a[...], r_k[...],
        seg_blk[...], cos_ref[...], sin_ref[...])
    for i, f in enumerate(feats):
        feat_ref[i] = f
    for i in range(6):
        ft_ref[i] = feats[i].T
    q_ref[...] = (q * ATTN_SCALE).astype(BF16)
    k_ref[...] = ks
    v_ref[...] = vs
    p_ref[...] = p_rwkv


def _const_spec(arr, grid_rank):
    zeros = (0,) * arr.ndim
    if grid_rank == 1:
        return pl.BlockSpec(arr.shape, lambda i: zeros)
    return pl.BlockSpec(arr.shape, lambda i, j: zeros)


def _mix_prompt_call(x, mod_p, cos, sin, consts, tm):
    b, t, _ = x.shape
    nt = t // tm
    in_specs = [
        pl.BlockSpec((1, tm, D_MODEL), lambda i, j: (i, j, 0)),
        pl.BlockSpec((6, 1, 1, D_MODEL), lambda i, j: (0, i, 0, 0)),
        pl.BlockSpec((tm, LANES), lambda i, j: (j, 0)),
        pl.BlockSpec((tm, LANES), lambda i, j: (j, 0)),
    ] + [_const_spec(c, 2) for c in consts]
    out_shape = (
        jax.ShapeDtypeStruct((b, t, RWKV_WIDTH), BF16),
        jax.ShapeDtypeStruct((b, HEAD_DIM, RWKV_WIDTH), F32),
        jax.ShapeDtypeStruct((b, t, SWA_WIDTH), BF16),
        jax.ShapeDtypeStruct((b, t, SWA_WIDTH), BF16),
        jax.ShapeDtypeStruct((b, t, SWA_WIDTH), BF16),
        jax.ShapeDtypeStruct((b, WINDOW, KV_WIDTH), F32),
        jax.ShapeDtypeStruct((b, WINDOW, KV_WIDTH), F32),
        jax.ShapeDtypeStruct((b, SUBLANES, RWKV_COLS), F32),
    )
    out_specs = (
        pl.BlockSpec((1, tm, RWKV_WIDTH), lambda i, j: (i, j, 0)),
        pl.BlockSpec((1, HEAD_DIM, RWKV_WIDTH), lambda i, j: (i, 0, 0)),
        pl.BlockSpec((1, tm, SWA_WIDTH), lambda i, j: (i, j, 0)),
        pl.BlockSpec((1, tm, SWA_WIDTH), lambda i, j: (i, j, 0)),
        pl.BlockSpec((1, tm, SWA_WIDTH), lambda i, j: (i, j, 0)),
        pl.BlockSpec((1, WINDOW, KV_WIDTH), lambda i, j: (i, 0, 0)),
        pl.BlockSpec((1, WINDOW, KV_WIDTH), lambda i, j: (i, 0, 0)),
        pl.BlockSpec((1, SUBLANES, RWKV_COLS), lambda i, j: (i, 0, 0)),
    )
    return pl.pallas_call(
        _mix_prompt_kernel,
        grid=(b, nt),
        in_specs=in_specs,
        out_specs=out_specs,
        out_shape=out_shape,
        scratch_shapes=[
            pltpu.VMEM((SUBLANES, RWKV_COLS), F32),
            pltpu.VMEM((RWKV_WIDTH // GROUP_LANES, HEAD_DIM, GROUP_LANES), F32),
        ],
        compiler_params=pltpu.CompilerParams(
            dimension_semantics=("arbitrary", "arbitrary"), vmem_limit_bytes=VMEM_LIMIT),
        name="mix_prompt",
    )(x, mod_p, cos, sin, *consts)


def _proj_sample_call(x, mod_s, cos, sin, prev, consts):
    n = x.shape[0]
    args = (x, mod_s, cos, sin, prev) + tuple(consts)
    out_shape = (
        jax.ShapeDtypeStruct((8, n, RWKV_WIDTH), F32),
        jax.ShapeDtypeStruct((6, RWKV_WIDTH, n), F32),
        jax.ShapeDtypeStruct((n, SWA_WIDTH), BF16),
        jax.ShapeDtypeStruct((n, KV_WIDTH), F32),
        jax.ShapeDtypeStruct((n, KV_WIDTH), F32),
        jax.ShapeDtypeStruct((n, RWKV_COLS), F32),
    )
    return pl.pallas_call(
        _proj_sample_kernel,
        grid=(1,),
        in_specs=[_const_spec(a, 1) for a in args],
        out_specs=tuple(pl.BlockSpec(s.shape, lambda i, nd=len(s.shape): (0,) * nd)
                        for s in out_shape),
        out_shape=out_shape,
        compiler_params=pltpu.CompilerParams(
            dimension_semantics=("arbitrary",), vmem_limit_bytes=VMEM_LIMIT),
        name="proj_sample",
    )(*args)


def _wkv_tile(feats, ln_w, ln_b, y_ref, s_ref):
    tt = feats[0].shape[0]
    n_chunks = tt // CHUNK
    gl = GROUP_LANES
    n_groups = RWKV_WIDTH // gl
    heads_per_group = gl // HEAD_DIM
    probs = [(c, g) for c in range(n_chunks) for g in range(n_groups)]

    row_c = lax.broadcasted_iota(jnp.int32, (CHUNK, gl), 0)
    col_c = lax.broadcasted_iota(jnp.int32, (CHUNK, gl), 1) % CHUNK
    strict = row_c > col_c
    incl = row_c >= col_c
    eye_cat = (row_c == col_c).astype(F32)
    rb = lax.broadcasted_iota(jnp.int32, (gl, gl), 0) // HEAD_DIM
    cb = lax.broadcasted_iota(jnp.int32, (gl, gl), 1) // HEAD_DIM
    bd_mask = rb == cb
    tri_r = lax.broadcasted_iota(jnp.int32, (CHUNK, CHUNK), 0)
    tri_c = lax.broadcasted_iota(jnp.int32, (CHUNK, CHUNK), 1)
    tril_ones = (tri_r >= tri_c).astype(BF16)
    seg = bd_mask.astype(BF16)

    def bd(x):
        xb = x.astype(BF16)
        return jnp.where(bd_mask, jnp.concatenate([xb] * heads_per_group, axis=0),
                         jnp.zeros((), BF16))

    def fold(x):
        xm = jnp.where(bd_mask, x, 0.0)
        acc = xm[0:HEAD_DIM]
        for hh in range(1, heads_per_group):
            acc = acc + xm[hh * HEAD_DIM:(hh + 1) * HEAD_DIM]
        return acc

    def ld(i, p):
        c, g = p
        return feats[i][c * CHUNK:(c + 1) * CHUNK, g * gl:(g + 1) * gl]

    lw = [ld(1, p) for p in probs]
    na = [ld(4, p) for p in probs]
    bb = [ld(5, p) for p in probs]
    kf = [ld(2, p) for p in probs]
    r = [ld(0, p) for p in probs]
    v = [ld(3, p) for p in probs]

    def cumsum(x):
        hi = x.astype(BF16)
        lo = (x - hi.astype(F32)).astype(BF16)
        both = jnp.dot(tril_ones, jnp.concatenate([hi, lo], axis=1), preferred_element_type=F32)
        return both[:, :gl] + both[:, gl:]

    cum = _every(cumsum, lw)
    cum_last = [x[CHUNK - 1:CHUNK, :] for x in cum]
    e_out = [jnp.exp(-x) for x in cum]
    e_end = _every(lambda cl, x: jnp.exp(cl - x), cum_last, cum)
    a_t = _every(lambda n_, x, l_: n_ * jnp.exp(x - l_), na, cum, lw)
    r_t = _every(lambda r_, x: r_ * jnp.exp(x), r, cum)
    b_t = _every(lambda b_, e: b_ * e, bb, e_out)
    k_t = _every(lambda k_, e: k_ * e, kf, e_out)
    b_end = _every(lambda b_, e: b_ * e, bb, e_end)
    k_end = _every(lambda k_, e: k_ * e, kf, e_end)
    gamma = [jnp.exp(x) for x in cum_last]

    ar = _every(lambda a_, r_: jnp.concatenate([a_, r_], axis=0), a_t, r_t)
    pb = _every(lambda x, y: _dot_nt(x, bd(y)), ar, b_t)
    pk = _every(lambda x, y: _dot_nt(x, bd(y)), ar, k_t)
    l_ab = [jnp.where(strict, x[:CHUNK], 0.0) for x in pb]
    l_ak = [jnp.where(strict, x[:CHUNK], 0.0) for x in pk]
    m_rb = [jnp.where(incl, x[CHUNK:], 0.0) for x in pb]
    m_rk = [jnp.where(incl, x[CHUNK:], 0.0) for x in pk]

    x_acc = [eye_cat + l for l in l_ab]
    pw = _every(lambda l: _dot(l, bd(l)), l_ab)
    n_sq = int(math.log2(CHUNK)) - 1
    for lvl in range(n_sq):
        rhs = [bd(p_) for p_ in pw]
        if lvl < n_sq - 1:
            both = _every(lambda x, p_, w_: _dot(jnp.concatenate([x, p_], axis=0), w_),
                          x_acc, pw, rhs)
            x_acc = _every(lambda x, b_: x + b_[:CHUNK], x_acc, both)
            pw = [b_[CHUNK:] for b_ in both]
        else:
            x_acc = _every(lambda x, w_: x + _dot(x, w_), x_acc, rhs)
    t_inv = x_acc

    kv = _every(lambda la, mk, x: _dot(jnp.concatenate([la, mk], axis=0), bd(x)), l_ak, m_rk, v)
    y_loc = [x[:CHUNK] for x in kv]
    wu = _every(lambda t_, a_, y_: _dot(t_, jnp.concatenate([bd(a_), bd(y_)], axis=1)),
                t_inv, a_t, y_loc)
    w_t = [x[:, :gl] for x in wu]
    u_loc = [x[:, gl:] for x in wu]

    mwu = _every(lambda mb, w_, u_: _dot(mb, jnp.concatenate([bd(w_), bd(u_)], axis=1)),
                 m_rb, w_t, u_loc)
    q_c = _every(lambda r_, x: r_ + x[:, :gl], r_t, mwu)
    o_loc = _every(lambda x, y_: x[:, gl:] + y_[CHUNK:], mwu, kv)
    m_low = _every(lambda w_, b_: jnp.where(bd_mask, _dot_tn(w_, b_), 0.0).astype(BF16),
                   w_t, b_end)
    n_loc = _every(lambda u_, v_, b_, k_: fold(_dot_tn(jnp.concatenate([u_, v_], axis=0),
                                                       jnp.concatenate([b_, k_], axis=0))),
                   u_loc, v, b_end, k_end)

    state = [s_ref[g] for g in range(n_groups)]
    starts = []
    for c in range(n_chunks):
        starts.append(state)
        idx = [c * n_groups + g for g in range(n_groups)]
        state = [state[g] * gamma[i] + _dot(state[g], m_low[i]) + n_loc[i]
                 for g, i in enumerate(idx)]
    for g in range(n_groups):
        s_ref[g] = state[g]

    s0 = [starts[c][g] for (c, g) in probs]
    o = _every(lambda q_, s_, ol: _dot_nt(q_, bd(s_)) + ol, q_c, s0, o_loc)

    n_p = len(probs)
    unstack = lambda x: [x[i * CHUNK:(i + 1) * CHUNK] for i in range(n_p)]
    mu = unstack(_dot(jnp.concatenate(o, axis=0), seg) * (1.0 / HEAD_DIM))
    d = _every(lambda x, m_: x - m_, o, mu)
    var = unstack(_dot(jnp.concatenate([x * x for x in d], axis=0), seg) * (1.0 / HEAD_DIM))
    for i, (c, g) in enumerate(probs):
        ls = slice(g * gl, (g + 1) * gl)
        gn = d[i] * lax.rsqrt(var[i] + RWKV_GN_EPS) * ln_w[:, ls] + ln_b[:, ls]
        y = (gn + ld(7, (c, g))) * ld(6, (c, g))
        y_ref[c * CHUNK:(c + 1) * CHUNK, ls] = y.astype(y_ref.dtype)


def _wkv_sample_kernel(ft_ref, s_ref, s_out_ref, o_ref):
    hd = HEAD_DIM
    r, kf, na, bb = ft_ref[0], ft_ref[2], ft_ref[4], ft_ref[5]
    w = jnp.exp(ft_ref[1])

    def value_block(vb, carry):
        v0 = pl.multiple_of(vb * SUBLANES, SUBLANES)
        v_rows = ft_ref[3, pl.ds(v0, SUBLANES), :]
        outs = []
        for j in range(SUBLANES):
            s = s_ref[0, v0 + j]
            sa = jnp.sum(s * na, axis=0, keepdims=True)
            s_new = s * w + sa * bb + v_rows[j:j + 1] * kf
            s_out_ref[0, v0 + j] = s_new
            outs.append(jnp.sum(s_new * r, axis=0, keepdims=True))
        o_ref[pl.ds(v0, SUBLANES), :] = jnp.concatenate(outs, axis=0)
        return carry

    lax.fori_loop(0, hd // SUBLANES, value_block, 0)


def _wkv_sample_call(ft_s, state_t):
    h, hd, _, n = state_t.shape
    return pl.pallas_call(
        _wkv_sample_kernel,
        grid=(h,),
        in_specs=[
            pl.BlockSpec((6, hd, n), lambda i: (0, i, 0)),
            pl.BlockSpec((1, hd, hd, n), lambda i: (i, 0, 0, 0)),
        ],
        out_specs=(
            pl.BlockSpec((1, hd, hd, n), lambda i: (i, 0, 0, 0)),
            pl.BlockSpec((hd, n), lambda i: (i, 0)),
        ),
        out_shape=(
            jax.ShapeDtypeStruct(state_t.shape, F32),
            jax.ShapeDtypeStruct((h * hd, n), F32),
        ),
        compiler_params=pltpu.CompilerParams(
            dimension_semantics=("arbitrary",), vmem_limit_bytes=VMEM_LIMIT),
        name="wkv_sample",
    )(ft_s, state_t)


def _swa_prompt_kernel(sink_ref, q_ref, kp_ref, kc_ref, vp_ref, vc_ref, o_ref):
    j = pl.program_id(1)
    w = WINDOW
    tq = q_ref.shape[1]
    n_blk = tq // w
    gl = GROUP_LANES
    rows = SWA_GROUP * w
    probs = [(qi, g) for qi in range(n_blk) for g in range(SWA_KV_HEADS)]

    lane_head = lax.broadcasted_iota(jnp.int32, (w, gl), 1) // HEAD_DIM
    head_mask = [lane_head == h for h in range(SWA_GROUP)]
    qi_ = lax.broadcasted_iota(jnp.int32, (rows, 2 * w), 0) % w
    ki_ = lax.broadcasted_iota(jnp.int32, (rows, 2 * w), 1)
    diff = qi_ - (ki_ - w)
    in_window = (diff >= 0) & (diff < WINDOW)
    first_valid = in_window & ((j * tq + ki_ - w) >= 0)

    def keys(ref_prev, ref_cur, qi, g):
        ls = slice(g * gl, (g + 1) * gl)
        prev = ref_prev[0, :, ls] if qi == 0 else ref_cur[0, (qi - 1) * w:qi * w, ls]
        return jnp.concatenate([prev, ref_cur[0, qi * w:(qi + 1) * w, ls]], axis=0)

    def lhs(qi, g):
        qg = q_ref[0, qi * w:(qi + 1) * w, g * gl:(g + 1) * gl]
        return jnp.concatenate([jnp.where(m, qg, jnp.zeros((), BF16)) for m in head_mask], axis=0)

    sinks = []
    for g in range(SWA_KV_HEADS):
        sinks.append(jnp.concatenate(
            [jnp.full((w, 1), sink_ref[g * SWA_GROUP + h], F32) for h in range(SWA_GROUP)], axis=0))

    s = [_dot_nt(lhs(qi, g), keys(kp_ref, kc_ref, qi, g)) for qi, g in probs]
    s = [jnp.where(first_valid if qi == 0 else in_window, x, MASK_VALUE)
         for x, (qi, g) in zip(s, probs)]
    sink = [sinks[g] for qi, g in probs]
    m = _every(lambda x, sk: jnp.maximum(jnp.max(x, axis=-1, keepdims=True), sk), s, sink)
    e = _every(lambda x, m_: jnp.exp(x - m_), s, m)
    inv = _every(lambda e_, sk, m_: 1.0 / (jnp.sum(e_, axis=-1, keepdims=True) + jnp.exp(sk - m_)),
                 e, sink, m)
    prob = _every(lambda e_, i_: (e_ * i_).astype(BF16), e, inv)
    og = [_dot(p_, keys(vp_ref, vc_ref, qi, g)) for p_, (qi, g) in zip(prob, probs)]
    for x, (qi, g) in zip(og, probs):
        y = jnp.where(head_mask[0], x[0:w], 0.0)
        for h in range(1, SWA_GROUP):
            y = y + jnp.where(head_mask[h], x[h * w:(h + 1) * w], 0.0)
        o_ref[0, qi * w:(qi + 1) * w, g * gl:(g + 1) * gl] = y.astype(o_ref.dtype)


def _swa_prompt_call(sinks, q, k, v, tq):
    b, t, _ = q.shape
    w = WINDOW
    per = tq // w
    prev = lambda i, j: (i, jnp.maximum(j * per - 1, 0), 0)
    cur = lambda i, j: (i, j, 0)
    return pl.pallas_call(
        _swa_prompt_kernel,
        grid=(b, t // tq),
        in_specs=[
            pl.BlockSpec(memory_space=pltpu.SMEM),
            pl.BlockSpec((1, tq, SWA_WIDTH), cur),
            pl.BlockSpec((1, w, SWA_WIDTH), prev),
            pl.BlockSpec((1, tq, SWA_WIDTH), cur),
            pl.BlockSpec((1, w, SWA_WIDTH), prev),
            pl.BlockSpec((1, tq, SWA_WIDTH), cur),
        ],
        out_specs=pl.BlockSpec((1, tq, SWA_WIDTH), cur),
        out_shape=jax.ShapeDtypeStruct((b, t, SWA_WIDTH), BF16),
        compiler_params=pltpu.CompilerParams(
            dimension_semantics=("arbitrary", "arbitrary"), vmem_limit_bytes=VMEM_LIMIT),
        name="swa_prompt",
    )(sinks, q, k, k, v, v)


def _swa_sample_kernel(sink_ref, q_ref, knr_ref, vnr_ref, knt_ref, vnt_ref, ck_ref, cv_ref,
                       o_ref, ko_ref, vo_ref):
    nb, _, hd, l = ck_ref.shape
    q = q_ref[...]
    ck = ck_ref[...].reshape(nb * SWA_KV_HEADS, hd, l)
    cv = cv_ref[...].reshape(nb * SWA_KV_HEADS, hd, l)
    knr = knr_ref[...].astype(BF16).astype(F32)
    vnr = vnr_ref[...].astype(BF16).astype(F32)
    ki = lax.broadcasted_iota(jnp.int32, (1, 1, l), 2)
    kpos = PAST_LEN - l + ki
    diff = PAST_LEN - kpos
    valid = (diff >= 0) & (diff < WINDOW) & (kpos >= 0)
    s_c = lax.dot_general(q, ck.astype(BF16), (((2,), (1,)), ((0,), (0,))),
                          preferred_element_type=F32)
    s_c = jnp.where(valid, s_c, MASK_VALUE)
    s_n = jnp.sum(q.astype(F32) * knr, axis=-1, keepdims=True)
    sink = sink_ref[:, :, 0:1]
    m = jnp.maximum(jnp.maximum(jnp.max(s_c, axis=-1, keepdims=True), s_n), sink)
    e_c = jnp.exp(s_c - m)
    e_n = jnp.exp(s_n - m)
    denom = jnp.sum(e_c, axis=-1, keepdims=True) + e_n + jnp.exp(sink - m)
    p_c = (e_c / denom).astype(BF16)
    p_n = (e_n / denom).astype(BF16).astype(F32)
    o = lax.dot_general(p_c, cv.astype(BF16), (((2,), (2,)), ((0,), (0,))),
                        preferred_element_type=F32)
    o_ref[...] = o + p_n * vnr
    lane = lax.broadcasted_iota(jnp.int32, (hd, l), 1)
    for b in range(nb):
        for g in range(SWA_KV_HEADS):
            rows = slice(g * hd, (g + 1) * hd)
            ko_ref[b, g] = jnp.where(lane == l - 1, knt_ref[0, rows, b:b + 1],
                                     pltpu.roll(ck_ref[b, g], l - 1, 1))
            vo_ref[b, g] = jnp.where(lane == l - 1, vnt_ref[0, rows, b:b + 1],
                                     pltpu.roll(cv_ref[b, g], l - 1, 1))


def _swa_sample_call(sink_t, q8, knr, vnr, knt, vnt, ck, cv, nb):
    n, g, hd, l = ck.shape
    rows = nb * g
    blk3 = lambda i: (i, 0, 0)
    blk4 = lambda i: (i, 0, 0, 0)
    return pl.pallas_call(
        _swa_sample_kernel,
        grid=(n // nb,),
        in_specs=[
            pl.BlockSpec((rows, SUBLANES, LANES), lambda i: (0, 0, 0)),
            pl.BlockSpec((rows, SUBLANES, hd), blk3),
            pl.BlockSpec((rows, 1, hd), blk3),
            pl.BlockSpec((rows, 1, hd), blk3),
            pl.BlockSpec((1, g * hd, nb), blk3),
            pl.BlockSpec((1, g * hd, nb), blk3),
            pl.BlockSpec((nb, g, hd, l), blk4),
            pl.BlockSpec((nb, g, hd, l), blk4),
        ],
        out_specs=(
            pl.BlockSpec((rows, SUBLANES, hd), blk3),
            pl.BlockSpec((nb, g, hd, l), blk4),
            pl.BlockSpec((nb, g, hd, l), blk4),
        ),
        out_shape=(
            jax.ShapeDtypeStruct((n * g, SUBLANES, hd), F32),
            jax.ShapeDtypeStruct(ck.shape, F32),
            jax.ShapeDtypeStruct(cv.shape, F32),
        ),
        compiler_params=pltpu.CompilerParams(
            dimension_semantics=("arbitrary",), vmem_limit_bytes=VMEM_LIMIT),
        name="swa_sample",
    )(sink_t, q8, knr, vnr, knt, vnt, ck, cv)


def _tail_mix(x, y_rwkv, y_swa, g1, sh2, sc2, w_out, n_post_mix, n_pre_ffn):
    mix = _dot(y_rwkv, w_out[:RWKV_WIDTH]) + _dot(y_swa, w_out[RWKV_WIDTH:])
    x1 = x + g1 * _rmsnorm(mix, n_post_mix)
    h2 = (_rmsnorm(x1, n_pre_ffn) * (1.0 + sc2) + sh2).astype(BF16)
    return x1, h2


def _ffn_gate(act_ref, cw, cb, taps_fn, n_rows):
    def row_block(rb, carry):
        r0 = pl.multiple_of(rb * FF_ROWS, FF_ROWS)
        for c in range(N_FF_CHUNKS):
            z = []
            for half in range(2):
                cols = slice(half * D_FF + c * FF_CHUNK, half * D_FF + (c + 1) * FF_CHUNK)
                p1, p2, u = taps_fn(r0, cols)
                z.append(cb[:, cols] + p2 * cw[0:1, cols] + p1 * cw[1:2, cols] + u * cw[2:3, cols])
            act_ref[pl.ds(r0, FF_ROWS), c * FF_CHUNK:(c + 1) * FF_CHUNK] = (
                _silu(z[0]) * z[1]).astype(BF16)
        return carry

    lax.fori_loop(0, n_rows // FF_ROWS, row_block, 0)


def _tail_prompt_kernel(x_ref, yr_ref, ys_ref, mod_ref, w_out, n_post_mix, n_pre_ffn,
                        n_post_ffn, w_up, w_down, cw_ref, cb_ref,
                        y_ref, cp_ref, carry_ref, act_ref):
    t = pl.program_id(1)
    tm = x_ref.shape[1]
    sub = min(tm, TAIL_SUB)
    grp = min(tm, TAIL_GROUP)

    @pl.when(t == 0)
    def _():
        carry_ref[...] = jnp.zeros_like(carry_ref)

    g1, sh2, sc2, g2 = mod_ref[2, 0], mod_ref[3, 0], mod_ref[4, 0], mod_ref[5, 0]

    def cols_of(c, half):
        return slice(half * D_FF + c * FF_CHUNK, half * D_FF + (c + 1) * FF_CHUNK)

    def conv(u, cols):
        ext = jnp.concatenate([carry_ref[:, cols], u], axis=0)
        last = u[sub - SUBLANES:]
        carry_ref[:, cols] = last
        cp_ref[0, :, cols] = last
        return (cb_ref[:, cols] + pltpu.roll(ext, 2, 0)[SUBLANES:] * cw_ref[0:1, cols]
                + pltpu.roll(ext, 1, 0)[SUBLANES:] * cw_ref[1:2, cols] + u * cw_ref[2:3, cols])

    def group(gi, carry):
        base = pl.multiple_of(gi * grp, grp)
        blocks = [(pl.ds(base + j * sub, sub), slice(j * sub, (j + 1) * sub))
                  for j in range(grp // sub)]
        mixes = [_dot(yr_ref[0, rb, :], w_out[:RWKV_WIDTH]) + _dot(ys_ref[0, rb, :], w_out[RWKV_WIDTH:])
                 for rb, _ in blocks]
        for (rb, ab), mix in zip(blocks, mixes):
            x1 = x_ref[0, rb, :] + g1 * _rmsnorm(mix, n_post_mix[...])
            h2 = (_rmsnorm(x1, n_pre_ffn[...]) * (1.0 + sc2) + sh2).astype(BF16)

            def up(c):
                return [jnp.dot(h2, w_up[:, cols_of(c, half)], preferred_element_type=F32)
                        for half in range(2)]

            def down(c):
                rows = slice(c * FF_CHUNK, (c + 1) * FF_CHUNK)
                return jnp.dot(act_ref[ab, rows], w_down[rows, :], preferred_element_type=F32)

            u_next = up(0)
            ff = None
            for c in range(N_FF_CHUNKS):
                u_cur = u_next
                if c + 1 < N_FF_CHUNKS:
                    u_next = up(c + 1)
                if c >= 1:
                    part = down(c - 1)
                    ff = part if ff is None else ff + part
                za, zb = [conv(u_cur[half], cols_of(c, half)) for half in range(2)]
                act_ref[ab, c * FF_CHUNK:(c + 1) * FF_CHUNK] = (_silu(za) * zb).astype(BF16)
            ff = ff + down(N_FF_CHUNKS - 1)
            y_ref[0, rb, :] = x1 + g2 * _rmsnorm(ff, n_post_ffn[...])
        return carry

    lax.fori_loop(0, tm // grp, group, 0)


def _tail_sample_kernel(x_ref, o_ref, feat_ref, ys_ref, mod_ref, p0_ref, p1_ref, ln_w, ln_b,
                        w_out, n_post_mix, n_pre_ffn, n_post_ffn, w_up, w_down,
                        cw_ref, cb_ref, y_ref, u_ref, act_ref):
    n = x_ref.shape[0]
    seg = _seg_ones(RWKV_WIDTH)
    y_rwkv = _gn_epilogue(o_ref[...].T, feat_ref[7], feat_ref[6], ln_w[...], ln_b[...], seg)
    g1, sh2, sc2, g2 = mod_ref[2], mod_ref[3], mod_ref[4], mod_ref[5]
    x1, h2 = _tail_mix(x_ref[...], y_rwkv, ys_ref[...], g1, sh2, sc2, w_out[...],
                       n_post_mix[...], n_pre_ffn[...])
    u_ref[...] = jnp.dot(h2, w_up[...], preferred_element_type=F32)

    def taps_fn(r0, cols):
        rows = pl.ds(r0, FF_ROWS)
        return p1_ref[rows, cols], p0_ref[rows, cols], u_ref[rows, cols]

    _ffn_gate(act_ref, cw_ref, cb_ref, taps_fn, n)
    ff = jnp.dot(act_ref[...], w_down[...], preferred_element_type=F32)
    y_ref[...] = x1 + g2 * _rmsnorm(ff, n_post_ffn[...])


def _tail_prompt_call(x, y_rwkv, y_swa, mod_p, consts, tm):
    b, t, _ = x.shape
    tok = lambda i, j: (i, j, 0)
    in_specs = [
        pl.BlockSpec((1, tm, D_MODEL), tok),
        pl.BlockSpec((1, tm, RWKV_WIDTH), tok),
        pl.BlockSpec((1, tm, SWA_WIDTH), tok),
        pl.BlockSpec((6, 1, 1, D_MODEL), lambda i, j: (0, i, 0, 0)),
    ] + [_const_spec(c, 2) for c in consts]
    return pl.pallas_call(
        _tail_prompt_kernel,
        grid=(b, t // tm),
        in_specs=in_specs,
        out_specs=(
            pl.BlockSpec((1, tm, D_MODEL), tok),
            pl.BlockSpec((1, SUBLANES, 2 * D_FF), lambda i, j: (i, 0, 0)),
        ),
        out_shape=(
            jax.ShapeDtypeStruct((b, t, D_MODEL), F32),
            jax.ShapeDtypeStruct((b, SUBLANES, 2 * D_FF), F32),
        ),
        scratch_shapes=[
            pltpu.VMEM((SUBLANES, 2 * D_FF), F32),
            pltpu.VMEM((min(tm, TAIL_GROUP), D_FF), BF16),
        ],
        compiler_params=pltpu.CompilerParams(
            dimension_semantics=("arbitrary", "arbitrary"), vmem_limit_bytes=VMEM_LIMIT),
        name="tail_prompt",
    )(x, y_rwkv, y_swa, mod_p, *consts)


def _tail_sample_call(x, o_s, feat_s, y_swa, mod_s, p0, p1, ln_w, ln_b, consts):
    n = x.shape[0]
    args = (x, o_s, feat_s, y_swa, mod_s, p0, p1, ln_w, ln_b) + tuple(consts)
    out_shape = (
        jax.ShapeDtypeStruct((n, D_MODEL), F32),
        jax.ShapeDtypeStruct((n, 2 * D_FF), F32),
    )
    return pl.pallas_call(
        _tail_sample_kernel,
        grid=(1,),
        in_specs=[_const_spec(a, 1) for a in args],
        out_specs=tuple(pl.BlockSpec(s.shape, lambda i, nd=len(s.shape): (0,) * nd)
                        for s in out_shape),
        out_shape=out_shape,
        scratch_shapes=[pltpu.VMEM((n, D_FF), BF16)],
        compiler_params=pltpu.CompilerParams(
            dimension_semantics=("arbitrary",), vmem_limit_bytes=VMEM_LIMIT),
        name="tail_sample",
    )(*args)


def _rope_tables(pos):
    half = HEAD_DIM // 2
    inv = ROPE_THETA ** (-jnp.arange(half, dtype=F32) / half)
    ang = pos.astype(F32)[:, None] * inv[None, :]
    cos, sin = jnp.cos(ang), jnp.sin(ang)
    cos_h = jnp.concatenate([cos, cos], axis=-1)
    sin_h = jnp.concatenate([-sin, sin], axis=-1)
    return jnp.tile(cos_h, (1, LANES // HEAD_DIM)), jnp.tile(sin_h, (1, LANES // HEAD_DIM))


def kernel(x_prompt, x_sample, state_rwkv_wkv, state_rwkv_shift, cache_swa_k, cache_swa_v,
           state_ffn_conv, c_prompt, c_sample, w_ada, b_ada, norm_pre_mix, norm_post_mix,
           norm_pre_ffn, norm_post_ffn, w_in, rwkv_mu, rwkv_w0, rwkv_w_up, rwkv_a0, rwkv_a_up,
           rwkv_g_up, rwkv_k_k, rwkv_k_a, rwkv_r_k, rwkv_ln_w, rwkv_ln_b, swa_sinks, w_out,
           ffn_w_up, ffn_conv_w, ffn_conv_b, ffn_w_down):
    depth = w_ada.shape[0]
    assert depth == 1 and x_sample.shape[1] == 1
    b, t, _ = x_prompt.shape
    n = x_sample.shape[0]
    l = cache_swa_k.shape[2]
    tm = min(256, t)
    assert t % tm == 0 and tm % WINDOW == 0 and tm % CHUNK == 0
    assert tm % FF_ROWS == 0 and n % FF_ROWS == 0
    nb_swa = 8 if n % 8 == 0 else n
    li = 0

    row = lambda v: v.reshape(1, -1)
    w_in_b = w_in[li].astype(BF16)
    zeros_l = jnp.zeros((64, RWKV_WIDTH), F32)
    wa_up = jnp.concatenate([
        jnp.concatenate([rwkv_w_up[li], zeros_l], axis=1),
        jnp.concatenate([zeros_l, rwkv_a_up[li]], axis=1)], axis=0).astype(BF16)
    hid = jnp.arange(GROUP_LANES) // HEAD_DIM
    seg_blk = (hid[:, None] == hid[None, :]).astype(BF16)
    proj_consts = (row(norm_pre_mix[li]), w_in_b, row(rwkv_mu[li]), row(rwkv_w0[li]), wa_up,
                   row(rwkv_a0[li]), rwkv_g_up[li].astype(BF16), row(rwkv_k_k[li]),
                   row(rwkv_k_a[li]), row(rwkv_r_k[li]), seg_blk)
    tail_consts = (w_out[li].astype(BF16), row(norm_post_mix[li]), row(norm_pre_ffn[li]),
                   row(norm_post_ffn[li]), ffn_w_up[li].astype(BF16), ffn_w_down[li].astype(BF16),
                   ffn_conv_w[li], row(ffn_conv_b[li]))
    ln_w, ln_b = row(rwkv_ln_w[li]), row(rwkv_ln_b[li])

    mod = _ada_call(jnp.concatenate([c_prompt, c_sample], axis=0), w_ada[li], row(b_ada[li]))
    mod_p = mod[:, :b].reshape(6, b, 1, D_MODEL)
    mod_s = mod[:, b:]

    cos_p, sin_p = _rope_tables(jnp.arange(t, dtype=jnp.int32))
    y_rwkv_p, s_cat, q_p, kx_p, vx_p, klast, vlast, plast = _mix_prompt_call(
        x_prompt, mod_p, cos_p, sin_p, proj_consts + (ln_w, ln_b), min(4 * tm, t))
    y_swa_p = _swa_prompt_call(swa_sinks[li], q_p, kx_p, vx_p, min(4 * tm, t))
    y_p, cp = _tail_prompt_call(x_prompt, y_rwkv_p, y_swa_p, mod_p, tail_consts, min(4 * tm, t))

    wkv_p = s_cat.reshape(b, HEAD_DIM, RWKV_HEADS, HEAD_DIM).transpose(0, 2, 1, 3)
    shift_p = plast[:, SUBLANES - 1]
    k_p = klast.reshape(b, WINDOW, SWA_KV_HEADS, HEAD_DIM)
    v_p = vlast.reshape(b, WINDOW, SWA_KV_HEADS, HEAD_DIM)
    conv_p = cp[:, SUBLANES - 2:]

    cos_s, sin_s = _rope_tables(jnp.full((1,), PAST_LEN, jnp.int32))
    feat_s, ft_s, q_s, kn_s, vn_s, p_s = _proj_sample_call(
        x_sample[:, 0], mod_s, cos_s, sin_s, state_rwkv_shift[li], proj_consts)
    state_t = jnp.transpose(state_rwkv_wkv[li], (1, 2, 3, 0))
    wkv_t, o_t = _wkv_sample_call(ft_s, state_t)
    wkv_s = jnp.transpose(wkv_t, (3, 0, 1, 2))

    g2 = SWA_KV_HEADS
    q4 = q_s.reshape(n * g2, SWA_GROUP, HEAD_DIM)
    q8 = jnp.concatenate([q4, jnp.zeros_like(q4)], axis=1)
    sink_t = jnp.broadcast_to(
        jnp.concatenate([swa_sinks[li].reshape(g2, SWA_GROUP),
                         jnp.full((g2, SWA_GROUP), MASK_VALUE, F32)], axis=1)[None, :, :, None],
        (nb_swa, g2, SUBLANES, LANES)).reshape(nb_swa * g2, SUBLANES, LANES)
    cols = lambda a: a.reshape(n // nb_swa, nb_swa, KV_WIDTH).transpose(0, 2, 1)
    ck = jnp.transpose(cache_swa_k[li], (0, 2, 3, 1))
    cv = jnp.transpose(cache_swa_v[li], (0, 2, 3, 1))
    o_att, k_t, v_t = _swa_sample_call(
        sink_t, q8, kn_s.reshape(n * g2, 1, HEAD_DIM), vn_s.reshape(n * g2, 1, HEAD_DIM),
        cols(kn_s), cols(vn_s), ck, cv, nb_swa)
    y_swa_s = o_att[:, :SWA_GROUP].reshape(n, SWA_WIDTH)
    k_s = jnp.transpose(k_t, (0, 3, 1, 2))
    v_s = jnp.transpose(v_t, (0, 3, 1, 2))

    conv0 = state_ffn_conv[li]
    y_s, u_s = _tail_sample_call(x_sample[:, 0], o_t, feat_s, y_swa_s.astype(BF16), mod_s,
                                 conv0[:, 0], conv0[:, 1], ln_w, ln_b, tail_consts)
    conv_s = jnp.stack([conv0[:, 1], u_s], axis=1)

    expand = lambda a: a[None]
    return (y_p, y_s[:, None, :], expand(wkv_p), expand(shift_p), expand(k_p), expand(v_p),
            expand(conv_p), expand(wkv_s), expand(p_s),
            expand(k_s), expand(v_s), expand(conv_s))
```

```python
import math

import jax
import jax.numpy as jnp
from jax import lax
from jax.experimental import pallas as pl
from jax.experimental.pallas import tpu as pltpu

D_MODEL = 1024
HEAD_DIM = 64
RWKV_WIDTH = 512
RWKV_HEADS = 8
RWKV_COLS = 1792
RWKV_GN_EPS = 64e-5
SWA_WIDTH = 512
SWA_HEADS = 8
SWA_KV_HEADS = 2
SWA_GROUP = 4
KV_WIDTH = SWA_KV_HEADS * HEAD_DIM
WINDOW = 128
PAST_LEN = 16384
ROPE_THETA = 10000.0
ATTN_SCALE = HEAD_DIM ** -0.5
D_FF = 2816
NORM_EPS = 1e-6
MASK_VALUE = -1e30
PROJ_COLS = RWKV_COLS + SWA_WIDTH + 2 * KV_WIDTH

LANES = 128
SUBLANES = 8
CHUNK = 64
GROUP_LANES = 256
FF_CHUNK = 256
N_FF_CHUNKS = D_FF // FF_CHUNK
TAIL_SUB = 256
MIX_SUB = 512
TAIL_GROUP = 512
VMEM_LIMIT = 56 * 1024 * 1024

F32 = jnp.float32
BF16 = jnp.bfloat16


def _sigmoid(x):
    return 1.0 / (1.0 + jnp.exp(-x))


def _silu(x):
    return x * _sigmoid(x)


def _rmsnorm(x, g):
    return x * lax.rsqrt(jnp.mean(x * x, axis=-1, keepdims=True) + NORM_EPS) * g


def _dot(a, b):
    return jnp.dot(a.astype(BF16), b.astype(BF16), preferred_element_type=F32)


def _dot_nt(a, b):
    return lax.dot_general(a.astype(BF16), b.astype(BF16), (((1,), (1,)), ((), ())),
                           preferred_element_type=F32)


def _dot_tn(a, b):
    return lax.dot_general(a.astype(BF16), b.astype(BF16), (((0,), (0,)), ((), ())),
                           preferred_element_type=F32)


def _every(fn, *lists):
    return [fn(*a) for a in zip(*lists)]


def _swap_halves(x):
    w = x.shape[-1]
    lane = lax.broadcasted_iota(jnp.int32, x.shape, x.ndim - 1)
    lo = (lane & (HEAD_DIM // 2)) == 0
    return jnp.where(lo, pltpu.roll(x, w - HEAD_DIM // 2, x.ndim - 1),
                     pltpu.roll(x, HEAD_DIM // 2, x.ndim - 1))


def _rope(x, cos, sin):
    reps = x.shape[-1] // LANES
    cos_w = jnp.concatenate([cos] * reps, axis=-1) if reps > 1 else cos
    sin_w = jnp.concatenate([sin] * reps, axis=-1) if reps > 1 else sin
    return x * cos_w + _swap_halves(x) * sin_w


def _proj_features(x, shift, scale, g_pre, w_in, prev_fn, mu, w0, wa_up, a0, g_up,
                   k_k, k_a, r_k, seg_blk, cos, sin):
    h = _rmsnorm(x, g_pre) * (1.0 + scale) + shift
    p = _dot(h, w_in)
    p_rwkv = p[:, :RWKV_COLS]
    prev = prev_fn(p_rwkv)
    xm = p_rwkv + (prev - p_rwkv) * mu
    r = xm[:, 0:512]
    k = xm[:, 512:1024]
    v = xm[:, 1024:1536]
    wa = xm[:, 1536:1664]
    gd = xm[:, 1664:1792]
    lane = lax.broadcasted_iota(jnp.int32, wa.shape, 1)
    wa_act = jnp.where(lane < 64, jnp.tanh(wa), wa)
    lora = _dot(wa_act, wa_up)
    lw = -math.exp(-0.5) * _sigmoid(w0 + lora[:, :512])
    a = _sigmoid(a0 + lora[:, 512:])
    g = _dot(_sigmoid(gd), g_up)
    kk = k * k_k
    kf = k * (1.0 + (a - 1.0) * k_a)
    gl = seg_blk.shape[0]
    n = x.shape[0]
    sums = [_dot(jnp.concatenate([(kk * kk)[:, i:i + gl], (r * kf * r_k)[:, i:i + gl]], axis=0),
                 seg_blk) for i in range(0, RWKV_WIDTH, gl)]
    ss = jnp.concatenate([s_[:n] for s_ in sums], axis=1)
    kk = kk / jnp.maximum(jnp.sqrt(ss), 1e-12)
    bonus = jnp.concatenate([s_[n:] for s_ in sums], axis=1) * v
    feats = (r, lw, kf, v, -kk, kk * a, g, bonus)
    q = _rope(p[:, RWKV_COLS:RWKV_COLS + SWA_WIDTH], cos, sin)
    ks = _rope(p[:, RWKV_COLS + SWA_WIDTH:RWKV_COLS + SWA_WIDTH + KV_WIDTH], cos, sin)
    vs = p[:, RWKV_COLS + SWA_WIDTH + KV_WIDTH:]
    return feats, q, ks, vs, p_rwkv


def _gn_epilogue(o, bonus, g, ln_w, ln_b, seg):
    mu = _dot(o, seg) * (1.0 / HEAD_DIM)
    d = o - mu
    var = _dot(d * d, seg) * (1.0 / HEAD_DIM)
    gn = d * lax.rsqrt(var + RWKV_GN_EPS) * ln_w + ln_b
    return (gn + bonus) * g


def _seg_ones(n):
    r = lax.broadcasted_iota(jnp.int32, (n, n), 0) // HEAD_DIM
    c = lax.broadcasted_iota(jnp.int32, (n, n), 1) // HEAD_DIM
    return (r == c).astype(BF16)


def _ada_kernel(c_ref, w_ref, b_ref, o_ref):
    o_ref[0] = _dot(_silu(c_ref[...]), w_ref[...]) + b_ref[...]


def _ada_call(c_all, w_ada, b_ada):
    rows = c_all.shape[0]
    return pl.pallas_call(
        _ada_kernel,
        grid=(6,),
        in_specs=[
            pl.BlockSpec((rows, D_MODEL), lambda j: (0, 0)),
            pl.BlockSpec((D_MODEL, D_MODEL), lambda j: (0, j)),
            pl.BlockSpec((1, D_MODEL), lambda j: (0, j)),
        ],
        out_specs=pl.BlockSpec((1, rows, D_MODEL), lambda j: (j, 0, 0)),
        out_shape=jax.ShapeDtypeStruct((6, rows, D_MODEL), F32),
        compiler_params=pltpu.CompilerParams(
            dimension_semantics=("arbitrary",), vmem_limit_bytes=VMEM_LIMIT),
        name="ada",
    )(c_all, w_ada, b_ada)


def _expand_kv(x):
    lane = lax.broadcasted_iota(jnp.int32, x.shape, 1)
    rolled = pltpu.roll(x, HEAD_DIM, 1)
    g0 = jnp.where(lane < HEAD_DIM, x, rolled)
    g1 = jnp.where(lane < HEAD_DIM, rolled, x)
    return jnp.concatenate([g0, g0, g1, g1], axis=1)


def _mix_prompt_kernel(x_ref, mod_ref, cos_ref, sin_ref, g_pre, w_in, mu, w0, wa_up, a0,
                       g_up, k_k, k_a, r_k, seg_blk, ln_w, ln_b,
                       y_ref, s_out_ref, q_ref, k_ref, v_ref, klast_ref, vlast_ref, plast_ref,
                       carry_ref, s_ref):
    t = pl.program_id(1)

    @pl.when(t == 0)
    def _():
        carry_ref[...] = jnp.zeros_like(carry_ref)
        s_ref[...] = jnp.zeros_like(s_ref)

    sub = min(x_ref.shape[1], MIX_SUB)

    def sub_tile(i, carry):
        rows = pl.ds(pl.multiple_of(i * sub, sub), sub)
        carry_row = carry_ref[SUBLANES - 1:SUBLANES, :]

        def prev_fn(p_rwkv):
            row = lax.broadcasted_iota(jnp.int32, p_rwkv.shape, 0)
            return jnp.where(row == 0, carry_row, pltpu.roll(p_rwkv, 1, 0))

        feats, q, ks, vs, p_rwkv = _proj_features(
            x_ref[0, rows, :], mod_ref[0, 0], mod_ref[1, 0], g_pre[...], w_in[...], prev_fn,
            mu[...], w0[...], wa_up[...], a0[...], g_up[...], k_k[...], k_a[...], r_k[...],
            seg_blk[...], cos_ref[rows, :], sin_ref[rows, :])
        q_ref[0, rows, :] = (q * ATTN_SCALE).astype(BF16)
        k_ref[0, rows, :] = _expand_kv(ks).astype(BF16)
        v_ref[0, rows, :] = _expand_kv(vs).astype(BF16)
        last = p_rwkv[sub - SUBLANES:, :]
        carry_ref[...] = last
        plast_ref[0] = last
        klast_ref[0] = ks[sub - WINDOW:, :]
        vlast_ref[0] = vs[sub - WINDOW:, :]
        _wkv_tile(feats, ln_w, ln_b, y_ref.at[0, rows, :], s_ref)
        return carry

    lax.fori_loop(0, x_ref.shape[1] // sub, sub_tile, 0)

    @pl.when(t == pl.num_programs(1) - 1)
    def _():
        for g in range(RWKV_WIDTH // GROUP_LANES):
            s_out_ref[0, :, g * GROUP_LANES:(g + 1) * GROUP_LANES] = s_ref[g]


def _proj_sample_kernel(x_ref, mod_ref, cos_ref, sin_ref, prev_ref, g_pre, w_in, mu, w0,
                        wa_up, a0, g_up, k_k, k_a, r_k, seg_blk,
                        feat_ref, ft_ref, q_ref, k_ref, v_ref, p_ref, w_in_b_ref):
    w_in_b = w_in[...].astype(BF16)
    w_in_b_ref[...] = w_in_b
    feats, q, ks, vs, p_rwkv = _proj_features(
        x_ref[...], mod_ref[0], mod_ref[1], g_pre[...], w_in_b, lambda p: prev_ref[...],
        mu[...], w0[...], wa_up[...], a0[...], g_up[...], k_k[...], k_a[...], r_k[...],
        seg_blk[...], cos_ref[...], sin_ref[...])
    for i, f in enumerate(feats):
        feat_ref[i] = f
    for i in range(6):
        ft_ref[i] = feats[i].T
    q_ref[...] = (q * ATTN_SCALE).astype(BF16)
    k_ref[...] = ks
    v_ref[...] = vs
    p_ref[...] = p_rwkv


def _const_spec(arr, grid_rank):
    zeros = (0,) * arr.ndim
    if grid_rank == 1:
        return pl.BlockSpec(arr.shape, lambda i: zeros)
    return pl.BlockSpec(arr.shape, lambda i, j: zeros)


def _mix_prompt_call(x, mod_p, cos, sin, consts, tm):
    b, t, _ = x.shape
    nt = t // tm
    in_specs = [
        pl.BlockSpec((1, tm, D_MODEL), lambda i, j: (i, j, 0)),
        pl.BlockSpec((6, 1, 1, D_MODEL), lambda i, j: (0, i, 0, 0)),
        pl.BlockSpec((tm, LANES), lambda i, j: (j, 0)),
        pl.BlockSpec((tm, LANES), lambda i, j: (j, 0)),
    ] + [_const_spec(c, 2) for c in consts]
    out_shape = (
        jax.ShapeDtypeStruct((b, t, RWKV_WIDTH), BF16),
        jax.ShapeDtypeStruct((b, HEAD_DIM, RWKV_WIDTH), F32),
        jax.ShapeDtypeStruct((b, t, SWA_WIDTH), BF16),
        jax.ShapeDtypeStruct((b, t, SWA_WIDTH), BF16),
        jax.ShapeDtypeStruct((b, t, SWA_WIDTH), BF16),
        jax.ShapeDtypeStruct((b, WINDOW, KV_WIDTH), F32),
        jax.ShapeDtypeStruct((b, WINDOW, KV_WIDTH), F32),
        jax.ShapeDtypeStruct((b, SUBLANES, RWKV_COLS), F32),
    )
    out_specs = (
        pl.BlockSpec((1, tm, RWKV_WIDTH), lambda i, j: (i, j, 0)),
        pl.BlockSpec((1, HEAD_DIM, RWKV_WIDTH), lambda i, j: (i, 0, 0)),
        pl.BlockSpec((1, tm, SWA_WIDTH), lambda i, j: (i, j, 0)),
        pl.BlockSpec((1, tm, SWA_WIDTH), lambda i, j: (i, j, 0)),
        pl.BlockSpec((1, tm, SWA_WIDTH), lambda i, j: (i, j, 0)),
        pl.BlockSpec((1, WINDOW, KV_WIDTH), lambda i, j: (i, 0, 0)),
        pl.BlockSpec((1, WINDOW, KV_WIDTH), lambda i, j: (i, 0, 0)),
        pl.BlockSpec((1, SUBLANES, RWKV_COLS), lambda i, j: (i, 0, 0)),
    )
    return pl.pallas_call(
        _mix_prompt_kernel,
        grid=(b, nt),
        in_specs=in_specs,
        out_specs=out_specs,
        out_shape=out_shape,
        scratch_shapes=[
            pltpu.VMEM((SUBLANES, RWKV_COLS), F32),
            pltpu.VMEM((RWKV_WIDTH // GROUP_LANES, HEAD_DIM, GROUP_LANES), F32),
        ],
        compiler_params=pltpu.CompilerParams(
            dimension_semantics=("arbitrary", "arbitrary"), vmem_limit_bytes=VMEM_LIMIT),
        name="mix_prompt",
    )(x, mod_p, cos, sin, *consts)


def _mod_rows_spec(mod, n):
    return pl.BlockSpec((mod.shape[0], n, mod.shape[2]), lambda i: (0, 0, 0))


def _proj_sample_call(x, mod, cos, sin, prev, consts):
    n = x.shape[0]
    args = (x, mod, cos, sin, prev) + tuple(consts)
    out_shape = (
        jax.ShapeDtypeStruct((8, n, RWKV_WIDTH), F32),
        jax.ShapeDtypeStruct((6, RWKV_WIDTH, n), F32),
        jax.ShapeDtypeStruct((n, SWA_WIDTH), BF16),
        jax.ShapeDtypeStruct((n, KV_WIDTH), F32),
        jax.ShapeDtypeStruct((n, KV_WIDTH), F32),
        jax.ShapeDtypeStruct((n, RWKV_COLS), F32),
        jax.ShapeDtypeStruct((D_MODEL, PROJ_COLS), BF16),
    )
    return pl.pallas_call(
        _proj_sample_kernel,
        grid=(1,),
        in_specs=[_mod_rows_spec(a, n) if i == 1 else _const_spec(a, 1)
                  for i, a in enumerate(args)],
        out_specs=tuple(pl.BlockSpec(s.shape, lambda i, nd=len(s.shape): (0,) * nd)
                        for s in out_shape),
        out_shape=out_shape,
        compiler_params=pltpu.CompilerParams(
            dimension_semantics=("arbitrary",), vmem_limit_bytes=VMEM_LIMIT),
        name="proj_sample",
    )(*args)


def _wkv_tile(feats, ln_w, ln_b, y_ref, s_ref):
    tt = feats[0].shape[0]
    n_chunks = tt // CHUNK
    gl = GROUP_LANES
    n_groups = RWKV_WIDTH // gl
    heads_per_group = gl // HEAD_DIM
    probs = [(c, g) for c in range(n_chunks) for g in range(n_groups)]

    row_c = lax.broadcasted_iota(jnp.int32, (CHUNK, gl), 0)
    col_c = lax.broadcasted_iota(jnp.int32, (CHUNK, gl), 1) % CHUNK
    strict = row_c > col_c
    incl = row_c >= col_c
    eye_cat = (row_c == col_c).astype(F32)
    rb = lax.broadcasted_iota(jnp.int32, (gl, gl), 0) // HEAD_DIM
    cb = lax.broadcasted_iota(jnp.int32, (gl, gl), 1) // HEAD_DIM
    bd_mask = rb == cb
    tri_r = lax.broadcasted_iota(jnp.int32, (CHUNK, CHUNK), 0)
    tri_c = lax.broadcasted_iota(jnp.int32, (CHUNK, CHUNK), 1)
    tril_ones = (tri_r >= tri_c).astype(BF16)
    seg = bd_mask.astype(BF16)

    def bd(x):
        xb = x.astype(BF16)
        return jnp.where(bd_mask, jnp.concatenate([xb] * heads_per_group, axis=0),
                         jnp.zeros((), BF16))

    def fold(x):
        xm = jnp.where(bd_mask, x, 0.0)
        acc = xm[0:HEAD_DIM]
        for hh in range(1, heads_per_group):
            acc = acc + xm[hh * HEAD_DIM:(hh + 1) * HEAD_DIM]
        return acc

    def ld(i, p):
        c, g = p
        return feats[i][c * CHUNK:(c + 1) * CHUNK, g * gl:(g + 1) * gl]

    lw = [ld(1, p) for p in probs]
    na = [ld(4, p) for p in probs]
    bb = [ld(5, p) for p in probs]
    kf = [ld(2, p) for p in probs]
    r = [ld(0, p) for p in probs]
    v = [ld(3, p) for p in probs]

    def cumsum(x):
        hi = x.astype(BF16)
        lo = (x - hi.astype(F32)).astype(BF16)
        both = jnp.dot(tril_ones, jnp.concatenate([hi, lo], axis=1), preferred_element_type=F32)
        return both[:, :gl] + both[:, gl:]

    cum = _every(cumsum, lw)
    cum_last = [x[CHUNK - 1:CHUNK, :] for x in cum]
    e_out = [jnp.exp(-x) for x in cum]
    e_end = _every(lambda cl, x: jnp.exp(cl - x), cum_last, cum)
    a_t = _every(lambda n_, x, l_: n_ * jnp.exp(x - l_), na, cum, lw)
    r_t = _every(lambda r_, x: r_ * jnp.exp(x), r, cum)
    b_t = _every(lambda b_, e: b_ * e, bb, e_out)
    k_t = _every(lambda k_, e: k_ * e, kf, e_out)
    b_end = _every(lambda b_, e: b_ * e, bb, e_end)
    k_end = _every(lambda k_, e: k_ * e, kf, e_end)
    gamma = [jnp.exp(x) for x in cum_last]

    ar = _every(lambda a_, r_: jnp.concatenate([a_, r_], axis=0), a_t, r_t)
    pb = _every(lambda x, y: _dot_nt(x, bd(y)), ar, b_t)
    pk = _every(lambda x, y: _dot_nt(x, bd(y)), ar, k_t)
    l_ab = [jnp.where(strict, x[:CHUNK], 0.0) for x in pb]
    l_ak = [jnp.where(strict, x[:CHUNK], 0.0) for x in pk]
    m_rb = [jnp.where(incl, x[CHUNK:], 0.0) for x in pb]
    m_rk = [jnp.where(incl, x[CHUNK:], 0.0) for x in pk]

    x_acc = [eye_cat + l for l in l_ab]
    pw = _every(lambda l: _dot(l, bd(l)), l_ab)
    n_sq = int(math.log2(CHUNK)) - 1
    for lvl in range(n_sq):
        rhs = [bd(p_) for p_ in pw]
        if lvl < n_sq - 1:
            both = _every(lambda x, p_, w_: _dot(jnp.concatenate([x, p_], axis=0), w_),
                          x_acc, pw, rhs)
            x_acc = _every(lambda x, b_: x + b_[:CHUNK], x_acc, both)
            pw = [b_[CHUNK:] for b_ in both]
        else:
            x_acc = _every(lambda x, w_: x + _dot(x, w_), x_acc, rhs)
    t_inv = x_acc

    kv = _every(lambda la, mk, x: _dot(jnp.concatenate([la, mk], axis=0), bd(x)), l_ak, m_rk, v)
    y_loc = [x[:CHUNK] for x in kv]
    wu = _every(lambda t_, a_, y_: _dot(t_, jnp.concatenate([bd(a_), bd(y_)], axis=1)),
                t_inv, a_t, y_loc)
    w_t = [x[:, :gl] for x in wu]
    u_loc = [x[:, gl:] for x in wu]

    mwu = _every(lambda mb, w_, u_: _dot(mb, jnp.concatenate([bd(w_), bd(u_)], axis=1)),
                 m_rb, w_t, u_loc)
    q_c = _every(lambda r_, x: r_ + x[:, :gl], r_t, mwu)
    o_loc = _every(lambda x, y_: x[:, gl:] + y_[CHUNK:], mwu, kv)
    m_low = _every(lambda w_, b_: jnp.where(bd_mask, _dot_tn(w_, b_), 0.0).astype(BF16),
                   w_t, b_end)
    n_loc = _every(lambda u_, v_, b_, k_: fold(_dot_tn(jnp.concatenate([u_, v_], axis=0),
                                                       jnp.concatenate([b_, k_], axis=0))),
                   u_loc, v, b_end, k_end)

    state = [s_ref[g] for g in range(n_groups)]
    starts = []
    for c in range(n_chunks):
        starts.append(state)
        idx = [c * n_groups + g for g in range(n_groups)]
        state = [state[g] * gamma[i] + _dot(state[g], m_low[i]) + n_loc[i]
                 for g, i in enumerate(idx)]
    for g in range(n_groups):
        s_ref[g] = state[g]

    s0 = [starts[c][g] for (c, g) in probs]
    o = _every(lambda q_, s_, ol: _dot_nt(q_, bd(s_)) + ol, q_c, s0, o_loc)

    n_p = len(probs)
    unstack = lambda x: [x[i * CHUNK:(i + 1) * CHUNK] for i in range(n_p)]
    mu = unstack(_dot(jnp.concatenate(o, axis=0), seg) * (1.0 / HEAD_DIM))
    d = _every(lambda x, m_: x - m_, o, mu)
    var = unstack(_dot(jnp.concatenate([x * x for x in d], axis=0), seg) * (1.0 / HEAD_DIM))
    for i, (c, g) in enumerate(probs):
        ls = slice(g * gl, (g + 1) * gl)
        gn = d[i] * lax.rsqrt(var[i] + RWKV_GN_EPS) * ln_w[:, ls] + ln_b[:, ls]
        y = (gn + ld(7, (c, g))) * ld(6, (c, g))
        y_ref[c * CHUNK:(c + 1) * CHUNK, ls] = y.astype(y_ref.dtype)


def _wkv_sample_kernel(ft_ref, s_ref, s_out_ref, o_ref):
    hd = HEAD_DIM
    r, kf, na, bb = ft_ref[0], ft_ref[2], ft_ref[4], ft_ref[5]
    w = jnp.exp(ft_ref[1])

    def value_block(vb, carry):
        v0 = pl.multiple_of(vb * SUBLANES, SUBLANES)
        v_rows = ft_ref[3, pl.ds(v0, SUBLANES), :]
        outs = []
        for j in range(SUBLANES):
            s = s_ref[0, v0 + j]
            sa = jnp.sum(s * na, axis=0, keepdims=True)
            s_new = s * w + sa * bb + v_rows[j:j + 1] * kf
            s_out_ref[0, v0 + j] = s_new
            outs.append(jnp.sum(s_new * r, axis=0, keepdims=True))
        o_ref[pl.ds(v0, SUBLANES), :] = jnp.concatenate(outs, axis=0)
        return carry

    lax.fori_loop(0, hd // SUBLANES, value_block, 0)


def _wkv_sample_call(ft_s, state_t):
    h, hd, _, n = state_t.shape
    return pl.pallas_call(
        _wkv_sample_kernel,
        grid=(h,),
        in_specs=[
            pl.BlockSpec((6, hd, n), lambda i: (0, i, 0)),
            pl.BlockSpec((1, hd, hd, n), lambda i: (i, 0, 0, 0)),
        ],
        out_specs=(
            pl.BlockSpec((1, hd, hd, n), lambda i: (i, 0, 0, 0)),
            pl.BlockSpec((hd, n), lambda i: (i, 0)),
        ),
        out_shape=(
            jax.ShapeDtypeStruct(state_t.shape, F32),
            jax.ShapeDtypeStruct((h * hd, n), F32),
        ),
        compiler_params=pltpu.CompilerParams(
            dimension_semantics=("arbitrary",), vmem_limit_bytes=VMEM_LIMIT),
        name="wkv_sample",
    )(ft_s, state_t)


def _swa_prompt_kernel(sink_ref, q_ref, kp_ref, kc_ref, vp_ref, vc_ref, o_ref):
    j = pl.program_id(1)
    w = WINDOW
    tq = q_ref.shape[1]
    n_blk = tq // w
    gl = GROUP_LANES
    rows = SWA_GROUP * w
    probs = [(qi, g) for qi in range(n_blk) for g in range(SWA_KV_HEADS)]

    lane_head = lax.broadcasted_iota(jnp.int32, (w, gl), 1) // HEAD_DIM
    head_mask = [lane_head == h for h in range(SWA_GROUP)]
    qi_ = lax.broadcasted_iota(jnp.int32, (rows, 2 * w), 0) % w
    ki_ = lax.broadcasted_iota(jnp.int32, (rows, 2 * w), 1)
    diff = qi_ - (ki_ - w)
    in_window = (diff >= 0) & (diff < WINDOW)
    first_valid = in_window & ((j * tq + ki_ - w) >= 0)

    def keys(ref_prev, ref_cur, qi, g):
        ls = slice(g * gl, (g + 1) * gl)
        prev = ref_prev[0, :, ls] if qi == 0 else ref_cur[0, (qi - 1) * w:qi * w, ls]
        return jnp.concatenate([prev, ref_cur[0, qi * w:(qi + 1) * w, ls]], axis=0)

    def lhs(qi, g):
        qg = q_ref[0, qi * w:(qi + 1) * w, g * gl:(g + 1) * gl]
        return jnp.concatenate([jnp.where(m, qg, jnp.zeros((), BF16)) for m in head_mask], axis=0)

    sinks = []
    for g in range(SWA_KV_HEADS):
        sinks.append(jnp.concatenate(
            [jnp.full((w, 1), sink_ref[g * SWA_GROUP + h], F32) for h in range(SWA_GROUP)], axis=0))

    s = [_dot_nt(lhs(qi, g), keys(kp_ref, kc_ref, qi, g)) for qi, g in probs]
    s = [jnp.where(first_valid if qi == 0 else in_window, x, MASK_VALUE)
         for x, (qi, g) in zip(s, probs)]
    sink = [sinks[g] for qi, g in probs]
    m = _every(lambda x, sk: jnp.maximum(jnp.max(x, axis=-1, keepdims=True), sk), s, sink)
    e = _every(lambda x, m_: jnp.exp(x - m_), s, m)
    inv = _every(lambda e_, sk, m_: 1.0 / (jnp.sum(e_, axis=-1, keepdims=True) + jnp.exp(sk - m_)),
                 e, sink, m)
    prob = _every(lambda e_, i_: (e_ * i_).astype(BF16), e, inv)
    og = [_dot(p_, keys(vp_ref, vc_ref, qi, g)) for p_, (qi, g) in zip(prob, probs)]
    for x, (qi, g) in zip(og, probs):
        y = jnp.where(head_mask[0], x[0:w], 0.0)
        for h in range(1, SWA_GROUP):
            y = y + jnp.where(head_mask[h], x[h * w:(h + 1) * w], 0.0)
        o_ref[0, qi * w:(qi + 1) * w, g * gl:(g + 1) * gl] = y.astype(o_ref.dtype)


def _swa_prompt_call(sinks, q, k, v, tq):
    b, t, _ = q.shape
    w = WINDOW
    per = tq // w
    prev = lambda i, j: (i, jnp.maximum(j * per - 1, 0), 0)
    cur = lambda i, j: (i, j, 0)
    return pl.pallas_call(
        _swa_prompt_kernel,
        grid=(b, t // tq),
        in_specs=[
            pl.BlockSpec(memory_space=pltpu.SMEM),
            pl.BlockSpec((1, tq, SWA_WIDTH), cur),
            pl.BlockSpec((1, w, SWA_WIDTH), prev),
            pl.BlockSpec((1, tq, SWA_WIDTH), cur),
            pl.BlockSpec((1, w, SWA_WIDTH), prev),
            pl.BlockSpec((1, tq, SWA_WIDTH), cur),
        ],
        out_specs=pl.BlockSpec((1, tq, SWA_WIDTH), cur),
        out_shape=jax.ShapeDtypeStruct((b, t, SWA_WIDTH), BF16),
        compiler_params=pltpu.CompilerParams(
            dimension_semantics=("arbitrary", "arbitrary"), vmem_limit_bytes=VMEM_LIMIT),
        name="swa_prompt",
    )(sinks, q, k, k, v, v)


def _swa_sample_kernel(sink_ref, q_ref, knr_ref, vnr_ref, knt_ref, vnt_ref, ck_ref, cv_ref,
                       o_ref, ko_ref, vo_ref):
    nb, _, hd, l = ck_ref.shape
    q = q_ref[...]
    ck = ck_ref[...].reshape(nb * SWA_KV_HEADS, hd, l)
    cv = cv_ref[...].reshape(nb * SWA_KV_HEADS, hd, l)
    knr = knr_ref[...].astype(BF16).astype(F32)
    vnr = vnr_ref[...].astype(BF16).astype(F32)
    ki = lax.broadcasted_iota(jnp.int32, (1, 1, l), 2)
    kpos = PAST_LEN - l + ki
    diff = PAST_LEN - kpos
    valid = (diff >= 0) & (diff < WINDOW) & (kpos >= 0)
    s_c = lax.dot_general(q, ck.astype(BF16), (((2,), (1,)), ((0,), (0,))),
                          preferred_element_type=F32)
    s_c = jnp.where(valid, s_c, MASK_VALUE)
    s_n = jnp.sum(q.astype(F32) * knr, axis=-1, keepdims=True)
    sink = sink_ref[:, :, 0:1]
    m = jnp.maximum(jnp.maximum(jnp.max(s_c, axis=-1, keepdims=True), s_n), sink)
    e_c = jnp.exp(s_c - m)
    e_n = jnp.exp(s_n - m)
    denom = jnp.sum(e_c, axis=-1, keepdims=True) + e_n + jnp.exp(sink - m)
    p_c = (e_c / denom).astype(BF16)
    p_n = (e_n / denom).astype(BF16).astype(F32)
    o = lax.dot_general(p_c, cv.astype(BF16), (((2,), (2,)), ((0,), (0,))),
                        preferred_element_type=F32)
    o_ref[...] = o + p_n * vnr
    lane = lax.broadcasted_iota(jnp.int32, (hd, l), 1)
    for b in range(nb):
        for g in range(SWA_KV_HEADS):
            rows = slice(g * hd, (g + 1) * hd)
            ko_ref[b, g] = jnp.where(lane == l - 1, knt_ref[0, rows, b:b + 1],
                                     pltpu.roll(ck_ref[b, g], l - 1, 1))
            vo_ref[b, g] = jnp.where(lane == l - 1, vnt_ref[0, rows, b:b + 1],
                                     pltpu.roll(cv_ref[b, g], l - 1, 1))


def _swa_sample_call(sink_t, q8, knr, vnr, knt, vnt, ck, cv, nb):
    n, g, hd, l = ck.shape
    rows = nb * g
    blk3 = lambda i: (i, 0, 0)
    blk4 = lambda i: (i, 0, 0, 0)
    return pl.pallas_call(
        _swa_sample_kernel,
        grid=(n // nb,),
        in_specs=[
            pl.BlockSpec((rows, SUBLANES, LANES), lambda i: (0, 0, 0)),
            pl.BlockSpec((rows, SUBLANES, hd), blk3),
            pl.BlockSpec((rows, 1, hd), blk3),
            pl.BlockSpec((rows, 1, hd), blk3),
            pl.BlockSpec((1, g * hd, nb), blk3),
            pl.BlockSpec((1, g * hd, nb), blk3),
            pl.BlockSpec((nb, g, hd, l), blk4),
            pl.BlockSpec((nb, g, hd, l), blk4),
        ],
        out_specs=(
            pl.BlockSpec((rows, SUBLANES, hd), blk3),
            pl.BlockSpec((nb, g, hd, l), blk4),
            pl.BlockSpec((nb, g, hd, l), blk4),
        ),
        out_shape=(
            jax.ShapeDtypeStruct((n * g, SUBLANES, hd), F32),
            jax.ShapeDtypeStruct(ck.shape, F32),
            jax.ShapeDtypeStruct(cv.shape, F32),
        ),
        compiler_params=pltpu.CompilerParams(
            dimension_semantics=("arbitrary",), vmem_limit_bytes=VMEM_LIMIT),
        name="swa_sample",
    )(sink_t, q8, knr, vnr, knt, vnt, ck, cv)


def _tail_mix(x, y_rwkv, y_swa, g1, sh2, sc2, w_out, n_post_mix, n_pre_ffn):
    mix = _dot(y_rwkv, w_out[:RWKV_WIDTH]) + _dot(y_swa, w_out[RWKV_WIDTH:])
    x1 = x + g1 * _rmsnorm(mix, n_post_mix)
    h2 = (_rmsnorm(x1, n_pre_ffn) * (1.0 + sc2) + sh2).astype(BF16)
    return x1, h2


def _tail_prompt_kernel(x_ref, yr_ref, ys_ref, mod_ref, w_out, n_post_mix, n_pre_ffn,
                        n_post_ffn, w_up_a, w_up_b, w_down, cw_ref, cb_ref,
                        y_ref, cp_ref, carry_ref, act_ref):
    t = pl.program_id(1)
    tm = x_ref.shape[1]
    sub = min(tm, TAIL_SUB)
    grp = min(tm, TAIL_GROUP)

    @pl.when(t == 0)
    def _():
        carry_ref[...] = jnp.zeros_like(carry_ref)

    g1, sh2, sc2, g2 = mod_ref[2, 0], mod_ref[3, 0], mod_ref[4, 0], mod_ref[5, 0]

    def cols_of(c, half):
        return slice(half * D_FF + c * FF_CHUNK, half * D_FF + (c + 1) * FF_CHUNK)

    def conv(u, cols):
        ext = jnp.concatenate([carry_ref[:, cols], u], axis=0)
        last = u[sub - SUBLANES:]
        carry_ref[:, cols] = last
        cp_ref[0, :, cols] = last
        return (cb_ref[:, cols] + pltpu.roll(ext, 2, 0)[SUBLANES:] * cw_ref[0:1, cols]
                + pltpu.roll(ext, 1, 0)[SUBLANES:] * cw_ref[1:2, cols] + u * cw_ref[2:3, cols])

    def group(gi, carry):
        base = pl.multiple_of(gi * grp, grp)
        blocks = [(pl.ds(base + j * sub, sub), slice(j * sub, (j + 1) * sub))
                  for j in range(grp // sub)]
        mixes = [_dot(yr_ref[0, rb, :], w_out[:RWKV_WIDTH]) + _dot(ys_ref[0, rb, :], w_out[RWKV_WIDTH:])
                 for rb, _ in blocks]
        for (rb, ab), mix in zip(blocks, mixes):
            x1 = x_ref[0, rb, :] + g1 * _rmsnorm(mix, n_post_mix[...])
            h2 = (_rmsnorm(x1, n_pre_ffn[...]) * (1.0 + sc2) + sh2).astype(BF16)

            def up(c):
                chunk = slice(c * FF_CHUNK, (c + 1) * FF_CHUNK)
                return [jnp.dot(h2, w[:, chunk], preferred_element_type=F32)
                        for w in (w_up_a, w_up_b)]

            def down(c):
                rows = slice(c * FF_CHUNK, (c + 1) * FF_CHUNK)
                return jnp.dot(act_ref[ab, rows], w_down[rows, :], preferred_element_type=F32)

            u_next = up(0)
            ff = None
            for c in range(N_FF_CHUNKS):
                u_cur = u_next
                if c + 1 < N_FF_CHUNKS:
                    u_next = up(c + 1)
                if c >= 1:
                    part = down(c - 1)
                    ff = part if ff is None else ff + part
                za, zb = [conv(u_cur[half], cols_of(c, half)) for half in range(2)]
                act_ref[ab, c * FF_CHUNK:(c + 1) * FF_CHUNK] = (_silu(za) * zb).astype(BF16)
            ff = ff + down(N_FF_CHUNKS - 1)
            y_ref[0, rb, :] = x1 + g2 * _rmsnorm(ff, n_post_ffn[...])
        return carry

    lax.fori_loop(0, tm // grp, group, 0)


def _tail_sample_kernel(x_ref, o_ref, feat_ref, ys_ref, mod_ref, ln_w, ln_b, w_out, n_post_mix,
                        n_pre_ffn, n_post_ffn, wa_ref, wb_ref, wd_ref, p0a, p0b, p1a, p1b,
                        cwa, cwb, cba, cbb,
                        y_ref, ua_ref, ub_ref, w_out_b_ref, wa_b_ref, wb_b_ref, wd_b_ref,
                        x1_ref, h2_ref, acc_ref):
    c = pl.program_id(0)

    @pl.when(c == 0)
    def _():
        w_out_b = w_out[...].astype(BF16)
        w_out_b_ref[...] = w_out_b
        seg = _seg_ones(RWKV_WIDTH)
        y_rwkv = _gn_epilogue(o_ref[...].T, feat_ref[7], feat_ref[6], ln_w[...], ln_b[...], seg)
        x1, h2 = _tail_mix(x_ref[...], y_rwkv, ys_ref[...], mod_ref[2], mod_ref[3], mod_ref[4],
                           w_out_b, n_post_mix[...], n_pre_ffn[...])
        x1_ref[...] = x1
        h2_ref[...] = h2
        acc_ref[...] = jnp.zeros_like(acc_ref)

    wa = wa_ref[...].astype(BF16)
    wb = wb_ref[...].astype(BF16)
    wd = wd_ref[...].astype(BF16)
    wa_b_ref[...] = wa
    wb_b_ref[...] = wb
    wd_b_ref[...] = wd
    h2 = h2_ref[...]
    ua = jnp.dot(h2, wa, preferred_element_type=F32)
    ub = jnp.dot(h2, wb, preferred_element_type=F32)
    ua_ref[...] = ua
    ub_ref[...] = ub
    za = cba[...] + p0a[...] * cwa[0:1, :] + p1a[...] * cwa[1:2, :] + ua * cwa[2:3, :]
    zb = cbb[...] + p0b[...] * cwb[0:1, :] + p1b[...] * cwb[1:2, :] + ub * cwb[2:3, :]
    acc_ref[...] += _dot(_silu(za) * zb, wd)

    @pl.when(c == pl.num_programs(0) - 1)
    def _():
        y_ref[...] = x1_ref[...] + mod_ref[5] * _rmsnorm(acc_ref[...], n_post_ffn[...])


def _tail_prompt_call(x, y_rwkv, y_swa, mod_p, consts, tm):
    b, t, _ = x.shape
    tok = lambda i, j: (i, j, 0)
    in_specs = [
        pl.BlockSpec((1, tm, D_MODEL), tok),
        pl.BlockSpec((1, tm, RWKV_WIDTH), tok),
        pl.BlockSpec((1, tm, SWA_WIDTH), tok),
        pl.BlockSpec((6, 1, 1, D_MODEL), lambda i, j: (0, i, 0, 0)),
    ] + [_const_spec(c, 2) for c in consts]
    return pl.pallas_call(
        _tail_prompt_kernel,
        grid=(b, t // tm),
        in_specs=in_specs,
        out_specs=(
            pl.BlockSpec((1, tm, D_MODEL), tok),
            pl.BlockSpec((1, SUBLANES, 2 * D_FF), lambda i, j: (i, 0, 0)),
        ),
        out_shape=(
            jax.ShapeDtypeStruct((b, t, D_MODEL), F32),
            jax.ShapeDtypeStruct((b, SUBLANES, 2 * D_FF), F32),
        ),
        scratch_shapes=[
            pltpu.VMEM((SUBLANES, 2 * D_FF), F32),
            pltpu.VMEM((min(tm, TAIL_GROUP), D_FF), BF16),
        ],
        compiler_params=pltpu.CompilerParams(
            dimension_semantics=("arbitrary", "arbitrary"), vmem_limit_bytes=VMEM_LIMIT),
        name="tail_prompt",
    )(x, y_rwkv, y_swa, mod_p, *consts)


def _tail_sample_call(x, o_t, feat_s, y_swa, mod, p0, p1, ln_w, ln_b, w_out, n_post_mix,
                      n_pre_ffn, n_post_ffn, w_up, w_down, cw, cb):
    n = x.shape[0]
    nc = N_FF_CHUNKS
    fc = FF_CHUNK
    whole = lambda a: pl.BlockSpec(a.shape, lambda c, nd=a.ndim: (0,) * nd)
    gate = lambda rows: pl.BlockSpec((rows, fc), lambda c: (0, c))
    value = lambda rows: pl.BlockSpec((rows, fc), lambda c: (0, nc + c))
    in_specs = [
        whole(x), whole(o_t), whole(feat_s), whole(y_swa),
        pl.BlockSpec((mod.shape[0], n, D_MODEL), lambda c: (0, 0, 0)),
        whole(ln_w), whole(ln_b), whole(w_out), whole(n_post_mix), whole(n_pre_ffn),
        whole(n_post_ffn),
        gate(D_MODEL), value(D_MODEL), pl.BlockSpec((fc, D_MODEL), lambda c: (c, 0)),
        gate(n), value(n), gate(n), value(n), gate(3), value(3), gate(1), value(1),
    ]
    out_shape = (
        jax.ShapeDtypeStruct((n, D_MODEL), F32),
        jax.ShapeDtypeStruct((n, D_FF), F32),
        jax.ShapeDtypeStruct((n, D_FF), F32),
        jax.ShapeDtypeStruct((D_MODEL, D_MODEL), BF16),
        jax.ShapeDtypeStruct((D_MODEL, D_FF), BF16),
        jax.ShapeDtypeStruct((D_MODEL, D_FF), BF16),
        jax.ShapeDtypeStruct((D_FF, D_MODEL), BF16),
    )
    chunk_cols = lambda rows: pl.BlockSpec((rows, fc), lambda c: (0, c))
    out_specs = (
        pl.BlockSpec((n, D_MODEL), lambda c: (0, 0)),
        chunk_cols(n), chunk_cols(n),
        pl.BlockSpec((D_MODEL, D_MODEL), lambda c: (0, 0)),
        chunk_cols(D_MODEL), chunk_cols(D_MODEL),
        pl.BlockSpec((fc, D_MODEL), lambda c: (c, 0)),
    )
    return pl.pallas_call(
        _tail_sample_kernel,
        grid=(nc,),
        in_specs=in_specs,
        out_specs=out_specs,
        out_shape=out_shape,
        scratch_shapes=[
            pltpu.VMEM((n, D_MODEL), F32),
            pltpu.VMEM((n, D_MODEL), BF16),
            pltpu.VMEM((n, D_MODEL), F32),
        ],
        compiler_params=pltpu.CompilerParams(
            dimension_semantics=("arbitrary",), vmem_limit_bytes=VMEM_LIMIT),
        name="tail_sample",
    )(x, o_t, feat_s, y_swa, mod, ln_w, ln_b, w_out, n_post_mix, n_pre_ffn, n_post_ffn,
      w_up, w_up, w_down, p0, p0, p1, p1, cw, cw, cb, cb)


def _rope_tables(pos):
    half = HEAD_DIM // 2
    inv = ROPE_THETA ** (-jnp.arange(half, dtype=F32) / half)
    ang = pos.astype(F32)[:, None] * inv[None, :]
    cos, sin = jnp.cos(ang), jnp.sin(ang)
    cos_h = jnp.concatenate([cos, cos], axis=-1)
    sin_h = jnp.concatenate([-sin, sin], axis=-1)
    return jnp.tile(cos_h, (1, LANES // HEAD_DIM)), jnp.tile(sin_h, (1, LANES // HEAD_DIM))


def kernel(x_prompt, x_sample, state_rwkv_wkv, state_rwkv_shift, cache_swa_k, cache_swa_v,
           state_ffn_conv, c_prompt, c_sample, w_ada, b_ada, norm_pre_mix, norm_post_mix,
           norm_pre_ffn, norm_post_ffn, w_in, rwkv_mu, rwkv_w0, rwkv_w_up, rwkv_a0, rwkv_a_up,
           rwkv_g_up, rwkv_k_k, rwkv_k_a, rwkv_r_k, rwkv_ln_w, rwkv_ln_b, swa_sinks, w_out,
           ffn_w_up, ffn_conv_w, ffn_conv_b, ffn_w_down):
    depth = w_ada.shape[0]
    assert depth == 1 and x_sample.shape[1] == 1
    b, t, _ = x_prompt.shape
    n = x_sample.shape[0]
    tm = min(256, t)
    blk = min(4 * tm, t)
    assert t % blk == 0 and tm % WINDOW == 0 and tm % CHUNK == 0 and n % SUBLANES == 0
    nb_swa = SUBLANES
    li = 0

    row = lambda v: v.reshape(1, -1)
    zeros_l = jnp.zeros((64, RWKV_WIDTH), F32)
    wa_up = jnp.concatenate([
        jnp.concatenate([rwkv_w_up[li], zeros_l], axis=1),
        jnp.concatenate([zeros_l, rwkv_a_up[li]], axis=1)], axis=0).astype(BF16)
    hid = jnp.arange(GROUP_LANES) // HEAD_DIM
    seg_blk = (hid[:, None] == hid[None, :]).astype(BF16)
    proj_consts = (row(norm_pre_mix[li]), w_in[li], row(rwkv_mu[li]), row(rwkv_w0[li]), wa_up,
                   row(rwkv_a0[li]), rwkv_g_up[li].astype(BF16), row(rwkv_k_k[li]),
                   row(rwkv_k_a[li]), row(rwkv_r_k[li]), seg_blk)
    norms = (row(norm_post_mix[li]), row(norm_pre_ffn[li]), row(norm_post_ffn[li]))
    cw, cb = ffn_conv_w[li], row(ffn_conv_b[li])
    ln_w, ln_b = row(rwkv_ln_w[li]), row(rwkv_ln_b[li])

    mod = _ada_call(jnp.concatenate([c_sample, c_prompt], axis=0), w_ada[li], row(b_ada[li]))
    mod_p = mod[:, n:].reshape(6, b, 1, D_MODEL)

    cos_s, sin_s = _rope_tables(jnp.full((1,), PAST_LEN, jnp.int32))
    feat_s, ft_s, q_s, kn_s, vn_s, p_s, w_in_b = _proj_sample_call(
        x_sample[:, 0], mod, cos_s, sin_s, state_rwkv_shift[li], proj_consts)

    state_t = jnp.transpose(state_rwkv_wkv[li], (1, 2, 3, 0))
    wkv_t, o_t = _wkv_sample_call(ft_s, state_t)
    wkv_s = jnp.transpose(wkv_t, (3, 0, 1, 2))

    g2 = SWA_KV_HEADS
    q4 = q_s.reshape(n * g2, SWA_GROUP, HEAD_DIM)
    q8 = jnp.concatenate([q4, jnp.zeros_like(q4)], axis=1)
    sink_t = jnp.broadcast_to(
        jnp.concatenate([swa_sinks[li].reshape(g2, SWA_GROUP),
                         jnp.full((g2, SWA_GROUP), MASK_VALUE, F32)], axis=1)[None, :, :, None],
        (nb_swa, g2, SUBLANES, LANES)).reshape(nb_swa * g2, SUBLANES, LANES)
    cols = lambda a: a.reshape(n // nb_swa, nb_swa, KV_WIDTH).transpose(0, 2, 1)
    ck = jnp.transpose(cache_swa_k[li], (0, 2, 3, 1))
    cv = jnp.transpose(cache_swa_v[li], (0, 2, 3, 1))
    o_att, k_t, v_t = _swa_sample_call(
        sink_t, q8, kn_s.reshape(n * g2, 1, HEAD_DIM), vn_s.reshape(n * g2, 1, HEAD_DIM),
        cols(kn_s), cols(vn_s), ck, cv, nb_swa)
    y_swa_s = o_att[:, :SWA_GROUP].reshape(n, SWA_WIDTH)
    k_s = jnp.transpose(k_t, (0, 3, 1, 2))
    v_s = jnp.transpose(v_t, (0, 3, 1, 2))

    conv0 = state_ffn_conv[li]
    y_s, ua_s, ub_s, w_out_b, w_up_a_b, w_up_b_b, w_down_b = _tail_sample_call(
        x_sample[:, 0], o_t, feat_s, y_swa_s.astype(BF16), mod, conv0[:, 0], conv0[:, 1],
        ln_w, ln_b, w_out[li], *norms, ffn_w_up[li], ffn_w_down[li], cw, cb)
    conv_s = jnp.stack([conv0[:, 1], jnp.concatenate([ua_s, ub_s], axis=1)], axis=1)

    cos_p, sin_p = _rope_tables(jnp.arange(t, dtype=jnp.int32))
    proj_consts_p = proj_consts[:1] + (w_in_b,) + proj_consts[2:]
    y_rwkv_p, s_cat, q_p, kx_p, vx_p, klast, vlast, plast = _mix_prompt_call(
        x_prompt, mod_p, cos_p, sin_p, proj_consts_p + (ln_w, ln_b), blk)
    y_swa_p = _swa_prompt_call(swa_sinks[li], q_p, kx_p, vx_p, blk)
    tail_consts = (w_out_b,) + norms + (w_up_a_b, w_up_b_b, w_down_b, cw, cb)
    y_p, cp = _tail_prompt_call(x_prompt, y_rwkv_p, y_swa_p, mod_p, tail_consts, blk)

    wkv_p = s_cat.reshape(b, HEAD_DIM, RWKV_HEADS, HEAD_DIM).transpose(0, 2, 1, 3)
    shift_p = plast[:, SUBLANES - 1]
    k_p = klast.reshape(b, WINDOW, SWA_KV_HEADS, HEAD_DIM)
    v_p = vlast.reshape(b, WINDOW, SWA_KV_HEADS, HEAD_DIM)
    conv_p = cp[:, SUBLANES - 2:]

    expand = lambda a: a[None]
    return (y_p, y_s[:, None, :], expand(wkv_p), expand(shift_p), expand(k_p), expand(v_p),
            expand(conv_p), expand(wkv_s), expand(p_s),
            expand(k_s), expand(v_s), expand(conv_s))
```

```python
import math

import jax
import jax.numpy as jnp
from jax import lax
from jax.experimental import pallas as pl
from jax.experimental.pallas import tpu as pltpu

D_MODEL = 1024
HEAD_DIM = 64
RWKV_WIDTH = 512
RWKV_HEADS = 8
RWKV_COLS = 1792
RWKV_GN_EPS = 64e-5
SWA_WIDTH = 512
SWA_HEADS = 8
SWA_KV_HEADS = 2
SWA_GROUP = 4
KV_WIDTH = SWA_KV_HEADS * HEAD_DIM
WINDOW = 128
PAST_LEN = 16384
ROPE_THETA = 10000.0
ATTN_SCALE = HEAD_DIM ** -0.5
D_FF = 2816
NORM_EPS = 1e-6
MASK_VALUE = -1e30
PROJ_COLS = RWKV_COLS + SWA_WIDTH + 2 * KV_WIDTH

LANES = 128
SUBLANES = 8
CHUNK = 64
GROUP_LANES = 256
FF_CHUNK = 256
N_FF_CHUNKS = D_FF // FF_CHUNK
TAIL_SUB = 256
MIX_SUB = 512
TAIL_GROUP = 512
VMEM_LIMIT = 56 * 1024 * 1024

F32 = jnp.float32
BF16 = jnp.bfloat16


def _sigmoid(x):
    return 1.0 / (1.0 + jnp.exp(-x))


def _silu(x):
    return x * _sigmoid(x)


def _rmsnorm(x, g):
    return x * lax.rsqrt(jnp.mean(x * x, axis=-1, keepdims=True) + NORM_EPS) * g


def _dot(a, b):
    return jnp.dot(a.astype(BF16), b.astype(BF16), preferred_element_type=F32)


def _dot_nt(a, b):
    return lax.dot_general(a.astype(BF16), b.astype(BF16), (((1,), (1,)), ((), ())),
                           preferred_element_type=F32)


def _dot_tn(a, b):
    return lax.dot_general(a.astype(BF16), b.astype(BF16), (((0,), (0,)), ((), ())),
                           preferred_element_type=F32)


def _every(fn, *lists):
    return [fn(*a) for a in zip(*lists)]


def _swap_halves(x):
    w = x.shape[-1]
    lane = lax.broadcasted_iota(jnp.int32, x.shape, x.ndim - 1)
    lo = (lane & (HEAD_DIM // 2)) == 0
    return jnp.where(lo, pltpu.roll(x, w - HEAD_DIM // 2, x.ndim - 1),
                     pltpu.roll(x, HEAD_DIM // 2, x.ndim - 1))


def _rope(x, cos, sin):
    reps = x.shape[-1] // LANES
    cos_w = jnp.concatenate([cos] * reps, axis=-1) if reps > 1 else cos
    sin_w = jnp.concatenate([sin] * reps, axis=-1) if reps > 1 else sin
    return x * cos_w + _swap_halves(x) * sin_w


def _proj_features(x, shift, scale, g_pre, w_in, prev_fn, mu, w0, wa_up, a0, g_up,
                   k_k, k_a, r_k, seg_blk, cos, sin):
    h = _rmsnorm(x, g_pre) * (1.0 + scale) + shift
    p = _dot(h, w_in)
    p_rwkv = p[:, :RWKV_COLS]
    prev = prev_fn(p_rwkv)
    xm = p_rwkv + (prev - p_rwkv) * mu
    r = xm[:, 0:512]
    k = xm[:, 512:1024]
    v = xm[:, 1024:1536]
    wa = xm[:, 1536:1664]
    gd = xm[:, 1664:1792]
    lane = lax.broadcasted_iota(jnp.int32, wa.shape, 1)
    wa_act = jnp.where(lane < 64, jnp.tanh(wa), wa)
    lora = _dot(wa_act, wa_up)
    lw = -math.exp(-0.5) * _sigmoid(w0 + lora[:, :512])
    a = _sigmoid(a0 + lora[:, 512:])
    g = _dot(_sigmoid(gd), g_up)
    kk = k * k_k
    kf = k * (1.0 + (a - 1.0) * k_a)
    gl = seg_blk.shape[0]
    n = x.shape[0]
    sums = [_dot(jnp.concatenate([(kk * kk)[:, i:i + gl], (r * kf * r_k)[:, i:i + gl]], axis=0),
                 seg_blk) for i in range(0, RWKV_WIDTH, gl)]
    ss = jnp.concatenate([s_[:n] for s_ in sums], axis=1)
    kk = kk / jnp.maximum(jnp.sqrt(ss), 1e-12)
    bonus = jnp.concatenate([s_[n:] for s_ in sums], axis=1) * v
    feats = (r, lw, kf, v, -kk, kk * a, g, bonus)
    q = _rope(p[:, RWKV_COLS:RWKV_COLS + SWA_WIDTH], cos, sin)
    ks = _rope(p[:, RWKV_COLS + SWA_WIDTH:RWKV_COLS + SWA_WIDTH + KV_WIDTH], cos, sin)
    vs = p[:, RWKV_COLS + SWA_WIDTH + KV_WIDTH:]
    return feats, q, ks, vs, p_rwkv


def _gn_epilogue(o, bonus, g, ln_w, ln_b, seg):
    mu = _dot(o, seg) * (1.0 / HEAD_DIM)
    d = o - mu
    var = _dot(d * d, seg) * (1.0 / HEAD_DIM)
    gn = d * lax.rsqrt(var + RWKV_GN_EPS) * ln_w + ln_b
    return (gn + bonus) * g


def _seg_ones(n):
    r = lax.broadcasted_iota(jnp.int32, (n, n), 0) // HEAD_DIM
    c = lax.broadcasted_iota(jnp.int32, (n, n), 1) // HEAD_DIM
    return (r == c).astype(BF16)


def _ada_kernel(c_ref, w_ref, b_ref, o_ref):
    o_ref[0] = _dot(_silu(c_ref[...]), w_ref[...]) + b_ref[...]


def _ada_call(c_all, w_ada, b_ada):
    rows = c_all.shape[0]
    return pl.pallas_call(
        _ada_kernel,
        grid=(6,),
        in_specs=[
            pl.BlockSpec((rows, D_MODEL), lambda j: (0, 0)),
            pl.BlockSpec((D_MODEL, D_MODEL), lambda j: (0, j)),
            pl.BlockSpec((1, D_MODEL), lambda j: (0, j)),
        ],
        out_specs=pl.BlockSpec((1, rows, D_MODEL), lambda j: (j, 0, 0)),
        out_shape=jax.ShapeDtypeStruct((6, rows, D_MODEL), F32),
        compiler_params=pltpu.CompilerParams(
            dimension_semantics=("arbitrary",), vmem_limit_bytes=VMEM_LIMIT),
        name="ada",
    )(c_all, w_ada, b_ada)


def _expand_kv(x):
    lane = lax.broadcasted_iota(jnp.int32, x.shape, 1)
    rolled = pltpu.roll(x, HEAD_DIM, 1)
    g0 = jnp.where(lane < HEAD_DIM, x, rolled)
    g1 = jnp.where(lane < HEAD_DIM, rolled, x)
    return jnp.concatenate([g0, g0, g1, g1], axis=1)


def _mix_prompt_kernel(x_ref, mod_ref, cos_ref, sin_ref, g_pre, w_in, mu, w0, wa_up, a0,
                       g_up, k_k, k_a, r_k, seg_blk, ln_w, ln_b,
                       y_ref, s_out_ref, q_ref, k_ref, v_ref, klast_ref, vlast_ref, plast_ref,
                       carry_ref, s_ref):
    t = pl.program_id(1)

    @pl.when(t == 0)
    def _():
        carry_ref[...] = jnp.zeros_like(carry_ref)
        s_ref[...] = jnp.zeros_like(s_ref)

    sub = min(x_ref.shape[1], MIX_SUB)

    def sub_tile(i, carry):
        rows = pl.ds(pl.multiple_of(i * sub, sub), sub)
        carry_row = carry_ref[SUBLANES - 1:SUBLANES, :]

        def prev_fn(p_rwkv):
            row = lax.broadcasted_iota(jnp.int32, p_rwkv.shape, 0)
            return jnp.where(row == 0, carry_row, pltpu.roll(p_rwkv, 1, 0))

        feats, q, ks, vs, p_rwkv = _proj_features(
            x_ref[0, rows, :], mod_ref[0, 0], mod_ref[1, 0], g_pre[...], w_in[...], prev_fn,
            mu[...], w0[...], wa_up[...], a0[...], g_up[...], k_k[...], k_a[...], r_k[...],
            seg_blk[...], cos_ref[rows, :], sin_ref[rows, :])
        q_ref[0, rows, :] = (q * ATTN_SCALE).astype(BF16)
        k_ref[0, rows, :] = _expand_kv(ks).astype(BF16)
        v_ref[0, rows, :] = _expand_kv(vs).astype(BF16)
        last = p_rwkv[sub - SUBLANES:, :]
        carry_ref[...] = last
        plast_ref[0] = last
        klast_ref[0] = ks[sub - WINDOW:, :]
        vlast_ref[0] = vs[sub - WINDOW:, :]
        _wkv_tile(feats, ln_w, ln_b, y_ref.at[0, rows, :], s_ref)
        return carry

    lax.fori_loop(0, x_ref.shape[1] // sub, sub_tile, 0)

    @pl.when(t == pl.num_programs(1) - 1)
    def _():
        for g in range(RWKV_WIDTH // GROUP_LANES):
            s_out_ref[0, :, g * GROUP_LANES:(g + 1) * GROUP_LANES] = s_ref[g]


def _proj_sample_kernel(x_ref, mod_ref, cos_ref, sin_ref, prev_ref, g_pre, w_in, mu, w0,
                        wa_up, a0, g_up, k_k, k_a, r_k, seg_blk,
                        feat_ref, ft_ref, q_ref, k_ref, v_ref, p_ref, w_in_b_ref):
    w_in_b = w_in[...].astype(BF16)
    w_in_b_ref[...] = w_in_b
    feats, q, ks, vs, p_rwkv = _proj_features(
        x_ref[...], mod_ref[0], mod_ref[1], g_pre[...], w_in_b, lambda p: prev_ref[...],
        mu[...], w0[...], wa_up[...], a0[...], g_up[...], k_k[...], k_a[...], r_k[...],
        seg_blk[...], cos_ref[...], sin_ref[...])
    for i, f in enumerate(feats):
        feat_ref[i] = f
    for i in range(6):
        ft_ref[i] = feats[i].T
    q_ref[...] = (q * ATTN_SCALE).astype(BF16)
    k_ref[...] = ks
    v_ref[...] = vs
    p_ref[...] = p_rwkv


def _const_spec(arr, grid_rank):
    zeros = (0,) * arr.ndim
    if grid_rank == 1:
        return pl.BlockSpec(arr.shape, lambda i: zeros)
    return pl.BlockSpec(arr.shape, lambda i, j: zeros)


def _mix_prompt_call(x, mod_p, cos, sin, consts, tm):
    b, t, _ = x.shape
    nt = t // tm
    in_specs = [
        pl.BlockSpec((1, tm, D_MODEL), lambda i, j: (i, j, 0)),
        pl.BlockSpec((6, 1, 1, D_MODEL), lambda i, j: (0, i, 0, 0)),
        pl.BlockSpec((tm, LANES), lambda i, j: (j, 0)),
        pl.BlockSpec((tm, LANES), lambda i, j: (j, 0)),
    ] + [_const_spec(c, 2) for c in consts]
    out_shape = (
        jax.ShapeDtypeStruct((b, t, RWKV_WIDTH), BF16),
        jax.ShapeDtypeStruct((b, HEAD_DIM, RWKV_WIDTH), F32),
        jax.ShapeDtypeStruct((b, t, SWA_WIDTH), BF16),
        jax.ShapeDtypeStruct((b, t, SWA_WIDTH), BF16),
        jax.ShapeDtypeStruct((b, t, SWA_WIDTH), BF16),
        jax.ShapeDtypeStruct((b, WINDOW, KV_WIDTH), F32),
        jax.ShapeDtypeStruct((b, WINDOW, KV_WIDTH), F32),
        jax.ShapeDtypeStruct((b, SUBLANES, RWKV_COLS), F32),
    )
    out_specs = (
        pl.BlockSpec((1, tm, RWKV_WIDTH), lambda i, j: (i, j, 0)),
        pl.BlockSpec((1, HEAD_DIM, RWKV_WIDTH), lambda i, j: (i, 0, 0)),
        pl.BlockSpec((1, tm, SWA_WIDTH), lambda i, j: (i, j, 0)),
        pl.BlockSpec((1, tm, SWA_WIDTH), lambda i, j: (i, j, 0)),
        pl.BlockSpec((1, tm, SWA_WIDTH), lambda i, j: (i, j, 0)),
        pl.BlockSpec((1, WINDOW, KV_WIDTH), lambda i, j: (i, 0, 0)),
        pl.BlockSpec((1, WINDOW, KV_WIDTH), lambda i, j: (i, 0, 0)),
        pl.BlockSpec((1, SUBLANES, RWKV_COLS), lambda i, j: (i, 0, 0)),
    )
    return pl.pallas_call(
        _mix_prompt_kernel,
        grid=(b, nt),
        in_specs=in_specs,
        out_specs=out_specs,
        out_shape=out_shape,
        scratch_shapes=[
            pltpu.VMEM((SUBLANES, RWKV_COLS), F32),
            pltpu.VMEM((RWKV_WIDTH // GROUP_LANES, HEAD_DIM, GROUP_LANES), F32),
        ],
        compiler_params=pltpu.CompilerParams(
            dimension_semantics=("arbitrary", "arbitrary"), vmem_limit_bytes=VMEM_LIMIT),
        name="mix_prompt",
    )(x, mod_p, cos, sin, *consts)


def _mod_rows_spec(mod, n):
    return pl.BlockSpec((mod.shape[0], n, mod.shape[2]), lambda i: (0, 0, 0))


def _proj_sample_call(x, mod, cos, sin, prev, consts):
    n = x.shape[0]
    args = (x, mod, cos, sin, prev) + tuple(consts)
    out_shape = (
        jax.ShapeDtypeStruct((8, n, RWKV_WIDTH), F32),
        jax.ShapeDtypeStruct((6, RWKV_WIDTH, n), F32),
        jax.ShapeDtypeStruct((n, SWA_WIDTH), BF16),
        jax.ShapeDtypeStruct((n, KV_WIDTH), F32),
        jax.ShapeDtypeStruct((n, KV_WIDTH), F32),
        jax.ShapeDtypeStruct((n, RWKV_COLS), F32),
        jax.ShapeDtypeStruct((D_MODEL, PROJ_COLS), BF16),
    )
    return pl.pallas_call(
        _proj_sample_kernel,
        grid=(1,),
        in_specs=[_mod_rows_spec(a, n) if i == 1 else _const_spec(a, 1)
                  for i, a in enumerate(args)],
        out_specs=tuple(pl.BlockSpec(s.shape, lambda i, nd=len(s.shape): (0,) * nd)
                        for s in out_shape),
        out_shape=out_shape,
        compiler_params=pltpu.CompilerParams(
            dimension_semantics=("arbitrary",), vmem_limit_bytes=VMEM_LIMIT),
        name="proj_sample",
    )(*args)


def _wkv_tile(feats, ln_w, ln_b, y_ref, s_ref):
    tt = feats[0].shape[0]
    n_chunks = tt // CHUNK
    gl = GROUP_LANES
    n_groups = RWKV_WIDTH // gl
    heads_per_group = gl // HEAD_DIM
    probs = [(c, g) for c in range(n_chunks) for g in range(n_groups)]

    row_c = lax.broadcasted_iota(jnp.int32, (CHUNK, gl), 0)
    col_c = lax.broadcasted_iota(jnp.int32, (CHUNK, gl), 1) % CHUNK
    strict = row_c > col_c
    incl = row_c >= col_c
    eye_cat = (row_c == col_c).astype(F32)
    rb = lax.broadcasted_iota(jnp.int32, (gl, gl), 0) // HEAD_DIM
    cb = lax.broadcasted_iota(jnp.int32, (gl, gl), 1) // HEAD_DIM
    bd_mask = rb == cb
    tri_r = lax.broadcasted_iota(jnp.int32, (CHUNK, CHUNK), 0)
    tri_c = lax.broadcasted_iota(jnp.int32, (CHUNK, CHUNK), 1)
    tril_ones = (tri_r >= tri_c).astype(BF16)
    seg = bd_mask.astype(BF16)

    def bd(x):
        xb = x.astype(BF16)
        return jnp.where(bd_mask, jnp.concatenate([xb] * heads_per_group, axis=0),
                         jnp.zeros((), BF16))

    def fold(x):
        xm = jnp.where(bd_mask, x, 0.0)
        acc = xm[0:HEAD_DIM]
        for hh in range(1, heads_per_group):
            acc = acc + xm[hh * HEAD_DIM:(hh + 1) * HEAD_DIM]
        return acc

    def ld(i, p):
        c, g = p
        return feats[i][c * CHUNK:(c + 1) * CHUNK, g * gl:(g + 1) * gl]

    lw = [ld(1, p) for p in probs]
    na = [ld(4, p) for p in probs]
    bb = [ld(5, p) for p in probs]
    kf = [ld(2, p) for p in probs]
    r = [ld(0, p) for p in probs]
    v = [ld(3, p) for p in probs]

    def cumsum(x):
        hi = x.astype(BF16)
        lo = (x - hi.astype(F32)).astype(BF16)
        both = jnp.dot(tril_ones, jnp.concatenate([hi, lo], axis=1), preferred_element_type=F32)
        return both[:, :gl] + both[:, gl:]

    cum = _every(cumsum, lw)
    cum_last = [x[CHUNK - 1:CHUNK, :] for x in cum]
    e_out = [jnp.exp(-x) for x in cum]
    e_end = _every(lambda cl, x: jnp.exp(cl - x), cum_last, cum)
    a_t = _every(lambda n_, x, l_: n_ * jnp.exp(x - l_), na, cum, lw)
    r_t = _every(lambda r_, x: r_ * jnp.exp(x), r, cum)
    b_t = _every(lambda b_, e: b_ * e, bb, e_out)
    k_t = _every(lambda k_, e: k_ * e, kf, e_out)
    b_end = _every(lambda b_, e: b_ * e, bb, e_end)
    k_end = _every(lambda k_, e: k_ * e, kf, e_end)
    gamma = [jnp.exp(x) for x in cum_last]

    ar = _every(lambda a_, r_: jnp.concatenate([a_, r_], axis=0), a_t, r_t)
    pb = _every(lambda x, y: _dot_nt(x, bd(y)), ar, b_t)
    pk = _every(lambda x, y: _dot_nt(x, bd(y)), ar, k_t)
    l_ab = [jnp.where(strict, x[:CHUNK], 0.0) for x in pb]
    l_ak = [jnp.where(strict, x[:CHUNK], 0.0) for x in pk]
    m_rb = [jnp.where(incl, x[CHUNK:], 0.0) for x in pb]
    m_rk = [jnp.where(incl, x[CHUNK:], 0.0) for x in pk]

    x_acc = [eye_cat + l for l in l_ab]
    pw = _every(lambda l: _dot(l, bd(l)), l_ab)
    n_sq = int(math.log2(CHUNK)) - 1
    for lvl in range(n_sq):
        rhs = [bd(p_) for p_ in pw]
        if lvl < n_sq - 1:
            both = _every(lambda x, p_, w_: _dot(jnp.concatenate([x, p_], axis=0), w_),
                          x_acc, pw, rhs)
            x_acc = _every(lambda x, b_: x + b_[:CHUNK], x_acc, both)
            pw = [b_[CHUNK:] for b_ in both]
        else:
            x_acc = _every(lambda x, w_: x + _dot(x, w_), x_acc, rhs)
    t_inv = x_acc

    kv = _every(lambda la, mk, x: _dot(jnp.concatenate([la, mk], axis=0), bd(x)), l_ak, m_rk, v)
    y_loc = [x[:CHUNK] for x in kv]
    wu = _every(lambda t_, a_, y_: _dot(t_, jnp.concatenate([bd(a_), bd(y_)], axis=1)),
                t_inv, a_t, y_loc)
    w_t = [x[:, :gl] for x in wu]
    u_loc = [x[:, gl:] for x in wu]

    mwu = _every(lambda mb, w_, u_: _dot(mb, jnp.concatenate([bd(w_), bd(u_)], axis=1)),
                 m_rb, w_t, u_loc)
    q_c = _every(lambda r_, x: r_ + x[:, :gl], r_t, mwu)
    o_loc = _every(lambda x, y_: x[:, gl:] + y_[CHUNK:], mwu, kv)
    m_low = _every(lambda w_, b_: jnp.where(bd_mask, _dot_tn(w_, b_), 0.0).astype(BF16),
                   w_t, b_end)
    n_loc = _every(lambda u_, v_, b_, k_: fold(_dot_tn(jnp.concatenate([u_, v_], axis=0),
                                                       jnp.concatenate([b_, k_], axis=0))),
                   u_loc, v, b_end, k_end)

    state = [s_ref[g] for g in range(n_groups)]
    starts = []
    for c in range(n_chunks):
        starts.append(state)
        idx = [c * n_groups + g for g in range(n_groups)]
        state = [state[g] * gamma[i] + _dot(state[g], m_low[i]) + n_loc[i]
                 for g, i in enumerate(idx)]
    for g in range(n_groups):
        s_ref[g] = state[g]

    s0 = [starts[c][g] for (c, g) in probs]
    o = _every(lambda q_, s_, ol: _dot_nt(q_, bd(s_)) + ol, q_c, s0, o_loc)

    n_p = len(probs)
    unstack = lambda x: [x[i * CHUNK:(i + 1) * CHUNK] for i in range(n_p)]
    mu = unstack(_dot(jnp.concatenate(o, axis=0), seg) * (1.0 / HEAD_DIM))
    d = _every(lambda x, m_: x - m_, o, mu)
    var = unstack(_dot(jnp.concatenate([x * x for x in d], axis=0), seg) * (1.0 / HEAD_DIM))
    for i, (c, g) in enumerate(probs):
        ls = slice(g * gl, (g + 1) * gl)
        gn = d[i] * lax.rsqrt(var[i] + RWKV_GN_EPS) * ln_w[:, ls] + ln_b[:, ls]
        y = (gn + ld(7, (c, g))) * ld(6, (c, g))
        y_ref[c * CHUNK:(c + 1) * CHUNK, ls] = y.astype(y_ref.dtype)


def _wkv_sample_kernel(ft_ref, s_ref, s_out_ref, o_ref):
    hd = HEAD_DIM
    r, kf, na, bb = ft_ref[0], ft_ref[2], ft_ref[4], ft_ref[5]
    w = jnp.exp(ft_ref[1])

    def value_block(vb, carry):
        v0 = pl.multiple_of(vb * SUBLANES, SUBLANES)
        v_rows = ft_ref[3, pl.ds(v0, SUBLANES), :]
        outs = []
        for j in range(SUBLANES):
            s = s_ref[0, v0 + j]
            sa = jnp.sum(s * na, axis=0, keepdims=True)
            s_new = s * w + sa * bb + v_rows[j:j + 1] * kf
            s_out_ref[0, v0 + j] = s_new
            outs.append(jnp.sum(s_new * r, axis=0, keepdims=True))
        o_ref[pl.ds(v0, SUBLANES), :] = jnp.concatenate(outs, axis=0)
        return carry

    lax.fori_loop(0, hd // SUBLANES, value_block, 0)


def _wkv_sample_call(ft_s, state_t):
    h, hd, _, n = state_t.shape
    return pl.pallas_call(
        _wkv_sample_kernel,
        grid=(h,),
        in_specs=[
            pl.BlockSpec((6, hd, n), lambda i: (0, i, 0)),
            pl.BlockSpec((1, hd, hd, n), lambda i: (i, 0, 0, 0)),
        ],
        out_specs=(
            pl.BlockSpec((1, hd, hd, n), lambda i: (i, 0, 0, 0)),
            pl.BlockSpec((hd, n), lambda i: (i, 0)),
        ),
        out_shape=(
            jax.ShapeDtypeStruct(state_t.shape, F32),
            jax.ShapeDtypeStruct((h * hd, n), F32),
        ),
        compiler_params=pltpu.CompilerParams(
            dimension_semantics=("arbitrary",), vmem_limit_bytes=VMEM_LIMIT),
        name="wkv_sample",
    )(ft_s, state_t)


def _swa_prompt_kernel(sink_ref, q_ref, kp_ref, kc_ref, vp_ref, vc_ref, o_ref):
    j = pl.program_id(1)
    w = WINDOW
    tq = q_ref.shape[1]
    n_blk = tq // w
    gl = GROUP_LANES
    rows = SWA_GROUP * w
    probs = [(qi, g) for qi in range(n_blk) for g in range(SWA_KV_HEADS)]

    lane_head = lax.broadcasted_iota(jnp.int32, (w, gl), 1) // HEAD_DIM
    head_mask = [lane_head == h for h in range(SWA_GROUP)]
    qi_ = lax.broadcasted_iota(jnp.int32, (rows, 2 * w), 0) % w
    ki_ = lax.broadcasted_iota(jnp.int32, (rows, 2 * w), 1)
    diff = qi_ - (ki_ - w)
    in_window = (diff >= 0) & (diff < WINDOW)
    first_valid = in_window & ((j * tq + ki_ - w) >= 0)

    def keys(ref_prev, ref_cur, qi, g):
        ls = slice(g * gl, (g + 1) * gl)
        prev = ref_prev[0, :, ls] if qi == 0 else ref_cur[0, (qi - 1) * w:qi * w, ls]
        return jnp.concatenate([prev, ref_cur[0, qi * w:(qi + 1) * w, ls]], axis=0)

    def lhs(qi, g):
        qg = q_ref[0, qi * w:(qi + 1) * w, g * gl:(g + 1) * gl]
        return jnp.concatenate([jnp.where(m, qg, jnp.zeros((), BF16)) for m in head_mask], axis=0)

    sinks = []
    for g in range(SWA_KV_HEADS):
        sinks.append(jnp.concatenate(
            [jnp.full((w, 1), sink_ref[g * SWA_GROUP + h], F32) for h in range(SWA_GROUP)], axis=0))

    s = [_dot_nt(lhs(qi, g), keys(kp_ref, kc_ref, qi, g)) for qi, g in probs]
    s = [jnp.where(first_valid if qi == 0 else in_window, x, MASK_VALUE)
         for x, (qi, g) in zip(s, probs)]
    sink = [sinks[g] for qi, g in probs]
    m = _every(lambda x, sk: jnp.maximum(jnp.max(x, axis=-1, keepdims=True), sk), s, sink)
    e = _every(lambda x, m_: jnp.exp(x - m_), s, m)
    inv = _every(lambda e_, sk, m_: 1.0 / (jnp.sum(e_, axis=-1, keepdims=True) + jnp.exp(sk - m_)),
                 e, sink, m)
    prob = _every(lambda e_, i_: (e_ * i_).astype(BF16), e, inv)
    og = [_dot(p_, keys(vp_ref, vc_ref, qi, g)) for p_, (qi, g) in zip(prob, probs)]
    for x, (qi, g) in zip(og, probs):
        y = jnp.where(head_mask[0], x[0:w], 0.0)
        for h in range(1, SWA_GROUP):
            y = y + jnp.where(head_mask[h], x[h * w:(h + 1) * w], 0.0)
        o_ref[0, qi * w:(qi + 1) * w, g * gl:(g + 1) * gl] = y.astype(o_ref.dtype)


def _swa_prompt_call(sinks, q, k, v, tq):
    b, t, _ = q.shape
    w = WINDOW
    per = tq // w
    prev = lambda i, j: (i, jnp.maximum(j * per - 1, 0), 0)
    cur = lambda i, j: (i, j, 0)
    return pl.pallas_call(
        _swa_prompt_kernel,
        grid=(b, t // tq),
        in_specs=[
            pl.BlockSpec(memory_space=pltpu.SMEM),
            pl.BlockSpec((1, tq, SWA_WIDTH), cur),
            pl.BlockSpec((1, w, SWA_WIDTH), prev),
            pl.BlockSpec((1, tq, SWA_WIDTH), cur),
            pl.BlockSpec((1, w, SWA_WIDTH), prev),
            pl.BlockSpec((1, tq, SWA_WIDTH), cur),
        ],
        out_specs=pl.BlockSpec((1, tq, SWA_WIDTH), cur),
        out_shape=jax.ShapeDtypeStruct((b, t, SWA_WIDTH), BF16),
        compiler_params=pltpu.CompilerParams(
            dimension_semantics=("arbitrary", "arbitrary"), vmem_limit_bytes=VMEM_LIMIT),
        name="swa_prompt",
    )(sinks, q, k, k, v, v)


def _swa_sample_kernel(sink_ref, q_ref, knr_ref, vnr_ref, knt_ref, vnt_ref, ck_ref, cv_ref,
                       o_ref, ko_ref, vo_ref):
    nb, _, hd, l = ck_ref.shape
    q = q_ref[...]
    ck = ck_ref[...].reshape(nb * SWA_KV_HEADS, hd, l)
    cv = cv_ref[...].reshape(nb * SWA_KV_HEADS, hd, l)
    knr = knr_ref[...].astype(BF16).astype(F32)
    vnr = vnr_ref[...].astype(BF16).astype(F32)
    ki = lax.broadcasted_iota(jnp.int32, (1, 1, l), 2)
    kpos = PAST_LEN - l + ki
    diff = PAST_LEN - kpos
    valid = (diff >= 0) & (diff < WINDOW) & (kpos >= 0)
    s_c = lax.dot_general(q, ck.astype(BF16), (((2,), (1,)), ((0,), (0,))),
                          preferred_element_type=F32)
    s_c = jnp.where(valid, s_c, MASK_VALUE)
    s_n = jnp.sum(q.astype(F32) * knr, axis=-1, keepdims=True)
    sink = sink_ref[:, :, 0:1]
    m = jnp.maximum(jnp.maximum(jnp.max(s_c, axis=-1, keepdims=True), s_n), sink)
    e_c = jnp.exp(s_c - m)
    e_n = jnp.exp(s_n - m)
    denom = jnp.sum(e_c, axis=-1, keepdims=True) + e_n + jnp.exp(sink - m)
    p_c = (e_c / denom).astype(BF16)
    p_n = (e_n / denom).astype(BF16).astype(F32)
    o = lax.dot_general(p_c, cv.astype(BF16), (((2,), (2,)), ((0,), (0,))),
                        preferred_element_type=F32)
    o_ref[...] = o + p_n * vnr
    lane = lax.broadcasted_iota(jnp.int32, (hd, l), 1)
    for b in range(nb):
        for g in range(SWA_KV_HEADS):
            rows = slice(g * hd, (g + 1) * hd)
            ko_ref[b, g] = jnp.where(lane == l - 1, knt_ref[0, rows, b:b + 1],
                                     pltpu.roll(ck_ref[b, g], l - 1, 1))
            vo_ref[b, g] = jnp.where(lane == l - 1, vnt_ref[0, rows, b:b + 1],
                                     pltpu.roll(cv_ref[b, g], l - 1, 1))


def _swa_sample_call(sink_t, q8, knr, vnr, knt, vnt, ck, cv, nb):
    n, g, hd, l = ck.shape
    rows = nb * g
    blk3 = lambda i: (i, 0, 0)
    blk4 = lambda i: (i, 0, 0, 0)
    return pl.pallas_call(
        _swa_sample_kernel,
        grid=(n // nb,),
        in_specs=[
            pl.BlockSpec((rows, SUBLANES, LANES), lambda i: (0, 0, 0)),
            pl.BlockSpec((rows, SUBLANES, hd), blk3),
            pl.BlockSpec((rows, 1, hd), blk3),
            pl.BlockSpec((rows, 1, hd), blk3),
            pl.BlockSpec((1, g * hd, nb), blk3),
            pl.BlockSpec((1, g * hd, nb), blk3),
            pl.BlockSpec((nb, g, hd, l), blk4),
            pl.BlockSpec((nb, g, hd, l), blk4),
        ],
        out_specs=(
            pl.BlockSpec((rows, SUBLANES, hd), blk3),
            pl.BlockSpec((nb, g, hd, l), blk4),
            pl.BlockSpec((nb, g, hd, l), blk4),
        ),
        out_shape=(
            jax.ShapeDtypeStruct((n * g, SUBLANES, hd), F32),
            jax.ShapeDtypeStruct(ck.shape, F32),
            jax.ShapeDtypeStruct(cv.shape, F32),
        ),
        compiler_params=pltpu.CompilerParams(
            dimension_semantics=("arbitrary",), vmem_limit_bytes=VMEM_LIMIT),
        name="swa_sample",
    )(sink_t, q8, knr, vnr, knt, vnt, ck, cv)


def _tail_mix(x, y_rwkv, y_swa, g1, sh2, sc2, w_out, n_post_mix, n_pre_ffn):
    mix = _dot(y_rwkv, w_out[:RWKV_WIDTH]) + _dot(y_swa, w_out[RWKV_WIDTH:])
    x1 = x + g1 * _rmsnorm(mix, n_post_mix)
    h2 = (_rmsnorm(x1, n_pre_ffn) * (1.0 + sc2) + sh2).astype(BF16)
    return x1, h2


def _tail_prompt_kernel(x_ref, yr_ref, ys_ref, mod_ref, w_out, n_post_mix, n_pre_ffn,
                        n_post_ffn, w_up_a, w_up_b, w_down, cw_ref, cb_ref,
                        y_ref, cp_ref, carry_ref, act_ref):
    t = pl.program_id(1)
    tm = x_ref.shape[1]
    sub = min(tm, TAIL_SUB)
    grp = min(tm, TAIL_GROUP)

    @pl.when(t == 0)
    def _():
        carry_ref[...] = jnp.zeros_like(carry_ref)

    g1, sh2, sc2, g2 = mod_ref[2, 0], mod_ref[3, 0], mod_ref[4, 0], mod_ref[5, 0]

    def cols_of(c, half):
        return slice(half * D_FF + c * FF_CHUNK, half * D_FF + (c + 1) * FF_CHUNK)

    def conv(u, cols):
        ext = jnp.concatenate([carry_ref[:, cols], u], axis=0)
        last = u[sub - SUBLANES:]
        carry_ref[:, cols] = last
        cp_ref[0, :, cols] = last
        return (cb_ref[:, cols] + pltpu.roll(ext, 2, 0)[SUBLANES:] * cw_ref[0:1, cols]
                + pltpu.roll(ext, 1, 0)[SUBLANES:] * cw_ref[1:2, cols] + u * cw_ref[2:3, cols])

    def group(gi, carry):
        base = pl.multiple_of(gi * grp, grp)
        blocks = [(pl.ds(base + j * sub, sub), slice(j * sub, (j + 1) * sub))
                  for j in range(grp // sub)]
        mixes = [_dot(yr_ref[0, rb, :], w_out[:RWKV_WIDTH]) + _dot(ys_ref[0, rb, :], w_out[RWKV_WIDTH:])
                 for rb, _ in blocks]
        for (rb, ab), mix in zip(blocks, mixes):
            x1 = x_ref[0, rb, :] + g1 * _rmsnorm(mix, n_post_mix[...])
            h2 = (_rmsnorm(x1, n_pre_ffn[...]) * (1.0 + sc2) + sh2).astype(BF16)

            def up(c):
                chunk = slice(c * FF_CHUNK, (c + 1) * FF_CHUNK)
                return [jnp.dot(h2, w[:, chunk], preferred_element_type=F32)
                        for w in (w_up_a, w_up_b)]

            def down(c):
                rows = slice(c * FF_CHUNK, (c + 1) * FF_CHUNK)
                return jnp.dot(act_ref[ab, rows], w_down[rows, :], preferred_element_type=F32)

            u_next = up(0)
            ff = None
            for c in range(N_FF_CHUNKS):
                u_cur = u_next
                if c + 1 < N_FF_CHUNKS:
                    u_next = up(c + 1)
                if c >= 1:
                    part = down(c - 1)
                    ff = part if ff is None else ff + part
                za, zb = [conv(u_cur[half], cols_of(c, half)) for half in range(2)]
                act_ref[ab, c * FF_CHUNK:(c + 1) * FF_CHUNK] = (_silu(za) * zb).astype(BF16)
            ff = ff + down(N_FF_CHUNKS - 1)
            y_ref[0, rb, :] = x1 + g2 * _rmsnorm(ff, n_post_ffn[...])
        return carry

    lax.fori_loop(0, tm // grp, group, 0)


def _tail_sample_kernel(x_ref, o_ref, feat_ref, ys_ref, mod_ref, ln_w, ln_b, w_out, n_post_mix,
                        n_pre_ffn, n_post_ffn, wa_ref, wb_ref, wd_ref, p0a, p0b, p1a, p1b,
                        cwa, cwb, cba, cbb,
                        y_ref, ua_ref, ub_ref, w_out_b_ref, wa_b_ref, wb_b_ref, wd_b_ref,
                        x1_ref, h2_ref, acc_ref):
    c = pl.program_id(0)

    @pl.when(c == 0)
    def _():
        w_out_b = w_out[...].astype(BF16)
        w_out_b_ref[...] = w_out_b
        seg = _seg_ones(RWKV_WIDTH)
        y_rwkv = _gn_epilogue(o_ref[...].T, feat_ref[7], feat_ref[6], ln_w[...], ln_b[...], seg)
        x1, h2 = _tail_mix(x_ref[...], y_rwkv, ys_ref[...], mod_ref[2], mod_ref[3], mod_ref[4],
                           w_out_b, n_post_mix[...], n_pre_ffn[...])
        x1_ref[...] = x1
        h2_ref[...] = h2
        acc_ref[...] = jnp.zeros_like(acc_ref)

    wa = wa_ref[...].astype(BF16)
    wb = wb_ref[...].astype(BF16)
    wd = wd_ref[...].astype(BF16)
    wa_b_ref[...] = wa
    wb_b_ref[...] = wb
    wd_b_ref[...] = wd
    h2 = h2_ref[...]
    ua = jnp.dot(h2, wa, preferred_element_type=F32)
    ub = jnp.dot(h2, wb, preferred_element_type=F32)
    ua_ref[...] = ua
    ub_ref[...] = ub
    za = cba[...] + p0a[...] * cwa[0:1, :] + p1a[...] * cwa[1:2, :] + ua * cwa[2:3, :]
    zb = cbb[...] + p0b[...] * cwb[0:1, :] + p1b[...] * cwb[1:2, :] + ub * cwb[2:3, :]
    acc_ref[...] += _dot(_silu(za) * zb, wd)

    @pl.when(c == pl.num_programs(0) - 1)
    def _():
        y_ref[...] = x1_ref[...] + mod_ref[5] * _rmsnorm(acc_ref[...], n_post_ffn[...])


def _tail_prompt_call(x, y_rwkv, y_swa, mod_p, consts, tm):
    b, t, _ = x.shape
    tok = lambda i, j: (i, j, 0)
    in_specs = [
        pl.BlockSpec((1, tm, D_MODEL), tok),
        pl.BlockSpec((1, tm, RWKV_WIDTH), tok),
        pl.BlockSpec((1, tm, SWA_WIDTH), tok),
        pl.BlockSpec((6, 1, 1, D_MODEL), lambda i, j: (0, i, 0, 0)),
    ] + [_const_spec(c, 2) for c in consts]
    return pl.pallas_call(
        _tail_prompt_kernel,
        grid=(b, t // tm),
        in_specs=in_specs,
        out_specs=(
            pl.BlockSpec((1, tm, D_MODEL), tok),
            pl.BlockSpec((1, SUBLANES, 2 * D_FF), lambda i, j: (i, 0, 0)),
        ),
        out_shape=(
            jax.ShapeDtypeStruct((b, t, D_MODEL), F32),
            jax.ShapeDtypeStruct((b, SUBLANES, 2 * D_FF), F32),
        ),
        scratch_shapes=[
            pltpu.VMEM((SUBLANES, 2 * D_FF), F32),
            pltpu.VMEM((min(tm, TAIL_GROUP), D_FF), BF16),
        ],
        compiler_params=pltpu.CompilerParams(
            dimension_semantics=("arbitrary", "arbitrary"), vmem_limit_bytes=VMEM_LIMIT),
        name="tail_prompt",
    )(x, y_rwkv, y_swa, mod_p, *consts)


def _tail_sample_call(x, o_t, feat_s, y_swa, mod, p0, p1, ln_w, ln_b, w_out, n_post_mix,
                      n_pre_ffn, n_post_ffn, w_up, w_down, cw, cb):
    n = x.shape[0]
    nc = N_FF_CHUNKS
    fc = FF_CHUNK
    whole = lambda a: pl.BlockSpec(a.shape, lambda c, nd=a.ndim: (0,) * nd)
    gate = lambda rows: pl.BlockSpec((rows, fc), lambda c: (0, c))
    value = lambda rows: pl.BlockSpec((rows, fc), lambda c: (0, nc + c))
    in_specs = [
        whole(x), whole(o_t), whole(feat_s), whole(y_swa),
        pl.BlockSpec((mod.shape[0], n, D_MODEL), lambda c: (0, 0, 0)),
        whole(ln_w), whole(ln_b), whole(w_out), whole(n_post_mix), whole(n_pre_ffn),
        whole(n_post_ffn),
        gate(D_MODEL), value(D_MODEL), pl.BlockSpec((fc, D_MODEL), lambda c: (c, 0)),
        gate(n), value(n), gate(n), value(n), gate(3), value(3), gate(1), value(1),
    ]
    out_shape = (
        jax.ShapeDtypeStruct((n, D_MODEL), F32),
        jax.ShapeDtypeStruct((n, D_FF), F32),
        jax.ShapeDtypeStruct((n, D_FF), F32),
        jax.ShapeDtypeStruct((D_MODEL, D_MODEL), BF16),
        jax.ShapeDtypeStruct((D_MODEL, D_FF), BF16),
        jax.ShapeDtypeStruct((D_MODEL, D_FF), BF16),
        jax.ShapeDtypeStruct((D_FF, D_MODEL), BF16),
    )
    chunk_cols = lambda rows: pl.BlockSpec((rows, fc), lambda c: (0, c))
    out_specs = (
        pl.BlockSpec((n, D_MODEL), lambda c: (0, 0)),
        chunk_cols(n), chunk_cols(n),
        pl.BlockSpec((D_MODEL, D_MODEL), lambda c: (0, 0)),
        chunk_cols(D_MODEL), chunk_cols(D_MODEL),
        pl.BlockSpec((fc, D_MODEL), lambda c: (c, 0)),
    )
    return pl.pallas_call(
        _tail_sample_kernel,
        grid=(nc,),
        in_specs=in_specs,
        out_specs=out_specs,
        out_shape=out_shape,
        scratch_shapes=[
            pltpu.VMEM((n, D_MODEL), F32),
            pltpu.VMEM((n, D_MODEL), BF16),
            pltpu.VMEM((n, D_MODEL), F32),
        ],
        compiler_params=pltpu.CompilerParams(
            dimension_semantics=("arbitrary",), vmem_limit_bytes=VMEM_LIMIT),
        name="tail_sample",
    )(x, o_t, feat_s, y_swa, mod, ln_w, ln_b, w_out, n_post_mix, n_pre_ffn, n_post_ffn,
      w_up, w_up, w_down, p0, p0, p1, p1, cw, cw, cb, cb)


def _rope_tables(pos):
    half = HEAD_DIM // 2
    inv = ROPE_THETA ** (-jnp.arange(half, dtype=F32) / half)
    ang = pos.astype(F32)[:, None] * inv[None, :]
    cos, sin = jnp.cos(ang), jnp.sin(ang)
    cos_h = jnp.concatenate([cos, cos], axis=-1)
    sin_h = jnp.concatenate([-sin, sin], axis=-1)
    return jnp.tile(cos_h, (1, LANES // HEAD_DIM)), jnp.tile(sin_h, (1, LANES // HEAD_DIM))


def kernel(x_prompt, x_sample, state_rwkv_wkv, state_rwkv_shift, cache_swa_k, cache_swa_v,
           state_ffn_conv, c_prompt, c_sample, w_ada, b_ada, norm_pre_mix, norm_post_mix,
           norm_pre_ffn, norm_post_ffn, w_in, rwkv_mu, rwkv_w0, rwkv_w_up, rwkv_a0, rwkv_a_up,
           rwkv_g_up, rwkv_k_k, rwkv_k_a, rwkv_r_k, rwkv_ln_w, rwkv_ln_b, swa_sinks, w_out,
           ffn_w_up, ffn_conv_w, ffn_conv_b, ffn_w_down):
    depth = w_ada.shape[0]
    assert depth == 1 and x_sample.shape[1] == 1
    b, t, _ = x_prompt.shape
    n = x_sample.shape[0]
    tm = min(256, t)
    blk = min(4 * tm, t)
    assert t % blk == 0 and tm % WINDOW == 0 and tm % CHUNK == 0 and n % SUBLANES == 0
    nb_swa = 2 * SUBLANES if n % (2 * SUBLANES) == 0 else SUBLANES
    li = 0

    row = lambda v: v.reshape(1, -1)
    zeros_l = jnp.zeros((64, RWKV_WIDTH), F32)
    wa_up = jnp.concatenate([
        jnp.concatenate([rwkv_w_up[li], zeros_l], axis=1),
        jnp.concatenate([zeros_l, rwkv_a_up[li]], axis=1)], axis=0).astype(BF16)
    hid = jnp.arange(GROUP_LANES) // HEAD_DIM
    seg_blk = (hid[:, None] == hid[None, :]).astype(BF16)
    proj_consts = (row(norm_pre_mix[li]), w_in[li], row(rwkv_mu[li]), row(rwkv_w0[li]), wa_up,
                   row(rwkv_a0[li]), rwkv_g_up[li].astype(BF16), row(rwkv_k_k[li]),
                   row(rwkv_k_a[li]), row(rwkv_r_k[li]), seg_blk)
    norms = (row(norm_post_mix[li]), row(norm_pre_ffn[li]), row(norm_post_ffn[li]))
    cw, cb = ffn_conv_w[li], row(ffn_conv_b[li])
    ln_w, ln_b = row(rwkv_ln_w[li]), row(rwkv_ln_b[li])

    mod = _ada_call(jnp.concatenate([c_sample, c_prompt], axis=0), w_ada[li], row(b_ada[li]))
    mod_p = mod[:, n:].reshape(6, b, 1, D_MODEL)

    cos_s, sin_s = _rope_tables(jnp.full((1,), PAST_LEN, jnp.int32))
    feat_s, ft_s, q_s, kn_s, vn_s, p_s, w_in_b = _proj_sample_call(
        x_sample[:, 0], mod, cos_s, sin_s, state_rwkv_shift[li], proj_consts)

    state_t = jnp.transpose(state_rwkv_wkv[li], (1, 2, 3, 0))
    wkv_t, o_t = _wkv_sample_call(ft_s, state_t)
    wkv_s = jnp.transpose(wkv_t, (3, 0, 1, 2))

    g2 = SWA_KV_HEADS
    q4 = q_s.reshape(n * g2, SWA_GROUP, HEAD_DIM)
    q8 = jnp.concatenate([q4, jnp.zeros_like(q4)], axis=1)
    sink_t = jnp.broadcast_to(
        jnp.concatenate([swa_sinks[li].reshape(g2, SWA_GROUP),
                         jnp.full((g2, SWA_GROUP), MASK_VALUE, F32)], axis=1)[None, :, :, None],
        (nb_swa, g2, SUBLANES, LANES)).reshape(nb_swa * g2, SUBLANES, LANES)
    cols = lambda a: a.reshape(n // nb_swa, nb_swa, KV_WIDTH).transpose(0, 2, 1)
    ck = jnp.transpose(cache_swa_k[li], (0, 2, 3, 1))
    cv = jnp.transpose(cache_swa_v[li], (0, 2, 3, 1))
    o_att, k_t, v_t = _swa_sample_call(
        sink_t, q8, kn_s.reshape(n * g2, 1, HEAD_DIM), vn_s.reshape(n * g2, 1, HEAD_DIM),
        cols(kn_s), cols(vn_s), ck, cv, nb_swa)
    y_swa_s = o_att[:, :SWA_GROUP].reshape(n, SWA_WIDTH)
    k_s = jnp.transpose(k_t, (0, 3, 1, 2))
    v_s = jnp.transpose(v_t, (0, 3, 1, 2))

    conv0 = state_ffn_conv[li]
    y_s, ua_s, ub_s, w_out_b, w_up_a_b, w_up_b_b, w_down_b = _tail_sample_call(
        x_sample[:, 0], o_t, feat_s, y_swa_s.astype(BF16), mod, conv0[:, 0], conv0[:, 1],
        ln_w, ln_b, w_out[li], *norms, ffn_w_up[li], ffn_w_down[li], cw, cb)
    conv_s = jnp.stack([conv0[:, 1], jnp.concatenate([ua_s, ub_s], axis=1)], axis=1)

    cos_p, sin_p = _rope_tables(jnp.arange(t, dtype=jnp.int32))
    proj_consts_p = proj_consts[:1] + (w_in_b,) + proj_consts[2:]
    y_rwkv_p, s_cat, q_p, kx_p, vx_p, klast, vlast, plast = _mix_prompt_call(
        x_prompt, mod_p, cos_p, sin_p, proj_consts_p + (ln_w, ln_b), blk)
    y_swa_p = _swa_prompt_call(swa_sinks[li], q_p, kx_p, vx_p, min(2 * blk, t))
    tail_consts = (w_out_b,) + norms + (w_up_a_b, w_up_b_b, w_down_b, cw, cb)
    y_p, cp = _tail_prompt_call(x_prompt, y_rwkv_p, y_swa_p, mod_p, tail_consts, blk)

    wkv_p = s_cat.reshape(b, HEAD_DIM, RWKV_HEADS, HEAD_DIM).transpose(0, 2, 1, 3)
    shift_p = plast[:, SUBLANES - 1]
    k_p = klast.reshape(b, WINDOW, SWA_KV_HEADS, HEAD_DIM)
    v_p = vlast.reshape(b, WINDOW, SWA_KV_HEADS, HEAD_DIM)
    conv_p = cp[:, SUBLANES - 2:]

    expand = lambda a: a[None]
    return (y_p, y_s[:, None, :], expand(wkv_p), expand(shift_p), expand(k_p), expand(v_p),
            expand(conv_p), expand(wkv_s), expand(p_s),
            expand(k_s), expand(v_s), expand(conv_s))
```

```python
import math

import jax
import jax.numpy as jnp
from jax import lax
from jax.experimental import pallas as pl
from jax.experimental.pallas import tpu as pltpu

D_MODEL = 1024
HEAD_DIM = 64
RWKV_WIDTH = 512
RWKV_HEADS = 8
RWKV_COLS = 1792
RWKV_GN_EPS = 64e-5
SWA_WIDTH = 512
SWA_HEADS = 8
SWA_KV_HEADS = 2
SWA_GROUP = 4
KV_WIDTH = SWA_KV_HEADS * HEAD_DIM
WINDOW = 128
PAST_LEN = 16384
ROPE_THETA = 10000.0
ATTN_SCALE = HEAD_DIM ** -0.5
D_FF = 2816
NORM_EPS = 1e-6
MASK_VALUE = -1e30
PROJ_COLS = RWKV_COLS + SWA_WIDTH + 2 * KV_WIDTH

LANES = 128
SUBLANES = 8
CHUNK = 64
GROUP_LANES = 256
FF_CHUNK = 256
N_FF_CHUNKS = D_FF // FF_CHUNK
TAIL_SUB = 256
MIX_SUB = 512
TAIL_GROUP = 512
VMEM_LIMIT = 56 * 1024 * 1024

F32 = jnp.float32
BF16 = jnp.bfloat16


def _sigmoid(x):
    return 1.0 / (1.0 + jnp.exp(-x))


def _silu(x):
    return x * _sigmoid(x)


def _rmsnorm(x, g):
    return x * lax.rsqrt(jnp.mean(x * x, axis=-1, keepdims=True) + NORM_EPS) * g


def _dot(a, b):
    return jnp.dot(a.astype(BF16), b.astype(BF16), preferred_element_type=F32)


def _dot_nt(a, b):
    return lax.dot_general(a.astype(BF16), b.astype(BF16), (((1,), (1,)), ((), ())),
                           preferred_element_type=F32)


def _dot_tn(a, b):
    return lax.dot_general(a.astype(BF16), b.astype(BF16), (((0,), (0,)), ((), ())),
                           preferred_element_type=F32)


def _every(fn, *lists):
    return [fn(*a) for a in zip(*lists)]


def _swap_halves(x):
    w = x.shape[-1]
    lane = lax.broadcasted_iota(jnp.int32, x.shape, x.ndim - 1)
    lo = (lane & (HEAD_DIM // 2)) == 0
    return jnp.where(lo, pltpu.roll(x, w - HEAD_DIM // 2, x.ndim - 1),
                     pltpu.roll(x, HEAD_DIM // 2, x.ndim - 1))


def _rope(x, cos, sin):
    reps = x.shape[-1] // LANES
    cos_w = jnp.concatenate([cos] * reps, axis=-1) if reps > 1 else cos
    sin_w = jnp.concatenate([sin] * reps, axis=-1) if reps > 1 else sin
    return x * cos_w + _swap_halves(x) * sin_w


def _proj_features(x, shift, scale, g_pre, w_in, prev_fn, mu, w0, wa_up, a0, g_up,
                   k_k, k_a, r_k, seg_blk, cos, sin):
    h = _rmsnorm(x, g_pre) * (1.0 + scale) + shift
    p = _dot(h, w_in)
    p_rwkv = p[:, :RWKV_COLS]
    prev = prev_fn(p_rwkv)
    xm = p_rwkv + (prev - p_rwkv) * mu
    r = xm[:, 0:512]
    k = xm[:, 512:1024]
    v = xm[:, 1024:1536]
    wa = xm[:, 1536:1664]
    gd = xm[:, 1664:1792]
    lane = lax.broadcasted_iota(jnp.int32, wa.shape, 1)
    wa_act = jnp.where(lane < 64, jnp.tanh(wa), wa)
    lora = _dot(wa_act, wa_up)
    lw = -math.exp(-0.5) * _sigmoid(w0 + lora[:, :512])
    a = _sigmoid(a0 + lora[:, 512:])
    g = _dot(_sigmoid(gd), g_up)
    kk = k * k_k
    kf = k * (1.0 + (a - 1.0) * k_a)
    gl = seg_blk.shape[0]
    n = x.shape[0]
    sums = [_dot(jnp.concatenate([(kk * kk)[:, i:i + gl], (r * kf * r_k)[:, i:i + gl]], axis=0),
                 seg_blk) for i in range(0, RWKV_WIDTH, gl)]
    ss = jnp.concatenate([s_[:n] for s_ in sums], axis=1)
    kk = kk / jnp.maximum(jnp.sqrt(ss), 1e-12)
    bonus = jnp.concatenate([s_[n:] for s_ in sums], axis=1) * v
    feats = (r, lw, kf, v, -kk, kk * a, g, bonus)
    q = _rope(p[:, RWKV_COLS:RWKV_COLS + SWA_WIDTH], cos, sin)
    ks = _rope(p[:, RWKV_COLS + SWA_WIDTH:RWKV_COLS + SWA_WIDTH + KV_WIDTH], cos, sin)
    vs = p[:, RWKV_COLS + SWA_WIDTH + KV_WIDTH:]
    return feats, q, ks, vs, p_rwkv


def _gn_epilogue(o, bonus, g, ln_w, ln_b, seg):
    mu = _dot(o, seg) * (1.0 / HEAD_DIM)
    d = o - mu
    var = _dot(d * d, seg) * (1.0 / HEAD_DIM)
    gn = d * lax.rsqrt(var + RWKV_GN_EPS) * ln_w + ln_b
    return (gn + bonus) * g


def _seg_ones(n):
    r = lax.broadcasted_iota(jnp.int32, (n, n), 0) // HEAD_DIM
    c = lax.broadcasted_iota(jnp.int32, (n, n), 1) // HEAD_DIM
    return (r == c).astype(BF16)


def _ada_kernel(c_ref, w_ref, b_ref, o_ref):
    o_ref[0] = _dot(_silu(c_ref[...]), w_ref[...]) + b_ref[...]


def _ada_call(c_all, w_ada, b_ada):
    rows = c_all.shape[0]
    return pl.pallas_call(
        _ada_kernel,
        grid=(6,),
        in_specs=[
            pl.BlockSpec((rows, D_MODEL), lambda j: (0, 0)),
            pl.BlockSpec((D_MODEL, D_MODEL), lambda j: (0, j)),
            pl.BlockSpec((1, D_MODEL), lambda j: (0, j)),
        ],
        out_specs=pl.BlockSpec((1, rows, D_MODEL), lambda j: (j, 0, 0)),
        out_shape=jax.ShapeDtypeStruct((6, rows, D_MODEL), F32),
        compiler_params=pltpu.CompilerParams(
            dimension_semantics=("arbitrary",), vmem_limit_bytes=VMEM_LIMIT),
        name="ada",
    )(c_all, w_ada, b_ada)


def _expand_kv(x):
    lane = lax.broadcasted_iota(jnp.int32, x.shape, 1)
    rolled = pltpu.roll(x, HEAD_DIM, 1)
    g0 = jnp.where(lane < HEAD_DIM, x, rolled)
    g1 = jnp.where(lane < HEAD_DIM, rolled, x)
    return jnp.concatenate([g0, g0, g1, g1], axis=1)


def _mix_prompt_kernel(x_ref, mod_ref, cos_ref, sin_ref, g_pre, w_in, mu, w0, wa_up, a0,
                       g_up, k_k, k_a, r_k, seg_blk, ln_w, ln_b,
                       y_ref, s_out_ref, q_ref, k_ref, v_ref, klast_ref, vlast_ref, plast_ref,
                       carry_ref, s_ref):
    t = pl.program_id(1)

    @pl.when(t == 0)
    def _():
        carry_ref[...] = jnp.zeros_like(carry_ref)
        s_ref[...] = jnp.zeros_like(s_ref)

    sub = min(x_ref.shape[1], MIX_SUB)

    def sub_tile(i, carry):
        rows = pl.ds(pl.multiple_of(i * sub, sub), sub)
        carry_row = carry_ref[SUBLANES - 1:SUBLANES, :]

        def prev_fn(p_rwkv):
            row = lax.broadcasted_iota(jnp.int32, p_rwkv.shape, 0)
            return jnp.where(row == 0, carry_row, pltpu.roll(p_rwkv, 1, 0))

        feats, q, ks, vs, p_rwkv = _proj_features(
            x_ref[0, rows, :], mod_ref[0, 0], mod_ref[1, 0], g_pre[...], w_in[...], prev_fn,
            mu[...], w0[...], wa_up[...], a0[...], g_up[...], k_k[...], k_a[...], r_k[...],
            seg_blk[...], cos_ref[rows, :], sin_ref[rows, :])
        q_ref[0, rows, :] = (q * ATTN_SCALE).astype(BF16)
        k_ref[0, rows, :] = _expand_kv(ks).astype(BF16)
        v_ref[0, rows, :] = _expand_kv(vs).astype(BF16)
        last = p_rwkv[sub - SUBLANES:, :]
        carry_ref[...] = last
        plast_ref[0] = last
        klast_ref[0] = ks[sub - WINDOW:, :]
        vlast_ref[0] = vs[sub - WINDOW:, :]
        _wkv_tile(feats, ln_w, ln_b, y_ref.at[0, rows, :], s_ref)
        return carry

    lax.fori_loop(0, x_ref.shape[1] // sub, sub_tile, 0)

    @pl.when(t == pl.num_programs(1) - 1)
    def _():
        for g in range(RWKV_WIDTH // GROUP_LANES):
            s_out_ref[0, :, g * GROUP_LANES:(g + 1) * GROUP_LANES] = s_ref[g]


def _proj_sample_kernel(x_ref, mod_ref, cos_ref, sin_ref, prev_ref, g_pre, w_in, mu, w0,
                        wa_up, a0, g_up, k_k, k_a, r_k, seg_blk,
                        feat_ref, ft_ref, q_ref, k_ref, v_ref, p_ref, w_in_b_ref):
    w_in_b = w_in[...].astype(BF16)
    w_in_b_ref[...] = w_in_b
    feats, q, ks, vs, p_rwkv = _proj_features(
        x_ref[...], mod_ref[0], mod_ref[1], g_pre[...], w_in_b, lambda p: prev_ref[...],
        mu[...], w0[...], wa_up[...], a0[...], g_up[...], k_k[...], k_a[...], r_k[...],
        seg_blk[...], cos_ref[...], sin_ref[...])
    for i, f in enumerate(feats):
        feat_ref[i] = f
    for i in range(6):
        ft_ref[i] = feats[i].T
    q_ref[...] = (q * ATTN_SCALE).astype(BF16)
    k_ref[...] = ks
    v_ref[...] = vs
    p_ref[...] = p_rwkv


def _const_spec(arr, grid_rank):
    zeros = (0,) * arr.ndim
    if grid_rank == 1:
        return pl.BlockSpec(arr.shape, lambda i: zeros)
    return pl.BlockSpec(arr.shape, lambda i, j: zeros)


def _mix_prompt_call(x, mod_p, cos, sin, consts, tm):
    b, t, _ = x.shape
    nt = t // tm
    in_specs = [
        pl.BlockSpec((1, tm, D_MODEL), lambda i, j: (i, j, 0)),
        pl.BlockSpec((6, 1, 1, D_MODEL), lambda i, j: (0, i, 0, 0)),
        pl.BlockSpec((tm, LANES), lambda i, j: (j, 0)),
        pl.BlockSpec((tm, LANES), lambda i, j: (j, 0)),
    ] + [_const_spec(c, 2) for c in consts]
    out_shape = (
        jax.ShapeDtypeStruct((b, t, RWKV_WIDTH), BF16),
        jax.ShapeDtypeStruct((b, HEAD_DIM, RWKV_WIDTH), F32),
        jax.ShapeDtypeStruct((b, t, SWA_WIDTH), BF16),
        jax.ShapeDtypeStruct((b, t, SWA_WIDTH), BF16),
        jax.ShapeDtypeStruct((b, t, SWA_WIDTH), BF16),
        jax.ShapeDtypeStruct((b, WINDOW, KV_WIDTH), F32),
        jax.ShapeDtypeStruct((b, WINDOW, KV_WIDTH), F32),
        jax.ShapeDtypeStruct((b, SUBLANES, RWKV_COLS), F32),
    )
    out_specs = (
        pl.BlockSpec((1, tm, RWKV_WIDTH), lambda i, j: (i, j, 0)),
        pl.BlockSpec((1, HEAD_DIM, RWKV_WIDTH), lambda i, j: (i, 0, 0)),
        pl.BlockSpec((1, tm, SWA_WIDTH), lambda i, j: (i, j, 0)),
        pl.BlockSpec((1, tm, SWA_WIDTH), lambda i, j: (i, j, 0)),
        pl.BlockSpec((1, tm, SWA_WIDTH), lambda i, j: (i, j, 0)),
        pl.BlockSpec((1, WINDOW, KV_WIDTH), lambda i, j: (i, 0, 0)),
        pl.BlockSpec((1, WINDOW, KV_WIDTH), lambda i, j: (i, 0, 0)),
        pl.BlockSpec((1, SUBLANES, RWKV_COLS), lambda i, j: (i, 0, 0)),
    )
    return pl.pallas_call(
        _mix_prompt_kernel,
        grid=(b, nt),
        in_specs=in_specs,
        out_specs=out_specs,
        out_shape=out_shape,
        scratch_shapes=[
            pltpu.VMEM((SUBLANES, RWKV_COLS), F32),
            pltpu.VMEM((RWKV_WIDTH // GROUP_LANES, HEAD_DIM, GROUP_LANES), F32),
        ],
        compiler_params=pltpu.CompilerParams(
            dimension_semantics=("arbitrary", "arbitrary"), vmem_limit_bytes=VMEM_LIMIT),
        name="mix_prompt",
    )(x, mod_p, cos, sin, *consts)


def _mod_rows_spec(mod, n):
    return pl.BlockSpec((mod.shape[0], n, mod.shape[2]), lambda i: (0, 0, 0))


def _proj_sample_call(x, mod, cos, sin, prev, consts):
    n = x.shape[0]
    args = (x, mod, cos, sin, prev) + tuple(consts)
    out_shape = (
        jax.ShapeDtypeStruct((8, n, RWKV_WIDTH), F32),
        jax.ShapeDtypeStruct((6, RWKV_WIDTH, n), F32),
        jax.ShapeDtypeStruct((n, SWA_WIDTH), BF16),
        jax.ShapeDtypeStruct((n, KV_WIDTH), F32),
        jax.ShapeDtypeStruct((n, KV_WIDTH), F32),
        jax.ShapeDtypeStruct((n, RWKV_COLS), F32),
        jax.ShapeDtypeStruct((D_MODEL, PROJ_COLS), BF16),
    )
    return pl.pallas_call(
        _proj_sample_kernel,
        grid=(1,),
        in_specs=[_mod_rows_spec(a, n) if i == 1 else _const_spec(a, 1)
                  for i, a in enumerate(args)],
        out_specs=tuple(pl.BlockSpec(s.shape, lambda i, nd=len(s.shape): (0,) * nd)
                        for s in out_shape),
        out_shape=out_shape,
        compiler_params=pltpu.CompilerParams(
            dimension_semantics=("arbitrary",), vmem_limit_bytes=VMEM_LIMIT),
        name="proj_sample",
    )(*args)


def _wkv_tile(feats, ln_w, ln_b, y_ref, s_ref):
    tt = feats[0].shape[0]
    n_chunks = tt // CHUNK
    gl = GROUP_LANES
    n_groups = RWKV_WIDTH // gl
    heads_per_group = gl // HEAD_DIM
    probs = [(c, g) for c in range(n_chunks) for g in range(n_groups)]

    row_c = lax.broadcasted_iota(jnp.int32, (CHUNK, gl), 0)
    col_c = lax.broadcasted_iota(jnp.int32, (CHUNK, gl), 1) % CHUNK
    strict = row_c > col_c
    incl = row_c >= col_c
    eye_cat = (row_c == col_c).astype(F32)
    rb = lax.broadcasted_iota(jnp.int32, (gl, gl), 0) // HEAD_DIM
    cb = lax.broadcasted_iota(jnp.int32, (gl, gl), 1) // HEAD_DIM
    bd_mask = rb == cb
    tri_r = lax.broadcasted_iota(jnp.int32, (CHUNK, CHUNK), 0)
    tri_c = lax.broadcasted_iota(jnp.int32, (CHUNK, CHUNK), 1)
    tril_ones = (tri_r >= tri_c).astype(BF16)
    seg = bd_mask.astype(BF16)

    def bd(x):
        xb = x.astype(BF16)
        return jnp.where(bd_mask, jnp.concatenate([xb] * heads_per_group, axis=0),
                         jnp.zeros((), BF16))

    def fold(x):
        xm = jnp.where(bd_mask, x, 0.0)
        acc = xm[0:HEAD_DIM]
        for hh in range(1, heads_per_group):
            acc = acc + xm[hh * HEAD_DIM:(hh + 1) * HEAD_DIM]
        return acc

    def ld(i, p):
        c, g = p
        return feats[i][c * CHUNK:(c + 1) * CHUNK, g * gl:(g + 1) * gl]

    lw = [ld(1, p) for p in probs]
    na = [ld(4, p) for p in probs]
    bb = [ld(5, p) for p in probs]
    kf = [ld(2, p) for p in probs]
    r = [ld(0, p) for p in probs]
    v = [ld(3, p) for p in probs]

    def cumsum(x):
        hi = x.astype(BF16)
        lo = (x - hi.astype(F32)).astype(BF16)
        both = jnp.dot(tril_ones, jnp.concatenate([hi, lo], axis=1), preferred_element_type=F32)
        return both[:, :gl] + both[:, gl:]

    cum = _every(cumsum, lw)
    cum_last = [x[CHUNK - 1:CHUNK, :] for x in cum]
    e_out = [jnp.exp(-x) for x in cum]
    e_end = _every(lambda cl, x: jnp.exp(cl - x), cum_last, cum)
    a_t = _every(lambda n_, x, l_: n_ * jnp.exp(x - l_), na, cum, lw)
    r_t = _every(lambda r_, x: r_ * jnp.exp(x), r, cum)
    b_t = _every(lambda b_, e: b_ * e, bb, e_out)
    k_t = _every(lambda k_, e: k_ * e, kf, e_out)
    b_end = _every(lambda b_, e: b_ * e, bb, e_end)
    k_end = _every(lambda k_, e: k_ * e, kf, e_end)
    gamma = [jnp.exp(x) for x in cum_last]

    ar = _every(lambda a_, r_: jnp.concatenate([a_, r_], axis=0), a_t, r_t)
    pb = _every(lambda x, y: _dot_nt(x, bd(y)), ar, b_t)
    pk = _every(lambda x, y: _dot_nt(x, bd(y)), ar, k_t)
    l_ab = [jnp.where(strict, x[:CHUNK], 0.0) for x in pb]
    l_ak = [jnp.where(strict, x[:CHUNK], 0.0) for x in pk]
    m_rb = [jnp.where(incl, x[CHUNK:], 0.0) for x in pb]
    m_rk = [jnp.where(incl, x[CHUNK:], 0.0) for x in pk]

    x_acc = [eye_cat + l for l in l_ab]
    pw = _every(lambda l: _dot(l, bd(l)), l_ab)
    n_sq = int(math.log2(CHUNK)) - 1
    for lvl in range(n_sq):
        rhs = [bd(p_) for p_ in pw]
        if lvl < n_sq - 1:
            both = _every(lambda x, p_, w_: _dot(jnp.concatenate([x, p_], axis=0), w_),
                          x_acc, pw, rhs)
            x_acc = _every(lambda x, b_: x + b_[:CHUNK], x_acc, both)
            pw = [b_[CHUNK:] for b_ in both]
        else:
            x_acc = _every(lambda x, w_: x + _dot(x, w_), x_acc, rhs)
    t_inv = x_acc

    kv = _every(lambda la, mk, x: _dot(jnp.concatenate([la, mk], axis=0), bd(x)), l_ak, m_rk, v)
    y_loc = [x[:CHUNK] for x in kv]
    wu = _every(lambda t_, a_, y_: _dot(t_, jnp.concatenate([bd(a_), bd(y_)], axis=1)),
                t_inv, a_t, y_loc)
    w_t = [x[:, :gl] for x in wu]
    u_loc = [x[:, gl:] for x in wu]

    mwu = _every(lambda mb, w_, u_: _dot(mb, jnp.concatenate([bd(w_), bd(u_)], axis=1)),
                 m_rb, w_t, u_loc)
    q_c = _every(lambda r_, x: r_ + x[:, :gl], r_t, mwu)
    o_loc = _every(lambda x, y_: x[:, gl:] + y_[CHUNK:], mwu, kv)
    m_low = _every(lambda w_, b_: jnp.where(bd_mask, _dot_tn(w_, b_), 0.0).astype(BF16),
                   w_t, b_end)
    n_loc = _every(lambda u_, v_, b_, k_: fold(_dot_tn(jnp.concatenate([u_, v_], axis=0),
                                                       jnp.concatenate([b_, k_], axis=0))),
                   u_loc, v, b_end, k_end)

    state = [s_ref[g] for g in range(n_groups)]
    starts = []
    for c in range(n_chunks):
        starts.append(state)
        idx = [c * n_groups + g for g in range(n_groups)]
        state = [state[g] * gamma[i] + _dot(state[g], m_low[i]) + n_loc[i]
                 for g, i in enumerate(idx)]
    for g in range(n_groups):
        s_ref[g] = state[g]

    s0 = [starts[c][g] for (c, g) in probs]
    o = _every(lambda q_, s_, ol: _dot_nt(q_, bd(s_)) + ol, q_c, s0, o_loc)

    n_p = len(probs)
    unstack = lambda x: [x[i * CHUNK:(i + 1) * CHUNK] for i in range(n_p)]
    mu = unstack(_dot(jnp.concatenate(o, axis=0), seg) * (1.0 / HEAD_DIM))
    d = _every(lambda x, m_: x - m_, o, mu)
    var = unstack(_dot(jnp.concatenate([x * x for x in d], axis=0), seg) * (1.0 / HEAD_DIM))
    for i, (c, g) in enumerate(probs):
        ls = slice(g * gl, (g + 1) * gl)
        gn = d[i] * lax.rsqrt(var[i] + RWKV_GN_EPS) * ln_w[:, ls] + ln_b[:, ls]
        y = (gn + ld(7, (c, g))) * ld(6, (c, g))
        y_ref[c * CHUNK:(c + 1) * CHUNK, ls] = y.astype(y_ref.dtype)


def _wkv_sample_kernel(ft_ref, s_ref, s_out_ref, o_ref):
    hd = HEAD_DIM
    r, kf, na, bb = ft_ref[0], ft_ref[2], ft_ref[4], ft_ref[5]
    w = jnp.exp(ft_ref[1])

    def value_block(vb, carry):
        v0 = pl.multiple_of(vb * SUBLANES, SUBLANES)
        v_rows = ft_ref[3, pl.ds(v0, SUBLANES), :]
        outs = []
        for j in range(SUBLANES):
            s = s_ref[0, v0 + j]
            sa = jnp.sum(s * na, axis=0, keepdims=True)
            s_new = s * w + sa * bb + v_rows[j:j + 1] * kf
            s_out_ref[0, v0 + j] = s_new
            outs.append(jnp.sum(s_new * r, axis=0, keepdims=True))
        o_ref[pl.ds(v0, SUBLANES), :] = jnp.concatenate(outs, axis=0)
        return carry

    lax.fori_loop(0, hd // SUBLANES, value_block, 0)


def _wkv_sample_call(ft_s, state_t):
    h, hd, _, n = state_t.shape
    return pl.pallas_call(
        _wkv_sample_kernel,
        grid=(h,),
        in_specs=[
            pl.BlockSpec((6, hd, n), lambda i: (0, i, 0)),
            pl.BlockSpec((1, hd, hd, n), lambda i: (i, 0, 0, 0)),
        ],
        out_specs=(
            pl.BlockSpec((1, hd, hd, n), lambda i: (i, 0, 0, 0)),
            pl.BlockSpec((hd, n), lambda i: (i, 0)),
        ),
        out_shape=(
            jax.ShapeDtypeStruct(state_t.shape, F32),
            jax.ShapeDtypeStruct((h * hd, n), F32),
        ),
        compiler_params=pltpu.CompilerParams(
            dimension_semantics=("arbitrary",), vmem_limit_bytes=VMEM_LIMIT),
        name="wkv_sample",
    )(ft_s, state_t)


def _swa_prompt_kernel(sink_ref, q_ref, kp_ref, kc_ref, vp_ref, vc_ref, o_ref):
    j = pl.program_id(1)
    w = WINDOW
    tq = q_ref.shape[1]
    n_blk = tq // w
    gl = GROUP_LANES
    rows = SWA_GROUP * w
    probs = [(qi, g) for qi in range(n_blk) for g in range(SWA_KV_HEADS)]

    lane_head = lax.broadcasted_iota(jnp.int32, (w, gl), 1) // HEAD_DIM
    head_mask = [lane_head == h for h in range(SWA_GROUP)]
    qi_ = lax.broadcasted_iota(jnp.int32, (rows, 2 * w), 0) % w
    ki_ = lax.broadcasted_iota(jnp.int32, (rows, 2 * w), 1)
    diff = qi_ - (ki_ - w)
    in_window = (diff >= 0) & (diff < WINDOW)
    first_valid = in_window & ((j * tq + ki_ - w) >= 0)

    def keys(ref_prev, ref_cur, qi, g):
        ls = slice(g * gl, (g + 1) * gl)
        prev = ref_prev[0, :, ls] if qi == 0 else ref_cur[0, (qi - 1) * w:qi * w, ls]
        return jnp.concatenate([prev, ref_cur[0, qi * w:(qi + 1) * w, ls]], axis=0)

    def lhs(qi, g):
        qg = q_ref[0, qi * w:(qi + 1) * w, g * gl:(g + 1) * gl]
        return jnp.concatenate([jnp.where(m, qg, jnp.zeros((), BF16)) for m in head_mask], axis=0)

    sinks = []
    for g in range(SWA_KV_HEADS):
        sinks.append(jnp.concatenate(
            [jnp.full((w, 1), sink_ref[g * SWA_GROUP + h], F32) for h in range(SWA_GROUP)], axis=0))

    sink_lane = ki_ == qi_
    fill = [jnp.where(sink_lane, sk, MASK_VALUE) for sk in sinks]
    ones_k = jnp.ones((2 * w, LANES), BF16)

    s = [_dot_nt(lhs(qi, g), keys(kp_ref, kc_ref, qi, g)) for qi, g in probs]
    s = [jnp.where(first_valid if qi == 0 else in_window, x, fill[g])
         for x, (qi, g) in zip(s, probs)]
    m = [jnp.max(x, axis=-1, keepdims=True) for x in s]
    e = _every(lambda x, m_: jnp.exp(x - m_), s, m)
    inv = [1.0 / _dot(x, ones_k) for x in e]
    prob = _every(lambda e_, i_: jnp.where(sink_lane, jnp.zeros((), BF16),
                                           (e_ * jnp.concatenate([i_, i_], axis=1)).astype(BF16)),
                  e, inv)
    og = [_dot(p_, keys(vp_ref, vc_ref, qi, g)) for p_, (qi, g) in zip(prob, probs)]
    for x, (qi, g) in zip(og, probs):
        y = jnp.where(head_mask[0], x[0:w], 0.0)
        for h in range(1, SWA_GROUP):
            y = y + jnp.where(head_mask[h], x[h * w:(h + 1) * w], 0.0)
        o_ref[0, qi * w:(qi + 1) * w, g * gl:(g + 1) * gl] = y.astype(o_ref.dtype)


def _swa_prompt_call(sinks, q, k, v, tq):
    b, t, _ = q.shape
    w = WINDOW
    per = tq // w
    prev = lambda i, j: (i, jnp.maximum(j * per - 1, 0), 0)
    cur = lambda i, j: (i, j, 0)
    return pl.pallas_call(
        _swa_prompt_kernel,
        grid=(b, t // tq),
        in_specs=[
            pl.BlockSpec(memory_space=pltpu.SMEM),
            pl.BlockSpec((1, tq, SWA_WIDTH), cur),
            pl.BlockSpec((1, w, SWA_WIDTH), prev),
            pl.BlockSpec((1, tq, SWA_WIDTH), cur),
            pl.BlockSpec((1, w, SWA_WIDTH), prev),
            pl.BlockSpec((1, tq, SWA_WIDTH), cur),
        ],
        out_specs=pl.BlockSpec((1, tq, SWA_WIDTH), cur),
        out_shape=jax.ShapeDtypeStruct((b, t, SWA_WIDTH), BF16),
        compiler_params=pltpu.CompilerParams(
            dimension_semantics=("arbitrary", "arbitrary"), vmem_limit_bytes=VMEM_LIMIT),
        name="swa_prompt",
    )(sinks, q, k, k, v, v)


def _swa_sample_kernel(sink_ref, q_ref, knr_ref, vnr_ref, knt_ref, vnt_ref, ck_ref, cv_ref,
                       o_ref, ko_ref, vo_ref):
    nb, _, hd, l = ck_ref.shape
    q = q_ref[...]
    ck = ck_ref[...].reshape(nb * SWA_KV_HEADS, hd, l)
    cv = cv_ref[...].reshape(nb * SWA_KV_HEADS, hd, l)
    knr = knr_ref[...].astype(BF16).astype(F32)
    vnr = vnr_ref[...].astype(BF16).astype(F32)
    ki = lax.broadcasted_iota(jnp.int32, (1, 1, l), 2)
    kpos = PAST_LEN - l + ki
    diff = PAST_LEN - kpos
    valid = (diff >= 0) & (diff < WINDOW) & (kpos >= 0)
    s_c = lax.dot_general(q, ck.astype(BF16), (((2,), (1,)), ((0,), (0,))),
                          preferred_element_type=F32)
    s_c = jnp.where(valid, s_c, MASK_VALUE)
    s_n = jnp.sum(q.astype(F32) * knr, axis=-1, keepdims=True)
    sink = sink_ref[:, :, 0:1]
    m = jnp.maximum(jnp.maximum(jnp.max(s_c, axis=-1, keepdims=True), s_n), sink)
    e_c = jnp.exp(s_c - m)
    e_n = jnp.exp(s_n - m)
    denom = jnp.sum(e_c, axis=-1, keepdims=True) + e_n + jnp.exp(sink - m)
    p_c = (e_c / denom).astype(BF16)
    p_n = (e_n / denom).astype(BF16).astype(F32)
    o = lax.dot_general(p_c, cv.astype(BF16), (((2,), (2,)), ((0,), (0,))),
                        preferred_element_type=F32)
    o_ref[...] = o + p_n * vnr
    lane = lax.broadcasted_iota(jnp.int32, (hd, l), 1)
    for b in range(nb):
        for g in range(SWA_KV_HEADS):
            rows = slice(g * hd, (g + 1) * hd)
            ko_ref[b, g] = jnp.where(lane == l - 1, knt_ref[0, rows, b:b + 1],
                                     pltpu.roll(ck_ref[b, g], l - 1, 1))
            vo_ref[b, g] = jnp.where(lane == l - 1, vnt_ref[0, rows, b:b + 1],
                                     pltpu.roll(cv_ref[b, g], l - 1, 1))


def _swa_sample_call(sink_t, q8, knr, vnr, knt, vnt, ck, cv, nb):
    n, g, hd, l = ck.shape
    rows = nb * g
    blk3 = lambda i: (i, 0, 0)
    blk4 = lambda i: (i, 0, 0, 0)
    return pl.pallas_call(
        _swa_sample_kernel,
        grid=(n // nb,),
        in_specs=[
            pl.BlockSpec((rows, SUBLANES, LANES), lambda i: (0, 0, 0)),
            pl.BlockSpec((rows, SUBLANES, hd), blk3),
            pl.BlockSpec((rows, 1, hd), blk3),
            pl.BlockSpec((rows, 1, hd), blk3),
            pl.BlockSpec((1, g * hd, nb), blk3),
            pl.BlockSpec((1, g * hd, nb), blk3),
            pl.BlockSpec((nb, g, hd, l), blk4),
            pl.BlockSpec((nb, g, hd, l), blk4),
        ],
        out_specs=(
            pl.BlockSpec((rows, SUBLANES, hd), blk3),
            pl.BlockSpec((nb, g, hd, l), blk4),
            pl.BlockSpec((nb, g, hd, l), blk4),
        ),
        out_shape=(
            jax.ShapeDtypeStruct((n * g, SUBLANES, hd), F32),
            jax.ShapeDtypeStruct(ck.shape, F32),
            jax.ShapeDtypeStruct(cv.shape, F32),
        ),
        compiler_params=pltpu.CompilerParams(
            dimension_semantics=("arbitrary",), vmem_limit_bytes=VMEM_LIMIT),
        name="swa_sample",
    )(sink_t, q8, knr, vnr, knt, vnt, ck, cv)


def _tail_mix(x, y_rwkv, y_swa, g1, sh2, sc2, w_out, n_post_mix, n_pre_ffn):
    mix = _dot(y_rwkv, w_out[:RWKV_WIDTH]) + _dot(y_swa, w_out[RWKV_WIDTH:])
    x1 = x + g1 * _rmsnorm(mix, n_post_mix)
    h2 = (_rmsnorm(x1, n_pre_ffn) * (1.0 + sc2) + sh2).astype(BF16)
    return x1, h2


def _tail_prompt_kernel(x_ref, yr_ref, ys_ref, mod_ref, w_out, n_post_mix, n_pre_ffn,
                        n_post_ffn, w_up_a, w_up_b, w_down, cw_ref, cb_ref,
                        y_ref, cp_ref, carry_ref, act_ref):
    t = pl.program_id(1)
    tm = x_ref.shape[1]
    sub = min(tm, TAIL_SUB)
    grp = min(tm, TAIL_GROUP)

    @pl.when(t == 0)
    def _():
        carry_ref[...] = jnp.zeros_like(carry_ref)

    g1, sh2, sc2, g2 = mod_ref[2, 0], mod_ref[3, 0], mod_ref[4, 0], mod_ref[5, 0]

    def cols_of(c, half):
        return slice(half * D_FF + c * FF_CHUNK, half * D_FF + (c + 1) * FF_CHUNK)

    def conv(u, cols):
        ext = jnp.concatenate([carry_ref[:, cols], u], axis=0)
        last = u[sub - SUBLANES:]
        carry_ref[:, cols] = last
        cp_ref[0, :, cols] = last
        return (cb_ref[:, cols] + pltpu.roll(ext, 2, 0)[SUBLANES:] * cw_ref[0:1, cols]
                + pltpu.roll(ext, 1, 0)[SUBLANES:] * cw_ref[1:2, cols] + u * cw_ref[2:3, cols])

    def group(gi, carry):
        base = pl.multiple_of(gi * grp, grp)
        blocks = [(pl.ds(base + j * sub, sub), slice(j * sub, (j + 1) * sub))
                  for j in range(grp // sub)]
        mixes = [_dot(yr_ref[0, rb, :], w_out[:RWKV_WIDTH]) + _dot(ys_ref[0, rb, :], w_out[RWKV_WIDTH:])
                 for rb, _ in blocks]
        for (rb, ab), mix in zip(blocks, mixes):
            x1 = x_ref[0, rb, :] + g1 * _rmsnorm(mix, n_post_mix[...])
            h2 = (_rmsnorm(x1, n_pre_ffn[...]) * (1.0 + sc2) + sh2).astype(BF16)

            def up(c):
                chunk = slice(c * FF_CHUNK, (c + 1) * FF_CHUNK)
                return [jnp.dot(h2, w[:, chunk], preferred_element_type=F32)
                        for w in (w_up_a, w_up_b)]

            def down(c):
                rows = slice(c * FF_CHUNK, (c + 1) * FF_CHUNK)
                return jnp.dot(act_ref[ab, rows], w_down[rows, :], preferred_element_type=F32)

            u_next = up(0)
            ff = None
            for c in range(N_FF_CHUNKS):
                u_cur = u_next
                if c + 1 < N_FF_CHUNKS:
                    u_next = up(c + 1)
                if c >= 1:
                    part = down(c - 1)
                    ff = part if ff is None else ff + part
                za, zb = [conv(u_cur[half], cols_of(c, half)) for half in range(2)]
                act_ref[ab, c * FF_CHUNK:(c + 1) * FF_CHUNK] = (_silu(za) * zb).astype(BF16)
            ff = ff + down(N_FF_CHUNKS - 1)
            y_ref[0, rb, :] = x1 + g2 * _rmsnorm(ff, n_post_ffn[...])
        return carry

    lax.fori_loop(0, tm // grp, group, 0)


def _tail_sample_kernel(x_ref, o_ref, feat_ref, ys_ref, mod_ref, ln_w, ln_b, w_out, n_post_mix,
                        n_pre_ffn, n_post_ffn, wa_ref, wb_ref, wd_ref, p0a, p0b, p1a, p1b,
                        cwa, cwb, cba, cbb,
                        y_ref, ua_ref, ub_ref, w_out_b_ref, wa_b_ref, wb_b_ref, wd_b_ref,
                        x1_ref, h2_ref, acc_ref):
    c = pl.program_id(0)

    @pl.when(c == 0)
    def _():
        w_out_b = w_out[...].astype(BF16)
        w_out_b_ref[...] = w_out_b
        seg = _seg_ones(RWKV_WIDTH)
        y_rwkv = _gn_epilogue(o_ref[...].T, feat_ref[7], feat_ref[6], ln_w[...], ln_b[...], seg)
        x1, h2 = _tail_mix(x_ref[...], y_rwkv, ys_ref[...], mod_ref[2], mod_ref[3], mod_ref[4],
                           w_out_b, n_post_mix[...], n_pre_ffn[...])
        x1_ref[...] = x1
        h2_ref[...] = h2
        acc_ref[...] = jnp.zeros_like(acc_ref)

    wa = wa_ref[...].astype(BF16)
    wb = wb_ref[...].astype(BF16)
    wd = wd_ref[...].astype(BF16)
    wa_b_ref[...] = wa
    wb_b_ref[...] = wb
    wd_b_ref[...] = wd
    h2 = h2_ref[...]
    ua = jnp.dot(h2, wa, preferred_element_type=F32)
    ub = jnp.dot(h2, wb, preferred_element_type=F32)
    ua_ref[...] = ua
    ub_ref[...] = ub
    za = cba[...] + p0a[...] * cwa[0:1, :] + p1a[...] * cwa[1:2, :] + ua * cwa[2:3, :]
    zb = cbb[...] + p0b[...] * cwb[0:1, :] + p1b[...] * cwb[1:2, :] + ub * cwb[2:3, :]
    acc_ref[...] += _dot(_silu(za) * zb, wd)

    @pl.when(c == pl.num_programs(0) - 1)
    def _():
        y_ref[...] = x1_ref[...] + mod_ref[5] * _rmsnorm(acc_ref[...], n_post_ffn[...])


def _tail_prompt_call(x, y_rwkv, y_swa, mod_p, consts, tm):
    b, t, _ = x.shape
    tok = lambda i, j: (i, j, 0)
    in_specs = [
        pl.BlockSpec((1, tm, D_MODEL), tok),
        pl.BlockSpec((1, tm, RWKV_WIDTH), tok),
        pl.BlockSpec((1, tm, SWA_WIDTH), tok),
        pl.BlockSpec((6, 1, 1, D_MODEL), lambda i, j: (0, i, 0, 0)),
    ] + [_const_spec(c, 2) for c in consts]
    return pl.pallas_call(
        _tail_prompt_kernel,
        grid=(b, t // tm),
        in_specs=in_specs,
        out_specs=(
            pl.BlockSpec((1, tm, D_MODEL), tok),
            pl.BlockSpec((1, SUBLANES, 2 * D_FF), lambda i, j: (i, 0, 0)),
        ),
        out_shape=(
            jax.ShapeDtypeStruct((b, t, D_MODEL), F32),
            jax.ShapeDtypeStruct((b, SUBLANES, 2 * D_FF), F32),
        ),
        scratch_shapes=[
            pltpu.VMEM((SUBLANES, 2 * D_FF), F32),
            pltpu.VMEM((min(tm, TAIL_GROUP), D_FF), BF16),
        ],
        compiler_params=pltpu.CompilerParams(
            dimension_semantics=("arbitrary", "arbitrary"), vmem_limit_bytes=VMEM_LIMIT),
        name="tail_prompt",
    )(x, y_rwkv, y_swa, mod_p, *consts)


def _tail_sample_call(x, o_t, feat_s, y_swa, mod, p0, p1, ln_w, ln_b, w_out, n_post_mix,
                      n_pre_ffn, n_post_ffn, w_up, w_down, cw, cb):
    n = x.shape[0]
    nc = N_FF_CHUNKS
    fc = FF_CHUNK
    whole = lambda a: pl.BlockSpec(a.shape, lambda c, nd=a.ndim: (0,) * nd)
    gate = lambda rows: pl.BlockSpec((rows, fc), lambda c: (0, c))
    value = lambda rows: pl.BlockSpec((rows, fc), lambda c: (0, nc + c))
    in_specs = [
        whole(x), whole(o_t), whole(feat_s), whole(y_swa),
        pl.BlockSpec((mod.shape[0], n, D_MODEL), lambda c: (0, 0, 0)),
        whole(ln_w), whole(ln_b), whole(w_out), whole(n_post_mix), whole(n_pre_ffn),
        whole(n_post_ffn),
        gate(D_MODEL), value(D_MODEL), pl.BlockSpec((fc, D_MODEL), lambda c: (c, 0)),
        gate(n), value(n), gate(n), value(n), gate(3), value(3), gate(1), value(1),
    ]
    out_shape = (
        jax.ShapeDtypeStruct((n, D_MODEL), F32),
        jax.ShapeDtypeStruct((n, D_FF), F32),
        jax.ShapeDtypeStruct((n, D_FF), F32),
        jax.ShapeDtypeStruct((D_MODEL, D_MODEL), BF16),
        jax.ShapeDtypeStruct((D_MODEL, D_FF), BF16),
        jax.ShapeDtypeStruct((D_MODEL, D_FF), BF16),
        jax.ShapeDtypeStruct((D_FF, D_MODEL), BF16),
    )
    chunk_cols = lambda rows: pl.BlockSpec((rows, fc), lambda c: (0, c))
    out_specs = (
        pl.BlockSpec((n, D_MODEL), lambda c: (0, 0)),
        chunk_cols(n), chunk_cols(n),
        pl.BlockSpec((D_MODEL, D_MODEL), lambda c: (0, 0)),
        chunk_cols(D_MODEL), chunk_cols(D_MODEL),
        pl.BlockSpec((fc, D_MODEL), lambda c: (c, 0)),
    )
    return pl.pallas_call(
        _tail_sample_kernel,
        grid=(nc,),
        in_specs=in_specs,
        out_specs=out_specs,
        out_shape=out_shape,
        scratch_shapes=[
            pltpu.VMEM((n, D_MODEL), F32),
            pltpu.VMEM((n, D_MODEL), BF16),
            pltpu.VMEM((n, D_MODEL), F32),
        ],
        compiler_params=pltpu.CompilerParams(
            dimension_semantics=("arbitrary",), vmem_limit_bytes=VMEM_LIMIT),
        name="tail_sample",
    )(x, o_t, feat_s, y_swa, mod, ln_w, ln_b, w_out, n_post_mix, n_pre_ffn, n_post_ffn,
      w_up, w_up, w_down, p0, p0, p1, p1, cw, cw, cb, cb)


def _rope_tables(pos):
    half = HEAD_DIM // 2
    inv = ROPE_THETA ** (-jnp.arange(half, dtype=F32) / half)
    ang = pos.astype(F32)[:, None] * inv[None, :]
    cos, sin = jnp.cos(ang), jnp.sin(ang)
    cos_h = jnp.concatenate([cos, cos], axis=-1)
    sin_h = jnp.concatenate([-sin, sin], axis=-1)
    return jnp.tile(cos_h, (1, LANES // HEAD_DIM)), jnp.tile(sin_h, (1, LANES // HEAD_DIM))


def kernel(x_prompt, x_sample, state_rwkv_wkv, state_rwkv_shift, cache_swa_k, cache_swa_v,
           state_ffn_conv, c_prompt, c_sample, w_ada, b_ada, norm_pre_mix, norm_post_mix,
           norm_pre_ffn, norm_post_ffn, w_in, rwkv_mu, rwkv_w0, rwkv_w_up, rwkv_a0, rwkv_a_up,
           rwkv_g_up, rwkv_k_k, rwkv_k_a, rwkv_r_k, rwkv_ln_w, rwkv_ln_b, swa_sinks, w_out,
           ffn_w_up, ffn_conv_w, ffn_conv_b, ffn_w_down):
    depth = w_ada.shape[0]
    assert depth == 1 and x_sample.shape[1] == 1
    b, t, _ = x_prompt.shape
    n = x_sample.shape[0]
    tm = min(256, t)
    blk = min(4 * tm, t)
    assert t % blk == 0 and tm % WINDOW == 0 and tm % CHUNK == 0 and n % SUBLANES == 0
    nb_swa = 2 * SUBLANES if n % (2 * SUBLANES) == 0 else SUBLANES
    li = 0

    row = lambda v: v.reshape(1, -1)
    zeros_l = jnp.zeros((64, RWKV_WIDTH), F32)
    wa_up = jnp.concatenate([
        jnp.concatenate([rwkv_w_up[li], zeros_l], axis=1),
        jnp.concatenate([zeros_l, rwkv_a_up[li]], axis=1)], axis=0).astype(BF16)
    hid = jnp.arange(GROUP_LANES) // HEAD_DIM
    seg_blk = (hid[:, None] == hid[None, :]).astype(BF16)
    proj_consts = (row(norm_pre_mix[li]), w_in[li], row(rwkv_mu[li]), row(rwkv_w0[li]), wa_up,
                   row(rwkv_a0[li]), rwkv_g_up[li].astype(BF16), row(rwkv_k_k[li]),
                   row(rwkv_k_a[li]), row(rwkv_r_k[li]), seg_blk)
    norms = (row(norm_post_mix[li]), row(norm_pre_ffn[li]), row(norm_post_ffn[li]))
    cw, cb = ffn_conv_w[li], row(ffn_conv_b[li])
    ln_w, ln_b = row(rwkv_ln_w[li]), row(rwkv_ln_b[li])

    mod = _ada_call(jnp.concatenate([c_sample, c_prompt], axis=0), w_ada[li], row(b_ada[li]))
    mod_p = mod[:, n:].reshape(6, b, 1, D_MODEL)

    cos_s, sin_s = _rope_tables(jnp.full((1,), PAST_LEN, jnp.int32))
    feat_s, ft_s, q_s, kn_s, vn_s, p_s, w_in_b = _proj_sample_call(
        x_sample[:, 0], mod, cos_s, sin_s, state_rwkv_shift[li], proj_consts)

    state_t = jnp.transpose(state_rwkv_wkv[li], (1, 2, 3, 0))
    wkv_t, o_t = _wkv_sample_call(ft_s, state_t)
    wkv_s = jnp.transpose(wkv_t, (3, 0, 1, 2))

    g2 = SWA_KV_HEADS
    q4 = q_s.reshape(n * g2, SWA_GROUP, HEAD_DIM)
    q8 = jnp.concatenate([q4, jnp.zeros_like(q4)], axis=1)
    sink_t = jnp.broadcast_to(
        jnp.concatenate([swa_sinks[li].reshape(g2, SWA_GROUP),
                         jnp.full((g2, SWA_GROUP), MASK_VALUE, F32)], axis=1)[None, :, :, None],
        (nb_swa, g2, SUBLANES, LANES)).reshape(nb_swa * g2, SUBLANES, LANES)
    cols = lambda a: a.reshape(n // nb_swa, nb_swa, KV_WIDTH).transpose(0, 2, 1)
    ck = jnp.transpose(cache_swa_k[li], (0, 2, 3, 1))
    cv = jnp.transpose(cache_swa_v[li], (0, 2, 3, 1))
    o_att, k_t, v_t = _swa_sample_call(
        sink_t, q8, kn_s.reshape(n * g2, 1, HEAD_DIM), vn_s.reshape(n * g2, 1, HEAD_DIM),
        cols(kn_s), cols(vn_s), ck, cv, nb_swa)
    y_swa_s = o_att[:, :SWA_GROUP].reshape(n, SWA_WIDTH)
    k_s = jnp.transpose(k_t, (0, 3, 1, 2))
    v_s = jnp.transpose(v_t, (0, 3, 1, 2))

    conv0 = state_ffn_conv[li]
    y_s, ua_s, ub_s, w_out_b, w_up_a_b, w_up_b_b, w_down_b = _tail_sample_call(
        x_sample[:, 0], o_t, feat_s, y_swa_s.astype(BF16), mod, conv0[:, 0], conv0[:, 1],
        ln_w, ln_b, w_out[li], *norms, ffn_w_up[li], ffn_w_down[li], cw, cb)
    conv_s = jnp.stack([conv0[:, 1], jnp.concatenate([ua_s, ub_s], axis=1)], axis=1)

    cos_p, sin_p = _rope_tables(jnp.arange(t, dtype=jnp.int32))
    proj_consts_p = proj_consts[:1] + (w_in_b,) + proj_consts[2:]
    y_rwkv_p, s_cat, q_p, kx_p, vx_p, klast, vlast, plast = _mix_prompt_call(
        x_prompt, mod_p, cos_p, sin_p, proj_consts_p + (ln_w, ln_b), blk)
    y_swa_p = _swa_prompt_call(swa_sinks[li], q_p, kx_p, vx_p, min(2 * blk, t))
    tail_consts = (w_out_b,) + norms + (w_up_a_b, w_up_b_b, w_down_b, cw, cb)
    y_p, cp = _tail_prompt_call(x_prompt, y_rwkv_p, y_swa_p, mod_p, tail_consts, blk)

    wkv_p = s_cat.reshape(b, HEAD_DIM, RWKV_HEADS, HEAD_DIM).transpose(0, 2, 1, 3)
    shift_p = plast[:, SUBLANES - 1]
    k_p = klast.reshape(b, WINDOW, SWA_KV_HEADS, HEAD_DIM)
    v_p = vlast.reshape(b, WINDOW, SWA_KV_HEADS, HEAD_DIM)
    conv_p = cp[:, SUBLANES - 2:]

    expand = lambda a: a[None]
    return (y_p, y_s[:, None, :], expand(wkv_p), expand(shift_p), expand(k_p), expand(v_p),
            expand(conv_p), expand(wkv_s), expand(p_s),
            expand(k_s), expand(v_s), expand(conv_s))
```

```python
import math

import jax
import jax.numpy as jnp
from jax import lax
from jax.experimental import pallas as pl
from jax.experimental.pallas import tpu as pltpu

D_MODEL = 1024
HEAD_DIM = 64
RWKV_WIDTH = 512
RWKV_HEADS = 8
RWKV_COLS = 1792
RWKV_GN_EPS = 64e-5
SWA_WIDTH = 512
SWA_HEADS = 8
SWA_KV_HEADS = 2
SWA_GROUP = 4
KV_WIDTH = SWA_KV_HEADS * HEAD_DIM
WINDOW = 128
PAST_LEN = 16384
ROPE_THETA = 10000.0
ATTN_SCALE = HEAD_DIM ** -0.5
D_FF = 2816
NORM_EPS = 1e-6
MASK_VALUE = -1e30
PROJ_COLS = RWKV_COLS + SWA_WIDTH + 2 * KV_WIDTH

LANES = 128
SUBLANES = 8
CHUNK = 64
GROUP_LANES = 256
FF_CHUNK = 256
N_FF_CHUNKS = D_FF // FF_CHUNK
TAIL_SUB = 256
MIX_SUB = 256
TAIL_GROUP = 256
VMEM_LIMIT = 56 * 1024 * 1024

F32 = jnp.float32
BF16 = jnp.bfloat16


def _sigmoid(x):
    return 1.0 / (1.0 + jnp.exp(-x))


def _silu(x):
    return x * _sigmoid(x)


def _rmsnorm(x, g):
    return x * lax.rsqrt(jnp.mean(x * x, axis=-1, keepdims=True) + NORM_EPS) * g


def _dot(a, b):
    return jnp.dot(a.astype(BF16), b.astype(BF16), preferred_element_type=F32)


def _dot_nt(a, b):
    return lax.dot_general(a.astype(BF16), b.astype(BF16), (((1,), (1,)), ((), ())),
                           preferred_element_type=F32)


def _dot_tn(a, b):
    return lax.dot_general(a.astype(BF16), b.astype(BF16), (((0,), (0,)), ((), ())),
                           preferred_element_type=F32)


def _every(fn, *lists):
    return [fn(*a) for a in zip(*lists)]


def _swap_halves(x):
    w = x.shape[-1]
    lane = lax.broadcasted_iota(jnp.int32, x.shape, x.ndim - 1)
    lo = (lane & (HEAD_DIM // 2)) == 0
    return jnp.where(lo, pltpu.roll(x, w - HEAD_DIM // 2, x.ndim - 1),
                     pltpu.roll(x, HEAD_DIM // 2, x.ndim - 1))


def _rope(x, cos, sin):
    reps = x.shape[-1] // LANES
    cos_w = jnp.concatenate([cos] * reps, axis=-1) if reps > 1 else cos
    sin_w = jnp.concatenate([sin] * reps, axis=-1) if reps > 1 else sin
    return x * cos_w + _swap_halves(x) * sin_w


def _proj_features(x, shift, scale, g_pre, w_in, prev_fn, mu, w0, wa_up, a0, g_up,
                   k_k, k_a, r_k, seg_blk, cos, sin):
    h = _rmsnorm(x, g_pre) * (1.0 + scale) + shift
    p = _dot(h, w_in)
    p_rwkv = p[:, :RWKV_COLS]
    prev = prev_fn(p_rwkv)
    xm = p_rwkv + (prev - p_rwkv) * mu
    r = xm[:, 0:512]
    k = xm[:, 512:1024]
    v = xm[:, 1024:1536]
    wa = xm[:, 1536:1664]
    gd = xm[:, 1664:1792]
    lane = lax.broadcasted_iota(jnp.int32, wa.shape, 1)
    wa_act = jnp.where(lane < 64, jnp.tanh(wa), wa)
    lora = _dot(wa_act, wa_up)
    lw = -math.exp(-0.5) * _sigmoid(w0 + lora[:, :512])
    a = _sigmoid(a0 + lora[:, 512:])
    g = _dot(_sigmoid(gd), g_up)
    kk = k * k_k
    kf = k * (1.0 + (a - 1.0) * k_a)
    gl = seg_blk.shape[0]
    n = x.shape[0]
    sums = [_dot(jnp.concatenate([(kk * kk)[:, i:i + gl], (r * kf * r_k)[:, i:i + gl]], axis=0),
                 seg_blk) for i in range(0, RWKV_WIDTH, gl)]
    ss = jnp.concatenate([s_[:n] for s_ in sums], axis=1)
    kk = kk / jnp.maximum(jnp.sqrt(ss), 1e-12)
    bonus = jnp.concatenate([s_[n:] for s_ in sums], axis=1) * v
    feats = (r, lw, kf, v, -kk, kk * a, g, bonus)
    q = _rope(p[:, RWKV_COLS:RWKV_COLS + SWA_WIDTH], cos, sin)
    ks = _rope(p[:, RWKV_COLS + SWA_WIDTH:RWKV_COLS + SWA_WIDTH + KV_WIDTH], cos, sin)
    vs = p[:, RWKV_COLS + SWA_WIDTH + KV_WIDTH:]
    return feats, q, ks, vs, p_rwkv


def _gn_epilogue(o, bonus, g, ln_w, ln_b, seg):
    mu = _dot(o, seg) * (1.0 / HEAD_DIM)
    d = o - mu
    var = _dot(d * d, seg) * (1.0 / HEAD_DIM)
    gn = d * lax.rsqrt(var + RWKV_GN_EPS) * ln_w + ln_b
    return (gn + bonus) * g


def _seg_ones(n):
    r = lax.broadcasted_iota(jnp.int32, (n, n), 0) // HEAD_DIM
    c = lax.broadcasted_iota(jnp.int32, (n, n), 1) // HEAD_DIM
    return (r == c).astype(BF16)


def _ada_kernel(c_ref, w_ref, b_ref, o_ref):
    o_ref[0] = _dot(_silu(c_ref[...]), w_ref[...]) + b_ref[...]


def _ada_call(c_all, w_ada, b_ada):
    rows = c_all.shape[0]
    return pl.pallas_call(
        _ada_kernel,
        grid=(6,),
        in_specs=[
            pl.BlockSpec((rows, D_MODEL), lambda j: (0, 0)),
            pl.BlockSpec((D_MODEL, D_MODEL), lambda j: (0, j)),
            pl.BlockSpec((1, D_MODEL), lambda j: (0, j)),
        ],
        out_specs=pl.BlockSpec((1, rows, D_MODEL), lambda j: (j, 0, 0)),
        out_shape=jax.ShapeDtypeStruct((6, rows, D_MODEL), F32),
        compiler_params=pltpu.CompilerParams(
            dimension_semantics=("arbitrary",), vmem_limit_bytes=VMEM_LIMIT),
        name="ada",
    )(c_all, w_ada, b_ada)


def _expand_kv(x):
    lane = lax.broadcasted_iota(jnp.int32, x.shape, 1)
    rolled = pltpu.roll(x, HEAD_DIM, 1)
    g0 = jnp.where(lane < HEAD_DIM, x, rolled)
    g1 = jnp.where(lane < HEAD_DIM, rolled, x)
    return jnp.concatenate([g0, g0, g1, g1], axis=1)


def _mix_prompt_kernel(x_ref, mod_ref, cos_ref, sin_ref, g_pre, w_in, mu, w0, wa_up, a0,
                       g_up, k_k, k_a, r_k, seg_blk, ln_w, ln_b,
                       y_ref, s_out_ref, q_ref, k_ref, v_ref, klast_ref, vlast_ref, plast_ref,
                       carry_ref, s_ref):
    t = pl.program_id(1)

    @pl.when(t == 0)
    def _():
        carry_ref[...] = jnp.zeros_like(carry_ref)
        s_ref[...] = jnp.zeros_like(s_ref)

    sub = min(x_ref.shape[1], MIX_SUB)

    def sub_tile(i, carry):
        rows = pl.ds(pl.multiple_of(i * sub, sub), sub)
        carry_row = carry_ref[SUBLANES - 1:SUBLANES, :]

        def prev_fn(p_rwkv):
            row = lax.broadcasted_iota(jnp.int32, p_rwkv.shape, 0)
            return jnp.where(row == 0, carry_row, pltpu.roll(p_rwkv, 1, 0))

        feats, q, ks, vs, p_rwkv = _proj_features(
            x_ref[0, rows, :], mod_ref[0, 0], mod_ref[1, 0], g_pre[...], w_in[...], prev_fn,
            mu[...], w0[...], wa_up[...], a0[...], g_up[...], k_k[...], k_a[...], r_k[...],
            seg_blk[...], cos_ref[rows, :], sin_ref[rows, :])
        q_ref[0, rows, :] = (q * ATTN_SCALE).astype(BF16)
        k_ref[0, rows, :] = _expand_kv(ks).astype(BF16)
        v_ref[0, rows, :] = _expand_kv(vs).astype(BF16)
        last = p_rwkv[sub - SUBLANES:, :]
        carry_ref[...] = last
        plast_ref[0] = last
        klast_ref[0] = ks[sub - WINDOW:, :]
        vlast_ref[0] = vs[sub - WINDOW:, :]
        _wkv_tile(feats, ln_w, ln_b, y_ref.at[0, rows, :], s_ref)
        return carry

    lax.fori_loop(0, x_ref.shape[1] // sub, sub_tile, 0)

    @pl.when(t == pl.num_programs(1) - 1)
    def _():
        for g in range(RWKV_WIDTH // GROUP_LANES):
            s_out_ref[0, :, g * GROUP_LANES:(g + 1) * GROUP_LANES] = s_ref[g]


def _proj_sample_kernel(x_ref, mod_ref, cos_ref, sin_ref, prev_ref, g_pre, w_in, mu, w0,
                        wa_up, a0, g_up, k_k, k_a, r_k, seg_blk,
                        feat_ref, ft_ref, q_ref, k_ref, v_ref, p_ref, w_in_b_ref):
    w_in_b = w_in[...].astype(BF16)
    w_in_b_ref[...] = w_in_b
    feats, q, ks, vs, p_rwkv = _proj_features(
        x_ref[...], mod_ref[0], mod_ref[1], g_pre[...], w_in_b, lambda p: prev_ref[...],
        mu[...], w0[...], wa_up[...], a0[...], g_up[...], k_k[...], k_a[...], r_k[...],
        seg_blk[...], cos_ref[...], sin_ref[...])
    for i, f in enumerate(feats):
        feat_ref[i] = f
    for i in range(6):
        ft_ref[i] = feats[i].T
    q_ref[...] = (q * ATTN_SCALE).astype(BF16)
    k_ref[...] = ks
    v_ref[...] = vs
    p_ref[...] = p_rwkv


def _const_spec(arr, grid_rank):
    zeros = (0,) * arr.ndim
    if grid_rank == 1:
        return pl.BlockSpec(arr.shape, lambda i: zeros)
    return pl.BlockSpec(arr.shape, lambda i, j: zeros)


def _mix_prompt_call(x, mod_p, cos, sin, consts, tm):
    b, t, _ = x.shape
    nt = t // tm
    in_specs = [
        pl.BlockSpec((1, tm, D_MODEL), lambda i, j: (i, j, 0)),
        pl.BlockSpec((6, 1, 1, D_MODEL), lambda i, j: (0, i, 0, 0)),
        pl.BlockSpec((tm, LANES), lambda i, j: (j, 0)),
        pl.BlockSpec((tm, LANES), lambda i, j: (j, 0)),
    ] + [_const_spec(c, 2) for c in consts]
    out_shape = (
        jax.ShapeDtypeStruct((b, t, RWKV_WIDTH), BF16),
        jax.ShapeDtypeStruct((b, HEAD_DIM, RWKV_WIDTH), F32),
        jax.ShapeDtypeStruct((b, t, SWA_WIDTH), BF16),
        jax.ShapeDtypeStruct((b, t, SWA_WIDTH), BF16),
        jax.ShapeDtypeStruct((b, t, SWA_WIDTH), BF16),
        jax.ShapeDtypeStruct((b, WINDOW, KV_WIDTH), F32),
        jax.ShapeDtypeStruct((b, WINDOW, KV_WIDTH), F32),
        jax.ShapeDtypeStruct((b, SUBLANES, RWKV_COLS), F32),
    )
    out_specs = (
        pl.BlockSpec((1, tm, RWKV_WIDTH), lambda i, j: (i, j, 0)),
        pl.BlockSpec((1, HEAD_DIM, RWKV_WIDTH), lambda i, j: (i, 0, 0)),
        pl.BlockSpec((1, tm, SWA_WIDTH), lambda i, j: (i, j, 0)),
        pl.BlockSpec((1, tm, SWA_WIDTH), lambda i, j: (i, j, 0)),
        pl.BlockSpec((1, tm, SWA_WIDTH), lambda i, j: (i, j, 0)),
        pl.BlockSpec((1, WINDOW, KV_WIDTH), lambda i, j: (i, 0, 0)),
        pl.BlockSpec((1, WINDOW, KV_WIDTH), lambda i, j: (i, 0, 0)),
        pl.BlockSpec((1, SUBLANES, RWKV_COLS), lambda i, j: (i, 0, 0)),
    )
    return pl.pallas_call(
        _mix_prompt_kernel,
        grid=(b, nt),
        in_specs=in_specs,
        out_specs=out_specs,
        out_shape=out_shape,
        scratch_shapes=[
            pltpu.VMEM((SUBLANES, RWKV_COLS), F32),
            pltpu.VMEM((RWKV_WIDTH // GROUP_LANES, HEAD_DIM, GROUP_LANES), F32),
        ],
        compiler_params=pltpu.CompilerParams(
            dimension_semantics=("arbitrary", "arbitrary"), vmem_limit_bytes=VMEM_LIMIT),
        name="mix_prompt",
    )(x, mod_p, cos, sin, *consts)


def _mod_rows_spec(mod, n):
    return pl.BlockSpec((mod.shape[0], n, mod.shape[2]), lambda i: (0, 0, 0))


def _proj_sample_call(x, mod, cos, sin, prev, consts):
    n = x.shape[0]
    args = (x, mod, cos, sin, prev) + tuple(consts)
    out_shape = (
        jax.ShapeDtypeStruct((8, n, RWKV_WIDTH), F32),
        jax.ShapeDtypeStruct((6, RWKV_WIDTH, n), F32),
        jax.ShapeDtypeStruct((n, SWA_WIDTH), BF16),
        jax.ShapeDtypeStruct((n, KV_WIDTH), F32),
        jax.ShapeDtypeStruct((n, KV_WIDTH), F32),
        jax.ShapeDtypeStruct((n, RWKV_COLS), F32),
        jax.ShapeDtypeStruct((D_MODEL, PROJ_COLS), BF16),
    )
    return pl.pallas_call(
        _proj_sample_kernel,
        grid=(1,),
        in_specs=[_mod_rows_spec(a, n) if i == 1 else _const_spec(a, 1)
                  for i, a in enumerate(args)],
        out_specs=tuple(pl.BlockSpec(s.shape, lambda i, nd=len(s.shape): (0,) * nd)
                        for s in out_shape),
        out_shape=out_shape,
        compiler_params=pltpu.CompilerParams(
            dimension_semantics=("arbitrary",), vmem_limit_bytes=VMEM_LIMIT),
        name="proj_sample",
    )(*args)


def _wkv_tile(feats, ln_w, ln_b, y_ref, s_ref):
    tt = feats[0].shape[0]
    n_chunks = tt // CHUNK
    gl = GROUP_LANES
    n_groups = RWKV_WIDTH // gl
    heads_per_group = gl // HEAD_DIM
    probs = [(c, g) for c in range(n_chunks) for g in range(n_groups)]

    row_c = lax.broadcasted_iota(jnp.int32, (CHUNK, gl), 0)
    col_c = lax.broadcasted_iota(jnp.int32, (CHUNK, gl), 1) % CHUNK
    strict = row_c > col_c
    incl = row_c >= col_c
    eye_cat = (row_c == col_c).astype(F32)
    rb = lax.broadcasted_iota(jnp.int32, (gl, gl), 0) // HEAD_DIM
    cb = lax.broadcasted_iota(jnp.int32, (gl, gl), 1) // HEAD_DIM
    bd_mask = rb == cb
    tri_r = lax.broadcasted_iota(jnp.int32, (CHUNK, CHUNK), 0)
    tri_c = lax.broadcasted_iota(jnp.int32, (CHUNK, CHUNK), 1)
    tril_ones = (tri_r >= tri_c).astype(BF16)
    seg = bd_mask.astype(BF16)

    def bd(x):
        xb = x.astype(BF16)
        return jnp.where(bd_mask, jnp.concatenate([xb] * heads_per_group, axis=0),
                         jnp.zeros((), BF16))

    def fold(x):
        xm = jnp.where(bd_mask, x, 0.0)
        acc = xm[0:HEAD_DIM]
        for hh in range(1, heads_per_group):
            acc = acc + xm[hh * HEAD_DIM:(hh + 1) * HEAD_DIM]
        return acc

    def ld(i, p):
        c, g = p
        return feats[i][c * CHUNK:(c + 1) * CHUNK, g * gl:(g + 1) * gl]

    lw = [ld(1, p) for p in probs]
    na = [ld(4, p) for p in probs]
    bb = [ld(5, p) for p in probs]
    kf = [ld(2, p) for p in probs]
    r = [ld(0, p) for p in probs]
    v = [ld(3, p) for p in probs]

    def cumsum(x):
        hi = x.astype(BF16)
        lo = (x - hi.astype(F32)).astype(BF16)
        both = jnp.dot(tril_ones, jnp.concatenate([hi, lo], axis=1), preferred_element_type=F32)
        return both[:, :gl] + both[:, gl:]

    cum = _every(cumsum, lw)
    cum_last = [x[CHUNK - 1:CHUNK, :] for x in cum]
    e_out = [jnp.exp(-x) for x in cum]
    e_end = _every(lambda cl, x: jnp.exp(cl - x), cum_last, cum)
    a_t = _every(lambda n_, x, l_: n_ * jnp.exp(x - l_), na, cum, lw)
    r_t = _every(lambda r_, x: r_ * jnp.exp(x), r, cum)
    b_t = _every(lambda b_, e: b_ * e, bb, e_out)
    k_t = _every(lambda k_, e: k_ * e, kf, e_out)
    b_end = _every(lambda b_, e: b_ * e, bb, e_end)
    k_end = _every(lambda k_, e: k_ * e, kf, e_end)
    gamma = [jnp.exp(x) for x in cum_last]

    ar = _every(lambda a_, r_: jnp.concatenate([a_, r_], axis=0), a_t, r_t)
    pb = _every(lambda x, y: _dot_nt(x, bd(y)), ar, b_t)
    pk = _every(lambda x, y: _dot_nt(x, bd(y)), ar, k_t)
    l_ab = [jnp.where(strict, x[:CHUNK], 0.0) for x in pb]
    l_ak = [jnp.where(strict, x[:CHUNK], 0.0) for x in pk]
    m_rb = [jnp.where(incl, x[CHUNK:], 0.0) for x in pb]
    m_rk = [jnp.where(incl, x[CHUNK:], 0.0) for x in pk]

    x_acc = [eye_cat + l for l in l_ab]
    pw = _every(lambda l: _dot(l, bd(l)), l_ab)
    n_sq = int(math.log2(CHUNK)) - 1
    for lvl in range(n_sq):
        rhs = [bd(p_) for p_ in pw]
        if lvl < n_sq - 1:
            both = _every(lambda x, p_, w_: _dot(jnp.concatenate([x, p_], axis=0), w_),
                          x_acc, pw, rhs)
            x_acc = _every(lambda x, b_: x + b_[:CHUNK], x_acc, both)
            pw = [b_[CHUNK:] for b_ in both]
        else:
            x_acc = _every(lambda x, w_: x + _dot(x, w_), x_acc, rhs)
    t_inv = x_acc

    kv = _every(lambda la, mk, x: _dot(jnp.concatenate([la, mk], axis=0), bd(x)), l_ak, m_rk, v)
    y_loc = [x[:CHUNK] for x in kv]
    wu = _every(lambda t_, a_, y_: _dot(t_, jnp.concatenate([bd(a_), bd(y_)], axis=1)),
                t_inv, a_t, y_loc)
    w_t = [x[:, :gl] for x in wu]
    u_loc = [x[:, gl:] for x in wu]

    mwu = _every(lambda mb, w_, u_: _dot(mb, jnp.concatenate([bd(w_), bd(u_)], axis=1)),
                 m_rb, w_t, u_loc)
    q_c = _every(lambda r_, x: r_ + x[:, :gl], r_t, mwu)
    o_loc = _every(lambda x, y_: x[:, gl:] + y_[CHUNK:], mwu, kv)
    m_low = _every(lambda w_, b_: jnp.where(bd_mask, _dot_tn(w_, b_), 0.0).astype(BF16),
                   w_t, b_end)
    n_loc = _every(lambda u_, v_, b_, k_: fold(_dot_tn(jnp.concatenate([u_, v_], axis=0),
                                                       jnp.concatenate([b_, k_], axis=0))),
                   u_loc, v, b_end, k_end)

    state = [s_ref[g] for g in range(n_groups)]
    starts = []
    for c in range(n_chunks):
        starts.append(state)
        idx = [c * n_groups + g for g in range(n_groups)]
        state = [state[g] * gamma[i] + _dot(state[g], m_low[i]) + n_loc[i]
                 for g, i in enumerate(idx)]
    for g in range(n_groups):
        s_ref[g] = state[g]

    s0 = [starts[c][g] for (c, g) in probs]
    o = _every(lambda q_, s_, ol: _dot_nt(q_, bd(s_)) + ol, q_c, s0, o_loc)

    n_p = len(probs)
    unstack = lambda x: [x[i * CHUNK:(i + 1) * CHUNK] for i in range(n_p)]
    mu = unstack(_dot(jnp.concatenate(o, axis=0), seg) * (1.0 / HEAD_DIM))
    d = _every(lambda x, m_: x - m_, o, mu)
    var = unstack(_dot(jnp.concatenate([x * x for x in d], axis=0), seg) * (1.0 / HEAD_DIM))
    for i, (c, g) in enumerate(probs):
        ls = slice(g * gl, (g + 1) * gl)
        gn = d[i] * lax.rsqrt(var[i] + RWKV_GN_EPS) * ln_w[:, ls] + ln_b[:, ls]
        y = (gn + ld(7, (c, g))) * ld(6, (c, g))
        y_ref[c * CHUNK:(c + 1) * CHUNK, ls] = y.astype(y_ref.dtype)


def _wkv_sample_kernel(ft_ref, s_ref, s_out_ref, o_ref):
    hd = HEAD_DIM
    r, kf, na, bb = ft_ref[0], ft_ref[2], ft_ref[4], ft_ref[5]
    w = jnp.exp(ft_ref[1])

    def value_block(vb, carry):
        v0 = pl.multiple_of(vb * SUBLANES, SUBLANES)
        v_rows = ft_ref[3, pl.ds(v0, SUBLANES), :]
        outs = []
        for j in range(SUBLANES):
            s = s_ref[0, v0 + j]
            sa = jnp.sum(s * na, axis=0, keepdims=True)
            s_new = s * w + sa * bb + v_rows[j:j + 1] * kf
            s_out_ref[0, v0 + j] = s_new
            outs.append(jnp.sum(s_new * r, axis=0, keepdims=True))
        o_ref[pl.ds(v0, SUBLANES), :] = jnp.concatenate(outs, axis=0)
        return carry

    lax.fori_loop(0, hd // SUBLANES, value_block, 0)


def _wkv_sample_call(ft_s, state_t):
    h, hd, _, n = state_t.shape
    return pl.pallas_call(
        _wkv_sample_kernel,
        grid=(h,),
        in_specs=[
            pl.BlockSpec((6, hd, n), lambda i: (0, i, 0)),
            pl.BlockSpec((1, hd, hd, n), lambda i: (i, 0, 0, 0)),
        ],
        out_specs=(
            pl.BlockSpec((1, hd, hd, n), lambda i: (i, 0, 0, 0)),
            pl.BlockSpec((hd, n), lambda i: (i, 0)),
        ),
        out_shape=(
            jax.ShapeDtypeStruct(state_t.shape, F32),
            jax.ShapeDtypeStruct((h * hd, n), F32),
        ),
        compiler_params=pltpu.CompilerParams(
            dimension_semantics=("arbitrary",), vmem_limit_bytes=VMEM_LIMIT),
        name="wkv_sample",
    )(ft_s, state_t)


def _swa_prompt_kernel(sink_ref, q_ref, kp_ref, kc_ref, vp_ref, vc_ref, o_ref):
    j = pl.program_id(1)
    w = WINDOW
    tq = q_ref.shape[1]
    n_blk = tq // w
    gl = GROUP_LANES
    rows = SWA_GROUP * w
    probs = [(qi, g) for qi in range(n_blk) for g in range(SWA_KV_HEADS)]

    lane_head = lax.broadcasted_iota(jnp.int32, (w, gl), 1) // HEAD_DIM
    head_mask = [lane_head == h for h in range(SWA_GROUP)]
    qi_ = lax.broadcasted_iota(jnp.int32, (rows, 2 * w), 0) % w
    ki_ = lax.broadcasted_iota(jnp.int32, (rows, 2 * w), 1)
    diff = qi_ - (ki_ - w)
    in_window = (diff >= 0) & (diff < WINDOW)
    first_valid = in_window & ((j * tq + ki_ - w) >= 0)

    def keys(ref_prev, ref_cur, qi, g):
        ls = slice(g * gl, (g + 1) * gl)
        prev = ref_prev[0, :, ls] if qi == 0 else ref_cur[0, (qi - 1) * w:qi * w, ls]
        return jnp.concatenate([prev, ref_cur[0, qi * w:(qi + 1) * w, ls]], axis=0)

    def lhs(qi, g):
        qg = q_ref[0, qi * w:(qi + 1) * w, g * gl:(g + 1) * gl]
        return jnp.concatenate([jnp.where(m, qg, jnp.zeros((), BF16)) for m in head_mask], axis=0)

    sinks = []
    for g in range(SWA_KV_HEADS):
        sinks.append(jnp.concatenate(
            [jnp.full((w, 1), sink_ref[g * SWA_GROUP + h], F32) for h in range(SWA_GROUP)], axis=0))

    sink_lane = ki_ == qi_
    fill = [jnp.where(sink_lane, sk, MASK_VALUE) for sk in sinks]
    ones_k = jnp.ones((2 * w, LANES), BF16)

    s = [_dot_nt(lhs(qi, g), keys(kp_ref, kc_ref, qi, g)) for qi, g in probs]
    s = [jnp.where(first_valid if qi == 0 else in_window, x, fill[g])
         for x, (qi, g) in zip(s, probs)]
    m = [jnp.max(x, axis=-1, keepdims=True) for x in s]
    e = _every(lambda x, m_: jnp.exp(x - m_), s, m)
    inv = [1.0 / _dot(x, ones_k) for x in e]
    prob = _every(lambda e_, i_: jnp.where(sink_lane, jnp.zeros((), BF16),
                                           (e_ * jnp.concatenate([i_, i_], axis=1)).astype(BF16)),
                  e, inv)
    og = [_dot(p_, keys(vp_ref, vc_ref, qi, g)) for p_, (qi, g) in zip(prob, probs)]
    for x, (qi, g) in zip(og, probs):
        y = jnp.where(head_mask[0], x[0:w], 0.0)
        for h in range(1, SWA_GROUP):
            y = y + jnp.where(head_mask[h], x[h * w:(h + 1) * w], 0.0)
        o_ref[0, qi * w:(qi + 1) * w, g * gl:(g + 1) * gl] = y.astype(o_ref.dtype)


def _swa_prompt_call(sinks, q, k, v, tq):
    b, t, _ = q.shape
    w = WINDOW
    per = tq // w
    prev = lambda i, j: (i, jnp.maximum(j * per - 1, 0), 0)
    cur = lambda i, j: (i, j, 0)
    return pl.pallas_call(
        _swa_prompt_kernel,
        grid=(b, t // tq),
        in_specs=[
            pl.BlockSpec(memory_space=pltpu.SMEM),
            pl.BlockSpec((1, tq, SWA_WIDTH), cur),
            pl.BlockSpec((1, w, SWA_WIDTH), prev),
            pl.BlockSpec((1, tq, SWA_WIDTH), cur),
            pl.BlockSpec((1, w, SWA_WIDTH), prev),
            pl.BlockSpec((1, tq, SWA_WIDTH), cur),
        ],
        out_specs=pl.BlockSpec((1, tq, SWA_WIDTH), cur),
        out_shape=jax.ShapeDtypeStruct((b, t, SWA_WIDTH), BF16),
        compiler_params=pltpu.CompilerParams(
            dimension_semantics=("arbitrary", "arbitrary"), vmem_limit_bytes=VMEM_LIMIT),
        name="swa_prompt",
    )(sinks, q, k, k, v, v)


def _swa_sample_kernel(sink_ref, q_ref, knr_ref, vnr_ref, knt_ref, vnt_ref, ck_ref, cv_ref,
                       o_ref, ko_ref, vo_ref):
    nb, _, hd, l = ck_ref.shape
    q = q_ref[...]
    ck = ck_ref[...].reshape(nb * SWA_KV_HEADS, hd, l)
    cv = cv_ref[...].reshape(nb * SWA_KV_HEADS, hd, l)
    knr = knr_ref[...].astype(BF16).astype(F32)
    vnr = vnr_ref[...].astype(BF16).astype(F32)
    ki = lax.broadcasted_iota(jnp.int32, (1, 1, l), 2)
    kpos = PAST_LEN - l + ki
    diff = PAST_LEN - kpos
    valid = (diff >= 0) & (diff < WINDOW) & (kpos >= 0)
    s_c = lax.dot_general(q, ck.astype(BF16), (((2,), (1,)), ((0,), (0,))),
                          preferred_element_type=F32)
    s_c = jnp.where(valid, s_c, MASK_VALUE)
    s_n = jnp.sum(q.astype(F32) * knr, axis=-1, keepdims=True)
    sink = sink_ref[:, :, 0:1]
    m = jnp.maximum(jnp.maximum(jnp.max(s_c, axis=-1, keepdims=True), s_n), sink)
    e_c = jnp.exp(s_c - m)
    e_n = jnp.exp(s_n - m)
    denom = jnp.sum(e_c, axis=-1, keepdims=True) + e_n + jnp.exp(sink - m)
    p_c = (e_c / denom).astype(BF16)
    p_n = (e_n / denom).astype(BF16).astype(F32)
    o = lax.dot_general(p_c, cv.astype(BF16), (((2,), (2,)), ((0,), (0,))),
                        preferred_element_type=F32)
    o_ref[...] = o + p_n * vnr
    lane = lax.broadcasted_iota(jnp.int32, (hd, l), 1)
    for b in range(nb):
        for g in range(SWA_KV_HEADS):
            rows = slice(g * hd, (g + 1) * hd)
            ko_ref[b, g] = jnp.where(lane == l - 1, knt_ref[0, rows, b:b + 1],
                                     pltpu.roll(ck_ref[b, g], l - 1, 1))
            vo_ref[b, g] = jnp.where(lane == l - 1, vnt_ref[0, rows, b:b + 1],
                                     pltpu.roll(cv_ref[b, g], l - 1, 1))


def _swa_sample_call(sink_t, q8, knr, vnr, knt, vnt, ck, cv, nb):
    n, g, hd, l = ck.shape
    rows = nb * g
    blk3 = lambda i: (i, 0, 0)
    blk4 = lambda i: (i, 0, 0, 0)
    return pl.pallas_call(
        _swa_sample_kernel,
        grid=(n // nb,),
        in_specs=[
            pl.BlockSpec((rows, SUBLANES, LANES), lambda i: (0, 0, 0)),
            pl.BlockSpec((rows, SUBLANES, hd), blk3),
            pl.BlockSpec((rows, 1, hd), blk3),
            pl.BlockSpec((rows, 1, hd), blk3),
            pl.BlockSpec((1, g * hd, nb), blk3),
            pl.BlockSpec((1, g * hd, nb), blk3),
            pl.BlockSpec((nb, g, hd, l), blk4),
            pl.BlockSpec((nb, g, hd, l), blk4),
        ],
        out_specs=(
            pl.BlockSpec((rows, SUBLANES, hd), blk3),
            pl.BlockSpec((nb, g, hd, l), blk4),
            pl.BlockSpec((nb, g, hd, l), blk4),
        ),
        out_shape=(
            jax.ShapeDtypeStruct((n * g, SUBLANES, hd), F32),
            jax.ShapeDtypeStruct(ck.shape, F32),
            jax.ShapeDtypeStruct(cv.shape, F32),
        ),
        compiler_params=pltpu.CompilerParams(
            dimension_semantics=("arbitrary",), vmem_limit_bytes=VMEM_LIMIT),
        name="swa_sample",
    )(sink_t, q8, knr, vnr, knt, vnt, ck, cv)


def _tail_mix(x, y_rwkv, y_swa, g1, sh2, sc2, w_out, n_post_mix, n_pre_ffn):
    mix = _dot(y_rwkv, w_out[:RWKV_WIDTH]) + _dot(y_swa, w_out[RWKV_WIDTH:])
    x1 = x + g1 * _rmsnorm(mix, n_post_mix)
    h2 = (_rmsnorm(x1, n_pre_ffn) * (1.0 + sc2) + sh2).astype(BF16)
    return x1, h2


def _tail_prompt_kernel(x_ref, yr_ref, ys_ref, mod_ref, w_out, n_post_mix, n_pre_ffn,
                        n_post_ffn, w_up_a, w_up_b, w_down, cw_ref, cb_ref,
                        y_ref, cp_ref, carry_ref, act_ref):
    t = pl.program_id(1)
    tm = x_ref.shape[1]
    sub = min(tm, TAIL_SUB)
    grp = min(tm, TAIL_GROUP)

    @pl.when(t == 0)
    def _():
        carry_ref[...] = jnp.zeros_like(carry_ref)

    g1, sh2, sc2, g2 = mod_ref[2, 0], mod_ref[3, 0], mod_ref[4, 0], mod_ref[5, 0]

    def cols_of(c, half):
        return slice(half * D_FF + c * FF_CHUNK, half * D_FF + (c + 1) * FF_CHUNK)

    def conv(u, cols):
        ext = jnp.concatenate([carry_ref[:, cols], u], axis=0)
        last = u[sub - SUBLANES:]
        carry_ref[:, cols] = last
        cp_ref[0, :, cols] = last
        return (cb_ref[:, cols] + pltpu.roll(ext, 2, 0)[SUBLANES:] * cw_ref[0:1, cols]
                + pltpu.roll(ext, 1, 0)[SUBLANES:] * cw_ref[1:2, cols] + u * cw_ref[2:3, cols])

    def group(gi, carry):
        base = pl.multiple_of(gi * grp, grp)
        blocks = [(pl.ds(base + j * sub, sub), slice(j * sub, (j + 1) * sub))
                  for j in range(grp // sub)]
        mixes = [_dot(yr_ref[0, rb, :], w_out[:RWKV_WIDTH]) + _dot(ys_ref[0, rb, :], w_out[RWKV_WIDTH:])
                 for rb, _ in blocks]
        for (rb, ab), mix in zip(blocks, mixes):
            x1 = x_ref[0, rb, :] + g1 * _rmsnorm(mix, n_post_mix[...])
            h2 = (_rmsnorm(x1, n_pre_ffn[...]) * (1.0 + sc2) + sh2).astype(BF16)

            def up(c):
                chunk = slice(c * FF_CHUNK, (c + 1) * FF_CHUNK)
                return [jnp.dot(h2, w[:, chunk], preferred_element_type=F32)
                        for w in (w_up_a, w_up_b)]

            def down(c):
                rows = slice(c * FF_CHUNK, (c + 1) * FF_CHUNK)
                return jnp.dot(act_ref[ab, rows], w_down[rows, :], preferred_element_type=F32)

            u_next = up(0)
            ff = None
            for c in range(N_FF_CHUNKS):
                u_cur = u_next
                if c + 1 < N_FF_CHUNKS:
                    u_next = up(c + 1)
                if c >= 1:
                    part = down(c - 1)
                    ff = part if ff is None else ff + part
                za, zb = [conv(u_cur[half], cols_of(c, half)) for half in range(2)]
                act_ref[ab, c * FF_CHUNK:(c + 1) * FF_CHUNK] = (_silu(za) * zb).astype(BF16)
            ff = ff + down(N_FF_CHUNKS - 1)
            y_ref[0, rb, :] = x1 + g2 * _rmsnorm(ff, n_post_ffn[...])
        return carry

    lax.fori_loop(0, tm // grp, group, 0)


def _tail_sample_kernel(x_ref, o_ref, feat_ref, ys_ref, mod_ref, ln_w, ln_b, w_out, n_post_mix,
                        n_pre_ffn, n_post_ffn, wa_ref, wb_ref, wd_ref, p0a, p0b, p1a, p1b,
                        cwa, cwb, cba, cbb,
                        y_ref, ua_ref, ub_ref, w_out_b_ref, wa_b_ref, wb_b_ref, wd_b_ref,
                        x1_ref, h2_ref, acc_ref):
    c = pl.program_id(0)

    @pl.when(c == 0)
    def _():
        w_out_b = w_out[...].astype(BF16)
        w_out_b_ref[...] = w_out_b
        seg = _seg_ones(RWKV_WIDTH)
        y_rwkv = _gn_epilogue(o_ref[...].T, feat_ref[7], feat_ref[6], ln_w[...], ln_b[...], seg)
        x1, h2 = _tail_mix(x_ref[...], y_rwkv, ys_ref[...], mod_ref[2], mod_ref[3], mod_ref[4],
                           w_out_b, n_post_mix[...], n_pre_ffn[...])
        x1_ref[...] = x1
        h2_ref[...] = h2
        acc_ref[...] = jnp.zeros_like(acc_ref)

    wa = wa_ref[...].astype(BF16)
    wb = wb_ref[...].astype(BF16)
    wd = wd_ref[...].astype(BF16)
    wa_b_ref[...] = wa
    wb_b_ref[...] = wb
    wd_b_ref[...] = wd
    h2 = h2_ref[...]
    ua = jnp.dot(h2, wa, preferred_element_type=F32)
    ub = jnp.dot(h2, wb, preferred_element_type=F32)
    ua_ref[...] = ua
    ub_ref[...] = ub
    za = cba[...] + p0a[...] * cwa[0:1, :] + p1a[...] * cwa[1:2, :] + ua * cwa[2:3, :]
    zb = cbb[...] + p0b[...] * cwb[0:1, :] + p1b[...] * cwb[1:2, :] + ub * cwb[2:3, :]
    acc_ref[...] += _dot(_silu(za) * zb, wd)

    @pl.when(c == pl.num_programs(0) - 1)
    def _():
        y_ref[...] = x1_ref[...] + mod_ref[5] * _rmsnorm(acc_ref[...], n_post_ffn[...])


def _tail_prompt_call(x, y_rwkv, y_swa, mod_p, consts, tm):
    b, t, _ = x.shape
    tok = lambda i, j: (i, j, 0)
    in_specs = [
        pl.BlockSpec((1, tm, D_MODEL), tok),
        pl.BlockSpec((1, tm, RWKV_WIDTH), tok),
        pl.BlockSpec((1, tm, SWA_WIDTH), tok),
        pl.BlockSpec((6, 1, 1, D_MODEL), lambda i, j: (0, i, 0, 0)),
    ] + [_const_spec(c, 2) for c in consts]
    return pl.pallas_call(
        _tail_prompt_kernel,
        grid=(b, t // tm),
        in_specs=in_specs,
        out_specs=(
            pl.BlockSpec((1, tm, D_MODEL), tok),
            pl.BlockSpec((1, SUBLANES, 2 * D_FF), lambda i, j: (i, 0, 0)),
        ),
        out_shape=(
            jax.ShapeDtypeStruct((b, t, D_MODEL), F32),
            jax.ShapeDtypeStruct((b, SUBLANES, 2 * D_FF), F32),
        ),
        scratch_shapes=[
            pltpu.VMEM((SUBLANES, 2 * D_FF), F32),
            pltpu.VMEM((min(tm, TAIL_GROUP), D_FF), BF16),
        ],
        compiler_params=pltpu.CompilerParams(
            dimension_semantics=("arbitrary", "arbitrary"), vmem_limit_bytes=VMEM_LIMIT),
        name="tail_prompt",
    )(x, y_rwkv, y_swa, mod_p, *consts)


def _tail_sample_call(x, o_t, feat_s, y_swa, mod, p0, p1, ln_w, ln_b, w_out, n_post_mix,
                      n_pre_ffn, n_post_ffn, w_up, w_down, cw, cb):
    n = x.shape[0]
    nc = N_FF_CHUNKS
    fc = FF_CHUNK
    whole = lambda a: pl.BlockSpec(a.shape, lambda c, nd=a.ndim: (0,) * nd)
    gate = lambda rows: pl.BlockSpec((rows, fc), lambda c: (0, c))
    value = lambda rows: pl.BlockSpec((rows, fc), lambda c: (0, nc + c))
    in_specs = [
        whole(x), whole(o_t), whole(feat_s), whole(y_swa),
        pl.BlockSpec((mod.shape[0], n, D_MODEL), lambda c: (0, 0, 0)),
        whole(ln_w), whole(ln_b), whole(w_out), whole(n_post_mix), whole(n_pre_ffn),
        whole(n_post_ffn),
        gate(D_MODEL), value(D_MODEL), pl.BlockSpec((fc, D_MODEL), lambda c: (c, 0)),
        gate(n), value(n), gate(n), value(n), gate(3), value(3), gate(1), value(1),
    ]
    out_shape = (
        jax.ShapeDtypeStruct((n, D_MODEL), F32),
        jax.ShapeDtypeStruct((n, D_FF), F32),
        jax.ShapeDtypeStruct((n, D_FF), F32),
        jax.ShapeDtypeStruct((D_MODEL, D_MODEL), BF16),
        jax.ShapeDtypeStruct((D_MODEL, D_FF), BF16),
        jax.ShapeDtypeStruct((D_MODEL, D_FF), BF16),
        jax.ShapeDtypeStruct((D_FF, D_MODEL), BF16),
    )
    chunk_cols = lambda rows: pl.BlockSpec((rows, fc), lambda c: (0, c))
    out_specs = (
        pl.BlockSpec((n, D_MODEL), lambda c: (0, 0)),
        chunk_cols(n), chunk_cols(n),
        pl.BlockSpec((D_MODEL, D_MODEL), lambda c: (0, 0)),
        chunk_cols(D_MODEL), chunk_cols(D_MODEL),
        pl.BlockSpec((fc, D_MODEL), lambda c: (c, 0)),
    )
    return pl.pallas_call(
        _tail_sample_kernel,
        grid=(nc,),
        in_specs=in_specs,
        out_specs=out_specs,
        out_shape=out_shape,
        scratch_shapes=[
            pltpu.VMEM((n, D_MODEL), F32),
            pltpu.VMEM((n, D_MODEL), BF16),
            pltpu.VMEM((n, D_MODEL), F32),
        ],
        compiler_params=pltpu.CompilerParams(
            dimension_semantics=("arbitrary",), vmem_limit_bytes=VMEM_LIMIT),
        name="tail_sample",
    )(x, o_t, feat_s, y_swa, mod, ln_w, ln_b, w_out, n_post_mix, n_pre_ffn, n_post_ffn,
      w_up, w_up, w_down, p0, p0, p1, p1, cw, cw, cb, cb)


def _rope_tables(pos):
    half = HEAD_DIM // 2
    inv = ROPE_THETA ** (-jnp.arange(half, dtype=F32) / half)
    ang = pos.astype(F32)[:, None] * inv[None, :]
    cos, sin = jnp.cos(ang), jnp.sin(ang)
    cos_h = jnp.concatenate([cos, cos], axis=-1)
    sin_h = jnp.concatenate([-sin, sin], axis=-1)
    return jnp.tile(cos_h, (1, LANES // HEAD_DIM)), jnp.tile(sin_h, (1, LANES // HEAD_DIM))


def kernel(x_prompt, x_sample, state_rwkv_wkv, state_rwkv_shift, cache_swa_k, cache_swa_v,
           state_ffn_conv, c_prompt, c_sample, w_ada, b_ada, norm_pre_mix, norm_post_mix,
           norm_pre_ffn, norm_post_ffn, w_in, rwkv_mu, rwkv_w0, rwkv_w_up, rwkv_a0, rwkv_a_up,
           rwkv_g_up, rwkv_k_k, rwkv_k_a, rwkv_r_k, rwkv_ln_w, rwkv_ln_b, swa_sinks, w_out,
           ffn_w_up, ffn_conv_w, ffn_conv_b, ffn_w_down):
    depth = w_ada.shape[0]
    assert depth == 1 and x_sample.shape[1] == 1
    b, t, _ = x_prompt.shape
    n = x_sample.shape[0]
    tm = min(256, t)
    blk = min(4 * tm, t)
    assert t % blk == 0 and tm % WINDOW == 0 and tm % CHUNK == 0 and n % SUBLANES == 0
    nb_swa = 2 * SUBLANES if n % (2 * SUBLANES) == 0 else SUBLANES
    li = 0

    row = lambda v: v.reshape(1, -1)
    zeros_l = jnp.zeros((64, RWKV_WIDTH), F32)
    wa_up = jnp.concatenate([
        jnp.concatenate([rwkv_w_up[li], zeros_l], axis=1),
        jnp.concatenate([zeros_l, rwkv_a_up[li]], axis=1)], axis=0).astype(BF16)
    hid = jnp.arange(GROUP_LANES) // HEAD_DIM
    seg_blk = (hid[:, None] == hid[None, :]).astype(BF16)
    proj_consts = (row(norm_pre_mix[li]), w_in[li], row(rwkv_mu[li]), row(rwkv_w0[li]), wa_up,
                   row(rwkv_a0[li]), rwkv_g_up[li].astype(BF16), row(rwkv_k_k[li]),
                   row(rwkv_k_a[li]), row(rwkv_r_k[li]), seg_blk)
    norms = (row(norm_post_mix[li]), row(norm_pre_ffn[li]), row(norm_post_ffn[li]))
    cw, cb = ffn_conv_w[li], row(ffn_conv_b[li])
    ln_w, ln_b = row(rwkv_ln_w[li]), row(rwkv_ln_b[li])

    mod = _ada_call(jnp.concatenate([c_sample, c_prompt], axis=0), w_ada[li], row(b_ada[li]))
    mod_p = mod[:, n:].reshape(6, b, 1, D_MODEL)

    cos_s, sin_s = _rope_tables(jnp.full((1,), PAST_LEN, jnp.int32))
    feat_s, ft_s, q_s, kn_s, vn_s, p_s, w_in_b = _proj_sample_call(
        x_sample[:, 0], mod, cos_s, sin_s, state_rwkv_shift[li], proj_consts)

    state_t = jnp.transpose(state_rwkv_wkv[li], (1, 2, 3, 0))
    wkv_t, o_t = _wkv_sample_call(ft_s, state_t)
    wkv_s = jnp.transpose(wkv_t, (3, 0, 1, 2))

    g2 = SWA_KV_HEADS
    q4 = q_s.reshape(n * g2, SWA_GROUP, HEAD_DIM)
    q8 = jnp.concatenate([q4, jnp.zeros_like(q4)], axis=1)
    sink_t = jnp.broadcast_to(
        jnp.concatenate([swa_sinks[li].reshape(g2, SWA_GROUP),
                         jnp.full((g2, SWA_GROUP), MASK_VALUE, F32)], axis=1)[None, :, :, None],
        (nb_swa, g2, SUBLANES, LANES)).reshape(nb_swa * g2, SUBLANES, LANES)
    cols = lambda a: a.reshape(n // nb_swa, nb_swa, KV_WIDTH).transpose(0, 2, 1)
    ck = jnp.transpose(cache_swa_k[li], (0, 2, 3, 1))
    cv = jnp.transpose(cache_swa_v[li], (0, 2, 3, 1))
    o_att, k_t, v_t = _swa_sample_call(
        sink_t, q8, kn_s.reshape(n * g2, 1, HEAD_DIM), vn_s.reshape(n * g2, 1, HEAD_DIM),
        cols(kn_s), cols(vn_s), ck, cv, nb_swa)
    y_swa_s = o_att[:, :SWA_GROUP].reshape(n, SWA_WIDTH)
    k_s = jnp.transpose(k_t, (0, 3, 1, 2))
    v_s = jnp.transpose(v_t, (0, 3, 1, 2))

    conv0 = state_ffn_conv[li]
    y_s, ua_s, ub_s, w_out_b, w_up_a_b, w_up_b_b, w_down_b = _tail_sample_call(
        x_sample[:, 0], o_t, feat_s, y_swa_s.astype(BF16), mod, conv0[:, 0], conv0[:, 1],
        ln_w, ln_b, w_out[li], *norms, ffn_w_up[li], ffn_w_down[li], cw, cb)
    conv_s = jnp.stack([conv0[:, 1], jnp.concatenate([ua_s, ub_s], axis=1)], axis=1)

    cos_p, sin_p = _rope_tables(jnp.arange(t, dtype=jnp.int32))
    proj_consts_p = proj_consts[:1] + (w_in_b,) + proj_consts[2:]
    y_rwkv_p, s_cat, q_p, kx_p, vx_p, klast, vlast, plast = _mix_prompt_call(
        x_prompt, mod_p, cos_p, sin_p, proj_consts_p + (ln_w, ln_b), blk)
    y_swa_p = _swa_prompt_call(swa_sinks[li], q_p, kx_p, vx_p, blk)
    tail_consts = (w_out_b,) + norms + (w_up_a_b, w_up_b_b, w_down_b, cw, cb)
    y_p, cp = _tail_prompt_call(x_prompt, y_rwkv_p, y_swa_p, mod_p, tail_consts, blk)

    wkv_p = s_cat.reshape(b, HEAD_DIM, RWKV_HEADS, HEAD_DIM).transpose(0, 2, 1, 3)
    shift_p = plast[:, SUBLANES - 1]
    k_p = klast.reshape(b, WINDOW, SWA_KV_HEADS, HEAD_DIM)
    v_p = vlast.reshape(b, WINDOW, SWA_KV_HEADS, HEAD_DIM)
    conv_p = cp[:, SUBLANES - 2:]

    expand = lambda a: a[None]
    return (y_p, y_s[:, None, :], expand(wkv_p), expand(shift_p), expand(k_p), expand(v_p),
            expand(conv_p), expand(wkv_s), expand(p_s),
            expand(k_s), expand(v_s), expand(conv_s))
```

```python
import math

import jax
import jax.numpy as jnp
from jax import lax
from jax.experimental import pallas as pl
from jax.experimental.pallas import tpu as pltpu

D_MODEL = 1024
HEAD_DIM = 64
RWKV_WIDTH = 512
RWKV_HEADS = 8
RWKV_COLS = 1792
RWKV_GN_EPS = 64e-5
SWA_WIDTH = 512
SWA_HEADS = 8
SWA_KV_HEADS = 2
SWA_GROUP = 4
KV_WIDTH = SWA_KV_HEADS * HEAD_DIM
WINDOW = 128
PAST_LEN = 16384
ROPE_THETA = 10000.0
ATTN_SCALE = HEAD_DIM ** -0.5
D_FF = 2816
NORM_EPS = 1e-6
MASK_VALUE = -1e30
PROJ_COLS = RWKV_COLS + SWA_WIDTH + 2 * KV_WIDTH

LANES = 128
SUBLANES = 8
CHUNK = 64
GROUP_LANES = 256
FF_CHUNK = 256
N_FF_CHUNKS = D_FF // FF_CHUNK
TAIL_SUB = 256
MIX_SUB = 1024
TAIL_GROUP = 512
VMEM_LIMIT = 56 * 1024 * 1024

F32 = jnp.float32
BF16 = jnp.bfloat16


def _sigmoid(x):
    return 1.0 / (1.0 + jnp.exp(-x))


def _silu(x):
    return x * _sigmoid(x)


def _rmsnorm(x, g):
    return x * lax.rsqrt(jnp.mean(x * x, axis=-1, keepdims=True) + NORM_EPS) * g


def _dot(a, b):
    return jnp.dot(a.astype(BF16), b.astype(BF16), preferred_element_type=F32)


def _dot_nt(a, b):
    return lax.dot_general(a.astype(BF16), b.astype(BF16), (((1,), (1,)), ((), ())),
                           preferred_element_type=F32)


def _dot_tn(a, b):
    return lax.dot_general(a.astype(BF16), b.astype(BF16), (((0,), (0,)), ((), ())),
                           preferred_element_type=F32)


def _every(fn, *lists):
    return [fn(*a) for a in zip(*lists)]


def _swap_halves(x):
    w = x.shape[-1]
    lane = lax.broadcasted_iota(jnp.int32, x.shape, x.ndim - 1)
    lo = (lane & (HEAD_DIM // 2)) == 0
    return jnp.where(lo, pltpu.roll(x, w - HEAD_DIM // 2, x.ndim - 1),
                     pltpu.roll(x, HEAD_DIM // 2, x.ndim - 1))


def _rope(x, cos, sin):
    reps = x.shape[-1] // LANES
    cos_w = jnp.concatenate([cos] * reps, axis=-1) if reps > 1 else cos
    sin_w = jnp.concatenate([sin] * reps, axis=-1) if reps > 1 else sin
    return x * cos_w + _swap_halves(x) * sin_w


def _proj_features(x, shift, scale, g_pre, w_in, prev_fn, mu, w0, wa_up, a0, g_up,
                   k_k, k_a, r_k, seg_blk, cos, sin):
    h = _rmsnorm(x, g_pre) * (1.0 + scale) + shift
    p = _dot(h, w_in)
    p_rwkv = p[:, :RWKV_COLS]
    prev = prev_fn(p_rwkv)
    xm = p_rwkv + (prev - p_rwkv) * mu
    r = xm[:, 0:512]
    k = xm[:, 512:1024]
    v = xm[:, 1024:1536]
    wa = xm[:, 1536:1664]
    gd = xm[:, 1664:1792]
    lane = lax.broadcasted_iota(jnp.int32, wa.shape, 1)
    wa_act = jnp.where(lane < 64, jnp.tanh(wa), wa)
    lora = _dot(wa_act, wa_up)
    lw = -math.exp(-0.5) * _sigmoid(w0 + lora[:, :512])
    a = _sigmoid(a0 + lora[:, 512:])
    g = _dot(_sigmoid(gd), g_up)
    kk = k * k_k
    kf = k * (1.0 + (a - 1.0) * k_a)
    gl = seg_blk.shape[0]
    n = x.shape[0]
    sums = [_dot(jnp.concatenate([(kk * kk)[:, i:i + gl], (r * kf * r_k)[:, i:i + gl]], axis=0),
                 seg_blk) for i in range(0, RWKV_WIDTH, gl)]
    ss = jnp.concatenate([s_[:n] for s_ in sums], axis=1)
    kk = kk / jnp.maximum(jnp.sqrt(ss), 1e-12)
    bonus = jnp.concatenate([s_[n:] for s_ in sums], axis=1) * v
    feats = (r, lw, kf, v, -kk, kk * a, g, bonus)
    q = _rope(p[:, RWKV_COLS:RWKV_COLS + SWA_WIDTH], cos, sin)
    ks = _rope(p[:, RWKV_COLS + SWA_WIDTH:RWKV_COLS + SWA_WIDTH + KV_WIDTH], cos, sin)
    vs = p[:, RWKV_COLS + SWA_WIDTH + KV_WIDTH:]
    return feats, q, ks, vs, p_rwkv


def _gn_epilogue(o, bonus, g, ln_w, ln_b, seg):
    mu = _dot(o, seg) * (1.0 / HEAD_DIM)
    d = o - mu
    var = _dot(d * d, seg) * (1.0 / HEAD_DIM)
    gn = d * lax.rsqrt(var + RWKV_GN_EPS) * ln_w + ln_b
    return (gn + bonus) * g


def _seg_ones(n):
    r = lax.broadcasted_iota(jnp.int32, (n, n), 0) // HEAD_DIM
    c = lax.broadcasted_iota(jnp.int32, (n, n), 1) // HEAD_DIM
    return (r == c).astype(BF16)


def _ada_kernel(c_ref, w_ref, b_ref, o_ref):
    o_ref[0] = _dot(_silu(c_ref[...]), w_ref[...]) + b_ref[...]


def _ada_call(c_all, w_ada, b_ada):
    rows = c_all.shape[0]
    return pl.pallas_call(
        _ada_kernel,
        grid=(6,),
        in_specs=[
            pl.BlockSpec((rows, D_MODEL), lambda j: (0, 0)),
            pl.BlockSpec((D_MODEL, D_MODEL), lambda j: (0, j)),
            pl.BlockSpec((1, D_MODEL), lambda j: (0, j)),
        ],
        out_specs=pl.BlockSpec((1, rows, D_MODEL), lambda j: (j, 0, 0)),
        out_shape=jax.ShapeDtypeStruct((6, rows, D_MODEL), F32),
        compiler_params=pltpu.CompilerParams(
            dimension_semantics=("arbitrary",), vmem_limit_bytes=VMEM_LIMIT),
        name="ada",
    )(c_all, w_ada, b_ada)


def _expand_kv(x):
    lane = lax.broadcasted_iota(jnp.int32, x.shape, 1)
    rolled = pltpu.roll(x, HEAD_DIM, 1)
    g0 = jnp.where(lane < HEAD_DIM, x, rolled)
    g1 = jnp.where(lane < HEAD_DIM, rolled, x)
    return jnp.concatenate([g0, g0, g1, g1], axis=1)


def _mix_prompt_kernel(x_ref, mod_ref, cos_ref, sin_ref, g_pre, w_in, mu, w0, wa_up, a0,
                       g_up, k_k, k_a, r_k, seg_blk, ln_w, ln_b,
                       y_ref, s_out_ref, q_ref, k_ref, v_ref, klast_ref, vlast_ref, plast_ref,
                       carry_ref, s_ref):
    t = pl.program_id(1)

    @pl.when(t == 0)
    def _():
        carry_ref[...] = jnp.zeros_like(carry_ref)
        s_ref[...] = jnp.zeros_like(s_ref)

    sub = min(x_ref.shape[1], MIX_SUB)

    def sub_tile(i, carry):
        rows = pl.ds(pl.multiple_of(i * sub, sub), sub)
        carry_row = carry_ref[SUBLANES - 1:SUBLANES, :]

        def prev_fn(p_rwkv):
            row = lax.broadcasted_iota(jnp.int32, p_rwkv.shape, 0)
            return jnp.where(row == 0, carry_row, pltpu.roll(p_rwkv, 1, 0))

        feats, q, ks, vs, p_rwkv = _proj_features(
            x_ref[0, rows, :], mod_ref[0, 0], mod_ref[1, 0], g_pre[...], w_in[...], prev_fn,
            mu[...], w0[...], wa_up[...], a0[...], g_up[...], k_k[...], k_a[...], r_k[...],
            seg_blk[...], cos_ref[rows, :], sin_ref[rows, :])
        q_ref[0, rows, :] = (q * ATTN_SCALE).astype(BF16)
        k_ref[0, rows, :] = _expand_kv(ks).astype(BF16)
        v_ref[0, rows, :] = _expand_kv(vs).astype(BF16)
        last = p_rwkv[sub - SUBLANES:, :]
        carry_ref[...] = last
        plast_ref[0] = last
        klast_ref[0] = ks[sub - WINDOW:, :]
        vlast_ref[0] = vs[sub - WINDOW:, :]
        _wkv_tile(feats, ln_w, ln_b, y_ref.at[0, rows, :], s_ref)
        return carry

    lax.fori_loop(0, x_ref.shape[1] // sub, sub_tile, 0)

    @pl.when(t == pl.num_programs(1) - 1)
    def _():
        for g in range(RWKV_WIDTH // GROUP_LANES):
            s_out_ref[0, :, g * GROUP_LANES:(g + 1) * GROUP_LANES] = s_ref[g]


def _proj_sample_kernel(x_ref, mod_ref, cos_ref, sin_ref, prev_ref, g_pre, w_in, mu, w0,
                        wa_up, a0, g_up, k_k, k_a, r_k, seg_blk,
                        feat_ref, ft_ref, q_ref, k_ref, v_ref, p_ref, w_in_b_ref):
    w_in_b = w_in[...].astype(BF16)
    w_in_b_ref[...] = w_in_b
    feats, q, ks, vs, p_rwkv = _proj_features(
        x_ref[...], mod_ref[0], mod_ref[1], g_pre[...], w_in_b, lambda p: prev_ref[...],
        mu[...], w0[...], wa_up[...], a0[...], g_up[...], k_k[...], k_a[...], r_k[...],
        seg_blk[...], cos_ref[...], sin_ref[...])
    for i, f in enumerate(feats):
        feat_ref[i] = f
    for i in range(6):
        ft_ref[i] = feats[i].T
    q_ref[...] = (q * ATTN_SCALE).astype(BF16)
    k_ref[...] = ks
    v_ref[...] = vs
    p_ref[...] = p_rwkv


def _const_spec(arr, grid_rank):
    zeros = (0,) * arr.ndim
    if grid_rank == 1:
        return pl.BlockSpec(arr.shape, lambda i: zeros)
    return pl.BlockSpec(arr.shape, lambda i, j: zeros)


def _mix_prompt_call(x, mod_p, cos, sin, consts, tm):
    b, t, _ = x.shape
    nt = t // tm
    in_specs = [
        pl.BlockSpec((1, tm, D_MODEL), lambda i, j: (i, j, 0)),
        pl.BlockSpec((6, 1, 1, D_MODEL), lambda i, j: (0, i, 0, 0)),
        pl.BlockSpec((tm, LANES), lambda i, j: (j, 0)),
        pl.BlockSpec((tm, LANES), lambda i, j: (j, 0)),
    ] + [_const_spec(c, 2) for c in consts]
    out_shape = (
        jax.ShapeDtypeStruct((b, t, RWKV_WIDTH), BF16),
        jax.ShapeDtypeStruct((b, HEAD_DIM, RWKV_WIDTH), F32),
        jax.ShapeDtypeStruct((b, t, SWA_WIDTH), BF16),
        jax.ShapeDtypeStruct((b, t, SWA_WIDTH), BF16),
        jax.ShapeDtypeStruct((b, t, SWA_WIDTH), BF16),
        jax.ShapeDtypeStruct((b, WINDOW, KV_WIDTH), F32),
        jax.ShapeDtypeStruct((b, WINDOW, KV_WIDTH), F32),
        jax.ShapeDtypeStruct((b, SUBLANES, RWKV_COLS), F32),
    )
    out_specs = (
        pl.BlockSpec((1, tm, RWKV_WIDTH), lambda i, j: (i, j, 0)),
        pl.BlockSpec((1, HEAD_DIM, RWKV_WIDTH), lambda i, j: (i, 0, 0)),
        pl.BlockSpec((1, tm, SWA_WIDTH), lambda i, j: (i, j, 0)),
        pl.BlockSpec((1, tm, SWA_WIDTH), lambda i, j: (i, j, 0)),
        pl.BlockSpec((1, tm, SWA_WIDTH), lambda i, j: (i, j, 0)),
        pl.BlockSpec((1, WINDOW, KV_WIDTH), lambda i, j: (i, 0, 0)),
        pl.BlockSpec((1, WINDOW, KV_WIDTH), lambda i, j: (i, 0, 0)),
        pl.BlockSpec((1, SUBLANES, RWKV_COLS), lambda i, j: (i, 0, 0)),
    )
    return pl.pallas_call(
        _mix_prompt_kernel,
        grid=(b, nt),
        in_specs=in_specs,
        out_specs=out_specs,
        out_shape=out_shape,
        scratch_shapes=[
            pltpu.VMEM((SUBLANES, RWKV_COLS), F32),
            pltpu.VMEM((RWKV_WIDTH // GROUP_LANES, HEAD_DIM, GROUP_LANES), F32),
        ],
        compiler_params=pltpu.CompilerParams(
            dimension_semantics=("arbitrary", "arbitrary"), vmem_limit_bytes=VMEM_LIMIT),
        name="mix_prompt",
    )(x, mod_p, cos, sin, *consts)


def _mod_rows_spec(mod, n):
    return pl.BlockSpec((mod.shape[0], n, mod.shape[2]), lambda i: (0, 0, 0))


def _proj_sample_call(x, mod, cos, sin, prev, consts):
    n = x.shape[0]
    args = (x, mod, cos, sin, prev) + tuple(consts)
    out_shape = (
        jax.ShapeDtypeStruct((8, n, RWKV_WIDTH), F32),
        jax.ShapeDtypeStruct((6, RWKV_WIDTH, n), F32),
        jax.ShapeDtypeStruct((n, SWA_WIDTH), BF16),
        jax.ShapeDtypeStruct((n, KV_WIDTH), F32),
        jax.ShapeDtypeStruct((n, KV_WIDTH), F32),
        jax.ShapeDtypeStruct((n, RWKV_COLS), F32),
        jax.ShapeDtypeStruct((D_MODEL, PROJ_COLS), BF16),
    )
    return pl.pallas_call(
        _proj_sample_kernel,
        grid=(1,),
        in_specs=[_mod_rows_spec(a, n) if i == 1 else _const_spec(a, 1)
                  for i, a in enumerate(args)],
        out_specs=tuple(pl.BlockSpec(s.shape, lambda i, nd=len(s.shape): (0,) * nd)
                        for s in out_shape),
        out_shape=out_shape,
        compiler_params=pltpu.CompilerParams(
            dimension_semantics=("arbitrary",), vmem_limit_bytes=VMEM_LIMIT),
        name="proj_sample",
    )(*args)


def _wkv_tile(feats, ln_w, ln_b, y_ref, s_ref):
    tt = feats[0].shape[0]
    n_chunks = tt // CHUNK
    gl = GROUP_LANES
    n_groups = RWKV_WIDTH // gl
    heads_per_group = gl // HEAD_DIM
    probs = [(c, g) for c in range(n_chunks) for g in range(n_groups)]

    row_c = lax.broadcasted_iota(jnp.int32, (CHUNK, gl), 0)
    col_c = lax.broadcasted_iota(jnp.int32, (CHUNK, gl), 1) % CHUNK
    strict = row_c > col_c
    incl = row_c >= col_c
    eye_cat = (row_c == col_c).astype(F32)
    rb = lax.broadcasted_iota(jnp.int32, (gl, gl), 0) // HEAD_DIM
    cb = lax.broadcasted_iota(jnp.int32, (gl, gl), 1) // HEAD_DIM
    bd_mask = rb == cb
    tri_r = lax.broadcasted_iota(jnp.int32, (CHUNK, CHUNK), 0)
    tri_c = lax.broadcasted_iota(jnp.int32, (CHUNK, CHUNK), 1)
    tril_ones = (tri_r >= tri_c).astype(BF16)
    seg = bd_mask.astype(BF16)

    def bd(x):
        xb = x.astype(BF16)
        return jnp.where(bd_mask, jnp.concatenate([xb] * heads_per_group, axis=0),
                         jnp.zeros((), BF16))

    def fold(x):
        xm = jnp.where(bd_mask, x, 0.0)
        acc = xm[0:HEAD_DIM]
        for hh in range(1, heads_per_group):
            acc = acc + xm[hh * HEAD_DIM:(hh + 1) * HEAD_DIM]
        return acc

    def ld(i, p):
        c, g = p
        return feats[i][c * CHUNK:(c + 1) * CHUNK, g * gl:(g + 1) * gl]

    lw = [ld(1, p) for p in probs]
    na = [ld(4, p) for p in probs]
    bb = [ld(5, p) for p in probs]
    kf = [ld(2, p) for p in probs]
    r = [ld(0, p) for p in probs]
    v = [ld(3, p) for p in probs]

    def cumsum(x):
        hi = x.astype(BF16)
        lo = (x - hi.astype(F32)).astype(BF16)
        both = jnp.dot(tril_ones, jnp.concatenate([hi, lo], axis=1), preferred_element_type=F32)
        return both[:, :gl] + both[:, gl:]

    cum = _every(cumsum, lw)
    cum_last = [x[CHUNK - 1:CHUNK, :] for x in cum]
    e_out = [jnp.exp(-x) for x in cum]
    e_end = _every(lambda cl, x: jnp.exp(cl - x), cum_last, cum)
    a_t = _every(lambda n_, x, l_: n_ * jnp.exp(x - l_), na, cum, lw)
    r_t = _every(lambda r_, x: r_ * jnp.exp(x), r, cum)
    b_t = _every(lambda b_, e: b_ * e, bb, e_out)
    k_t = _every(lambda k_, e: k_ * e, kf, e_out)
    b_end = _every(lambda b_, e: b_ * e, bb, e_end)
    k_end = _every(lambda k_, e: k_ * e, kf, e_end)
    gamma = [jnp.exp(x) for x in cum_last]

    ar = _every(lambda a_, r_: jnp.concatenate([a_, r_], axis=0), a_t, r_t)
    pb = _every(lambda x, y: _dot_nt(x, bd(y)), ar, b_t)
    pk = _every(lambda x, y: _dot_nt(x, bd(y)), ar, k_t)
    l_ab = [jnp.where(strict, x[:CHUNK], 0.0) for x in pb]
    l_ak = [jnp.where(strict, x[:CHUNK], 0.0) for x in pk]
    m_rb = [jnp.where(incl, x[CHUNK:], 0.0) for x in pb]
    m_rk = [jnp.where(incl, x[CHUNK:], 0.0) for x in pk]

    x_acc = [eye_cat + l for l in l_ab]
    pw = _every(lambda l: _dot(l, bd(l)), l_ab)
    n_sq = int(math.log2(CHUNK)) - 1
    for lvl in range(n_sq):
        rhs = [bd(p_) for p_ in pw]
        if lvl < n_sq - 1:
            both = _every(lambda x, p_, w_: _dot(jnp.concatenate([x, p_], axis=0), w_),
                          x_acc, pw, rhs)
            x_acc = _every(lambda x, b_: x + b_[:CHUNK], x_acc, both)
            pw = [b_[CHUNK:] for b_ in both]
        else:
            x_acc = _every(lambda x, w_: x + _dot(x, w_), x_acc, rhs)
    t_inv = x_acc

    kv = _every(lambda la, mk, x: _dot(jnp.concatenate([la, mk], axis=0), bd(x)), l_ak, m_rk, v)
    y_loc = [x[:CHUNK] for x in kv]
    wu = _every(lambda t_, a_, y_: _dot(t_, jnp.concatenate([bd(a_), bd(y_)], axis=1)),
                t_inv, a_t, y_loc)
    w_t = [x[:, :gl] for x in wu]
    u_loc = [x[:, gl:] for x in wu]

    mwu = _every(lambda mb, w_, u_: _dot(mb, jnp.concatenate([bd(w_), bd(u_)], axis=1)),
                 m_rb, w_t, u_loc)
    q_c = _every(lambda r_, x: r_ + x[:, :gl], r_t, mwu)
    o_loc = _every(lambda x, y_: x[:, gl:] + y_[CHUNK:], mwu, kv)
    m_low = _every(lambda w_, b_: jnp.where(bd_mask, _dot_tn(w_, b_), 0.0).astype(BF16),
                   w_t, b_end)
    n_loc = _every(lambda u_, v_, b_, k_: fold(_dot_tn(jnp.concatenate([u_, v_], axis=0),
                                                       jnp.concatenate([b_, k_], axis=0))),
                   u_loc, v, b_end, k_end)

    state = [s_ref[g] for g in range(n_groups)]
    starts = []
    for c in range(n_chunks):
        starts.append(state)
        idx = [c * n_groups + g for g in range(n_groups)]
        state = [state[g] * gamma[i] + _dot(state[g], m_low[i]) + n_loc[i]
                 for g, i in enumerate(idx)]
    for g in range(n_groups):
        s_ref[g] = state[g]

    s0 = [starts[c][g] for (c, g) in probs]
    o = _every(lambda q_, s_, ol: _dot_nt(q_, bd(s_)) + ol, q_c, s0, o_loc)

    n_p = len(probs)
    unstack = lambda x: [x[i * CHUNK:(i + 1) * CHUNK] for i in range(n_p)]
    mu = unstack(_dot(jnp.concatenate(o, axis=0), seg) * (1.0 / HEAD_DIM))
    d = _every(lambda x, m_: x - m_, o, mu)
    var = unstack(_dot(jnp.concatenate([x * x for x in d], axis=0), seg) * (1.0 / HEAD_DIM))
    for i, (c, g) in enumerate(probs):
        ls = slice(g * gl, (g + 1) * gl)
        gn = d[i] * lax.rsqrt(var[i] + RWKV_GN_EPS) * ln_w[:, ls] + ln_b[:, ls]
        y = (gn + ld(7, (c, g))) * ld(6, (c, g))
        y_ref[c * CHUNK:(c + 1) * CHUNK, ls] = y.astype(y_ref.dtype)


def _wkv_sample_kernel(ft_ref, s_ref, s_out_ref, o_ref):
    hd = HEAD_DIM
    r, kf, na, bb = ft_ref[0], ft_ref[2], ft_ref[4], ft_ref[5]
    w = jnp.exp(ft_ref[1])

    def value_block(vb, carry):
        v0 = pl.multiple_of(vb * SUBLANES, SUBLANES)
        v_rows = ft_ref[3, pl.ds(v0, SUBLANES), :]
        outs = []
        for j in range(SUBLANES):
            s = s_ref[0, v0 + j]
            sa = jnp.sum(s * na, axis=0, keepdims=True)
            s_new = s * w + sa * bb + v_rows[j:j + 1] * kf
            s_out_ref[0, v0 + j] = s_new
            outs.append(jnp.sum(s_new * r, axis=0, keepdims=True))
        o_ref[pl.ds(v0, SUBLANES), :] = jnp.concatenate(outs, axis=0)
        return carry

    lax.fori_loop(0, hd // SUBLANES, value_block, 0)


def _wkv_sample_call(ft_s, state_t):
    h, hd, _, n = state_t.shape
    return pl.pallas_call(
        _wkv_sample_kernel,
        grid=(h,),
        in_specs=[
            pl.BlockSpec((6, hd, n), lambda i: (0, i, 0)),
            pl.BlockSpec((1, hd, hd, n), lambda i: (i, 0, 0, 0)),
        ],
        out_specs=(
            pl.BlockSpec((1, hd, hd, n), lambda i: (i, 0, 0, 0)),
            pl.BlockSpec((hd, n), lambda i: (i, 0)),
        ),
        out_shape=(
            jax.ShapeDtypeStruct(state_t.shape, F32),
            jax.ShapeDtypeStruct((h * hd, n), F32),
        ),
        compiler_params=pltpu.CompilerParams(
            dimension_semantics=("arbitrary",), vmem_limit_bytes=VMEM_LIMIT),
        name="wkv_sample",
    )(ft_s, state_t)


def _swa_prompt_kernel(sink_ref, q_ref, kp_ref, kc_ref, vp_ref, vc_ref, o_ref):
    j = pl.program_id(1)
    w = WINDOW
    tq = q_ref.shape[1]
    n_blk = tq // w
    gl = GROUP_LANES
    rows = SWA_GROUP * w
    probs = [(qi, g) for qi in range(n_blk) for g in range(SWA_KV_HEADS)]

    lane_head = lax.broadcasted_iota(jnp.int32, (w, gl), 1) // HEAD_DIM
    head_mask = [lane_head == h for h in range(SWA_GROUP)]
    qi_ = lax.broadcasted_iota(jnp.int32, (rows, 2 * w), 0) % w
    ki_ = lax.broadcasted_iota(jnp.int32, (rows, 2 * w), 1)
    diff = qi_ - (ki_ - w)
    in_window = (diff >= 0) & (diff < WINDOW)
    first_valid = in_window & ((j * tq + ki_ - w) >= 0)

    def keys(ref_prev, ref_cur, qi, g):
        ls = slice(g * gl, (g + 1) * gl)
        prev = ref_prev[0, :, ls] if qi == 0 else ref_cur[0, (qi - 1) * w:qi * w, ls]
        return jnp.concatenate([prev, ref_cur[0, qi * w:(qi + 1) * w, ls]], axis=0)

    def lhs(qi, g):
        qg = q_ref[0, qi * w:(qi + 1) * w, g * gl:(g + 1) * gl]
        return jnp.concatenate([jnp.where(m, qg, jnp.zeros((), BF16)) for m in head_mask], axis=0)

    sinks = []
    for g in range(SWA_KV_HEADS):
        sinks.append(jnp.concatenate(
            [jnp.full((w, 1), sink_ref[g * SWA_GROUP + h], F32) for h in range(SWA_GROUP)], axis=0))

    sink_lane = ki_ == qi_
    fill = [jnp.where(sink_lane, sk, MASK_VALUE) for sk in sinks]
    ones_k = jnp.ones((2 * w, LANES), BF16)

    s = [_dot_nt(lhs(qi, g), keys(kp_ref, kc_ref, qi, g)) for qi, g in probs]
    s = [jnp.where(first_valid if qi == 0 else in_window, x, fill[g])
         for x, (qi, g) in zip(s, probs)]
    m = [jnp.max(x, axis=-1, keepdims=True) for x in s]
    e = _every(lambda x, m_: jnp.exp(x - m_), s, m)
    inv = [1.0 / _dot(x, ones_k) for x in e]
    prob = _every(lambda e_, i_: jnp.where(sink_lane, jnp.zeros((), BF16),
                                           (e_ * jnp.concatenate([i_, i_], axis=1)).astype(BF16)),
                  e, inv)
    og = [_dot(p_, keys(vp_ref, vc_ref, qi, g)) for p_, (qi, g) in zip(prob, probs)]
    for x, (qi, g) in zip(og, probs):
        y = jnp.where(head_mask[0], x[0:w], 0.0)
        for h in range(1, SWA_GROUP):
            y = y + jnp.where(head_mask[h], x[h * w:(h + 1) * w], 0.0)
        o_ref[0, qi * w:(qi + 1) * w, g * gl:(g + 1) * gl] = y.astype(o_ref.dtype)


def _swa_prompt_call(sinks, q, k, v, tq):
    b, t, _ = q.shape
    w = WINDOW
    per = tq // w
    prev = lambda i, j: (i, jnp.maximum(j * per - 1, 0), 0)
    cur = lambda i, j: (i, j, 0)
    return pl.pallas_call(
        _swa_prompt_kernel,
        grid=(b, t // tq),
        in_specs=[
            pl.BlockSpec(memory_space=pltpu.SMEM),
            pl.BlockSpec((1, tq, SWA_WIDTH), cur),
            pl.BlockSpec((1, w, SWA_WIDTH), prev),
            pl.BlockSpec((1, tq, SWA_WIDTH), cur),
            pl.BlockSpec((1, w, SWA_WIDTH), prev),
            pl.BlockSpec((1, tq, SWA_WIDTH), cur),
        ],
        out_specs=pl.BlockSpec((1, tq, SWA_WIDTH), cur),
        out_shape=jax.ShapeDtypeStruct((b, t, SWA_WIDTH), BF16),
        compiler_params=pltpu.CompilerParams(
            dimension_semantics=("arbitrary", "arbitrary"), vmem_limit_bytes=VMEM_LIMIT),
        name="swa_prompt",
    )(sinks, q, k, k, v, v)


def _swa_sample_kernel(sink_ref, q_ref, knr_ref, vnr_ref, knt_ref, vnt_ref, ck_ref, cv_ref,
                       o_ref, ko_ref, vo_ref):
    nb, _, hd, l = ck_ref.shape
    q = q_ref[...]
    ck = ck_ref[...].reshape(nb * SWA_KV_HEADS, hd, l)
    cv = cv_ref[...].reshape(nb * SWA_KV_HEADS, hd, l)
    knr = knr_ref[...].astype(BF16).astype(F32)
    vnr = vnr_ref[...].astype(BF16).astype(F32)
    ki = lax.broadcasted_iota(jnp.int32, (1, 1, l), 2)
    kpos = PAST_LEN - l + ki
    diff = PAST_LEN - kpos
    valid = (diff >= 0) & (diff < WINDOW) & (kpos >= 0)
    s_c = lax.dot_general(q, ck.astype(BF16), (((2,), (1,)), ((0,), (0,))),
                          preferred_element_type=F32)
    s_c = jnp.where(valid, s_c, MASK_VALUE)
    s_n = jnp.sum(q.astype(F32) * knr, axis=-1, keepdims=True)
    sink = sink_ref[:, :, 0:1]
    m = jnp.maximum(jnp.maximum(jnp.max(s_c, axis=-1, keepdims=True), s_n), sink)
    e_c = jnp.exp(s_c - m)
    e_n = jnp.exp(s_n - m)
    denom = jnp.sum(e_c, axis=-1, keepdims=True) + e_n + jnp.exp(sink - m)
    p_c = (e_c / denom).astype(BF16)
    p_n = (e_n / denom).astype(BF16).astype(F32)
    o = lax.dot_general(p_c, cv.astype(BF16), (((2,), (2,)), ((0,), (0,))),
                        preferred_element_type=F32)
    o_ref[...] = o + p_n * vnr
    lane = lax.broadcasted_iota(jnp.int32, (hd, l), 1)
    for b in range(nb):
        for g in range(SWA_KV_HEADS):
            rows = slice(g * hd, (g + 1) * hd)
            ko_ref[b, g] = jnp.where(lane == l - 1, knt_ref[0, rows, b:b + 1],
                                     pltpu.roll(ck_ref[b, g], l - 1, 1))
            vo_ref[b, g] = jnp.where(lane == l - 1, vnt_ref[0, rows, b:b + 1],
                                     pltpu.roll(cv_ref[b, g], l - 1, 1))


def _swa_sample_call(sink_t, q8, knr, vnr, knt, vnt, ck, cv, nb):
    n, g, hd, l = ck.shape
    rows = nb * g
    blk3 = lambda i: (i, 0, 0)
    blk4 = lambda i: (i, 0, 0, 0)
    return pl.pallas_call(
        _swa_sample_kernel,
        grid=(n // nb,),
        in_specs=[
            pl.BlockSpec((rows, SUBLANES, LANES), lambda i: (0, 0, 0)),
            pl.BlockSpec((rows, SUBLANES, hd), blk3),
            pl.BlockSpec((rows, 1, hd), blk3),
            pl.BlockSpec((rows, 1, hd), blk3),
            pl.BlockSpec((1, g * hd, nb), blk3),
            pl.BlockSpec((1, g * hd, nb), blk3),
            pl.BlockSpec((nb, g, hd, l), blk4),
            pl.BlockSpec((nb, g, hd, l), blk4),
        ],
        out_specs=(
            pl.BlockSpec((rows, SUBLANES, hd), blk3),
            pl.BlockSpec((nb, g, hd, l), blk4),
            pl.BlockSpec((nb, g, hd, l), blk4),
        ),
        out_shape=(
            jax.ShapeDtypeStruct((n * g, SUBLANES, hd), F32),
            jax.ShapeDtypeStruct(ck.shape, F32),
            jax.ShapeDtypeStruct(cv.shape, F32),
        ),
        compiler_params=pltpu.CompilerParams(
            dimension_semantics=("arbitrary",), vmem_limit_bytes=VMEM_LIMIT),
        name="swa_sample",
    )(sink_t, q8, knr, vnr, knt, vnt, ck, cv)


def _tail_mix(x, y_rwkv, y_swa, g1, sh2, sc2, w_out, n_post_mix, n_pre_ffn):
    mix = _dot(y_rwkv, w_out[:RWKV_WIDTH]) + _dot(y_swa, w_out[RWKV_WIDTH:])
    x1 = x + g1 * _rmsnorm(mix, n_post_mix)
    h2 = (_rmsnorm(x1, n_pre_ffn) * (1.0 + sc2) + sh2).astype(BF16)
    return x1, h2


def _tail_prompt_kernel(x_ref, yr_ref, ys_ref, mod_ref, w_out, n_post_mix, n_pre_ffn,
                        n_post_ffn, w_up_a, w_up_b, w_down, cw_ref, cb_ref,
                        y_ref, cp_ref, carry_ref, act_ref):
    t = pl.program_id(1)
    tm = x_ref.shape[1]
    sub = min(tm, TAIL_SUB)
    grp = min(tm, TAIL_GROUP)

    @pl.when(t == 0)
    def _():
        carry_ref[...] = jnp.zeros_like(carry_ref)

    g1, sh2, sc2, g2 = mod_ref[2, 0], mod_ref[3, 0], mod_ref[4, 0], mod_ref[5, 0]

    def cols_of(c, half):
        return slice(half * D_FF + c * FF_CHUNK, half * D_FF + (c + 1) * FF_CHUNK)

    def conv(u, cols):
        ext = jnp.concatenate([carry_ref[:, cols], u], axis=0)
        last = u[sub - SUBLANES:]
        carry_ref[:, cols] = last
        cp_ref[0, :, cols] = last
        return (cb_ref[:, cols] + pltpu.roll(ext, 2, 0)[SUBLANES:] * cw_ref[0:1, cols]
                + pltpu.roll(ext, 1, 0)[SUBLANES:] * cw_ref[1:2, cols] + u * cw_ref[2:3, cols])

    def group(gi, carry):
        base = pl.multiple_of(gi * grp, grp)
        blocks = [(pl.ds(base + j * sub, sub), slice(j * sub, (j + 1) * sub))
                  for j in range(grp // sub)]
        mixes = [_dot(yr_ref[0, rb, :], w_out[:RWKV_WIDTH]) + _dot(ys_ref[0, rb, :], w_out[RWKV_WIDTH:])
                 for rb, _ in blocks]
        for (rb, ab), mix in zip(blocks, mixes):
            x1 = x_ref[0, rb, :] + g1 * _rmsnorm(mix, n_post_mix[...])
            h2 = (_rmsnorm(x1, n_pre_ffn[...]) * (1.0 + sc2) + sh2).astype(BF16)

            def up(c):
                chunk = slice(c * FF_CHUNK, (c + 1) * FF_CHUNK)
                return [jnp.dot(h2, w[:, chunk], preferred_element_type=F32)
                        for w in (w_up_a, w_up_b)]

            def down(c):
                rows = slice(c * FF_CHUNK, (c + 1) * FF_CHUNK)
                return jnp.dot(act_ref[ab, rows], w_down[rows, :], preferred_element_type=F32)

            u_next = up(0)
            ff = None
            for c in range(N_FF_CHUNKS):
                u_cur = u_next
                if c + 1 < N_FF_CHUNKS:
                    u_next = up(c + 1)
                if c >= 1:
                    part = down(c - 1)
                    ff = part if ff is None else ff + part
                za, zb = [conv(u_cur[half], cols_of(c, half)) for half in range(2)]
                act_ref[ab, c * FF_CHUNK:(c + 1) * FF_CHUNK] = (_silu(za) * zb).astype(BF16)
            ff = ff + down(N_FF_CHUNKS - 1)
            y_ref[0, rb, :] = x1 + g2 * _rmsnorm(ff, n_post_ffn[...])
        return carry

    lax.fori_loop(0, tm // grp, group, 0)


def _tail_sample_kernel(x_ref, o_ref, feat_ref, ys_ref, mod_ref, ln_w, ln_b, w_out, n_post_mix,
                        n_pre_ffn, n_post_ffn, wa_ref, wb_ref, wd_ref, p0a, p0b, p1a, p1b,
                        cwa, cwb, cba, cbb,
                        y_ref, ua_ref, ub_ref, w_out_b_ref, wa_b_ref, wb_b_ref, wd_b_ref,
                        x1_ref, h2_ref, acc_ref):
    c = pl.program_id(0)

    @pl.when(c == 0)
    def _():
        w_out_b = w_out[...].astype(BF16)
        w_out_b_ref[...] = w_out_b
        seg = _seg_ones(RWKV_WIDTH)
        y_rwkv = _gn_epilogue(o_ref[...].T, feat_ref[7], feat_ref[6], ln_w[...], ln_b[...], seg)
        x1, h2 = _tail_mix(x_ref[...], y_rwkv, ys_ref[...], mod_ref[2], mod_ref[3], mod_ref[4],
                           w_out_b, n_post_mix[...], n_pre_ffn[...])
        x1_ref[...] = x1
        h2_ref[...] = h2
        acc_ref[...] = jnp.zeros_like(acc_ref)

    wa = wa_ref[...].astype(BF16)
    wb = wb_ref[...].astype(BF16)
    wd = wd_ref[...].astype(BF16)
    wa_b_ref[...] = wa
    wb_b_ref[...] = wb
    wd_b_ref[...] = wd
    h2 = h2_ref[...]
    ua = jnp.dot(h2, wa, preferred_element_type=F32)
    ub = jnp.dot(h2, wb, preferred_element_type=F32)
    ua_ref[...] = ua
    ub_ref[...] = ub
    za = cba[...] + p0a[...] * cwa[0:1, :] + p1a[...] * cwa[1:2, :] + ua * cwa[2:3, :]
    zb = cbb[...] + p0b[...] * cwb[0:1, :] + p1b[...] * cwb[1:2, :] + ub * cwb[2:3, :]
    acc_ref[...] += _dot(_silu(za) * zb, wd)

    @pl.when(c == pl.num_programs(0) - 1)
    def _():
        y_ref[...] = x1_ref[...] + mod_ref[5] * _rmsnorm(acc_ref[...], n_post_ffn[...])


def _tail_prompt_call(x, y_rwkv, y_swa, mod_p, consts, tm):
    b, t, _ = x.shape
    tok = lambda i, j: (i, j, 0)
    in_specs = [
        pl.BlockSpec((1, tm, D_MODEL), tok),
        pl.BlockSpec((1, tm, RWKV_WIDTH), tok),
        pl.BlockSpec((1, tm, SWA_WIDTH), tok),
        pl.BlockSpec((6, 1, 1, D_MODEL), lambda i, j: (0, i, 0, 0)),
    ] + [_const_spec(c, 2) for c in consts]
    return pl.pallas_call(
        _tail_prompt_kernel,
        grid=(b, t // tm),
        in_specs=in_specs,
        out_specs=(
            pl.BlockSpec((1, tm, D_MODEL), tok),
            pl.BlockSpec((1, SUBLANES, 2 * D_FF), lambda i, j: (i, 0, 0)),
        ),
        out_shape=(
            jax.ShapeDtypeStruct((b, t, D_MODEL), F32),
            jax.ShapeDtypeStruct((b, SUBLANES, 2 * D_FF), F32),
        ),
        scratch_shapes=[
            pltpu.VMEM((SUBLANES, 2 * D_FF), F32),
            pltpu.VMEM((min(tm, TAIL_GROUP), D_FF), BF16),
        ],
        compiler_params=pltpu.CompilerParams(
            dimension_semantics=("arbitrary", "arbitrary"), vmem_limit_bytes=VMEM_LIMIT),
        name="tail_prompt",
    )(x, y_rwkv, y_swa, mod_p, *consts)


def _tail_sample_call(x, o_t, feat_s, y_swa, mod, p0, p1, ln_w, ln_b, w_out, n_post_mix,
                      n_pre_ffn, n_post_ffn, w_up, w_down, cw, cb):
    n = x.shape[0]
    nc = N_FF_CHUNKS
    fc = FF_CHUNK
    whole = lambda a: pl.BlockSpec(a.shape, lambda c, nd=a.ndim: (0,) * nd)
    gate = lambda rows: pl.BlockSpec((rows, fc), lambda c: (0, c))
    value = lambda rows: pl.BlockSpec((rows, fc), lambda c: (0, nc + c))
    in_specs = [
        whole(x), whole(o_t), whole(feat_s), whole(y_swa),
        pl.BlockSpec((mod.shape[0], n, D_MODEL), lambda c: (0, 0, 0)),
        whole(ln_w), whole(ln_b), whole(w_out), whole(n_post_mix), whole(n_pre_ffn),
        whole(n_post_ffn),
        gate(D_MODEL), value(D_MODEL), pl.BlockSpec((fc, D_MODEL), lambda c: (c, 0)),
        gate(n), value(n), gate(n), value(n), gate(3), value(3), gate(1), value(1),
    ]
    out_shape = (
        jax.ShapeDtypeStruct((n, D_MODEL), F32),
        jax.ShapeDtypeStruct((n, D_FF), F32),
        jax.ShapeDtypeStruct((n, D_FF), F32),
        jax.ShapeDtypeStruct((D_MODEL, D_MODEL), BF16),
        jax.ShapeDtypeStruct((D_MODEL, D_FF), BF16),
        jax.ShapeDtypeStruct((D_MODEL, D_FF), BF16),
        jax.ShapeDtypeStruct((D_FF, D_MODEL), BF16),
    )
    chunk_cols = lambda rows: pl.BlockSpec((rows, fc), lambda c: (0, c))
    out_specs = (
        pl.BlockSpec((n, D_MODEL), lambda c: (0, 0)),
        chunk_cols(n), chunk_cols(n),
        pl.BlockSpec((D_MODEL, D_MODEL), lambda c: (0, 0)),
        chunk_cols(D_MODEL), chunk_cols(D_MODEL),
        pl.BlockSpec((fc, D_MODEL), lambda c: (c, 0)),
    )
    return pl.pallas_call(
        _tail_sample_kernel,
        grid=(nc,),
        in_specs=in_specs,
        out_specs=out_specs,
        out_shape=out_shape,
        scratch_shapes=[
            pltpu.VMEM((n, D_MODEL), F32),
            pltpu.VMEM((n, D_MODEL), BF16),
            pltpu.VMEM((n, D_MODEL), F32),
        ],
        compiler_params=pltpu.CompilerParams(
            dimension_semantics=("arbitrary",), vmem_limit_bytes=VMEM_LIMIT),
        name="tail_sample",
    )(x, o_t, feat_s, y_swa, mod, ln_w, ln_b, w_out, n_post_mix, n_pre_ffn, n_post_ffn,
      w_up, w_up, w_down, p0, p0, p1, p1, cw, cw, cb, cb)


def _rope_tables(pos):
    half = HEAD_DIM // 2
    inv = ROPE_THETA ** (-jnp.arange(half, dtype=F32) / half)
    ang = pos.astype(F32)[:, None] * inv[None, :]
    cos, sin = jnp.cos(ang), jnp.sin(ang)
    cos_h = jnp.concatenate([cos, cos], axis=-1)
    sin_h = jnp.concatenate([-sin, sin], axis=-1)
    return jnp.tile(cos_h, (1, LANES // HEAD_DIM)), jnp.tile(sin_h, (1, LANES // HEAD_DIM))


def kernel(x_prompt, x_sample, state_rwkv_wkv, state_rwkv_shift, cache_swa_k, cache_swa_v,
           state_ffn_conv, c_prompt, c_sample, w_ada, b_ada, norm_pre_mix, norm_post_mix,
           norm_pre_ffn, norm_post_ffn, w_in, rwkv_mu, rwkv_w0, rwkv_w_up, rwkv_a0, rwkv_a_up,
           rwkv_g_up, rwkv_k_k, rwkv_k_a, rwkv_r_k, rwkv_ln_w, rwkv_ln_b, swa_sinks, w_out,
           ffn_w_up, ffn_conv_w, ffn_conv_b, ffn_w_down):
    depth = w_ada.shape[0]
    assert depth == 1 and x_sample.shape[1] == 1
    b, t, _ = x_prompt.shape
    n = x_sample.shape[0]
    tm = min(256, t)
    blk = min(4 * tm, t)
    assert t % blk == 0 and tm % WINDOW == 0 and tm % CHUNK == 0 and n % SUBLANES == 0
    nb_swa = 2 * SUBLANES if n % (2 * SUBLANES) == 0 else SUBLANES
    li = 0

    row = lambda v: v.reshape(1, -1)
    zeros_l = jnp.zeros((64, RWKV_WIDTH), F32)
    wa_up = jnp.concatenate([
        jnp.concatenate([rwkv_w_up[li], zeros_l], axis=1),
        jnp.concatenate([zeros_l, rwkv_a_up[li]], axis=1)], axis=0).astype(BF16)
    hid = jnp.arange(GROUP_LANES) // HEAD_DIM
    seg_blk = (hid[:, None] == hid[None, :]).astype(BF16)
    proj_consts = (row(norm_pre_mix[li]), w_in[li], row(rwkv_mu[li]), row(rwkv_w0[li]), wa_up,
                   row(rwkv_a0[li]), rwkv_g_up[li].astype(BF16), row(rwkv_k_k[li]),
                   row(rwkv_k_a[li]), row(rwkv_r_k[li]), seg_blk)
    norms = (row(norm_post_mix[li]), row(norm_pre_ffn[li]), row(norm_post_ffn[li]))
    cw, cb = ffn_conv_w[li], row(ffn_conv_b[li])
    ln_w, ln_b = row(rwkv_ln_w[li]), row(rwkv_ln_b[li])

    mod = _ada_call(jnp.concatenate([c_sample, c_prompt], axis=0), w_ada[li], row(b_ada[li]))
    mod_p = mod[:, n:].reshape(6, b, 1, D_MODEL)

    cos_s, sin_s = _rope_tables(jnp.full((1,), PAST_LEN, jnp.int32))
    feat_s, ft_s, q_s, kn_s, vn_s, p_s, w_in_b = _proj_sample_call(
        x_sample[:, 0], mod, cos_s, sin_s, state_rwkv_shift[li], proj_consts)

    state_t = jnp.transpose(state_rwkv_wkv[li], (1, 2, 3, 0))
    wkv_t, o_t = _wkv_sample_call(ft_s, state_t)
    wkv_s = jnp.transpose(wkv_t, (3, 0, 1, 2))

    g2 = SWA_KV_HEADS
    q4 = q_s.reshape(n * g2, SWA_GROUP, HEAD_DIM)
    q8 = jnp.concatenate([q4, jnp.zeros_like(q4)], axis=1)
    sink_t = jnp.broadcast_to(
        jnp.concatenate([swa_sinks[li].reshape(g2, SWA_GROUP),
                         jnp.full((g2, SWA_GROUP), MASK_VALUE, F32)], axis=1)[None, :, :, None],
        (nb_swa, g2, SUBLANES, LANES)).reshape(nb_swa * g2, SUBLANES, LANES)
    cols = lambda a: a.reshape(n // nb_swa, nb_swa, KV_WIDTH).transpose(0, 2, 1)
    ck = jnp.transpose(cache_swa_k[li], (0, 2, 3, 1))
    cv = jnp.transpose(cache_swa_v[li], (0, 2, 3, 1))
    o_att, k_t, v_t = _swa_sample_call(
        sink_t, q8, kn_s.reshape(n * g2, 1, HEAD_DIM), vn_s.reshape(n * g2, 1, HEAD_DIM),
        cols(kn_s), cols(vn_s), ck, cv, nb_swa)
    y_swa_s = o_att[:, :SWA_GROUP].reshape(n, SWA_WIDTH)
    k_s = jnp.transpose(k_t, (0, 3, 1, 2))
    v_s = jnp.transpose(v_t, (0, 3, 1, 2))

    conv0 = state_ffn_conv[li]
    y_s, ua_s, ub_s, w_out_b, w_up_a_b, w_up_b_b, w_down_b = _tail_sample_call(
        x_sample[:, 0], o_t, feat_s, y_swa_s.astype(BF16), mod, conv0[:, 0], conv0[:, 1],
        ln_w, ln_b, w_out[li], *norms, ffn_w_up[li], ffn_w_down[li], cw, cb)
    conv_s = jnp.stack([conv0[:, 1], jnp.concatenate([ua_s, ub_s], axis=1)], axis=1)

    cos_p, sin_p = _rope_tables(jnp.arange(t, dtype=jnp.int32))
    proj_consts_p = proj_consts[:1] + (w_in_b,) + proj_consts[2:]
    y_rwkv_p, s_cat, q_p, kx_p, vx_p, klast, vlast, plast = _mix_prompt_call(
        x_prompt, mod_p, cos_p, sin_p, proj_consts_p + (ln_w, ln_b), blk)
    y_swa_p = _swa_prompt_call(swa_sinks[li], q_p, kx_p, vx_p, min(2 * blk, t))
    tail_consts = (w_out_b,) + norms + (w_up_a_b, w_up_b_b, w_down_b, cw, cb)
    y_p, cp = _tail_prompt_call(x_prompt, y_rwkv_p, y_swa_p, mod_p, tail_consts, blk)

    wkv_p = s_cat.reshape(b, HEAD_DIM, RWKV_HEADS, HEAD_DIM).transpose(0, 2, 1, 3)
    shift_p = plast[:, SUBLANES - 1]
    k_p = klast.reshape(b, WINDOW, SWA_KV_HEADS, HEAD_DIM)
    v_p = vlast.reshape(b, WINDOW, SWA_KV_HEADS, HEAD_DIM)
    conv_p = cp[:, SUBLANES - 2:]

    expand = lambda a: a[None]
    return (y_p, y_s[:, None, :], expand(wkv_p), expand(shift_p), expand(k_p), expand(v_p),
            expand(conv_p), expand(wkv_s), expand(p_s),
            expand(k_s), expand(v_s), expand(conv_s))
```

```python
import math

import jax
import jax.numpy as jnp
from jax import lax
from jax.experimental import pallas as pl
from jax.experimental.pallas import tpu as pltpu

D_MODEL = 1024
HEAD_DIM = 64
RWKV_WIDTH = 512
RWKV_HEADS = 8
RWKV_COLS = 1792
RWKV_GN_EPS = 64e-5
SWA_WIDTH = 512
SWA_HEADS = 8
SWA_KV_HEADS = 2
SWA_GROUP = 4
KV_WIDTH = SWA_KV_HEADS * HEAD_DIM
WINDOW = 128
PAST_LEN = 16384
ROPE_THETA = 10000.0
ATTN_SCALE = HEAD_DIM ** -0.5
D_FF = 2816
NORM_EPS = 1e-6
MASK_VALUE = -1e30
PROJ_COLS = RWKV_COLS + SWA_WIDTH + 2 * KV_WIDTH

LANES = 128
SUBLANES = 8
CHUNK = 64
GROUP_LANES = 256
FF_CHUNK = 256
N_FF_CHUNKS = D_FF // FF_CHUNK
TAIL_SUB = 256
MIX_SUB = 512
TAIL_GROUP = 512
VMEM_LIMIT = 56 * 1024 * 1024

F32 = jnp.float32
BF16 = jnp.bfloat16


def _sigmoid(x):
    return 1.0 / (1.0 + jnp.exp(-x))


def _silu(x):
    return x * _sigmoid(x)


def _rmsnorm(x, g):
    return x * lax.rsqrt(jnp.mean(x * x, axis=-1, keepdims=True) + NORM_EPS) * g


def _dot(a, b):
    return jnp.dot(a.astype(BF16), b.astype(BF16), preferred_element_type=F32)


def _dot_nt(a, b):
    return lax.dot_general(a.astype(BF16), b.astype(BF16), (((1,), (1,)), ((), ())),
                           preferred_element_type=F32)


def _dot_tn(a, b):
    return lax.dot_general(a.astype(BF16), b.astype(BF16), (((0,), (0,)), ((), ())),
                           preferred_element_type=F32)


def _every(fn, *lists):
    return [fn(*a) for a in zip(*lists)]


def _swap_halves(x):
    w = x.shape[-1]
    lane = lax.broadcasted_iota(jnp.int32, x.shape, x.ndim - 1)
    lo = (lane & (HEAD_DIM // 2)) == 0
    return jnp.where(lo, pltpu.roll(x, w - HEAD_DIM // 2, x.ndim - 1),
                     pltpu.roll(x, HEAD_DIM // 2, x.ndim - 1))


def _rope(x, cos, sin):
    reps = x.shape[-1] // LANES
    cos_w = jnp.concatenate([cos] * reps, axis=-1) if reps > 1 else cos
    sin_w = jnp.concatenate([sin] * reps, axis=-1) if reps > 1 else sin
    return x * cos_w + _swap_halves(x) * sin_w


def _proj_features(x, shift, scale, g_pre, w_in, prev_fn, mu, w0, wa_up, a0, g_up,
                   k_k, k_a, r_k, seg_blk, cos, sin):
    h = _rmsnorm(x, g_pre) * (1.0 + scale) + shift
    p = _dot(h, w_in)
    p_rwkv = p[:, :RWKV_COLS]
    prev = prev_fn(p_rwkv)
    xm = p_rwkv + (prev - p_rwkv) * mu
    r = xm[:, 0:512]
    k = xm[:, 512:1024]
    v = xm[:, 1024:1536]
    wa = xm[:, 1536:1664]
    gd = xm[:, 1664:1792]
    lane = lax.broadcasted_iota(jnp.int32, wa.shape, 1)
    wa_act = jnp.where(lane < 64, jnp.tanh(wa), wa)
    lora = _dot(wa_act, wa_up)
    lw = -math.exp(-0.5) * _sigmoid(w0 + lora[:, :512])
    a = _sigmoid(a0 + lora[:, 512:])
    g = _dot(_sigmoid(gd), g_up)
    kk = k * k_k
    kf = k * (1.0 + (a - 1.0) * k_a)
    gl = seg_blk.shape[0]
    n = x.shape[0]
    sums = [_dot(jnp.concatenate([(kk * kk)[:, i:i + gl], (r * kf * r_k)[:, i:i + gl]], axis=0),
                 seg_blk) for i in range(0, RWKV_WIDTH, gl)]
    ss = jnp.concatenate([s_[:n] for s_ in sums], axis=1)
    kk = kk / jnp.maximum(jnp.sqrt(ss), 1e-12)
    bonus = jnp.concatenate([s_[n:] for s_ in sums], axis=1) * v
    feats = (r, lw, kf, v, -kk, kk * a, g, bonus)
    q = _rope(p[:, RWKV_COLS:RWKV_COLS + SWA_WIDTH], cos, sin)
    ks = _rope(p[:, RWKV_COLS + SWA_WIDTH:RWKV_COLS + SWA_WIDTH + KV_WIDTH], cos, sin)
    vs = p[:, RWKV_COLS + SWA_WIDTH + KV_WIDTH:]
    return feats, q, ks, vs, p_rwkv


def _gn_epilogue(o, bonus, g, ln_w, ln_b, seg):
    mu = _dot(o, seg) * (1.0 / HEAD_DIM)
    d = o - mu
    var = _dot(d * d, seg) * (1.0 / HEAD_DIM)
    gn = d * lax.rsqrt(var + RWKV_GN_EPS) * ln_w + ln_b
    return (gn + bonus) * g


def _seg_ones(n):
    r = lax.broadcasted_iota(jnp.int32, (n, n), 0) // HEAD_DIM
    c = lax.broadcasted_iota(jnp.int32, (n, n), 1) // HEAD_DIM
    return (r == c).astype(BF16)


def _ada_kernel(c_ref, w_ref, b_ref, o_ref):
    o_ref[0] = _dot(_silu(c_ref[...]), w_ref[...]) + b_ref[...]


def _ada_call(c_all, w_ada, b_ada):
    rows = c_all.shape[0]
    return pl.pallas_call(
        _ada_kernel,
        grid=(6,),
        in_specs=[
            pl.BlockSpec((rows, D_MODEL), lambda j: (0, 0)),
            pl.BlockSpec((D_MODEL, D_MODEL), lambda j: (0, j)),
            pl.BlockSpec((1, D_MODEL), lambda j: (0, j)),
        ],
        out_specs=pl.BlockSpec((1, rows, D_MODEL), lambda j: (j, 0, 0)),
        out_shape=jax.ShapeDtypeStruct((6, rows, D_MODEL), F32),
        compiler_params=pltpu.CompilerParams(
            dimension_semantics=("arbitrary",), vmem_limit_bytes=VMEM_LIMIT),
        name="ada",
    )(c_all, w_ada, b_ada)


def _expand_kv(x):
    lane = lax.broadcasted_iota(jnp.int32, x.shape, 1)
    rolled = pltpu.roll(x, HEAD_DIM, 1)
    g0 = jnp.where(lane < HEAD_DIM, x, rolled)
    g1 = jnp.where(lane < HEAD_DIM, rolled, x)
    return jnp.concatenate([g0, g0, g1, g1], axis=1)


def _mix_prompt_kernel(x_ref, mod_ref, cos_ref, sin_ref, g_pre, w_in, mu, w0, wa_up, a0,
                       g_up, k_k, k_a, r_k, seg_blk, ln_w, ln_b,
                       y_ref, s_out_ref, q_ref, k_ref, v_ref, klast_ref, vlast_ref, plast_ref,
                       carry_ref, s_ref):
    t = pl.program_id(1)

    @pl.when(t == 0)
    def _():
        carry_ref[...] = jnp.zeros_like(carry_ref)
        s_ref[...] = jnp.zeros_like(s_ref)

    sub = min(x_ref.shape[1], MIX_SUB)

    def sub_tile(i, carry):
        rows = pl.ds(pl.multiple_of(i * sub, sub), sub)
        carry_row = carry_ref[SUBLANES - 1:SUBLANES, :]

        def prev_fn(p_rwkv):
            row = lax.broadcasted_iota(jnp.int32, p_rwkv.shape, 0)
            return jnp.where(row == 0, carry_row, pltpu.roll(p_rwkv, 1, 0))

        feats, q, ks, vs, p_rwkv = _proj_features(
            x_ref[0, rows, :], mod_ref[0, 0], mod_ref[1, 0], g_pre[...], w_in[...], prev_fn,
            mu[...], w0[...], wa_up[...], a0[...], g_up[...], k_k[...], k_a[...], r_k[...],
            seg_blk[...], cos_ref[rows, :], sin_ref[rows, :])
        q_ref[0, rows, :] = (q * ATTN_SCALE).astype(BF16)
        k_ref[0, rows, :] = _expand_kv(ks).astype(BF16)
        v_ref[0, rows, :] = _expand_kv(vs).astype(BF16)
        last = p_rwkv[sub - SUBLANES:, :]
        carry_ref[...] = last
        plast_ref[0] = last
        klast_ref[0] = ks[sub - WINDOW:, :]
        vlast_ref[0] = vs[sub - WINDOW:, :]
        _wkv_tile(feats, ln_w, ln_b, y_ref.at[0, rows, :], s_ref)
        return carry

    lax.fori_loop(0, x_ref.shape[1] // sub, sub_tile, 0)

    @pl.when(t == pl.num_programs(1) - 1)
    def _():
        for g in range(RWKV_WIDTH // GROUP_LANES):
            s_out_ref[0, :, g * GROUP_LANES:(g + 1) * GROUP_LANES] = s_ref[g]


def _proj_sample_kernel(x_ref, mod_ref, cos_ref, sin_ref, prev_ref, g_pre, w_in, mu, w0,
                        wa_up, a0, g_up, k_k, k_a, r_k, seg_blk,
                        feat_ref, ft_ref, q_ref, k_ref, v_ref, p_ref, w_in_b_ref):
    w_in_b = w_in[...].astype(BF16)
    w_in_b_ref[...] = w_in_b
    feats, q, ks, vs, p_rwkv = _proj_features(
        x_ref[...], mod_ref[0], mod_ref[1], g_pre[...], w_in_b, lambda p: prev_ref[...],
        mu[...], w0[...], wa_up[...], a0[...], g_up[...], k_k[...], k_a[...], r_k[...],
        seg_blk[...], cos_ref[...], sin_ref[...])
    for i, f in enumerate(feats):
        feat_ref[i] = f
    for i in range(6):
        ft_ref[i] = feats[i].T
    q_ref[...] = (q * ATTN_SCALE).astype(BF16)
    k_ref[...] = ks
    v_ref[...] = vs
    p_ref[...] = p_rwkv


def _const_spec(arr, grid_rank):
    zeros = (0,) * arr.ndim
    if grid_rank == 1:
        return pl.BlockSpec(arr.shape, lambda i: zeros)
    return pl.BlockSpec(arr.shape, lambda i, j: zeros)


def _mix_prompt_call(x, mod_p, cos, sin, consts, tm):
    b, t, _ = x.shape
    nt = t // tm
    in_specs = [
        pl.BlockSpec((1, tm, D_MODEL), lambda i, j: (i, j, 0)),
        pl.BlockSpec((6, 1, 1, D_MODEL), lambda i, j: (0, i, 0, 0)),
        pl.BlockSpec((tm, LANES), lambda i, j: (j, 0)),
        pl.BlockSpec((tm, LANES), lambda i, j: (j, 0)),
    ] + [_const_spec(c, 2) for c in consts]
    out_shape = (
        jax.ShapeDtypeStruct((b, t, RWKV_WIDTH), BF16),
        jax.ShapeDtypeStruct((b, HEAD_DIM, RWKV_WIDTH), F32),
        jax.ShapeDtypeStruct((b, t, SWA_WIDTH), BF16),
        jax.ShapeDtypeStruct((b, t, SWA_WIDTH), BF16),
        jax.ShapeDtypeStruct((b, t, SWA_WIDTH), BF16),
        jax.ShapeDtypeStruct((b, WINDOW, KV_WIDTH), F32),
        jax.ShapeDtypeStruct((b, WINDOW, KV_WIDTH), F32),
        jax.ShapeDtypeStruct((b, SUBLANES, RWKV_COLS), F32),
    )
    out_specs = (
        pl.BlockSpec((1, tm, RWKV_WIDTH), lambda i, j: (i, j, 0)),
        pl.BlockSpec((1, HEAD_DIM, RWKV_WIDTH), lambda i, j: (i, 0, 0)),
        pl.BlockSpec((1, tm, SWA_WIDTH), lambda i, j: (i, j, 0)),
        pl.BlockSpec((1, tm, SWA_WIDTH), lambda i, j: (i, j, 0)),
        pl.BlockSpec((1, tm, SWA_WIDTH), lambda i, j: (i, j, 0)),
        pl.BlockSpec((1, WINDOW, KV_WIDTH), lambda i, j: (i, 0, 0)),
        pl.BlockSpec((1, WINDOW, KV_WIDTH), lambda i, j: (i, 0, 0)),
        pl.BlockSpec((1, SUBLANES, RWKV_COLS), lambda i, j: (i, 0, 0)),
    )
    return pl.pallas_call(
        _mix_prompt_kernel,
        grid=(b, nt),
        in_specs=in_specs,
        out_specs=out_specs,
        out_shape=out_shape,
        scratch_shapes=[
            pltpu.VMEM((SUBLANES, RWKV_COLS), F32),
            pltpu.VMEM((RWKV_WIDTH // GROUP_LANES, HEAD_DIM, GROUP_LANES), F32),
        ],
        compiler_params=pltpu.CompilerParams(
            dimension_semantics=("arbitrary", "arbitrary"), vmem_limit_bytes=VMEM_LIMIT),
        name="mix_prompt",
    )(x, mod_p, cos, sin, *consts)


def _mod_rows_spec(mod, n):
    return pl.BlockSpec((mod.shape[0], n, mod.shape[2]), lambda i: (0, 0, 0))


def _proj_sample_call(x, mod, cos, sin, prev, consts):
    n = x.shape[0]
    args = (x, mod, cos, sin, prev) + tuple(consts)
    out_shape = (
        jax.ShapeDtypeStruct((8, n, RWKV_WIDTH), F32),
        jax.ShapeDtypeStruct((6, RWKV_WIDTH, n), F32),
        jax.ShapeDtypeStruct((n, SWA_WIDTH), BF16),
        jax.ShapeDtypeStruct((n, KV_WIDTH), F32),
        jax.ShapeDtypeStruct((n, KV_WIDTH), F32),
        jax.ShapeDtypeStruct((n, RWKV_COLS), F32),
        jax.ShapeDtypeStruct((D_MODEL, PROJ_COLS), BF16),
    )
    return pl.pallas_call(
        _proj_sample_kernel,
        grid=(1,),
        in_specs=[_mod_rows_spec(a, n) if i == 1 else _const_spec(a, 1)
                  for i, a in enumerate(args)],
        out_specs=tuple(pl.BlockSpec(s.shape, lambda i, nd=len(s.shape): (0,) * nd)
                        for s in out_shape),
        out_shape=out_shape,
        compiler_params=pltpu.CompilerParams(
            dimension_semantics=("arbitrary",), vmem_limit_bytes=VMEM_LIMIT),
        name="proj_sample",
    )(*args)


def _wkv_tile(feats, ln_w, ln_b, y_ref, s_ref):
    tt = feats[0].shape[0]
    n_chunks = tt // CHUNK
    gl = GROUP_LANES
    n_groups = RWKV_WIDTH // gl
    heads_per_group = gl // HEAD_DIM
    probs = [(c, g) for c in range(n_chunks) for g in range(n_groups)]

    row_c = lax.broadcasted_iota(jnp.int32, (CHUNK, gl), 0)
    col_c = lax.broadcasted_iota(jnp.int32, (CHUNK, gl), 1) % CHUNK
    strict = row_c > col_c
    incl = row_c >= col_c
    eye_cat = (row_c == col_c).astype(F32)
    rb = lax.broadcasted_iota(jnp.int32, (gl, gl), 0) // HEAD_DIM
    cb = lax.broadcasted_iota(jnp.int32, (gl, gl), 1) // HEAD_DIM
    bd_mask = rb == cb
    tri_r = lax.broadcasted_iota(jnp.int32, (CHUNK, CHUNK), 0)
    tri_c = lax.broadcasted_iota(jnp.int32, (CHUNK, CHUNK), 1)
    tril_ones = (tri_r >= tri_c).astype(BF16)
    seg = bd_mask.astype(BF16)

    def bd(x):
        xb = x.astype(BF16)
        return jnp.where(bd_mask, jnp.concatenate([xb] * heads_per_group, axis=0),
                         jnp.zeros((), BF16))

    def fold(x):
        xm = jnp.where(bd_mask, x, 0.0)
        acc = xm[0:HEAD_DIM]
        for hh in range(1, heads_per_group):
            acc = acc + xm[hh * HEAD_DIM:(hh + 1) * HEAD_DIM]
        return acc

    def ld(i, p):
        c, g = p
        return feats[i][c * CHUNK:(c + 1) * CHUNK, g * gl:(g + 1) * gl]

    lw = [ld(1, p) for p in probs]
    na = [ld(4, p) for p in probs]
    bb = [ld(5, p) for p in probs]
    kf = [ld(2, p) for p in probs]
    r = [ld(0, p) for p in probs]
    v = [ld(3, p) for p in probs]

    def cumsum(x):
        hi = x.astype(BF16)
        lo = (x - hi.astype(F32)).astype(BF16)
        both = jnp.dot(tril_ones, jnp.concatenate([hi, lo], axis=1), preferred_element_type=F32)
        return both[:, :gl] + both[:, gl:]

    cum = _every(cumsum, lw)
    cum_last = [x[CHUNK - 1:CHUNK, :] for x in cum]
    e_out = [jnp.exp(-x) for x in cum]
    e_end = _every(lambda cl, x: jnp.exp(cl - x), cum_last, cum)
    a_t = _every(lambda n_, x, l_: n_ * jnp.exp(x - l_), na, cum, lw)
    r_t = _every(lambda r_, x: r_ * jnp.exp(x), r, cum)
    b_t = _every(lambda b_, e: b_ * e, bb, e_out)
    k_t = _every(lambda k_, e: k_ * e, kf, e_out)
    b_end = _every(lambda b_, e: b_ * e, bb, e_end)
    k_end = _every(lambda k_, e: k_ * e, kf, e_end)
    gamma = [jnp.exp(x) for x in cum_last]

    ar = _every(lambda a_, r_: jnp.concatenate([a_, r_], axis=0), a_t, r_t)
    pb = _every(lambda x, y: _dot_nt(x, bd(y)), ar, b_t)
    pk = _every(lambda x, y: _dot_nt(x, bd(y)), ar, k_t)
    l_ab = [jnp.where(strict, x[:CHUNK], 0.0) for x in pb]
    l_ak = [jnp.where(strict, x[:CHUNK], 0.0) for x in pk]
    m_rb = [jnp.where(incl, x[CHUNK:], 0.0) for x in pb]
    m_rk = [jnp.where(incl, x[CHUNK:], 0.0) for x in pk]

    x_acc = [eye_cat + l for l in l_ab]
    pw = _every(lambda l: _dot(l, bd(l)), l_ab)
    n_sq = int(math.log2(CHUNK)) - 1
    for lvl in range(n_sq):
        rhs = [bd(p_) for p_ in pw]
        if lvl < n_sq - 1:
            both = _every(lambda x, p_, w_: _dot(jnp.concatenate([x, p_], axis=0), w_),
                          x_acc, pw, rhs)
            x_acc = _every(lambda x, b_: x + b_[:CHUNK], x_acc, both)
            pw = [b_[CHUNK:] for b_ in both]
        else:
            x_acc = _every(lambda x, w_: x + _dot(x, w_), x_acc, rhs)
    t_inv = x_acc

    kv = _every(lambda la, mk, x: _dot(jnp.concatenate([la, mk], axis=0), bd(x)), l_ak, m_rk, v)
    y_loc = [x[:CHUNK] for x in kv]
    wu = _every(lambda t_, a_, y_: _dot(t_, jnp.concatenate([bd(a_), bd(y_)], axis=1)),
                t_inv, a_t, y_loc)
    w_t = [x[:, :gl] for x in wu]
    u_loc = [x[:, gl:] for x in wu]

    mwu = _every(lambda mb, w_, u_: _dot(mb, jnp.concatenate([bd(w_), bd(u_)], axis=1)),
                 m_rb, w_t, u_loc)
    q_c = _every(lambda r_, x: r_ + x[:, :gl], r_t, mwu)
    o_loc = _every(lambda x, y_: x[:, gl:] + y_[CHUNK:], mwu, kv)
    m_low = _every(lambda w_, b_: jnp.where(bd_mask, _dot_tn(w_, b_), 0.0).astype(BF16),
                   w_t, b_end)
    n_loc = _every(lambda u_, v_, b_, k_: fold(_dot_tn(jnp.concatenate([u_, v_], axis=0),
                                                       jnp.concatenate([b_, k_], axis=0))),
                   u_loc, v, b_end, k_end)

    state = [s_ref[g] for g in range(n_groups)]
    starts = []
    for c in range(n_chunks):
        starts.append(state)
        idx = [c * n_groups + g for g in range(n_groups)]
        state = [state[g] * gamma[i] + _dot(state[g], m_low[i]) + n_loc[i]
                 for g, i in enumerate(idx)]
    for g in range(n_groups):
        s_ref[g] = state[g]

    s0 = [starts[c][g] for (c, g) in probs]
    o = _every(lambda q_, s_, ol: _dot_nt(q_, bd(s_)) + ol, q_c, s0, o_loc)

    n_p = len(probs)
    unstack = lambda x: [x[i * CHUNK:(i + 1) * CHUNK] for i in range(n_p)]
    mu = unstack(_dot(jnp.concatenate(o, axis=0), seg) * (1.0 / HEAD_DIM))
    d = _every(lambda x, m_: x - m_, o, mu)
    var = unstack(_dot(jnp.concatenate([x * x for x in d], axis=0), seg) * (1.0 / HEAD_DIM))
    for i, (c, g) in enumerate(probs):
        ls = slice(g * gl, (g + 1) * gl)
        gn = d[i] * lax.rsqrt(var[i] + RWKV_GN_EPS) * ln_w[:, ls] + ln_b[:, ls]
        y = (gn + ld(7, (c, g))) * ld(6, (c, g))
        y_ref[c * CHUNK:(c + 1) * CHUNK, ls] = y.astype(y_ref.dtype)


def _wkv_sample_kernel(ft_ref, s_ref, s_out_ref, o_ref):
    hd = HEAD_DIM
    r, kf, na, bb = ft_ref[0], ft_ref[2], ft_ref[4], ft_ref[5]
    w = jnp.exp(ft_ref[1])

    def value_block(vb, carry):
        v0 = pl.multiple_of(vb * SUBLANES, SUBLANES)
        v_rows = ft_ref[3, pl.ds(v0, SUBLANES), :]
        outs = []
        for j in range(SUBLANES):
            s = s_ref[0, v0 + j]
            sa = jnp.sum(s * na, axis=0, keepdims=True)
            s_new = s * w + sa * bb + v_rows[j:j + 1] * kf
            s_out_ref[0, v0 + j] = s_new
            outs.append(jnp.sum(s_new * r, axis=0, keepdims=True))
        o_ref[pl.ds(v0, SUBLANES), :] = jnp.concatenate(outs, axis=0)
        return carry

    lax.fori_loop(0, hd // SUBLANES, value_block, 0)


def _wkv_sample_call(ft_s, state_t):
    h, hd, _, n = state_t.shape
    return pl.pallas_call(
        _wkv_sample_kernel,
        grid=(h,),
        in_specs=[
            pl.BlockSpec((6, hd, n), lambda i: (0, i, 0)),
            pl.BlockSpec((1, hd, hd, n), lambda i: (i, 0, 0, 0)),
        ],
        out_specs=(
            pl.BlockSpec((1, hd, hd, n), lambda i: (i, 0, 0, 0)),
            pl.BlockSpec((hd, n), lambda i: (i, 0)),
        ),
        out_shape=(
            jax.ShapeDtypeStruct(state_t.shape, F32),
            jax.ShapeDtypeStruct((h * hd, n), F32),
        ),
        compiler_params=pltpu.CompilerParams(
            dimension_semantics=("arbitrary",), vmem_limit_bytes=VMEM_LIMIT),
        name="wkv_sample",
    )(ft_s, state_t)


def _swa_prompt_kernel(sink_ref, q_ref, kp_ref, kc_ref, vp_ref, vc_ref, o_ref):
    j = pl.program_id(1)
    w = WINDOW
    tq = q_ref.shape[1]
    n_blk = tq // w
    gl = GROUP_LANES
    rows = SWA_GROUP * w
    probs = [(qi, g) for qi in range(n_blk) for g in range(SWA_KV_HEADS)]

    lane_head = lax.broadcasted_iota(jnp.int32, (w, gl), 1) // HEAD_DIM
    head_mask = [lane_head == h for h in range(SWA_GROUP)]
    qi_ = lax.broadcasted_iota(jnp.int32, (rows, 2 * w), 0) % w
    ki_ = lax.broadcasted_iota(jnp.int32, (rows, 2 * w), 1)
    diff = qi_ - (ki_ - w)
    in_window = (diff >= 0) & (diff < WINDOW)
    first_valid = in_window & ((j * tq + ki_ - w) >= 0)

    def keys(ref_prev, ref_cur, qi, g):
        ls = slice(g * gl, (g + 1) * gl)
        prev = ref_prev[0, :, ls] if qi == 0 else ref_cur[0, (qi - 1) * w:qi * w, ls]
        return jnp.concatenate([prev, ref_cur[0, qi * w:(qi + 1) * w, ls]], axis=0)

    def lhs(qi, g):
        qg = q_ref[0, qi * w:(qi + 1) * w, g * gl:(g + 1) * gl]
        return jnp.concatenate([jnp.where(m, qg, jnp.zeros((), BF16)) for m in head_mask], axis=0)

    sinks = []
    for g in range(SWA_KV_HEADS):
        sinks.append(jnp.concatenate(
            [jnp.full((w, 1), sink_ref[g * SWA_GROUP + h], F32) for h in range(SWA_GROUP)], axis=0))

    sink_lane = ki_ == qi_
    fill = [jnp.where(sink_lane, sk, MASK_VALUE) for sk in sinks]
    ones_k = jnp.ones((2 * w, LANES), BF16)

    s = [_dot_nt(lhs(qi, g), keys(kp_ref, kc_ref, qi, g)) for qi, g in probs]
    s = [jnp.where(first_valid if qi == 0 else in_window, x, fill[g])
         for x, (qi, g) in zip(s, probs)]
    m = [jnp.max(x, axis=-1, keepdims=True) for x in s]
    e = _every(lambda x, m_: jnp.exp(x - m_), s, m)
    inv = [1.0 / _dot(x, ones_k) for x in e]
    prob = _every(lambda e_, i_: jnp.where(sink_lane, jnp.zeros((), BF16),
                                           (e_ * jnp.concatenate([i_, i_], axis=1)).astype(BF16)),
                  e, inv)
    lane_head2 = lax.broadcasted_iota(jnp.int32, (2 * w, gl), 1) // HEAD_DIM
    for p_, (qi, g) in zip(prob, probs):
        vk = keys(vp_ref, vc_ref, qi, g)
        rhs = jnp.concatenate([jnp.where(lane_head2 == h, vk, jnp.zeros((), BF16))
                               for h in range(SWA_GROUP)], axis=0)
        p_cat = jnp.concatenate([p_[h * w:(h + 1) * w] for h in range(SWA_GROUP)], axis=1)
        y = _dot(p_cat, rhs)
        o_ref[0, qi * w:(qi + 1) * w, g * gl:(g + 1) * gl] = y.astype(o_ref.dtype)


def _swa_prompt_call(sinks, q, k, v, tq):
    b, t, _ = q.shape
    w = WINDOW
    per = tq // w
    prev = lambda i, j: (i, jnp.maximum(j * per - 1, 0), 0)
    cur = lambda i, j: (i, j, 0)
    return pl.pallas_call(
        _swa_prompt_kernel,
        grid=(b, t // tq),
        in_specs=[
            pl.BlockSpec(memory_space=pltpu.SMEM),
            pl.BlockSpec((1, tq, SWA_WIDTH), cur),
            pl.BlockSpec((1, w, SWA_WIDTH), prev),
            pl.BlockSpec((1, tq, SWA_WIDTH), cur),
            pl.BlockSpec((1, w, SWA_WIDTH), prev),
            pl.BlockSpec((1, tq, SWA_WIDTH), cur),
        ],
        out_specs=pl.BlockSpec((1, tq, SWA_WIDTH), cur),
        out_shape=jax.ShapeDtypeStruct((b, t, SWA_WIDTH), BF16),
        compiler_params=pltpu.CompilerParams(
            dimension_semantics=("arbitrary", "arbitrary"), vmem_limit_bytes=VMEM_LIMIT),
        name="swa_prompt",
    )(sinks, q, k, k, v, v)


def _swa_sample_kernel(sink_ref, q_ref, knr_ref, vnr_ref, knt_ref, vnt_ref, ck_ref, cv_ref,
                       o_ref, ko_ref, vo_ref):
    nb, _, hd, l = ck_ref.shape
    q = q_ref[...]
    ck = ck_ref[...].reshape(nb * SWA_KV_HEADS, hd, l)
    cv = cv_ref[...].reshape(nb * SWA_KV_HEADS, hd, l)
    knr = knr_ref[...].astype(BF16).astype(F32)
    vnr = vnr_ref[...].astype(BF16).astype(F32)
    ki = lax.broadcasted_iota(jnp.int32, (1, 1, l), 2)
    kpos = PAST_LEN - l + ki
    diff = PAST_LEN - kpos
    valid = (diff >= 0) & (diff < WINDOW) & (kpos >= 0)
    s_c = lax.dot_general(q, ck.astype(BF16), (((2,), (1,)), ((0,), (0,))),
                          preferred_element_type=F32)
    s_c = jnp.where(valid, s_c, MASK_VALUE)
    s_n = jnp.sum(q.astype(F32) * knr, axis=-1, keepdims=True)
    sink = sink_ref[:, :, 0:1]
    m = jnp.maximum(jnp.maximum(jnp.max(s_c, axis=-1, keepdims=True), s_n), sink)
    e_c = jnp.exp(s_c - m)
    e_n = jnp.exp(s_n - m)
    denom = jnp.sum(e_c, axis=-1, keepdims=True) + e_n + jnp.exp(sink - m)
    p_c = (e_c / denom).astype(BF16)
    p_n = (e_n / denom).astype(BF16).astype(F32)
    o = lax.dot_general(p_c, cv.astype(BF16), (((2,), (2,)), ((0,), (0,))),
                        preferred_element_type=F32)
    o_ref[...] = o + p_n * vnr
    lane = lax.broadcasted_iota(jnp.int32, (hd, l), 1)
    for b in range(nb):
        for g in range(SWA_KV_HEADS):
            rows = slice(g * hd, (g + 1) * hd)
            ko_ref[b, g] = jnp.where(lane == l - 1, knt_ref[0, rows, b:b + 1],
                                     pltpu.roll(ck_ref[b, g], l - 1, 1))
            vo_ref[b, g] = jnp.where(lane == l - 1, vnt_ref[0, rows, b:b + 1],
                                     pltpu.roll(cv_ref[b, g], l - 1, 1))


def _swa_sample_call(sink_t, q8, knr, vnr, knt, vnt, ck, cv, nb):
    n, g, hd, l = ck.shape
    rows = nb * g
    blk3 = lambda i: (i, 0, 0)
    blk4 = lambda i: (i, 0, 0, 0)
    return pl.pallas_call(
        _swa_sample_kernel,
        grid=(n // nb,),
        in_specs=[
            pl.BlockSpec((rows, SUBLANES, LANES), lambda i: (0, 0, 0)),
            pl.BlockSpec((rows, SUBLANES, hd), blk3),
            pl.BlockSpec((rows, 1, hd), blk3),
            pl.BlockSpec((rows, 1, hd), blk3),
            pl.BlockSpec((1, g * hd, nb), blk3),
            pl.BlockSpec((1, g * hd, nb), blk3),
            pl.BlockSpec((nb, g, hd, l), blk4),
            pl.BlockSpec((nb, g, hd, l), blk4),
        ],
        out_specs=(
            pl.BlockSpec((rows, SUBLANES, hd), blk3),
            pl.BlockSpec((nb, g, hd, l), blk4),
            pl.BlockSpec((nb, g, hd, l), blk4),
        ),
        out_shape=(
            jax.ShapeDtypeStruct((n * g, SUBLANES, hd), F32),
            jax.ShapeDtypeStruct(ck.shape, F32),
            jax.ShapeDtypeStruct(cv.shape, F32),
        ),
        compiler_params=pltpu.CompilerParams(
            dimension_semantics=("arbitrary",), vmem_limit_bytes=VMEM_LIMIT),
        name="swa_sample",
    )(sink_t, q8, knr, vnr, knt, vnt, ck, cv)


def _tail_mix(x, y_rwkv, y_swa, g1, sh2, sc2, w_out, n_post_mix, n_pre_ffn):
    mix = _dot(y_rwkv, w_out[:RWKV_WIDTH]) + _dot(y_swa, w_out[RWKV_WIDTH:])
    x1 = x + g1 * _rmsnorm(mix, n_post_mix)
    h2 = (_rmsnorm(x1, n_pre_ffn) * (1.0 + sc2) + sh2).astype(BF16)
    return x1, h2


def _tail_prompt_kernel(x_ref, yr_ref, ys_ref, mod_ref, w_out, n_post_mix, n_pre_ffn,
                        n_post_ffn, w_up_a, w_up_b, w_down, cw_ref, cb_ref,
                        y_ref, cp_ref, carry_ref, act_ref):
    t = pl.program_id(1)
    tm = x_ref.shape[1]
    sub = min(tm, TAIL_SUB)
    grp = min(tm, TAIL_GROUP)

    @pl.when(t == 0)
    def _():
        carry_ref[...] = jnp.zeros_like(carry_ref)

    g1, sh2, sc2, g2 = mod_ref[2, 0], mod_ref[3, 0], mod_ref[4, 0], mod_ref[5, 0]

    def cols_of(c, half):
        return slice(half * D_FF + c * FF_CHUNK, half * D_FF + (c + 1) * FF_CHUNK)

    def conv(u, cols):
        ext = jnp.concatenate([carry_ref[:, cols], u], axis=0)
        last = u[sub - SUBLANES:]
        carry_ref[:, cols] = last
        cp_ref[0, :, cols] = last
        return (cb_ref[:, cols] + pltpu.roll(ext, 2, 0)[SUBLANES:] * cw_ref[0:1, cols]
                + pltpu.roll(ext, 1, 0)[SUBLANES:] * cw_ref[1:2, cols] + u * cw_ref[2:3, cols])

    def group(gi, carry):
        base = pl.multiple_of(gi * grp, grp)
        blocks = [(pl.ds(base + j * sub, sub), slice(j * sub, (j + 1) * sub))
                  for j in range(grp // sub)]
        mixes = [_dot(yr_ref[0, rb, :], w_out[:RWKV_WIDTH]) + _dot(ys_ref[0, rb, :], w_out[RWKV_WIDTH:])
                 for rb, _ in blocks]
        for (rb, ab), mix in zip(blocks, mixes):
            x1 = x_ref[0, rb, :] + g1 * _rmsnorm(mix, n_post_mix[...])
            h2 = (_rmsnorm(x1, n_pre_ffn[...]) * (1.0 + sc2) + sh2).astype(BF16)

            def up(c):
                chunk = slice(c * FF_CHUNK, (c + 1) * FF_CHUNK)
                return [jnp.dot(h2, w[:, chunk], preferred_element_type=F32)
                        for w in (w_up_a, w_up_b)]

            def down(c):
                rows = slice(c * FF_CHUNK, (c + 1) * FF_CHUNK)
                return jnp.dot(act_ref[ab, rows], w_down[rows, :], preferred_element_type=F32)

            u_next = up(0)
            ff = None
            for c in range(N_FF_CHUNKS):
                u_cur = u_next
                if c + 1 < N_FF_CHUNKS:
                    u_next = up(c + 1)
                if c >= 1:
                    part = down(c - 1)
                    ff = part if ff is None else ff + part
                za, zb = [conv(u_cur[half], cols_of(c, half)) for half in range(2)]
                act_ref[ab, c * FF_CHUNK:(c + 1) * FF_CHUNK] = (_silu(za) * zb).astype(BF16)
            ff = ff + down(N_FF_CHUNKS - 1)
            y_ref[0, rb, :] = x1 + g2 * _rmsnorm(ff, n_post_ffn[...])
        return carry

    lax.fori_loop(0, tm // grp, group, 0)


def _tail_sample_kernel(x_ref, o_ref, feat_ref, ys_ref, mod_ref, ln_w, ln_b, w_out, n_post_mix,
                        n_pre_ffn, n_post_ffn, wa_ref, wb_ref, wd_ref, p0a, p0b, p1a, p1b,
                        cwa, cwb, cba, cbb,
                        y_ref, ua_ref, ub_ref, w_out_b_ref, wa_b_ref, wb_b_ref, wd_b_ref,
                        x1_ref, h2_ref, acc_ref):
    c = pl.program_id(0)

    @pl.when(c == 0)
    def _():
        w_out_b = w_out[...].astype(BF16)
        w_out_b_ref[...] = w_out_b
        seg = _seg_ones(RWKV_WIDTH)
        y_rwkv = _gn_epilogue(o_ref[...].T, feat_ref[7], feat_ref[6], ln_w[...], ln_b[...], seg)
        x1, h2 = _tail_mix(x_ref[...], y_rwkv, ys_ref[...], mod_ref[2], mod_ref[3], mod_ref[4],
                           w_out_b, n_post_mix[...], n_pre_ffn[...])
        x1_ref[...] = x1
        h2_ref[...] = h2
        acc_ref[...] = jnp.zeros_like(acc_ref)

    wa = wa_ref[...].astype(BF16)
    wb = wb_ref[...].astype(BF16)
    wd = wd_ref[...].astype(BF16)
    wa_b_ref[...] = wa
    wb_b_ref[...] = wb
    wd_b_ref[...] = wd
    h2 = h2_ref[...]
    ua = jnp.dot(h2, wa, preferred_element_type=F32)
    ub = jnp.dot(h2, wb, preferred_element_type=F32)
    ua_ref[...] = ua
    ub_ref[...] = ub
    za = cba[...] + p0a[...] * cwa[0:1, :] + p1a[...] * cwa[1:2, :] + ua * cwa[2:3, :]
    zb = cbb[...] + p0b[...] * cwb[0:1, :] + p1b[...] * cwb[1:2, :] + ub * cwb[2:3, :]
    acc_ref[...] += _dot(_silu(za) * zb, wd)

    @pl.when(c == pl.num_programs(0) - 1)
    def _():
        y_ref[...] = x1_ref[...] + mod_ref[5] * _rmsnorm(acc_ref[...], n_post_ffn[...])


def _tail_prompt_call(x, y_rwkv, y_swa, mod_p, consts, tm):
    b, t, _ = x.shape
    tok = lambda i, j: (i, j, 0)
    in_specs = [
        pl.BlockSpec((1, tm, D_MODEL), tok),
        pl.BlockSpec((1, tm, RWKV_WIDTH), tok),
        pl.BlockSpec((1, tm, SWA_WIDTH), tok),
        pl.BlockSpec((6, 1, 1, D_MODEL), lambda i, j: (0, i, 0, 0)),
    ] + [_const_spec(c, 2) for c in consts]
    return pl.pallas_call(
        _tail_prompt_kernel,
        grid=(b, t // tm),
        in_specs=in_specs,
        out_specs=(
            pl.BlockSpec((1, tm, D_MODEL), tok),
            pl.BlockSpec((1, SUBLANES, 2 * D_FF), lambda i, j: (i, 0, 0)),
        ),
        out_shape=(
            jax.ShapeDtypeStruct((b, t, D_MODEL), F32),
            jax.ShapeDtypeStruct((b, SUBLANES, 2 * D_FF), F32),
        ),
        scratch_shapes=[
            pltpu.VMEM((SUBLANES, 2 * D_FF), F32),
            pltpu.VMEM((min(tm, TAIL_GROUP), D_FF), BF16),
        ],
        compiler_params=pltpu.CompilerParams(
            dimension_semantics=("arbitrary", "arbitrary"), vmem_limit_bytes=VMEM_LIMIT),
        name="tail_prompt",
    )(x, y_rwkv, y_swa, mod_p, *consts)


def _tail_sample_call(x, o_t, feat_s, y_swa, mod, p0, p1, ln_w, ln_b, w_out, n_post_mix,
                      n_pre_ffn, n_post_ffn, w_up, w_down, cw, cb):
    n = x.shape[0]
    nc = N_FF_CHUNKS
    fc = FF_CHUNK
    whole = lambda a: pl.BlockSpec(a.shape, lambda c, nd=a.ndim: (0,) * nd)
    gate = lambda rows: pl.BlockSpec((rows, fc), lambda c: (0, c))
    value = lambda rows: pl.BlockSpec((rows, fc), lambda c: (0, nc + c))
    in_specs = [
        whole(x), whole(o_t), whole(feat_s), whole(y_swa),
        pl.BlockSpec((mod.shape[0], n, D_MODEL), lambda c: (0, 0, 0)),
        whole(ln_w), whole(ln_b), whole(w_out), whole(n_post_mix), whole(n_pre_ffn),
        whole(n_post_ffn),
        gate(D_MODEL), value(D_MODEL), pl.BlockSpec((fc, D_MODEL), lambda c: (c, 0)),
        gate(n), value(n), gate(n), value(n), gate(3), value(3), gate(1), value(1),
    ]
    out_shape = (
        jax.ShapeDtypeStruct((n, D_MODEL), F32),
        jax.ShapeDtypeStruct((n, D_FF), F32),
        jax.ShapeDtypeStruct((n, D_FF), F32),
        jax.ShapeDtypeStruct((D_MODEL, D_MODEL), BF16),
        jax.ShapeDtypeStruct((D_MODEL, D_FF), BF16),
        jax.ShapeDtypeStruct((D_MODEL, D_FF), BF16),
        jax.ShapeDtypeStruct((D_FF, D_MODEL), BF16),
    )
    chunk_cols = lambda rows: pl.BlockSpec((rows, fc), lambda c: (0, c))
    out_specs = (
        pl.BlockSpec((n, D_MODEL), lambda c: (0, 0)),
        chunk_cols(n), chunk_cols(n),
        pl.BlockSpec((D_MODEL, D_MODEL), lambda c: (0, 0)),
        chunk_cols(D_MODEL), chunk_cols(D_MODEL),
        pl.BlockSpec((fc, D_MODEL), lambda c: (c, 0)),
    )
    return pl.pallas_call(
        _tail_sample_kernel,
        grid=(nc,),
        in_specs=in_specs,
        out_specs=out_specs,
        out_shape=out_shape,
        scratch_shapes=[
            pltpu.VMEM((n, D_MODEL), F32),
            pltpu.VMEM((n, D_MODEL), BF16),
            pltpu.VMEM((n, D_MODEL), F32),
        ],
        compiler_params=pltpu.CompilerParams(
            dimension_semantics=("arbitrary",), vmem_limit_bytes=VMEM_LIMIT),
        name="tail_sample",
    )(x, o_t, feat_s, y_swa, mod, ln_w, ln_b, w_out, n_post_mix, n_pre_ffn, n_post_ffn,
      w_up, w_up, w_down, p0, p0, p1, p1, cw, cw, cb, cb)


def _rope_tables(pos):
    half = HEAD_DIM // 2
    inv = ROPE_THETA ** (-jnp.arange(half, dtype=F32) / half)
    ang = pos.astype(F32)[:, None] * inv[None, :]
    cos, sin = jnp.cos(ang), jnp.sin(ang)
    cos_h = jnp.concatenate([cos, cos], axis=-1)
    sin_h = jnp.concatenate([-sin, sin], axis=-1)
    return jnp.tile(cos_h, (1, LANES // HEAD_DIM)), jnp.tile(sin_h, (1, LANES // HEAD_DIM))


def kernel(x_prompt, x_sample, state_rwkv_wkv, state_rwkv_shift, cache_swa_k, cache_swa_v,
           state_ffn_conv, c_prompt, c_sample, w_ada, b_ada, norm_pre_mix, norm_post_mix,
           norm_pre_ffn, norm_post_ffn, w_in, rwkv_mu, rwkv_w0, rwkv_w_up, rwkv_a0, rwkv_a_up,
           rwkv_g_up, rwkv_k_k, rwkv_k_a, rwkv_r_k, rwkv_ln_w, rwkv_ln_b, swa_sinks, w_out,
           ffn_w_up, ffn_conv_w, ffn_conv_b, ffn_w_down):
    depth = w_ada.shape[0]
    assert depth == 1 and x_sample.shape[1] == 1
    b, t, _ = x_prompt.shape
    n = x_sample.shape[0]
    tm = min(256, t)
    blk = min(4 * tm, t)
    assert t % blk == 0 and tm % WINDOW == 0 and tm % CHUNK == 0 and n % SUBLANES == 0
    nb_swa = 2 * SUBLANES if n % (2 * SUBLANES) == 0 else SUBLANES
    li = 0

    row = lambda v: v.reshape(1, -1)
    zeros_l = jnp.zeros((64, RWKV_WIDTH), F32)
    wa_up = jnp.concatenate([
        jnp.concatenate([rwkv_w_up[li], zeros_l], axis=1),
        jnp.concatenate([zeros_l, rwkv_a_up[li]], axis=1)], axis=0).astype(BF16)
    hid = jnp.arange(GROUP_LANES) // HEAD_DIM
    seg_blk = (hid[:, None] == hid[None, :]).astype(BF16)
    proj_consts = (row(norm_pre_mix[li]), w_in[li], row(rwkv_mu[li]), row(rwkv_w0[li]), wa_up,
                   row(rwkv_a0[li]), rwkv_g_up[li].astype(BF16), row(rwkv_k_k[li]),
                   row(rwkv_k_a[li]), row(rwkv_r_k[li]), seg_blk)
    norms = (row(norm_post_mix[li]), row(norm_pre_ffn[li]), row(norm_post_ffn[li]))
    cw, cb = ffn_conv_w[li], row(ffn_conv_b[li])
    ln_w, ln_b = row(rwkv_ln_w[li]), row(rwkv_ln_b[li])

    mod = _ada_call(jnp.concatenate([c_sample, c_prompt], axis=0), w_ada[li], row(b_ada[li]))
    mod_p = mod[:, n:].reshape(6, b, 1, D_MODEL)

    cos_s, sin_s = _rope_tables(jnp.full((1,), PAST_LEN, jnp.int32))
    feat_s, ft_s, q_s, kn_s, vn_s, p_s, w_in_b = _proj_sample_call(
        x_sample[:, 0], mod, cos_s, sin_s, state_rwkv_shift[li], proj_consts)

    state_t = jnp.transpose(state_rwkv_wkv[li], (1, 2, 3, 0))
    wkv_t, o_t = _wkv_sample_call(ft_s, state_t)
    wkv_s = jnp.transpose(wkv_t, (3, 0, 1, 2))

    g2 = SWA_KV_HEADS
    q4 = q_s.reshape(n * g2, SWA_GROUP, HEAD_DIM)
    q8 = jnp.concatenate([q4, jnp.zeros_like(q4)], axis=1)
    sink_t = jnp.broadcast_to(
        jnp.concatenate([swa_sinks[li].reshape(g2, SWA_GROUP),
                         jnp.full((g2, SWA_GROUP), MASK_VALUE, F32)], axis=1)[None, :, :, None],
        (nb_swa, g2, SUBLANES, LANES)).reshape(nb_swa * g2, SUBLANES, LANES)
    cols = lambda a: a.reshape(n // nb_swa, nb_swa, KV_WIDTH).transpose(0, 2, 1)
    ck = jnp.transpose(cache_swa_k[li], (0, 2, 3, 1))
    cv = jnp.transpose(cache_swa_v[li], (0, 2, 3, 1))
    o_att, k_t, v_t = _swa_sample_call(
        sink_t, q8, kn_s.reshape(n * g2, 1, HEAD_DIM), vn_s.reshape(n * g2, 1, HEAD_DIM),
        cols(kn_s), cols(vn_s), ck, cv, nb_swa)
    y_swa_s = o_att[:, :SWA_GROUP].reshape(n, SWA_WIDTH)
    k_s = jnp.transpose(k_t, (0, 3, 1, 2))
    v_s = jnp.transpose(v_t, (0, 3, 1, 2))

    conv0 = state_ffn_conv[li]
    y_s, ua_s, ub_s, w_out_b, w_up_a_b, w_up_b_b, w_down_b = _tail_sample_call(
        x_sample[:, 0], o_t, feat_s, y_swa_s.astype(BF16), mod, conv0[:, 0], conv0[:, 1],
        ln_w, ln_b, w_out[li], *norms, ffn_w_up[li], ffn_w_down[li], cw, cb)
    conv_s = jnp.stack([conv0[:, 1], jnp.concatenate([ua_s, ub_s], axis=1)], axis=1)

    cos_p, sin_p = _rope_tables(jnp.arange(t, dtype=jnp.int32))
    proj_consts_p = proj_consts[:1] + (w_in_b,) + proj_consts[2:]
    y_rwkv_p, s_cat, q_p, kx_p, vx_p, klast, vlast, plast = _mix_prompt_call(
        x_prompt, mod_p, cos_p, sin_p, proj_consts_p + (ln_w, ln_b), blk)
    y_swa_p = _swa_prompt_call(swa_sinks[li], q_p, kx_p, vx_p, min(2 * blk, t))
    tail_consts = (w_out_b,) + norms + (w_up_a_b, w_up_b_b, w_down_b, cw, cb)
    y_p, cp = _tail_prompt_call(x_prompt, y_rwkv_p, y_swa_p, mod_p, tail_consts, blk)

    wkv_p = s_cat.reshape(b, HEAD_DIM, RWKV_HEADS, HEAD_DIM).transpose(0, 2, 1, 3)
    shift_p = plast[:, SUBLANES - 1]
    k_p = klast.reshape(b, WINDOW, SWA_KV_HEADS, HEAD_DIM)
    v_p = vlast.reshape(b, WINDOW, SWA_KV_HEADS, HEAD_DIM)
    conv_p = cp[:, SUBLANES - 2:]

    expand = lambda a: a[None]
    return (y_p, y_s[:, None, :], expand(wkv_p), expand(shift_p), expand(k_p), expand(v_p),
            expand(conv_p), expand(wkv_s), expand(p_s),
            expand(k_s), expand(v_s), expand(conv_s))
```

```python
import math

import jax
import jax.numpy as jnp
from jax import lax
from jax.experimental import pallas as pl
from jax.experimental.pallas import tpu as pltpu

D_MODEL = 1024
HEAD_DIM = 64
RWKV_WIDTH = 512
RWKV_HEADS = 8
RWKV_COLS = 1792
RWKV_GN_EPS = 64e-5
SWA_WIDTH = 512
SWA_HEADS = 8
SWA_KV_HEADS = 2
SWA_GROUP = 4
KV_WIDTH = SWA_KV_HEADS * HEAD_DIM
WINDOW = 128
PAST_LEN = 16384
ROPE_THETA = 10000.0
ATTN_SCALE = HEAD_DIM ** -0.5
D_FF = 2816
NORM_EPS = 1e-6
MASK_VALUE = -1e30
PROJ_COLS = RWKV_COLS + SWA_WIDTH + 2 * KV_WIDTH

LANES = 128
SUBLANES = 8
CHUNK = 64
GROUP_LANES = 256
FF_CHUNK = 256
N_FF_CHUNKS = D_FF // FF_CHUNK
TAIL_SUB = 256
MIX_SUB = 512
TAIL_GROUP = 512
VMEM_LIMIT = 56 * 1024 * 1024

F32 = jnp.float32
BF16 = jnp.bfloat16


def _sigmoid(x):
    return 1.0 / (1.0 + jnp.exp(-x))


def _silu(x):
    return x * _sigmoid(x)


def _rmsnorm(x, g):
    return x * lax.rsqrt(jnp.mean(x * x, axis=-1, keepdims=True) + NORM_EPS) * g


def _dot(a, b):
    return jnp.dot(a.astype(BF16), b.astype(BF16), preferred_element_type=F32)


def _dot_nt(a, b):
    return lax.dot_general(a.astype(BF16), b.astype(BF16), (((1,), (1,)), ((), ())),
                           preferred_element_type=F32)


def _dot_tn(a, b):
    return lax.dot_general(a.astype(BF16), b.astype(BF16), (((0,), (0,)), ((), ())),
                           preferred_element_type=F32)


def _every(fn, *lists):
    return [fn(*a) for a in zip(*lists)]


def _swap_halves(x):
    w = x.shape[-1]
    lane = lax.broadcasted_iota(jnp.int32, x.shape, x.ndim - 1)
    lo = (lane & (HEAD_DIM // 2)) == 0
    return jnp.where(lo, pltpu.roll(x, w - HEAD_DIM // 2, x.ndim - 1),
                     pltpu.roll(x, HEAD_DIM // 2, x.ndim - 1))


def _rope(x, cos, sin):
    reps = x.shape[-1] // LANES
    cos_w = jnp.concatenate([cos] * reps, axis=-1) if reps > 1 else cos
    sin_w = jnp.concatenate([sin] * reps, axis=-1) if reps > 1 else sin
    return x * cos_w + _swap_halves(x) * sin_w


def _proj_features(x, shift, scale, g_pre, w_in, prev_fn, mu, w0, wa_up, a0, g_up,
                   k_k, k_a, r_k, seg_blk, cos, sin):
    h = _rmsnorm(x, g_pre) * (1.0 + scale) + shift
    p = _dot(h, w_in)
    p_rwkv = p[:, :RWKV_COLS]
    prev = prev_fn(p_rwkv)
    xm = p_rwkv + (prev - p_rwkv) * mu
    r = xm[:, 0:512]
    k = xm[:, 512:1024]
    v = xm[:, 1024:1536]
    wa = xm[:, 1536:1664]
    gd = xm[:, 1664:1792]
    lane = lax.broadcasted_iota(jnp.int32, wa.shape, 1)
    wa_act = jnp.where(lane < 64, jnp.tanh(wa), wa)
    lora = _dot(wa_act, wa_up)
    lw = -math.exp(-0.5) * _sigmoid(w0 + lora[:, :512])
    a = _sigmoid(a0 + lora[:, 512:])
    g = _dot(_sigmoid(gd), g_up)
    kk = k * k_k
    kf = k * (1.0 + (a - 1.0) * k_a)
    gl = seg_blk.shape[0]
    n = x.shape[0]
    sums = [_dot(jnp.concatenate([(kk * kk)[:, i:i + gl], (r * kf * r_k)[:, i:i + gl]], axis=0),
                 seg_blk) for i in range(0, RWKV_WIDTH, gl)]
    ss = jnp.concatenate([s_[:n] for s_ in sums], axis=1)
    kk = kk * lax.rsqrt(jnp.maximum(ss, 1e-24))
    bonus = jnp.concatenate([s_[n:] for s_ in sums], axis=1) * v
    feats = (r, lw, kf, v, -kk, kk * a, g, bonus)
    q = _rope(p[:, RWKV_COLS:RWKV_COLS + SWA_WIDTH], cos, sin)
    ks = _rope(p[:, RWKV_COLS + SWA_WIDTH:RWKV_COLS + SWA_WIDTH + KV_WIDTH], cos, sin)
    vs = p[:, RWKV_COLS + SWA_WIDTH + KV_WIDTH:]
    return feats, q, ks, vs, p_rwkv


def _gn_epilogue(o, bonus, g, ln_w, ln_b, seg):
    mu = _dot(o, seg) * (1.0 / HEAD_DIM)
    d = o - mu
    var = _dot(d * d, seg) * (1.0 / HEAD_DIM)
    gn = d * lax.rsqrt(var + RWKV_GN_EPS) * ln_w + ln_b
    return (gn + bonus) * g


def _seg_ones(n):
    r = lax.broadcasted_iota(jnp.int32, (n, n), 0) // HEAD_DIM
    c = lax.broadcasted_iota(jnp.int32, (n, n), 1) // HEAD_DIM
    return (r == c).astype(BF16)


def _ada_kernel(c_ref, w_ref, b_ref, o_ref):
    o_ref[0] = _dot(_silu(c_ref[...]), w_ref[...]) + b_ref[...]


def _ada_call(c_all, w_ada, b_ada):
    rows = c_all.shape[0]
    return pl.pallas_call(
        _ada_kernel,
        grid=(6,),
        in_specs=[
            pl.BlockSpec((rows, D_MODEL), lambda j: (0, 0)),
            pl.BlockSpec((D_MODEL, D_MODEL), lambda j: (0, j)),
            pl.BlockSpec((1, D_MODEL), lambda j: (0, j)),
        ],
        out_specs=pl.BlockSpec((1, rows, D_MODEL), lambda j: (j, 0, 0)),
        out_shape=jax.ShapeDtypeStruct((6, rows, D_MODEL), F32),
        compiler_params=pltpu.CompilerParams(
            dimension_semantics=("arbitrary",), vmem_limit_bytes=VMEM_LIMIT),
        name="ada",
    )(c_all, w_ada, b_ada)


def _expand_kv(x):
    lane = lax.broadcasted_iota(jnp.int32, x.shape, 1)
    rolled = pltpu.roll(x, HEAD_DIM, 1)
    g0 = jnp.where(lane < HEAD_DIM, x, rolled)
    g1 = jnp.where(lane < HEAD_DIM, rolled, x)
    return jnp.concatenate([g0, g0, g1, g1], axis=1)


def _mix_prompt_kernel(x_ref, mod_ref, cos_ref, sin_ref, g_pre, w_in, mu, w0, wa_up, a0,
                       g_up, k_k, k_a, r_k, seg_blk, ln_w, ln_b,
                       y_ref, s_out_ref, q_ref, k_ref, v_ref, klast_ref, vlast_ref, plast_ref,
                       carry_ref, s_ref):
    t = pl.program_id(1)

    @pl.when(t == 0)
    def _():
        carry_ref[...] = jnp.zeros_like(carry_ref)
        s_ref[...] = jnp.zeros_like(s_ref)

    sub = min(x_ref.shape[1], MIX_SUB)

    def sub_tile(i, carry):
        rows = pl.ds(pl.multiple_of(i * sub, sub), sub)
        carry_row = carry_ref[SUBLANES - 1:SUBLANES, :]

        def prev_fn(p_rwkv):
            row = lax.broadcasted_iota(jnp.int32, p_rwkv.shape, 0)
            return jnp.where(row == 0, carry_row, pltpu.roll(p_rwkv, 1, 0))

        feats, q, ks, vs, p_rwkv = _proj_features(
            x_ref[0, rows, :], mod_ref[0, 0], mod_ref[1, 0], g_pre[...], w_in[...], prev_fn,
            mu[...], w0[...], wa_up[...], a0[...], g_up[...], k_k[...], k_a[...], r_k[...],
            seg_blk[...], cos_ref[rows, :], sin_ref[rows, :])
        q_ref[0, rows, :] = (q * ATTN_SCALE).astype(BF16)
        k_ref[0, rows, :] = _expand_kv(ks).astype(BF16)
        v_ref[0, rows, :] = _expand_kv(vs).astype(BF16)
        last = p_rwkv[sub - SUBLANES:, :]
        carry_ref[...] = last
        plast_ref[0] = last
        klast_ref[0] = ks[sub - WINDOW:, :]
        vlast_ref[0] = vs[sub - WINDOW:, :]
        _wkv_tile(feats, ln_w, ln_b, y_ref.at[0, rows, :], s_ref)
        return carry

    lax.fori_loop(0, x_ref.shape[1] // sub, sub_tile, 0)

    @pl.when(t == pl.num_programs(1) - 1)
    def _():
        for g in range(RWKV_WIDTH // GROUP_LANES):
            s_out_ref[0, :, g * GROUP_LANES:(g + 1) * GROUP_LANES] = s_ref[g]


def _proj_sample_kernel(x_ref, mod_ref, cos_ref, sin_ref, prev_ref, g_pre, w_in, mu, w0,
                        wa_up, a0, g_up, k_k, k_a, r_k, seg_blk,
                        feat_ref, ft_ref, q_ref, k_ref, v_ref, p_ref, w_in_b_ref):
    w_in_b = w_in[...].astype(BF16)
    w_in_b_ref[...] = w_in_b
    feats, q, ks, vs, p_rwkv = _proj_features(
        x_ref[...], mod_ref[0], mod_ref[1], g_pre[...], w_in_b, lambda p: prev_ref[...],
        mu[...], w0[...], wa_up[...], a0[...], g_up[...], k_k[...], k_a[...], r_k[...],
        seg_blk[...], cos_ref[...], sin_ref[...])
    for i, f in enumerate(feats):
        feat_ref[i] = f
    for i in range(6):
        ft_ref[i] = feats[i].T
    q_ref[...] = (q * ATTN_SCALE).astype(BF16)
    k_ref[...] = ks
    v_ref[...] = vs
    p_ref[...] = p_rwkv


def _const_spec(arr, grid_rank):
    zeros = (0,) * arr.ndim
    if grid_rank == 1:
        return pl.BlockSpec(arr.shape, lambda i: zeros)
    return pl.BlockSpec(arr.shape, lambda i, j: zeros)


def _mix_prompt_call(x, mod_p, cos, sin, consts, tm):
    b, t, _ = x.shape
    nt = t // tm
    in_specs = [
        pl.BlockSpec((1, tm, D_MODEL), lambda i, j: (i, j, 0)),
        pl.BlockSpec((6, 1, 1, D_MODEL), lambda i, j: (0, i, 0, 0)),
        pl.BlockSpec((tm, LANES), lambda i, j: (j, 0)),
        pl.BlockSpec((tm, LANES), lambda i, j: (j, 0)),
    ] + [_const_spec(c, 2) for c in consts]
    out_shape = (
        jax.ShapeDtypeStruct((b, t, RWKV_WIDTH), BF16),
        jax.ShapeDtypeStruct((b, HEAD_DIM, RWKV_WIDTH), F32),
        jax.ShapeDtypeStruct((b, t, SWA_WIDTH), BF16),
        jax.ShapeDtypeStruct((b, t, SWA_WIDTH), BF16),
        jax.ShapeDtypeStruct((b, t, SWA_WIDTH), BF16),
        jax.ShapeDtypeStruct((b, WINDOW, KV_WIDTH), F32),
        jax.ShapeDtypeStruct((b, WINDOW, KV_WIDTH), F32),
        jax.ShapeDtypeStruct((b, SUBLANES, RWKV_COLS), F32),
    )
    out_specs = (
        pl.BlockSpec((1, tm, RWKV_WIDTH), lambda i, j: (i, j, 0)),
        pl.BlockSpec((1, HEAD_DIM, RWKV_WIDTH), lambda i, j: (i, 0, 0)),
        pl.BlockSpec((1, tm, SWA_WIDTH), lambda i, j: (i, j, 0)),
        pl.BlockSpec((1, tm, SWA_WIDTH), lambda i, j: (i, j, 0)),
        pl.BlockSpec((1, tm, SWA_WIDTH), lambda i, j: (i, j, 0)),
        pl.BlockSpec((1, WINDOW, KV_WIDTH), lambda i, j: (i, 0, 0)),
        pl.BlockSpec((1, WINDOW, KV_WIDTH), lambda i, j: (i, 0, 0)),
        pl.BlockSpec((1, SUBLANES, RWKV_COLS), lambda i, j: (i, 0, 0)),
    )
    return pl.pallas_call(
        _mix_prompt_kernel,
        grid=(b, nt),
        in_specs=in_specs,
        out_specs=out_specs,
        out_shape=out_shape,
        scratch_shapes=[
            pltpu.VMEM((SUBLANES, RWKV_COLS), F32),
            pltpu.VMEM((RWKV_WIDTH // GROUP_LANES, HEAD_DIM, GROUP_LANES), F32),
        ],
        compiler_params=pltpu.CompilerParams(
            dimension_semantics=("arbitrary", "arbitrary"), vmem_limit_bytes=VMEM_LIMIT),
        name="mix_prompt",
    )(x, mod_p, cos, sin, *consts)


def _mod_rows_spec(mod, n):
    return pl.BlockSpec((mod.shape[0], n, mod.shape[2]), lambda i: (0, 0, 0))


def _proj_sample_call(x, mod, cos, sin, prev, consts):
    n = x.shape[0]
    args = (x, mod, cos, sin, prev) + tuple(consts)
    out_shape = (
        jax.ShapeDtypeStruct((8, n, RWKV_WIDTH), F32),
        jax.ShapeDtypeStruct((6, RWKV_WIDTH, n), F32),
        jax.ShapeDtypeStruct((n, SWA_WIDTH), BF16),
        jax.ShapeDtypeStruct((n, KV_WIDTH), F32),
        jax.ShapeDtypeStruct((n, KV_WIDTH), F32),
        jax.ShapeDtypeStruct((n, RWKV_COLS), F32),
        jax.ShapeDtypeStruct((D_MODEL, PROJ_COLS), BF16),
    )
    return pl.pallas_call(
        _proj_sample_kernel,
        grid=(1,),
        in_specs=[_mod_rows_spec(a, n) if i == 1 else _const_spec(a, 1)
                  for i, a in enumerate(args)],
        out_specs=tuple(pl.BlockSpec(s.shape, lambda i, nd=len(s.shape): (0,) * nd)
                        for s in out_shape),
        out_shape=out_shape,
        compiler_params=pltpu.CompilerParams(
            dimension_semantics=("arbitrary",), vmem_limit_bytes=VMEM_LIMIT),
        name="proj_sample",
    )(*args)


def _wkv_tile(feats, ln_w, ln_b, y_ref, s_ref):
    tt = feats[0].shape[0]
    n_chunks = tt // CHUNK
    gl = GROUP_LANES
    n_groups = RWKV_WIDTH // gl
    heads_per_group = gl // HEAD_DIM
    probs = [(c, g) for c in range(n_chunks) for g in range(n_groups)]

    row_c = lax.broadcasted_iota(jnp.int32, (CHUNK, gl), 0)
    col_c = lax.broadcasted_iota(jnp.int32, (CHUNK, gl), 1) % CHUNK
    strict = row_c > col_c
    incl = row_c >= col_c
    eye_cat = (row_c == col_c).astype(F32)
    rb = lax.broadcasted_iota(jnp.int32, (gl, gl), 0) // HEAD_DIM
    cb = lax.broadcasted_iota(jnp.int32, (gl, gl), 1) // HEAD_DIM
    bd_mask = rb == cb
    tri_r = lax.broadcasted_iota(jnp.int32, (CHUNK, CHUNK), 0)
    tri_c = lax.broadcasted_iota(jnp.int32, (CHUNK, CHUNK), 1)
    tril_ones = (tri_r >= tri_c).astype(BF16)
    seg = bd_mask.astype(BF16)

    def bd(x):
        xb = x.astype(BF16)
        return jnp.where(bd_mask, jnp.concatenate([xb] * heads_per_group, axis=0),
                         jnp.zeros((), BF16))

    def fold(x):
        xm = jnp.where(bd_mask, x, 0.0)
        acc = xm[0:HEAD_DIM]
        for hh in range(1, heads_per_group):
            acc = acc + xm[hh * HEAD_DIM:(hh + 1) * HEAD_DIM]
        return acc

    def ld(i, p):
        c, g = p
        return feats[i][c * CHUNK:(c + 1) * CHUNK, g * gl:(g + 1) * gl]

    lw = [ld(1, p) for p in probs]
    na = [ld(4, p) for p in probs]
    bb = [ld(5, p) for p in probs]
    kf = [ld(2, p) for p in probs]
    r = [ld(0, p) for p in probs]
    v = [ld(3, p) for p in probs]

    def cumsum(x):
        hi = x.astype(BF16)
        lo = (x - hi.astype(F32)).astype(BF16)
        both = jnp.dot(tril_ones, jnp.concatenate([hi, lo], axis=1), preferred_element_type=F32)
        return both[:, :gl] + both[:, gl:]

    cum = _every(cumsum, lw)
    cum_last = [x[CHUNK - 1:CHUNK, :] for x in cum]
    e_out = [jnp.exp(-x) for x in cum]
    a_t = _every(lambda n_, x, l_: n_ * jnp.exp(x - l_), na, cum, lw)
    r_t = _every(lambda r_, x: r_ * jnp.exp(x), r, cum)
    b_t = _every(lambda b_, e: b_ * e, bb, e_out)
    k_t = _every(lambda k_, e: k_ * e, kf, e_out)
    gamma = [jnp.exp(x) for x in cum_last]
    e_end = _every(lambda e, g_: e * g_, e_out, gamma)
    b_end = _every(lambda b_, e: b_ * e, bb, e_end)
    k_end = _every(lambda k_, e: k_ * e, kf, e_end)

    ar = _every(lambda a_, r_: jnp.concatenate([a_, r_], axis=0), a_t, r_t)
    pb = _every(lambda x, y: _dot_nt(x, bd(y)), ar, b_t)
    pk = _every(lambda x, y: _dot_nt(x, bd(y)), ar, k_t)
    l_ab = [jnp.where(strict, x[:CHUNK], 0.0) for x in pb]
    l_ak = [jnp.where(strict, x[:CHUNK], 0.0) for x in pk]
    m_rb = [jnp.where(incl, x[CHUNK:], 0.0) for x in pb]
    m_rk = [jnp.where(incl, x[CHUNK:], 0.0) for x in pk]

    x_acc = [eye_cat + l for l in l_ab]
    pw = _every(lambda l: _dot(l, bd(l)), l_ab)
    n_sq = int(math.log2(CHUNK)) - 1
    for lvl in range(n_sq):
        rhs = [bd(p_) for p_ in pw]
        if lvl < n_sq - 1:
            both = _every(lambda x, p_, w_: _dot(jnp.concatenate([x, p_], axis=0), w_),
                          x_acc, pw, rhs)
            x_acc = _every(lambda x, b_: x + b_[:CHUNK], x_acc, both)
            pw = [b_[CHUNK:] for b_ in both]
        else:
            x_acc = _every(lambda x, w_: x + _dot(x, w_), x_acc, rhs)
    t_inv = x_acc

    kv = _every(lambda la, mk, x: _dot(jnp.concatenate([la, mk], axis=0), bd(x)), l_ak, m_rk, v)
    y_loc = [x[:CHUNK] for x in kv]
    wu = _every(lambda t_, a_, y_: _dot(t_, jnp.concatenate([bd(a_), bd(y_)], axis=1)),
                t_inv, a_t, y_loc)
    w_t = [x[:, :gl] for x in wu]
    u_loc = [x[:, gl:] for x in wu]

    mwu = _every(lambda mb, w_, u_: _dot(mb, jnp.concatenate([bd(w_), bd(u_)], axis=1)),
                 m_rb, w_t, u_loc)
    q_c = _every(lambda r_, x: r_ + x[:, :gl], r_t, mwu)
    o_loc = _every(lambda x, y_: x[:, gl:] + y_[CHUNK:], mwu, kv)
    m_low = _every(lambda w_, b_: jnp.where(bd_mask, _dot_tn(w_, b_), 0.0).astype(BF16),
                   w_t, b_end)
    n_loc = _every(lambda u_, v_, b_, k_: fold(_dot_tn(jnp.concatenate([u_, v_], axis=0),
                                                       jnp.concatenate([b_, k_], axis=0))),
                   u_loc, v, b_end, k_end)

    state = [s_ref[g] for g in range(n_groups)]
    starts = []
    for c in range(n_chunks):
        starts.append(state)
        idx = [c * n_groups + g for g in range(n_groups)]
        state = [state[g] * gamma[i] + _dot(state[g], m_low[i]) + n_loc[i]
                 for g, i in enumerate(idx)]
    for g in range(n_groups):
        s_ref[g] = state[g]

    s0 = [starts[c][g] for (c, g) in probs]
    o = _every(lambda q_, s_, ol: _dot_nt(q_, bd(s_)) + ol, q_c, s0, o_loc)

    n_p = len(probs)
    unstack = lambda x: [x[i * CHUNK:(i + 1) * CHUNK] for i in range(n_p)]
    mu = unstack(_dot(jnp.concatenate(o, axis=0), seg) * (1.0 / HEAD_DIM))
    d = _every(lambda x, m_: x - m_, o, mu)
    var = unstack(_dot(jnp.concatenate([x * x for x in d], axis=0), seg) * (1.0 / HEAD_DIM))
    for i, (c, g) in enumerate(probs):
        ls = slice(g * gl, (g + 1) * gl)
        gn = d[i] * lax.rsqrt(var[i] + RWKV_GN_EPS) * ln_w[:, ls] + ln_b[:, ls]
        y = (gn + ld(7, (c, g))) * ld(6, (c, g))
        y_ref[c * CHUNK:(c + 1) * CHUNK, ls] = y.astype(y_ref.dtype)


def _wkv_sample_kernel(ft_ref, s_ref, s_out_ref, o_ref):
    hd = HEAD_DIM
    r, kf, na, bb = ft_ref[0], ft_ref[2], ft_ref[4], ft_ref[5]
    w = jnp.exp(ft_ref[1])

    def value_block(vb, carry):
        v0 = pl.multiple_of(vb * SUBLANES, SUBLANES)
        v_rows = ft_ref[3, pl.ds(v0, SUBLANES), :]
        outs = []
        for j in range(SUBLANES):
            s = s_ref[0, v0 + j]
            sa = jnp.sum(s * na, axis=0, keepdims=True)
            s_new = s * w + sa * bb + v_rows[j:j + 1] * kf
            s_out_ref[0, v0 + j] = s_new
            outs.append(jnp.sum(s_new * r, axis=0, keepdims=True))
        o_ref[pl.ds(v0, SUBLANES), :] = jnp.concatenate(outs, axis=0)
        return carry

    lax.fori_loop(0, hd // SUBLANES, value_block, 0)


def _wkv_sample_call(ft_s, state_t):
    h, hd, _, n = state_t.shape
    return pl.pallas_call(
        _wkv_sample_kernel,
        grid=(h,),
        in_specs=[
            pl.BlockSpec((6, hd, n), lambda i: (0, i, 0)),
            pl.BlockSpec((1, hd, hd, n), lambda i: (i, 0, 0, 0)),
        ],
        out_specs=(
            pl.BlockSpec((1, hd, hd, n), lambda i: (i, 0, 0, 0)),
            pl.BlockSpec((hd, n), lambda i: (i, 0)),
        ),
        out_shape=(
            jax.ShapeDtypeStruct(state_t.shape, F32),
            jax.ShapeDtypeStruct((h * hd, n), F32),
        ),
        compiler_params=pltpu.CompilerParams(
            dimension_semantics=("arbitrary",), vmem_limit_bytes=VMEM_LIMIT),
        name="wkv_sample",
    )(ft_s, state_t)


def _swa_prompt_kernel(sink_ref, q_ref, kp_ref, kc_ref, vp_ref, vc_ref, o_ref):
    j = pl.program_id(1)
    w = WINDOW
    tq = q_ref.shape[1]
    n_blk = tq // w
    gl = GROUP_LANES
    rows = SWA_GROUP * w
    probs = [(qi, g) for qi in range(n_blk) for g in range(SWA_KV_HEADS)]

    lane_head = lax.broadcasted_iota(jnp.int32, (w, gl), 1) // HEAD_DIM
    head_mask = [lane_head == h for h in range(SWA_GROUP)]
    qi_ = lax.broadcasted_iota(jnp.int32, (rows, 2 * w), 0) % w
    ki_ = lax.broadcasted_iota(jnp.int32, (rows, 2 * w), 1)
    diff = qi_ - (ki_ - w)
    in_window = (diff >= 0) & (diff < WINDOW)
    first_valid = in_window & ((j * tq + ki_ - w) >= 0)

    def keys(ref_prev, ref_cur, qi, g):
        ls = slice(g * gl, (g + 1) * gl)
        prev = ref_prev[0, :, ls] if qi == 0 else ref_cur[0, (qi - 1) * w:qi * w, ls]
        return jnp.concatenate([prev, ref_cur[0, qi * w:(qi + 1) * w, ls]], axis=0)

    def lhs(qi, g):
        qg = q_ref[0, qi * w:(qi + 1) * w, g * gl:(g + 1) * gl]
        return jnp.concatenate([jnp.where(m, qg, jnp.zeros((), BF16)) for m in head_mask], axis=0)

    sinks = []
    for g in range(SWA_KV_HEADS):
        sinks.append(jnp.concatenate(
            [jnp.full((w, 1), sink_ref[g * SWA_GROUP + h], F32) for h in range(SWA_GROUP)], axis=0))

    sink_lane = ki_ == qi_
    fill = [jnp.where(sink_lane, sk, MASK_VALUE) for sk in sinks]
    ones_k = jnp.ones((2 * w, LANES), BF16)

    s = [_dot_nt(lhs(qi, g), keys(kp_ref, kc_ref, qi, g)) for qi, g in probs]
    s = [jnp.where(first_valid if qi == 0 else in_window, x, fill[g])
         for x, (qi, g) in zip(s, probs)]
    m = [jnp.max(x, axis=-1, keepdims=True) for x in s]
    e = _every(lambda x, m_: jnp.exp(x - m_), s, m)
    inv = [1.0 / _dot(x, ones_k) for x in e]
    prob = _every(lambda e_, i_: jnp.where(sink_lane, jnp.zeros((), BF16),
                                           (e_ * jnp.concatenate([i_, i_], axis=1)).astype(BF16)),
                  e, inv)
    lane_head2 = lax.broadcasted_iota(jnp.int32, (2 * w, gl), 1) // HEAD_DIM
    for p_, (qi, g) in zip(prob, probs):
        vk = keys(vp_ref, vc_ref, qi, g)
        rhs = jnp.concatenate([jnp.where(lane_head2 == h, vk, jnp.zeros((), BF16))
                               for h in range(SWA_GROUP)], axis=0)
        p_cat = jnp.concatenate([p_[h * w:(h + 1) * w] for h in range(SWA_GROUP)], axis=1)
        y = _dot(p_cat, rhs)
        o_ref[0, qi * w:(qi + 1) * w, g * gl:(g + 1) * gl] = y.astype(o_ref.dtype)


def _swa_prompt_call(sinks, q, k, v, tq):
    b, t, _ = q.shape
    w = WINDOW
    per = tq // w
    prev = lambda i, j: (i, jnp.maximum(j * per - 1, 0), 0)
    cur = lambda i, j: (i, j, 0)
    return pl.pallas_call(
        _swa_prompt_kernel,
        grid=(b, t // tq),
        in_specs=[
            pl.BlockSpec(memory_space=pltpu.SMEM),
            pl.BlockSpec((1, tq, SWA_WIDTH), cur),
            pl.BlockSpec((1, w, SWA_WIDTH), prev),
            pl.BlockSpec((1, tq, SWA_WIDTH), cur),
            pl.BlockSpec((1, w, SWA_WIDTH), prev),
            pl.BlockSpec((1, tq, SWA_WIDTH), cur),
        ],
        out_specs=pl.BlockSpec((1, tq, SWA_WIDTH), cur),
        out_shape=jax.ShapeDtypeStruct((b, t, SWA_WIDTH), BF16),
        compiler_params=pltpu.CompilerParams(
            dimension_semantics=("arbitrary", "arbitrary"), vmem_limit_bytes=VMEM_LIMIT),
        name="swa_prompt",
    )(sinks, q, k, k, v, v)


def _swa_sample_kernel(sink_ref, q_ref, knr_ref, vnr_ref, knt_ref, vnt_ref, ck_ref, cv_ref,
                       o_ref, ko_ref, vo_ref):
    nb, _, hd, l = ck_ref.shape
    q = q_ref[...]
    ck = ck_ref[...].reshape(nb * SWA_KV_HEADS, hd, l)
    cv = cv_ref[...].reshape(nb * SWA_KV_HEADS, hd, l)
    knr = knr_ref[...].astype(BF16).astype(F32)
    vnr = vnr_ref[...].astype(BF16).astype(F32)
    ki = lax.broadcasted_iota(jnp.int32, (1, 1, l), 2)
    kpos = PAST_LEN - l + ki
    diff = PAST_LEN - kpos
    valid = (diff >= 0) & (diff < WINDOW) & (kpos >= 0)
    s_c = lax.dot_general(q, ck.astype(BF16), (((2,), (1,)), ((0,), (0,))),
                          preferred_element_type=F32)
    s_c = jnp.where(valid, s_c, MASK_VALUE)
    s_n = jnp.sum(q.astype(F32) * knr, axis=-1, keepdims=True)
    sink = sink_ref[:, :, 0:1]
    m = jnp.maximum(jnp.maximum(jnp.max(s_c, axis=-1, keepdims=True), s_n), sink)
    e_c = jnp.exp(s_c - m)
    e_n = jnp.exp(s_n - m)
    denom = jnp.sum(e_c, axis=-1, keepdims=True) + e_n + jnp.exp(sink - m)
    p_c = (e_c / denom).astype(BF16)
    p_n = (e_n / denom).astype(BF16).astype(F32)
    o = lax.dot_general(p_c, cv.astype(BF16), (((2,), (2,)), ((0,), (0,))),
                        preferred_element_type=F32)
    o_ref[...] = o + p_n * vnr
    lane = lax.broadcasted_iota(jnp.int32, (hd, l), 1)
    for b in range(nb):
        for g in range(SWA_KV_HEADS):
            rows = slice(g * hd, (g + 1) * hd)
            ko_ref[b, g] = jnp.where(lane == l - 1, knt_ref[0, rows, b:b + 1],
                                     pltpu.roll(ck_ref[b, g], l - 1, 1))
            vo_ref[b, g] = jnp.where(lane == l - 1, vnt_ref[0, rows, b:b + 1],
                                     pltpu.roll(cv_ref[b, g], l - 1, 1))


def _swa_sample_call(sink_t, q8, knr, vnr, knt, vnt, ck, cv, nb):
    n, g, hd, l = ck.shape
    rows = nb * g
    blk3 = lambda i: (i, 0, 0)
    blk4 = lambda i: (i, 0, 0, 0)
    return pl.pallas_call(
        _swa_sample_kernel,
        grid=(n // nb,),
        in_specs=[
            pl.BlockSpec((rows, SUBLANES, LANES), lambda i: (0, 0, 0)),
            pl.BlockSpec((rows, SUBLANES, hd), blk3),
            pl.BlockSpec((rows, 1, hd), blk3),
            pl.BlockSpec((rows, 1, hd), blk3),
            pl.BlockSpec((1, g * hd, nb), blk3),
            pl.BlockSpec((1, g * hd, nb), blk3),
            pl.BlockSpec((nb, g, hd, l), blk4),
            pl.BlockSpec((nb, g, hd, l), blk4),
        ],
        out_specs=(
            pl.BlockSpec((rows, SUBLANES, hd), blk3),
            pl.BlockSpec((nb, g, hd, l), blk4),
            pl.BlockSpec((nb, g, hd, l), blk4),
        ),
        out_shape=(
            jax.ShapeDtypeStruct((n * g, SUBLANES, hd), F32),
            jax.ShapeDtypeStruct(ck.shape, F32),
            jax.ShapeDtypeStruct(cv.shape, F32),
        ),
        compiler_params=pltpu.CompilerParams(
            dimension_semantics=("arbitrary",), vmem_limit_bytes=VMEM_LIMIT),
        name="swa_sample",
    )(sink_t, q8, knr, vnr, knt, vnt, ck, cv)


def _tail_mix(x, y_rwkv, y_swa, g1, sh2, sc2, w_out, n_post_mix, n_pre_ffn):
    mix = _dot(y_rwkv, w_out[:RWKV_WIDTH]) + _dot(y_swa, w_out[RWKV_WIDTH:])
    x1 = x + g1 * _rmsnorm(mix, n_post_mix)
    h2 = (_rmsnorm(x1, n_pre_ffn) * (1.0 + sc2) + sh2).astype(BF16)
    return x1, h2


def _tail_prompt_kernel(x_ref, yr_ref, ys_ref, mod_ref, w_out, n_post_mix, n_pre_ffn,
                        n_post_ffn, w_up_a, w_up_b, w_down, cw_ref, cb_ref,
                        y_ref, cp_ref, carry_ref, act_ref):
    t = pl.program_id(1)
    tm = x_ref.shape[1]
    sub = min(tm, TAIL_SUB)
    grp = min(tm, TAIL_GROUP)

    @pl.when(t == 0)
    def _():
        carry_ref[...] = jnp.zeros_like(carry_ref)

    g1, sh2, sc2, g2 = mod_ref[2, 0], mod_ref[3, 0], mod_ref[4, 0], mod_ref[5, 0]

    def cols_of(c, half):
        return slice(half * D_FF + c * FF_CHUNK, half * D_FF + (c + 1) * FF_CHUNK)

    def conv(u, cols):
        ext = jnp.concatenate([carry_ref[:, cols], u], axis=0)
        last = u[sub - SUBLANES:]
        carry_ref[:, cols] = last
        cp_ref[0, :, cols] = last
        return (cb_ref[:, cols] + pltpu.roll(ext, 2, 0)[SUBLANES:] * cw_ref[0:1, cols]
                + pltpu.roll(ext, 1, 0)[SUBLANES:] * cw_ref[1:2, cols] + u * cw_ref[2:3, cols])

    def group(gi, carry):
        base = pl.multiple_of(gi * grp, grp)
        blocks = [(pl.ds(base + j * sub, sub), slice(j * sub, (j + 1) * sub))
                  for j in range(grp // sub)]
        mixes = [_dot(yr_ref[0, rb, :], w_out[:RWKV_WIDTH]) + _dot(ys_ref[0, rb, :], w_out[RWKV_WIDTH:])
                 for rb, _ in blocks]
        for (rb, ab), mix in zip(blocks, mixes):
            x1 = x_ref[0, rb, :] + g1 * _rmsnorm(mix, n_post_mix[...])
            h2 = (_rmsnorm(x1, n_pre_ffn[...]) * (1.0 + sc2) + sh2).astype(BF16)

            def up(c):
                chunk = slice(c * FF_CHUNK, (c + 1) * FF_CHUNK)
                return [jnp.dot(h2, w[:, chunk], preferred_element_type=F32)
                        for w in (w_up_a, w_up_b)]

            def down(c):
                rows = slice(c * FF_CHUNK, (c + 1) * FF_CHUNK)
                return jnp.dot(act_ref[ab, rows], w_down[rows, :], preferred_element_type=F32)

            u_next = up(0)
            ff = None
            for c in range(N_FF_CHUNKS):
                u_cur = u_next
                if c + 1 < N_FF_CHUNKS:
                    u_next = up(c + 1)
                if c >= 1:
                    part = down(c - 1)
                    ff = part if ff is None else ff + part
                za, zb = [conv(u_cur[half], cols_of(c, half)) for half in range(2)]
                act_ref[ab, c * FF_CHUNK:(c + 1) * FF_CHUNK] = (_silu(za) * zb).astype(BF16)
            ff = ff + down(N_FF_CHUNKS - 1)
            y_ref[0, rb, :] = x1 + g2 * _rmsnorm(ff, n_post_ffn[...])
        return carry

    lax.fori_loop(0, tm // grp, group, 0)


def _tail_sample_kernel(x_ref, o_ref, feat_ref, ys_ref, mod_ref, ln_w, ln_b, w_out, n_post_mix,
                        n_pre_ffn, n_post_ffn, wa_ref, wb_ref, wd_ref, p0a, p0b, p1a, p1b,
                        cwa, cwb, cba, cbb,
                        y_ref, ua_ref, ub_ref, w_out_b_ref, wa_b_ref, wb_b_ref, wd_b_ref,
                        x1_ref, h2_ref, acc_ref):
    c = pl.program_id(0)

    @pl.when(c == 0)
    def _():
        w_out_b = w_out[...].astype(BF16)
        w_out_b_ref[...] = w_out_b
        seg = _seg_ones(RWKV_WIDTH)
        y_rwkv = _gn_epilogue(o_ref[...].T, feat_ref[7], feat_ref[6], ln_w[...], ln_b[...], seg)
        x1, h2 = _tail_mix(x_ref[...], y_rwkv, ys_ref[...], mod_ref[2], mod_ref[3], mod_ref[4],
                           w_out_b, n_post_mix[...], n_pre_ffn[...])
        x1_ref[...] = x1
        h2_ref[...] = h2
        acc_ref[...] = jnp.zeros_like(acc_ref)

    wa = wa_ref[...].astype(BF16)
    wb = wb_ref[...].astype(BF16)
    wd = wd_ref[...].astype(BF16)
    wa_b_ref[...] = wa
    wb_b_ref[...] = wb
    wd_b_ref[...] = wd
    h2 = h2_ref[...]
    ua = jnp.dot(h2, wa, preferred_element_type=F32)
    ub = jnp.dot(h2, wb, preferred_element_type=F32)
    ua_ref[...] = ua
    ub_ref[...] = ub
    za = cba[...] + p0a[...] * cwa[0:1, :] + p1a[...] * cwa[1:2, :] + ua * cwa[2:3, :]
    zb = cbb[...] + p0b[...] * cwb[0:1, :] + p1b[...] * cwb[1:2, :] + ub * cwb[2:3, :]
    acc_ref[...] += _dot(_silu(za) * zb, wd)

    @pl.when(c == pl.num_programs(0) - 1)
    def _():
        y_ref[...] = x1_ref[...] + mod_ref[5] * _rmsnorm(acc_ref[...], n_post_ffn[...])


def _tail_prompt_call(x, y_rwkv, y_swa, mod_p, consts, tm):
    b, t, _ = x.shape
    tok = lambda i, j: (i, j, 0)
    in_specs = [
        pl.BlockSpec((1, tm, D_MODEL), tok),
        pl.BlockSpec((1, tm, RWKV_WIDTH), tok),
        pl.BlockSpec((1, tm, SWA_WIDTH), tok),
        pl.BlockSpec((6, 1, 1, D_MODEL), lambda i, j: (0, i, 0, 0)),
    ] + [_const_spec(c, 2) for c in consts]
    return pl.pallas_call(
        _tail_prompt_kernel,
        grid=(b, t // tm),
        in_specs=in_specs,
        out_specs=(
            pl.BlockSpec((1, tm, D_MODEL), tok),
            pl.BlockSpec((1, SUBLANES, 2 * D_FF), lambda i, j: (i, 0, 0)),
        ),
        out_shape=(
            jax.ShapeDtypeStruct((b, t, D_MODEL), F32),
            jax.ShapeDtypeStruct((b, SUBLANES, 2 * D_FF), F32),
        ),
        scratch_shapes=[
            pltpu.VMEM((SUBLANES, 2 * D_FF), F32),
            pltpu.VMEM((min(tm, TAIL_GROUP), D_FF), BF16),
        ],
        compiler_params=pltpu.CompilerParams(
            dimension_semantics=("arbitrary", "arbitrary"), vmem_limit_bytes=VMEM_LIMIT),
        name="tail_prompt",
    )(x, y_rwkv, y_swa, mod_p, *consts)


def _tail_sample_call(x, o_t, feat_s, y_swa, mod, p0, p1, ln_w, ln_b, w_out, n_post_mix,
                      n_pre_ffn, n_post_ffn, w_up, w_down, cw, cb):
    n = x.shape[0]
    nc = N_FF_CHUNKS
    fc = FF_CHUNK
    whole = lambda a: pl.BlockSpec(a.shape, lambda c, nd=a.ndim: (0,) * nd)
    gate = lambda rows: pl.BlockSpec((rows, fc), lambda c: (0, c))
    value = lambda rows: pl.BlockSpec((rows, fc), lambda c: (0, nc + c))
    in_specs = [
        whole(x), whole(o_t), whole(feat_s), whole(y_swa),
        pl.BlockSpec((mod.shape[0], n, D_MODEL), lambda c: (0, 0, 0)),
        whole(ln_w), whole(ln_b), whole(w_out), whole(n_post_mix), whole(n_pre_ffn),
        whole(n_post_ffn),
        gate(D_MODEL), value(D_MODEL), pl.BlockSpec((fc, D_MODEL), lambda c: (c, 0)),
        gate(n), value(n), gate(n), value(n), gate(3), value(3), gate(1), value(1),
    ]
    out_shape = (
        jax.ShapeDtypeStruct((n, D_MODEL), F32),
        jax.ShapeDtypeStruct((n, D_FF), F32),
        jax.ShapeDtypeStruct((n, D_FF), F32),
        jax.ShapeDtypeStruct((D_MODEL, D_MODEL), BF16),
        jax.ShapeDtypeStruct((D_MODEL, D_FF), BF16),
        jax.ShapeDtypeStruct((D_MODEL, D_FF), BF16),
        jax.ShapeDtypeStruct((D_FF, D_MODEL), BF16),
    )
    chunk_cols = lambda rows: pl.BlockSpec((rows, fc), lambda c: (0, c))
    out_specs = (
        pl.BlockSpec((n, D_MODEL), lambda c: (0, 0)),
        chunk_cols(n), chunk_cols(n),
        pl.BlockSpec((D_MODEL, D_MODEL), lambda c: (0, 0)),
        chunk_cols(D_MODEL), chunk_cols(D_MODEL),
        pl.BlockSpec((fc, D_MODEL), lambda c: (c, 0)),
    )
    return pl.pallas_call(
        _tail_sample_kernel,
        grid=(nc,),
        in_specs=in_specs,
        out_specs=out_specs,
        out_shape=out_shape,
        scratch_shapes=[
            pltpu.VMEM((n, D_MODEL), F32),
            pltpu.VMEM((n, D_MODEL), BF16),
            pltpu.VMEM((n, D_MODEL), F32),
        ],
        compiler_params=pltpu.CompilerParams(
            dimension_semantics=("arbitrary",), vmem_limit_bytes=VMEM_LIMIT),
        name="tail_sample",
    )(x, o_t, feat_s, y_swa, mod, ln_w, ln_b, w_out, n_post_mix, n_pre_ffn, n_post_ffn,
      w_up, w_up, w_down, p0, p0, p1, p1, cw, cw, cb, cb)


def _rope_tables(pos):
    half = HEAD_DIM // 2
    inv = ROPE_THETA ** (-jnp.arange(half, dtype=F32) / half)
    ang = pos.astype(F32)[:, None] * inv[None, :]
    cos, sin = jnp.cos(ang), jnp.sin(ang)
    cos_h = jnp.concatenate([cos, cos], axis=-1)
    sin_h = jnp.concatenate([-sin, sin], axis=-1)
    return jnp.tile(cos_h, (1, LANES // HEAD_DIM)), jnp.tile(sin_h, (1, LANES // HEAD_DIM))


def kernel(x_prompt, x_sample, state_rwkv_wkv, state_rwkv_shift, cache_swa_k, cache_swa_v,
           state_ffn_conv, c_prompt, c_sample, w_ada, b_ada, norm_pre_mix, norm_post_mix,
           norm_pre_ffn, norm_post_ffn, w_in, rwkv_mu, rwkv_w0, rwkv_w_up, rwkv_a0, rwkv_a_up,
           rwkv_g_up, rwkv_k_k, rwkv_k_a, rwkv_r_k, rwkv_ln_w, rwkv_ln_b, swa_sinks, w_out,
           ffn_w_up, ffn_conv_w, ffn_conv_b, ffn_w_down):
    depth = w_ada.shape[0]
    assert depth == 1 and x_sample.shape[1] == 1
    b, t, _ = x_prompt.shape
    n = x_sample.shape[0]
    tm = min(256, t)
    blk = min(4 * tm, t)
    assert t % blk == 0 and tm % WINDOW == 0 and tm % CHUNK == 0 and n % SUBLANES == 0
    nb_swa = 2 * SUBLANES if n % (2 * SUBLANES) == 0 else SUBLANES
    li = 0

    row = lambda v: v.reshape(1, -1)
    zeros_l = jnp.zeros((64, RWKV_WIDTH), F32)
    wa_up = jnp.concatenate([
        jnp.concatenate([rwkv_w_up[li], zeros_l], axis=1),
        jnp.concatenate([zeros_l, rwkv_a_up[li]], axis=1)], axis=0).astype(BF16)
    hid = jnp.arange(GROUP_LANES) // HEAD_DIM
    seg_blk = (hid[:, None] == hid[None, :]).astype(BF16)
    proj_consts = (row(norm_pre_mix[li]), w_in[li], row(rwkv_mu[li]), row(rwkv_w0[li]), wa_up,
                   row(rwkv_a0[li]), rwkv_g_up[li].astype(BF16), row(rwkv_k_k[li]),
                   row(rwkv_k_a[li]), row(rwkv_r_k[li]), seg_blk)
    norms = (row(norm_post_mix[li]), row(norm_pre_ffn[li]), row(norm_post_ffn[li]))
    cw, cb = ffn_conv_w[li], row(ffn_conv_b[li])
    ln_w, ln_b = row(rwkv_ln_w[li]), row(rwkv_ln_b[li])

    mod = _ada_call(jnp.concatenate([c_sample, c_prompt], axis=0), w_ada[li], row(b_ada[li]))
    mod_p = mod[:, n:].reshape(6, b, 1, D_MODEL)

    cos_s, sin_s = _rope_tables(jnp.full((1,), PAST_LEN, jnp.int32))
    feat_s, ft_s, q_s, kn_s, vn_s, p_s, w_in_b = _proj_sample_call(
        x_sample[:, 0], mod, cos_s, sin_s, state_rwkv_shift[li], proj_consts)

    state_t = jnp.transpose(state_rwkv_wkv[li], (1, 2, 3, 0))
    wkv_t, o_t = _wkv_sample_call(ft_s, state_t)
    wkv_s = jnp.transpose(wkv_t, (3, 0, 1, 2))

    g2 = SWA_KV_HEADS
    q4 = q_s.reshape(n * g2, SWA_GROUP, HEAD_DIM)
    q8 = jnp.concatenate([q4, jnp.zeros_like(q4)], axis=1)
    sink_t = jnp.broadcast_to(
        jnp.concatenate([swa_sinks[li].reshape(g2, SWA_GROUP),
                         jnp.full((g2, SWA_GROUP), MASK_VALUE, F32)], axis=1)[None, :, :, None],
        (nb_swa, g2, SUBLANES, LANES)).reshape(nb_swa * g2, SUBLANES, LANES)
    cols = lambda a: a.reshape(n // nb_swa, nb_swa, KV_WIDTH).transpose(0, 2, 1)
    ck = jnp.transpose(cache_swa_k[li], (0, 2, 3, 1))
    cv = jnp.transpose(cache_swa_v[li], (0, 2, 3, 1))
    o_att, k_t, v_t = _swa_sample_call(
        sink_t, q8, kn_s.reshape(n * g2, 1, HEAD_DIM), vn_s.reshape(n * g2, 1, HEAD_DIM),
        cols(kn_s), cols(vn_s), ck, cv, nb_swa)
    y_swa_s = o_att[:, :SWA_GROUP].reshape(n, SWA_WIDTH)
    k_s = jnp.transpose(k_t, (0, 3, 1, 2))
    v_s = jnp.transpose(v_t, (0, 3, 1, 2))

    conv0 = state_ffn_conv[li]
    y_s, ua_s, ub_s, w_out_b, w_up_a_b, w_up_b_b, w_down_b = _tail_sample_call(
        x_sample[:, 0], o_t, feat_s, y_swa_s.astype(BF16), mod, conv0[:, 0], conv0[:, 1],
        ln_w, ln_b, w_out[li], *norms, ffn_w_up[li], ffn_w_down[li], cw, cb)
    conv_s = jnp.stack([conv0[:, 1], jnp.concatenate([ua_s, ub_s], axis=1)], axis=1)

    cos_p, sin_p = _rope_tables(jnp.arange(t, dtype=jnp.int32))
    proj_consts_p = proj_consts[:1] + (w_in_b,) + proj_consts[2:]
    y_rwkv_p, s_cat, q_p, kx_p, vx_p, klast, vlast, plast = _mix_prompt_call(
        x_prompt, mod_p, cos_p, sin_p, proj_consts_p + (ln_w, ln_b), blk)
    y_swa_p = _swa_prompt_call(swa_sinks[li], q_p, kx_p, vx_p, min(2 * blk, t))
    tail_consts = (w_out_b,) + norms + (w_up_a_b, w_up_b_b, w_down_b, cw, cb)
    y_p, cp = _tail_prompt_call(x_prompt, y_rwkv_p, y_swa_p, mod_p, tail_consts, blk)

    wkv_p = s_cat.reshape(b, HEAD_DIM, RWKV_HEADS, HEAD_DIM).transpose(0, 2, 1, 3)
    shift_p = plast[:, SUBLANES - 1]
    k_p = klast.reshape(b, WINDOW, SWA_KV_HEADS, HEAD_DIM)
    v_p = vlast.reshape(b, WINDOW, SWA_KV_HEADS, HEAD_DIM)
    conv_p = cp[:, SUBLANES - 2:]

    expand = lambda a: a[None]
    return (y_p, y_s[:, None, :], expand(wkv_p), expand(shift_p), expand(k_p), expand(v_p),
            expand(conv_p), expand(wkv_s), expand(p_s),
            expand(k_s), expand(v_s), expand(conv_s))
```

```python
import math

import jax
import jax.numpy as jnp
from jax import lax
from jax.experimental import pallas as pl
from jax.experimental.pallas import tpu as pltpu

D_MODEL = 1024
HEAD_DIM = 64
RWKV_WIDTH = 512
RWKV_HEADS = 8
RWKV_COLS = 1792
RWKV_GN_EPS = 64e-5
SWA_WIDTH = 512
SWA_HEADS = 8
SWA_KV_HEADS = 2
SWA_GROUP = 4
KV_WIDTH = SWA_KV_HEADS * HEAD_DIM
WINDOW = 128
PAST_LEN = 16384
ROPE_THETA = 10000.0
ATTN_SCALE = HEAD_DIM ** -0.5
D_FF = 2816
NORM_EPS = 1e-6
MASK_VALUE = -1e30
PROJ_COLS = RWKV_COLS + SWA_WIDTH + 2 * KV_WIDTH

LANES = 128
SUBLANES = 8
CHUNK = 64
GROUP_LANES = 256
FF_CHUNK = 256
N_FF_CHUNKS = D_FF // FF_CHUNK
TAIL_SUB = 256
MIX_SUB = 512
TAIL_GROUP = 512
VMEM_LIMIT = 56 * 1024 * 1024

F32 = jnp.float32
BF16 = jnp.bfloat16


def _sigmoid(x):
    return 1.0 / (1.0 + jnp.exp(-x))


def _silu(x):
    return x * _sigmoid(x)


def _rmsnorm(x, g):
    return x * lax.rsqrt(jnp.mean(x * x, axis=-1, keepdims=True) + NORM_EPS) * g


def _dot(a, b):
    return jnp.dot(a.astype(BF16), b.astype(BF16), preferred_element_type=F32)


def _dot_nt(a, b):
    return lax.dot_general(a.astype(BF16), b.astype(BF16), (((1,), (1,)), ((), ())),
                           preferred_element_type=F32)


def _dot_tn(a, b):
    return lax.dot_general(a.astype(BF16), b.astype(BF16), (((0,), (0,)), ((), ())),
                           preferred_element_type=F32)


def _every(fn, *lists):
    return [fn(*a) for a in zip(*lists)]


def _swap_halves(x):
    w = x.shape[-1]
    lane = lax.broadcasted_iota(jnp.int32, x.shape, x.ndim - 1)
    lo = (lane & (HEAD_DIM // 2)) == 0
    return jnp.where(lo, pltpu.roll(x, w - HEAD_DIM // 2, x.ndim - 1),
                     pltpu.roll(x, HEAD_DIM // 2, x.ndim - 1))


def _rope(x, cos, sin):
    reps = x.shape[-1] // LANES
    cos_w = jnp.concatenate([cos] * reps, axis=-1) if reps > 1 else cos
    sin_w = jnp.concatenate([sin] * reps, axis=-1) if reps > 1 else sin
    return x * cos_w + _swap_halves(x) * sin_w


def _proj_features(x, shift, scale, g_pre, w_in, prev_fn, mu, w0, wa_up, a0, g_up,
                   k_k, k_a, r_k, seg_blk, cos, sin):
    h = _rmsnorm(x, g_pre) * (1.0 + scale) + shift
    p = _dot(h, w_in)
    p_rwkv = p[:, :RWKV_COLS]
    prev = prev_fn(p_rwkv)
    xm = p_rwkv + (prev - p_rwkv) * mu
    r = xm[:, 0:512]
    k = xm[:, 512:1024]
    v = xm[:, 1024:1536]
    wa = xm[:, 1536:1664]
    gd = xm[:, 1664:1792]
    lane = lax.broadcasted_iota(jnp.int32, wa.shape, 1)
    wa_act = jnp.where(lane < 64, jnp.tanh(wa), wa)
    lora = _dot(wa_act, wa_up)
    lw = -math.exp(-0.5) * _sigmoid(w0 + lora[:, :512])
    a = _sigmoid(a0 + lora[:, 512:])
    g = _dot(_sigmoid(gd), g_up)
    kk = k * k_k
    kf = k * (1.0 + (a - 1.0) * k_a)
    gl = seg_blk.shape[0]
    n = x.shape[0]
    sums = [_dot(jnp.concatenate([(kk * kk)[:, i:i + gl], (r * kf * r_k)[:, i:i + gl]], axis=0),
                 seg_blk) for i in range(0, RWKV_WIDTH, gl)]
    ss = jnp.concatenate([s_[:n] for s_ in sums], axis=1)
    kk = kk * lax.rsqrt(jnp.maximum(ss, 1e-24))
    bonus = jnp.concatenate([s_[n:] for s_ in sums], axis=1) * v
    feats = (r, lw, kf, v, -kk, kk * a, g, bonus)
    q = _rope(p[:, RWKV_COLS:RWKV_COLS + SWA_WIDTH], cos, sin)
    ks = _rope(p[:, RWKV_COLS + SWA_WIDTH:RWKV_COLS + SWA_WIDTH + KV_WIDTH], cos, sin)
    vs = p[:, RWKV_COLS + SWA_WIDTH + KV_WIDTH:]
    return feats, q, ks, vs, p_rwkv


def _gn_epilogue(o, bonus, g, ln_w, ln_b, seg):
    mu = _dot(o, seg) * (1.0 / HEAD_DIM)
    d = o - mu
    var = _dot(d * d, seg) * (1.0 / HEAD_DIM)
    gn = d * lax.rsqrt(var + RWKV_GN_EPS) * ln_w + ln_b
    return (gn + bonus) * g


def _seg_ones(n):
    r = lax.broadcasted_iota(jnp.int32, (n, n), 0) // HEAD_DIM
    c = lax.broadcasted_iota(jnp.int32, (n, n), 1) // HEAD_DIM
    return (r == c).astype(BF16)


def _ada_kernel(c_ref, w_ref, b_ref, o_ref):
    o_ref[0] = _dot(_silu(c_ref[...]), w_ref[...]) + b_ref[...]


def _ada_call(c_all, w_ada, b_ada):
    rows = c_all.shape[0]
    return pl.pallas_call(
        _ada_kernel,
        grid=(6,),
        in_specs=[
            pl.BlockSpec((rows, D_MODEL), lambda j: (0, 0)),
            pl.BlockSpec((D_MODEL, D_MODEL), lambda j: (0, j)),
            pl.BlockSpec((1, D_MODEL), lambda j: (0, j)),
        ],
        out_specs=pl.BlockSpec((1, rows, D_MODEL), lambda j: (j, 0, 0)),
        out_shape=jax.ShapeDtypeStruct((6, rows, D_MODEL), F32),
        compiler_params=pltpu.CompilerParams(
            dimension_semantics=("arbitrary",), vmem_limit_bytes=VMEM_LIMIT),
        name="ada",
    )(c_all, w_ada, b_ada)


def _expand_kv(x):
    lane = lax.broadcasted_iota(jnp.int32, x.shape, 1)
    rolled = pltpu.roll(x, HEAD_DIM, 1)
    g0 = jnp.where(lane < HEAD_DIM, x, rolled)
    g1 = jnp.where(lane < HEAD_DIM, rolled, x)
    return jnp.concatenate([g0, g0, g1, g1], axis=1)


def _mix_prompt_kernel(x_ref, mod_ref, cos_ref, sin_ref, g_pre, w_in, mu, w0, wa_up, a0,
                       g_up, k_k, k_a, r_k, seg_blk, ln_w, ln_b,
                       y_ref, s_out_ref, q_ref, k_ref, v_ref, klast_ref, vlast_ref, plast_ref,
                       carry_ref, s_ref):
    t = pl.program_id(1)

    @pl.when(t == 0)
    def _():
        carry_ref[...] = jnp.zeros_like(carry_ref)
        s_ref[...] = jnp.zeros_like(s_ref)

    sub = min(x_ref.shape[1], MIX_SUB)

    def sub_tile(i, carry):
        rows = pl.ds(pl.multiple_of(i * sub, sub), sub)
        carry_row = carry_ref[SUBLANES - 1:SUBLANES, :]

        def prev_fn(p_rwkv):
            row = lax.broadcasted_iota(jnp.int32, p_rwkv.shape, 0)
            return jnp.where(row == 0, carry_row, pltpu.roll(p_rwkv, 1, 0))

        feats, q, ks, vs, p_rwkv = _proj_features(
            x_ref[0, rows, :], mod_ref[0, 0], mod_ref[1, 0], g_pre[...], w_in[...], prev_fn,
            mu[...], w0[...], wa_up[...], a0[...], g_up[...], k_k[...], k_a[...], r_k[...],
            seg_blk[...], cos_ref[rows, :], sin_ref[rows, :])
        q_ref[0, rows, :] = (q * ATTN_SCALE).astype(BF16)
        k_ref[0, rows, :] = _expand_kv(ks).astype(BF16)
        v_ref[0, rows, :] = _expand_kv(vs).astype(BF16)
        last = p_rwkv[sub - SUBLANES:, :]
        carry_ref[...] = last
        plast_ref[0] = last
        klast_ref[0] = ks[sub - WINDOW:, :]
        vlast_ref[0] = vs[sub - WINDOW:, :]
        _wkv_tile(feats, ln_w, ln_b, y_ref.at[0, rows, :], s_ref)
        return carry

    lax.fori_loop(0, x_ref.shape[1] // sub, sub_tile, 0)

    @pl.when(t == pl.num_programs(1) - 1)
    def _():
        for g in range(RWKV_WIDTH // GROUP_LANES):
            s_out_ref[0, :, g * GROUP_LANES:(g + 1) * GROUP_LANES] = s_ref[g]


def _proj_sample_kernel(x_ref, mod_ref, cos_ref, sin_ref, prev_ref, g_pre, w_in, mu, w0,
                        wa_up, a0, g_up, k_k, k_a, r_k, seg_blk,
                        feat_ref, ft_ref, q_ref, k_ref, v_ref, p_ref, w_in_b_ref):
    w_in_b = w_in[...].astype(BF16)
    w_in_b_ref[...] = w_in_b
    feats, q, ks, vs, p_rwkv = _proj_features(
        x_ref[...], mod_ref[0], mod_ref[1], g_pre[...], w_in_b, lambda p: prev_ref[...],
        mu[...], w0[...], wa_up[...], a0[...], g_up[...], k_k[...], k_a[...], r_k[...],
        seg_blk[...], cos_ref[...], sin_ref[...])
    for i, f in enumerate(feats):
        feat_ref[i] = f
    for i in range(6):
        ft_ref[i] = feats[i].T
    q_ref[...] = (q * ATTN_SCALE).astype(BF16)
    k_ref[...] = ks
    v_ref[...] = vs
    p_ref[...] = p_rwkv


def _const_spec(arr, grid_rank):
    zeros = (0,) * arr.ndim
    if grid_rank == 1:
        return pl.BlockSpec(arr.shape, lambda i: zeros)
    return pl.BlockSpec(arr.shape, lambda i, j: zeros)


def _mix_prompt_call(x, mod_p, cos, sin, consts, tm):
    b, t, _ = x.shape
    nt = t // tm
    in_specs = [
        pl.BlockSpec((1, tm, D_MODEL), lambda i, j: (i, j, 0)),
        pl.BlockSpec((6, 1, 1, D_MODEL), lambda i, j: (0, i, 0, 0)),
        pl.BlockSpec((tm, LANES), lambda i, j: (j, 0)),
        pl.BlockSpec((tm, LANES), lambda i, j: (j, 0)),
    ] + [_const_spec(c, 2) for c in consts]
    out_shape = (
        jax.ShapeDtypeStruct((b, t, RWKV_WIDTH), BF16),
        jax.ShapeDtypeStruct((b, HEAD_DIM, RWKV_WIDTH), F32),
        jax.ShapeDtypeStruct((b, t, SWA_WIDTH), BF16),
        jax.ShapeDtypeStruct((b, t, SWA_WIDTH), BF16),
        jax.ShapeDtypeStruct((b, t, SWA_WIDTH), BF16),
        jax.ShapeDtypeStruct((b, WINDOW, KV_WIDTH), F32),
        jax.ShapeDtypeStruct((b, WINDOW, KV_WIDTH), F32),
        jax.ShapeDtypeStruct((b, SUBLANES, RWKV_COLS), F32),
    )
    out_specs = (
        pl.BlockSpec((1, tm, RWKV_WIDTH), lambda i, j: (i, j, 0)),
        pl.BlockSpec((1, HEAD_DIM, RWKV_WIDTH), lambda i, j: (i, 0, 0)),
        pl.BlockSpec((1, tm, SWA_WIDTH), lambda i, j: (i, j, 0)),
        pl.BlockSpec((1, tm, SWA_WIDTH), lambda i, j: (i, j, 0)),
        pl.BlockSpec((1, tm, SWA_WIDTH), lambda i, j: (i, j, 0)),
        pl.BlockSpec((1, WINDOW, KV_WIDTH), lambda i, j: (i, 0, 0)),
        pl.BlockSpec((1, WINDOW, KV_WIDTH), lambda i, j: (i, 0, 0)),
        pl.BlockSpec((1, SUBLANES, RWKV_COLS), lambda i, j: (i, 0, 0)),
    )
    return pl.pallas_call(
        _mix_prompt_kernel,
        grid=(b, nt),
        in_specs=in_specs,
        out_specs=out_specs,
        out_shape=out_shape,
        scratch_shapes=[
            pltpu.VMEM((SUBLANES, RWKV_COLS), F32),
            pltpu.VMEM((RWKV_WIDTH // GROUP_LANES, HEAD_DIM, GROUP_LANES), F32),
        ],
        compiler_params=pltpu.CompilerParams(
            dimension_semantics=("arbitrary", "arbitrary"), vmem_limit_bytes=VMEM_LIMIT),
        name="mix_prompt",
    )(x, mod_p, cos, sin, *consts)


def _mod_rows_spec(mod, n):
    return pl.BlockSpec((mod.shape[0], n, mod.shape[2]), lambda i: (0, 0, 0))


def _proj_sample_call(x, mod, cos, sin, prev, consts):
    n = x.shape[0]
    args = (x, mod, cos, sin, prev) + tuple(consts)
    out_shape = (
        jax.ShapeDtypeStruct((8, n, RWKV_WIDTH), F32),
        jax.ShapeDtypeStruct((6, RWKV_WIDTH, n), F32),
        jax.ShapeDtypeStruct((n, SWA_WIDTH), BF16),
        jax.ShapeDtypeStruct((n, KV_WIDTH), F32),
        jax.ShapeDtypeStruct((n, KV_WIDTH), F32),
        jax.ShapeDtypeStruct((n, RWKV_COLS), F32),
        jax.ShapeDtypeStruct((D_MODEL, PROJ_COLS), BF16),
    )
    return pl.pallas_call(
        _proj_sample_kernel,
        grid=(1,),
        in_specs=[_mod_rows_spec(a, n) if i == 1 else _const_spec(a, 1)
                  for i, a in enumerate(args)],
        out_specs=tuple(pl.BlockSpec(s.shape, lambda i, nd=len(s.shape): (0,) * nd)
                        for s in out_shape),
        out_shape=out_shape,
        compiler_params=pltpu.CompilerParams(
            dimension_semantics=("arbitrary",), vmem_limit_bytes=VMEM_LIMIT),
        name="proj_sample",
    )(*args)


def _wkv_tile(feats, ln_w, ln_b, y_ref, s_ref):
    tt = feats[0].shape[0]
    n_chunks = tt // CHUNK
    gl = GROUP_LANES
    n_groups = RWKV_WIDTH // gl
    heads_per_group = gl // HEAD_DIM
    probs = [(c, g) for c in range(n_chunks) for g in range(n_groups)]

    row_c = lax.broadcasted_iota(jnp.int32, (CHUNK, gl), 0)
    col_c = lax.broadcasted_iota(jnp.int32, (CHUNK, gl), 1) % CHUNK
    strict = row_c > col_c
    incl = row_c >= col_c
    eye_cat = (row_c == col_c).astype(F32)
    rb = lax.broadcasted_iota(jnp.int32, (gl, gl), 0) // HEAD_DIM
    cb = lax.broadcasted_iota(jnp.int32, (gl, gl), 1) // HEAD_DIM
    bd_mask = rb == cb
    tri_r = lax.broadcasted_iota(jnp.int32, (CHUNK, CHUNK), 0)
    tri_c = lax.broadcasted_iota(jnp.int32, (CHUNK, CHUNK), 1)
    tril_ones = (tri_r >= tri_c).astype(BF16)
    seg = bd_mask.astype(BF16)

    def bd(x):
        xb = x.astype(BF16)
        return jnp.where(bd_mask, jnp.concatenate([xb] * heads_per_group, axis=0),
                         jnp.zeros((), BF16))

    def fold(x):
        xm = jnp.where(bd_mask, x, 0.0)
        acc = xm[0:HEAD_DIM]
        for hh in range(1, heads_per_group):
            acc = acc + xm[hh * HEAD_DIM:(hh + 1) * HEAD_DIM]
        return acc

    def ld(i, p):
        c, g = p
        return feats[i][c * CHUNK:(c + 1) * CHUNK, g * gl:(g + 1) * gl]

    lw = [ld(1, p) for p in probs]
    na = [ld(4, p) for p in probs]
    bb = [ld(5, p) for p in probs]
    kf = [ld(2, p) for p in probs]
    r = [ld(0, p) for p in probs]
    v = [ld(3, p) for p in probs]

    def cumsum(x):
        hi = x.astype(BF16)
        lo = (x - hi.astype(F32)).astype(BF16)
        both = jnp.dot(tril_ones, jnp.concatenate([hi, lo], axis=1), preferred_element_type=F32)
        return both[:, :gl] + both[:, gl:]

    cum = _every(cumsum, lw)
    cum_last = [x[CHUNK - 1:CHUNK, :] for x in cum]
    e_out = [jnp.exp(-x) for x in cum]
    a_t = _every(lambda n_, x, l_: n_ * jnp.exp(x - l_), na, cum, lw)
    r_t = _every(lambda r_, x: r_ * jnp.exp(x), r, cum)
    b_t = _every(lambda b_, e: b_ * e, bb, e_out)
    k_t = _every(lambda k_, e: k_ * e, kf, e_out)
    gamma = [jnp.exp(x) for x in cum_last]
    e_end = _every(lambda e, g_: e * g_, e_out, gamma)
    b_end = _every(lambda b_, e: b_ * e, bb, e_end)
    k_end = _every(lambda k_, e: k_ * e, kf, e_end)

    ar = _every(lambda a_, r_: jnp.concatenate([a_, r_], axis=0), a_t, r_t)
    pb = _every(lambda x, y: _dot_nt(x, bd(y)), ar, b_t)
    pk = _every(lambda x, y: _dot_nt(x, bd(y)), ar, k_t)
    l_ab = [jnp.where(strict, x[:CHUNK], 0.0) for x in pb]
    l_ak = [jnp.where(strict, x[:CHUNK], 0.0) for x in pk]
    m_rb = [jnp.where(incl, x[CHUNK:], 0.0) for x in pb]
    m_rk = [jnp.where(incl, x[CHUNK:], 0.0) for x in pk]

    x_acc = [eye_cat + l for l in l_ab]
    pw = _every(lambda l: _dot(l, bd(l)), l_ab)
    n_sq = int(math.log2(CHUNK)) - 1
    for lvl in range(n_sq):
        rhs = [bd(p_) for p_ in pw]
        if lvl < n_sq - 1:
            both = _every(lambda x, p_, w_: _dot(jnp.concatenate([x, p_], axis=0), w_),
                          x_acc, pw, rhs)
            x_acc = _every(lambda x, b_: x + b_[:CHUNK], x_acc, both)
            pw = [b_[CHUNK:] for b_ in both]
        else:
            x_acc = _every(lambda x, w_: x + _dot(x, w_), x_acc, rhs)
    t_inv = x_acc

    kv = _every(lambda la, mk, x: _dot(jnp.concatenate([la, mk], axis=0), bd(x)), l_ak, m_rk, v)
    y_loc = [x[:CHUNK] for x in kv]
    wu = _every(lambda t_, a_, y_: _dot(t_, jnp.concatenate([bd(a_), bd(y_)], axis=1)),
                t_inv, a_t, y_loc)
    w_t = [x[:, :gl] for x in wu]
    u_loc = [x[:, gl:] for x in wu]

    mwu = _every(lambda mb, w_, u_: _dot(mb, jnp.concatenate([bd(w_), bd(u_)], axis=1)),
                 m_rb, w_t, u_loc)
    q_c = _every(lambda r_, x: r_ + x[:, :gl], r_t, mwu)
    o_loc = _every(lambda x, y_: x[:, gl:] + y_[CHUNK:], mwu, kv)
    m_low = _every(lambda w_, b_: jnp.where(bd_mask, _dot_tn(w_, b_), 0.0).astype(BF16),
                   w_t, b_end)
    n_loc = _every(lambda u_, v_, b_, k_: fold(_dot_tn(jnp.concatenate([u_, v_], axis=0),
                                                       jnp.concatenate([b_, k_], axis=0))),
                   u_loc, v, b_end, k_end)

    state = [s_ref[g] for g in range(n_groups)]
    starts = []
    for c in range(n_chunks):
        starts.append(state)
        idx = [c * n_groups + g for g in range(n_groups)]
        state = [state[g] * gamma[i] + _dot(state[g], m_low[i]) + n_loc[i]
                 for g, i in enumerate(idx)]
    for g in range(n_groups):
        s_ref[g] = state[g]

    s0 = [starts[c][g] for (c, g) in probs]
    o = _every(lambda q_, s_, ol: _dot_nt(q_, bd(s_)) + ol, q_c, s0, o_loc)

    n_p = len(probs)
    unstack = lambda x: [x[i * CHUNK:(i + 1) * CHUNK] for i in range(n_p)]
    mu = unstack(_dot(jnp.concatenate(o, axis=0), seg) * (1.0 / HEAD_DIM))
    d = _every(lambda x, m_: x - m_, o, mu)
    var = unstack(_dot(jnp.concatenate([x * x for x in d], axis=0), seg) * (1.0 / HEAD_DIM))
    for i, (c, g) in enumerate(probs):
        ls = slice(g * gl, (g + 1) * gl)
        gn = d[i] * lax.rsqrt(var[i] + RWKV_GN_EPS) * ln_w[:, ls] + ln_b[:, ls]
        y = (gn + ld(7, (c, g))) * ld(6, (c, g))
        y_ref[c * CHUNK:(c + 1) * CHUNK, ls] = y.astype(y_ref.dtype)


def _wkv_sample_kernel(ft_ref, s_ref, s_out_ref, o_ref):
    hd = HEAD_DIM
    r, kf, na, bb = ft_ref[0], ft_ref[2], ft_ref[4], ft_ref[5]
    w = jnp.exp(ft_ref[1])

    def value_block(vb, carry):
        v0 = pl.multiple_of(vb * SUBLANES, SUBLANES)
        v_rows = ft_ref[3, pl.ds(v0, SUBLANES), :]
        outs = []
        for j in range(SUBLANES):
            s = s_ref[0, v0 + j]
            sa = jnp.sum(s * na, axis=0, keepdims=True)
            s_new = s * w + sa * bb + v_rows[j:j + 1] * kf
            s_out_ref[0, v0 + j] = s_new
            outs.append(jnp.sum(s_new * r, axis=0, keepdims=True))
        o_ref[pl.ds(v0, SUBLANES), :] = jnp.concatenate(outs, axis=0)
        return carry

    lax.fori_loop(0, hd // SUBLANES, value_block, 0)


def _wkv_sample_call(ft_s, state_t):
    h, hd, _, n = state_t.shape
    return pl.pallas_call(
        _wkv_sample_kernel,
        grid=(h,),
        in_specs=[
            pl.BlockSpec((6, hd, n), lambda i: (0, i, 0)),
            pl.BlockSpec((1, hd, hd, n), lambda i: (i, 0, 0, 0)),
        ],
        out_specs=(
            pl.BlockSpec((1, hd, hd, n), lambda i: (i, 0, 0, 0)),
            pl.BlockSpec((hd, n), lambda i: (i, 0)),
        ),
        out_shape=(
            jax.ShapeDtypeStruct(state_t.shape, F32),
            jax.ShapeDtypeStruct((h * hd, n), F32),
        ),
        compiler_params=pltpu.CompilerParams(
            dimension_semantics=("arbitrary",), vmem_limit_bytes=VMEM_LIMIT),
        name="wkv_sample",
    )(ft_s, state_t)


def _swa_prompt_kernel(sink_ref, q_ref, kp_ref, kc_ref, vp_ref, vc_ref, o_ref):
    j = pl.program_id(1)
    w = WINDOW
    tq = q_ref.shape[1]
    n_blk = tq // w
    gl = GROUP_LANES
    rows = SWA_GROUP * w
    probs = [(qi, g) for qi in range(n_blk) for g in range(SWA_KV_HEADS)]

    lane_head = lax.broadcasted_iota(jnp.int32, (w, gl), 1) // HEAD_DIM
    head_mask = [lane_head == h for h in range(SWA_GROUP)]
    qi_ = lax.broadcasted_iota(jnp.int32, (rows, 2 * w), 0) % w
    ki_ = lax.broadcasted_iota(jnp.int32, (rows, 2 * w), 1)
    diff = qi_ - (ki_ - w)
    in_window = (diff >= 0) & (diff < WINDOW)
    first_valid = in_window & ((j * tq + ki_ - w) >= 0)

    def keys(ref_prev, ref_cur, qi, g):
        ls = slice(g * gl, (g + 1) * gl)
        prev = ref_prev[0, :, ls] if qi == 0 else ref_cur[0, (qi - 1) * w:qi * w, ls]
        return jnp.concatenate([prev, ref_cur[0, qi * w:(qi + 1) * w, ls]], axis=0)

    def lhs(qi, g):
        qg = q_ref[0, qi * w:(qi + 1) * w, g * gl:(g + 1) * gl]
        return jnp.concatenate([jnp.where(m, qg, jnp.zeros((), BF16)) for m in head_mask], axis=0)

    sinks = []
    for g in range(SWA_KV_HEADS):
        sinks.append(jnp.concatenate(
            [jnp.full((w, 1), sink_ref[g * SWA_GROUP + h], F32) for h in range(SWA_GROUP)], axis=0))

    sink_lane = ki_ == qi_
    fill = [jnp.where(sink_lane, sk, MASK_VALUE) for sk in sinks]
    ones_k = jnp.ones((2 * w, LANES), BF16)

    s = [_dot_nt(lhs(qi, g), keys(kp_ref, kc_ref, qi, g)) for qi, g in probs]
    s = [jnp.where(first_valid if qi == 0 else in_window, x, fill[g])
         for x, (qi, g) in zip(s, probs)]
    m = [jnp.max(x, axis=-1, keepdims=True) for x in s]
    e = _every(lambda x, m_: jnp.exp(x - m_), s, m)
    inv = [1.0 / _dot(x, ones_k) for x in e]
    prob = _every(lambda e_, i_: jnp.where(sink_lane, jnp.zeros((), BF16),
                                           (e_ * jnp.concatenate([i_, i_], axis=1)).astype(BF16)),
                  e, inv)
    lane_head2 = lax.broadcasted_iota(jnp.int32, (2 * w, gl), 1) // HEAD_DIM
    for p_, (qi, g) in zip(prob, probs):
        vk = keys(vp_ref, vc_ref, qi, g)
        rhs = jnp.concatenate([jnp.where(lane_head2 == h, vk, jnp.zeros((), BF16))
                               for h in range(SWA_GROUP)], axis=0)
        p_cat = jnp.concatenate([p_[h * w:(h + 1) * w] for h in range(SWA_GROUP)], axis=1)
        y = _dot(p_cat, rhs)
        o_ref[0, qi * w:(qi + 1) * w, g * gl:(g + 1) * gl] = y.astype(o_ref.dtype)


def _swa_prompt_call(sinks, q, k, v, tq):
    b, t, _ = q.shape
    w = WINDOW
    per = tq // w
    prev = lambda i, j: (i, jnp.maximum(j * per - 1, 0), 0)
    cur = lambda i, j: (i, j, 0)
    return pl.pallas_call(
        _swa_prompt_kernel,
        grid=(b, t // tq),
        in_specs=[
            pl.BlockSpec(memory_space=pltpu.SMEM),
            pl.BlockSpec((1, tq, SWA_WIDTH), cur),
            pl.BlockSpec((1, w, SWA_WIDTH), prev),
            pl.BlockSpec((1, tq, SWA_WIDTH), cur),
            pl.BlockSpec((1, w, SWA_WIDTH), prev),
            pl.BlockSpec((1, tq, SWA_WIDTH), cur),
        ],
        out_specs=pl.BlockSpec((1, tq, SWA_WIDTH), cur),
        out_shape=jax.ShapeDtypeStruct((b, t, SWA_WIDTH), BF16),
        compiler_params=pltpu.CompilerParams(
            dimension_semantics=("arbitrary", "arbitrary"), vmem_limit_bytes=VMEM_LIMIT),
        name="swa_prompt",
    )(sinks, q, k, k, v, v)


def _swa_sample_kernel(sink_ref, q_ref, knr_ref, vnr_ref, knt_ref, vnt_ref, ck_ref, cv_ref,
                       o_ref, ko_ref, vo_ref):
    nb, _, hd, l = ck_ref.shape
    q = q_ref[...]
    ck = ck_ref[...].reshape(nb * SWA_KV_HEADS, hd, l)
    cv = cv_ref[...].reshape(nb * SWA_KV_HEADS, hd, l)
    knr = knr_ref[...].astype(BF16).astype(F32)
    vnr = vnr_ref[...].astype(BF16).astype(F32)
    ki = lax.broadcasted_iota(jnp.int32, (1, 1, l), 2)
    kpos = PAST_LEN - l + ki
    diff = PAST_LEN - kpos
    valid = (diff >= 0) & (diff < WINDOW) & (kpos >= 0)
    s_c = lax.dot_general(q, ck.astype(BF16), (((2,), (1,)), ((0,), (0,))),
                          preferred_element_type=F32)
    s_c = jnp.where(valid, s_c, MASK_VALUE)
    s_n = jnp.sum(q.astype(F32) * knr, axis=-1, keepdims=True)
    sink = sink_ref[:, :, 0:1]
    m = jnp.maximum(jnp.maximum(jnp.max(s_c, axis=-1, keepdims=True), s_n), sink)
    e_c = jnp.exp(s_c - m)
    e_n = jnp.exp(s_n - m)
    denom = jnp.sum(e_c, axis=-1, keepdims=True) + e_n + jnp.exp(sink - m)
    p_c = (e_c / denom).astype(BF16)
    p_n = (e_n / denom).astype(BF16).astype(F32)
    o = lax.dot_general(p_c, cv.astype(BF16), (((2,), (2,)), ((0,), (0,))),
                        preferred_element_type=F32)
    o_ref[...] = o + p_n * vnr
    lane = lax.broadcasted_iota(jnp.int32, (hd, l), 1)
    for b in range(nb):
        for g in range(SWA_KV_HEADS):
            rows = slice(g * hd, (g + 1) * hd)
            ko_ref[b, g] = jnp.where(lane == l - 1, knt_ref[0, rows, b:b + 1],
                                     pltpu.roll(ck_ref[b, g], l - 1, 1))
            vo_ref[b, g] = jnp.where(lane == l - 1, vnt_ref[0, rows, b:b + 1],
                                     pltpu.roll(cv_ref[b, g], l - 1, 1))


def _swa_sample_call(sink_t, q8, knr, vnr, knt, vnt, ck, cv, nb):
    n, g, hd, l = ck.shape
    rows = nb * g
    blk3 = lambda i: (i, 0, 0)
    blk4 = lambda i: (i, 0, 0, 0)
    return pl.pallas_call(
        _swa_sample_kernel,
        grid=(n // nb,),
        in_specs=[
            pl.BlockSpec((rows, SUBLANES, LANES), lambda i: (0, 0, 0)),
            pl.BlockSpec((rows, SUBLANES, hd), blk3),
            pl.BlockSpec((rows, 1, hd), blk3),
            pl.BlockSpec((rows, 1, hd), blk3),
            pl.BlockSpec((1, g * hd, nb), blk3),
            pl.BlockSpec((1, g * hd, nb), blk3),
            pl.BlockSpec((nb, g, hd, l), blk4),
            pl.BlockSpec((nb, g, hd, l), blk4),
        ],
        out_specs=(
            pl.BlockSpec((rows, SUBLANES, hd), blk3),
            pl.BlockSpec((nb, g, hd, l), blk4),
            pl.BlockSpec((nb, g, hd, l), blk4),
        ),
        out_shape=(
            jax.ShapeDtypeStruct((n * g, SUBLANES, hd), F32),
            jax.ShapeDtypeStruct(ck.shape, F32),
            jax.ShapeDtypeStruct(cv.shape, F32),
        ),
        compiler_params=pltpu.CompilerParams(
            dimension_semantics=("arbitrary",), vmem_limit_bytes=VMEM_LIMIT),
        name="swa_sample",
    )(sink_t, q8, knr, vnr, knt, vnt, ck, cv)


def _tail_mix(x, y_rwkv, y_swa, g1, sh2, sc2, w_out, n_post_mix, n_pre_ffn):
    mix = _dot(y_rwkv, w_out[:RWKV_WIDTH]) + _dot(y_swa, w_out[RWKV_WIDTH:])
    x1 = x + g1 * _rmsnorm(mix, n_post_mix)
    h2 = (_rmsnorm(x1, n_pre_ffn) * (1.0 + sc2) + sh2).astype(BF16)
    return x1, h2


def _tail_prompt_kernel(x_ref, yr_ref, ys_ref, mod_ref, w_out, n_post_mix, n_pre_ffn,
                        n_post_ffn, w_up_a, w_up_b, w_down, cw_ref, cb_ref,
                        y_ref, cp_ref, carry_ref, act_ref):
    t = pl.program_id(1)
    tm = x_ref.shape[1]
    sub = min(tm, TAIL_SUB)
    grp = min(tm, TAIL_GROUP)

    @pl.when(t == 0)
    def _():
        carry_ref[...] = jnp.zeros_like(carry_ref)

    g1, sh2, sc2, g2 = mod_ref[2, 0], mod_ref[3, 0], mod_ref[4, 0], mod_ref[5, 0]

    def cols_of(c, half):
        return slice(half * D_FF + c * FF_CHUNK, half * D_FF + (c + 1) * FF_CHUNK)

    def conv(u, cols):
        ext = jnp.concatenate([carry_ref[:, cols], u], axis=0)
        last = u[sub - SUBLANES:]
        carry_ref[:, cols] = last
        cp_ref[0, :, cols] = last
        return (cb_ref[:, cols] + pltpu.roll(ext, 2, 0)[SUBLANES:] * cw_ref[0:1, cols]
                + pltpu.roll(ext, 1, 0)[SUBLANES:] * cw_ref[1:2, cols] + u * cw_ref[2:3, cols])

    def group(gi, carry):
        base = pl.multiple_of(gi * grp, grp)
        blocks = [(pl.ds(base + j * sub, sub), slice(j * sub, (j + 1) * sub))
                  for j in range(grp // sub)]
        mixes = [_dot(yr_ref[0, rb, :], w_out[:RWKV_WIDTH]) + _dot(ys_ref[0, rb, :], w_out[RWKV_WIDTH:])
                 for rb, _ in blocks]
        for (rb, ab), mix in zip(blocks, mixes):
            x1 = x_ref[0, rb, :] + g1 * _rmsnorm(mix, n_post_mix[...])
            h2 = (_rmsnorm(x1, n_pre_ffn[...]) * (1.0 + sc2) + sh2).astype(BF16)

            def up(c):
                chunk = slice(c * FF_CHUNK, (c + 1) * FF_CHUNK)
                return [jnp.dot(h2, w[:, chunk], preferred_element_type=F32)
                        for w in (w_up_a, w_up_b)]

            u_next = up(0)
            for c in range(N_FF_CHUNKS):
                u_cur = u_next
                if c + 1 < N_FF_CHUNKS:
                    u_next = up(c + 1)
                za, zb = [conv(u_cur[half], cols_of(c, half)) for half in range(2)]
                act_ref[ab, c * FF_CHUNK:(c + 1) * FF_CHUNK] = (_silu(za) * zb).astype(BF16)
            ff = jnp.dot(act_ref[ab, :], w_down[...], preferred_element_type=F32)
            y_ref[0, rb, :] = x1 + g2 * _rmsnorm(ff, n_post_ffn[...])
        return carry

    lax.fori_loop(0, tm // grp, group, 0)


def _tail_sample_kernel(x_ref, o_ref, feat_ref, ys_ref, mod_ref, ln_w, ln_b, w_out, n_post_mix,
                        n_pre_ffn, n_post_ffn, wa_ref, wb_ref, wd_ref, p0a, p0b, p1a, p1b,
                        cwa, cwb, cba, cbb,
                        y_ref, ua_ref, ub_ref, w_out_b_ref, wa_b_ref, wb_b_ref, wd_b_ref,
                        x1_ref, h2_ref, acc_ref):
    c = pl.program_id(0)

    @pl.when(c == 0)
    def _():
        w_out_b = w_out[...].astype(BF16)
        w_out_b_ref[...] = w_out_b
        seg = _seg_ones(RWKV_WIDTH)
        y_rwkv = _gn_epilogue(o_ref[...].T, feat_ref[7], feat_ref[6], ln_w[...], ln_b[...], seg)
        x1, h2 = _tail_mix(x_ref[...], y_rwkv, ys_ref[...], mod_ref[2], mod_ref[3], mod_ref[4],
                           w_out_b, n_post_mix[...], n_pre_ffn[...])
        x1_ref[...] = x1
        h2_ref[...] = h2
        acc_ref[...] = jnp.zeros_like(acc_ref)

    wa = wa_ref[...].astype(BF16)
    wb = wb_ref[...].astype(BF16)
    wd = wd_ref[...].astype(BF16)
    wa_b_ref[...] = wa
    wb_b_ref[...] = wb
    wd_b_ref[...] = wd
    h2 = h2_ref[...]
    ua = jnp.dot(h2, wa, preferred_element_type=F32)
    ub = jnp.dot(h2, wb, preferred_element_type=F32)
    ua_ref[...] = ua
    ub_ref[...] = ub
    za = cba[...] + p0a[...] * cwa[0:1, :] + p1a[...] * cwa[1:2, :] + ua * cwa[2:3, :]
    zb = cbb[...] + p0b[...] * cwb[0:1, :] + p1b[...] * cwb[1:2, :] + ub * cwb[2:3, :]
    acc_ref[...] += _dot(_silu(za) * zb, wd)

    @pl.when(c == pl.num_programs(0) - 1)
    def _():
        y_ref[...] = x1_ref[...] + mod_ref[5] * _rmsnorm(acc_ref[...], n_post_ffn[...])


def _tail_prompt_call(x, y_rwkv, y_swa, mod_p, consts, tm):
    b, t, _ = x.shape
    tok = lambda i, j: (i, j, 0)
    in_specs = [
        pl.BlockSpec((1, tm, D_MODEL), tok),
        pl.BlockSpec((1, tm, RWKV_WIDTH), tok),
        pl.BlockSpec((1, tm, SWA_WIDTH), tok),
        pl.BlockSpec((6, 1, 1, D_MODEL), lambda i, j: (0, i, 0, 0)),
    ] + [_const_spec(c, 2) for c in consts]
    return pl.pallas_call(
        _tail_prompt_kernel,
        grid=(b, t // tm),
        in_specs=in_specs,
        out_specs=(
            pl.BlockSpec((1, tm, D_MODEL), tok),
            pl.BlockSpec((1, SUBLANES, 2 * D_FF), lambda i, j: (i, 0, 0)),
        ),
        out_shape=(
            jax.ShapeDtypeStruct((b, t, D_MODEL), F32),
            jax.ShapeDtypeStruct((b, SUBLANES, 2 * D_FF), F32),
        ),
        scratch_shapes=[
            pltpu.VMEM((SUBLANES, 2 * D_FF), F32),
            pltpu.VMEM((min(tm, TAIL_GROUP), D_FF), BF16),
        ],
        compiler_params=pltpu.CompilerParams(
            dimension_semantics=("arbitrary", "arbitrary"), vmem_limit_bytes=VMEM_LIMIT),
        name="tail_prompt",
    )(x, y_rwkv, y_swa, mod_p, *consts)


def _tail_sample_call(x, o_t, feat_s, y_swa, mod, p0, p1, ln_w, ln_b, w_out, n_post_mix,
                      n_pre_ffn, n_post_ffn, w_up, w_down, cw, cb):
    n = x.shape[0]
    nc = N_FF_CHUNKS
    fc = FF_CHUNK
    whole = lambda a: pl.BlockSpec(a.shape, lambda c, nd=a.ndim: (0,) * nd)
    gate = lambda rows: pl.BlockSpec((rows, fc), lambda c: (0, c))
    value = lambda rows: pl.BlockSpec((rows, fc), lambda c: (0, nc + c))
    in_specs = [
        whole(x), whole(o_t), whole(feat_s), whole(y_swa),
        pl.BlockSpec((mod.shape[0], n, D_MODEL), lambda c: (0, 0, 0)),
        whole(ln_w), whole(ln_b), whole(w_out), whole(n_post_mix), whole(n_pre_ffn),
        whole(n_post_ffn),
        gate(D_MODEL), value(D_MODEL), pl.BlockSpec((fc, D_MODEL), lambda c: (c, 0)),
        gate(n), value(n), gate(n), value(n), gate(3), value(3), gate(1), value(1),
    ]
    out_shape = (
        jax.ShapeDtypeStruct((n, D_MODEL), F32),
        jax.ShapeDtypeStruct((n, D_FF), F32),
        jax.ShapeDtypeStruct((n, D_FF), F32),
        jax.ShapeDtypeStruct((D_MODEL, D_MODEL), BF16),
        jax.ShapeDtypeStruct((D_MODEL, D_FF), BF16),
        jax.ShapeDtypeStruct((D_MODEL, D_FF), BF16),
        jax.ShapeDtypeStruct((D_FF, D_MODEL), BF16),
    )
    chunk_cols = lambda rows: pl.BlockSpec((rows, fc), lambda c: (0, c))
    out_specs = (
        pl.BlockSpec((n, D_MODEL), lambda c: (0, 0)),
        chunk_cols(n), chunk_cols(n),
        pl.BlockSpec((D_MODEL, D_MODEL), lambda c: (0, 0)),
        chunk_cols(D_MODEL), chunk_cols(D_MODEL),
        pl.BlockSpec((fc, D_MODEL), lambda c: (c, 0)),
    )
    return pl.pallas_call(
        _tail_sample_kernel,
        grid=(nc,),
        in_specs=in_specs,
        out_specs=out_specs,
        out_shape=out_shape,
        scratch_shapes=[
            pltpu.VMEM((n, D_MODEL), F32),
            pltpu.VMEM((n, D_MODEL), BF16),
            pltpu.VMEM((n, D_MODEL), F32),
        ],
        compiler_params=pltpu.CompilerParams(
            dimension_semantics=("arbitrary",), vmem_limit_bytes=VMEM_LIMIT),
        name="tail_sample",
    )(x, o_t, feat_s, y_swa, mod, ln_w, ln_b, w_out, n_post_mix, n_pre_ffn, n_post_ffn,
      w_up, w_up, w_down, p0, p0, p1, p1, cw, cw, cb, cb)


def _rope_tables(pos):
    half = HEAD_DIM // 2
    inv = ROPE_THETA ** (-jnp.arange(half, dtype=F32) / half)
    ang = pos.astype(F32)[:, None] * inv[None, :]
    cos, sin = jnp.cos(ang), jnp.sin(ang)
    cos_h = jnp.concatenate([cos, cos], axis=-1)
    sin_h = jnp.concatenate([-sin, sin], axis=-1)
    return jnp.tile(cos_h, (1, LANES // HEAD_DIM)), jnp.tile(sin_h, (1, LANES // HEAD_DIM))


def kernel(x_prompt, x_sample, state_rwkv_wkv, state_rwkv_shift, cache_swa_k, cache_swa_v,
           state_ffn_conv, c_prompt, c_sample, w_ada, b_ada, norm_pre_mix, norm_post_mix,
           norm_pre_ffn, norm_post_ffn, w_in, rwkv_mu, rwkv_w0, rwkv_w_up, rwkv_a0, rwkv_a_up,
           rwkv_g_up, rwkv_k_k, rwkv_k_a, rwkv_r_k, rwkv_ln_w, rwkv_ln_b, swa_sinks, w_out,
           ffn_w_up, ffn_conv_w, ffn_conv_b, ffn_w_down):
    depth = w_ada.shape[0]
    assert depth == 1 and x_sample.shape[1] == 1
    b, t, _ = x_prompt.shape
    n = x_sample.shape[0]
    tm = min(256, t)
    blk = min(4 * tm, t)
    assert t % blk == 0 and tm % WINDOW == 0 and tm % CHUNK == 0 and n % SUBLANES == 0
    nb_swa = 2 * SUBLANES if n % (2 * SUBLANES) == 0 else SUBLANES
    li = 0

    row = lambda v: v.reshape(1, -1)
    zeros_l = jnp.zeros((64, RWKV_WIDTH), F32)
    wa_up = jnp.concatenate([
        jnp.concatenate([rwkv_w_up[li], zeros_l], axis=1),
        jnp.concatenate([zeros_l, rwkv_a_up[li]], axis=1)], axis=0).astype(BF16)
    hid = jnp.arange(GROUP_LANES) // HEAD_DIM
    seg_blk = (hid[:, None] == hid[None, :]).astype(BF16)
    proj_consts = (row(norm_pre_mix[li]), w_in[li], row(rwkv_mu[li]), row(rwkv_w0[li]), wa_up,
                   row(rwkv_a0[li]), rwkv_g_up[li].astype(BF16), row(rwkv_k_k[li]),
                   row(rwkv_k_a[li]), row(rwkv_r_k[li]), seg_blk)
    norms = (row(norm_post_mix[li]), row(norm_pre_ffn[li]), row(norm_post_ffn[li]))
    cw, cb = ffn_conv_w[li], row(ffn_conv_b[li])
    ln_w, ln_b = row(rwkv_ln_w[li]), row(rwkv_ln_b[li])

    mod = _ada_call(jnp.concatenate([c_sample, c_prompt], axis=0), w_ada[li], row(b_ada[li]))
    mod_p = mod[:, n:].reshape(6, b, 1, D_MODEL)

    cos_s, sin_s = _rope_tables(jnp.full((1,), PAST_LEN, jnp.int32))
    feat_s, ft_s, q_s, kn_s, vn_s, p_s, w_in_b = _proj_sample_call(
        x_sample[:, 0], mod, cos_s, sin_s, state_rwkv_shift[li], proj_consts)

    state_t = jnp.transpose(state_rwkv_wkv[li], (1, 2, 3, 0))
    wkv_t, o_t = _wkv_sample_call(ft_s, state_t)
    wkv_s = jnp.transpose(wkv_t, (3, 0, 1, 2))

    g2 = SWA_KV_HEADS
    q4 = q_s.reshape(n * g2, SWA_GROUP, HEAD_DIM)
    q8 = jnp.concatenate([q4, jnp.zeros_like(q4)], axis=1)
    sink_t = jnp.broadcast_to(
        jnp.concatenate([swa_sinks[li].reshape(g2, SWA_GROUP),
                         jnp.full((g2, SWA_GROUP), MASK_VALUE, F32)], axis=1)[None, :, :, None],
        (nb_swa, g2, SUBLANES, LANES)).reshape(nb_swa * g2, SUBLANES, LANES)
    cols = lambda a: a.reshape(n // nb_swa, nb_swa, KV_WIDTH).transpose(0, 2, 1)
    ck = jnp.transpose(cache_swa_k[li], (0, 2, 3, 1))
    cv = jnp.transpose(cache_swa_v[li], (0, 2, 3, 1))
    o_att, k_t, v_t = _swa_sample_call(
        sink_t, q8, kn_s.reshape(n * g2, 1, HEAD_DIM), vn_s.reshape(n * g2, 1, HEAD_DIM),
        cols(kn_s), cols(vn_s), ck, cv, nb_swa)
    y_swa_s = o_att[:, :SWA_GROUP].reshape(n, SWA_WIDTH)
    k_s = jnp.transpose(k_t, (0, 3, 1, 2))
    v_s = jnp.transpose(v_t, (0, 3, 1, 2))

    conv0 = state_ffn_conv[li]
    y_s, ua_s, ub_s, w_out_b, w_up_a_b, w_up_b_b, w_down_b = _tail_sample_call(
        x_sample[:, 0], o_t, feat_s, y_swa_s.astype(BF16), mod, conv0[:, 0], conv0[:, 1],
        ln_w, ln_b, w_out[li], *norms, ffn_w_up[li], ffn_w_down[li], cw, cb)
    conv_s = jnp.stack([conv0[:, 1], jnp.concatenate([ua_s, ub_s], axis=1)], axis=1)

    cos_p, sin_p = _rope_tables(jnp.arange(t, dtype=jnp.int32))
    proj_consts_p = proj_consts[:1] + (w_in_b,) + proj_consts[2:]
    y_rwkv_p, s_cat, q_p, kx_p, vx_p, klast, vlast, plast = _mix_prompt_call(
        x_prompt, mod_p, cos_p, sin_p, proj_consts_p + (ln_w, ln_b), blk)
    y_swa_p = _swa_prompt_call(swa_sinks[li], q_p, kx_p, vx_p, min(2 * blk, t))
    tail_consts = (w_out_b,) + norms + (w_up_a_b, w_up_b_b, w_down_b, cw, cb)
    y_p, cp = _tail_prompt_call(x_prompt, y_rwkv_p, y_swa_p, mod_p, tail_consts, blk)

    wkv_p = s_cat.reshape(b, HEAD_DIM, RWKV_HEADS, HEAD_DIM).transpose(0, 2, 1, 3)
    shift_p = plast[:, SUBLANES - 1]
    k_p = klast.reshape(b, WINDOW, SWA_KV_HEADS, HEAD_DIM)
    v_p = vlast.reshape(b, WINDOW, SWA_KV_HEADS, HEAD_DIM)
    conv_p = cp[:, SUBLANES - 2:]

    expand = lambda a: a[None]
    return (y_p, y_s[:, None, :], expand(wkv_p), expand(shift_p), expand(k_p), expand(v_p),
            expand(conv_p), expand(wkv_s), expand(p_s),
            expand(k_s), expand(v_s), expand(conv_s))
```
